```python
import math
import jax, jax.numpy as jnp
from jax import lax
import numpy as np

D_MODEL = 1024
BATCH = 32
SEQ = 2048
DEPTH = 1
DEC_BATCH = 32
DEC_SEQ = 64
PAST_LEN = 1024

CHUNK = 64
MIX_WIDTH = D_MODEL
SB_HEAD_DIM = 64
SB_WIDTH = MIX_WIDTH // 2
SB_HEADS = SB_WIDTH // SB_HEAD_DIM
ML_HEAD_DIM = 128
ML_WIDTH = MIX_WIDTH - SB_WIDTH
ML_HEADS = ML_WIDTH // ML_HEAD_DIM
CONV_W = 4
QBLK = 128
N_EXPERTS = 32
TOP_K = 4
D_FF = D_MODEL
SWIGLU_LIMIT = 7.0
SWIGLU_ALPHA = 1.702
MOE_BLK = 128
EPS = 1e-6
SPLITS = (SB_WIDTH, 2 * SB_WIDTH, 3 * SB_WIDTH, 3 * SB_WIDTH + 2 * ML_WIDTH,
          3 * SB_WIDTH + 3 * ML_WIDTH, 3 * SB_WIDTH + 3 * ML_WIDTH + 2 * ML_HEADS)
IN_COLS = 3 * SB_WIDTH + 4 * ML_WIDTH + 2 * ML_HEADS

kernel_name = "streaming_sb_mlstm_moe_adaln"


def rmsnorm(x, g):
    xf = x.astype(jnp.float32)
    y = xf * lax.rsqrt(jnp.mean(xf * xf, axis=-1, keepdims=True) + EPS)
    return (y * g.astype(jnp.float32)).astype(x.dtype)


def stick_breaking_attention(q, k, v, q_offset):
    sq = q.shape[1]
    scale = 1.0 / math.sqrt(q.shape[-1])
    outs = []
    for start in range(0, sq, QBLK):
        stop = min(start + QBLK, sq)
        kend = q_offset + stop
        qb = q[:, start:stop].astype(jnp.float32)
        kb = k[:, :kend].astype(jnp.float32)
        vb = v[:, :kend].astype(jnp.float32)
        z = jnp.einsum('bqhd,bkhd->bhqk', qb, kb) * scale
        qpos = q_offset + start + jnp.arange(stop - start)
        kpos = jnp.arange(kend)
        mask = kpos[None, :] < qpos[:, None]
        log_1m = jnp.where(mask, jax.nn.log_sigmoid(-z), 0.0)
        between = lax.cumsum(log_1m, axis=3, reverse=True) - log_1m
        a = jnp.where(mask, jnp.exp(jax.nn.log_sigmoid(z) + between), 0.0)
        outs.append(jnp.einsum('bhqk,bkhd->bqhd', a, vb))
    return jnp.concatenate(outs, axis=1).astype(q.dtype)


def mlstm_chunk(carry, inp):
    c_mat, n_vec, m_st = carry
    q, k, v, ig, lf = inp
    L = q.shape[1]
    b = jnp.cumsum(lf, axis=1)
    causal = jnp.tril(jnp.ones((L, L), dtype=bool))[None, :, :, None]
    d_log = jnp.where(causal, b[:, :, None, :] - b[:, None, :, :] + ig[:, None, :, :], -jnp.inf)
    inter = b + m_st[:, None, :]
    m_t = jnp.maximum(inter, jnp.max(d_log, axis=2))
    w_intra = jnp.exp(d_log - m_t[:, :, None, :])
    w_inter = jnp.exp(inter - m_t)
    s = jnp.einsum('bthd,bshd->btsh', q, k) * w_intra
    num = jnp.einsum('btsh,bshv->bthv', s, v) + w_inter[..., None] * jnp.einsum('bhvd,bthd->bthv', c_mat, q)
    den = jnp.sum(s, axis=2) + w_inter * jnp.einsum('bhd,bthd->bth', n_vec, q)
    h = num / jnp.maximum(jnp.abs(den), jnp.exp(-m_t))[..., None]
    b_last = b[:, -1]
    g = b_last[:, None, :] - b + ig
    m_new = jnp.maximum(b_last + m_st, jnp.max(g, axis=1))
    w_state = jnp.exp(g - m_new[:, None, :])
    decay = jnp.exp(b_last + m_st - m_new)
    c_new = decay[..., None, None] * c_mat + jnp.einsum('bsh,bshv,bshd->bhvd', w_state, v, k)
    n_new = decay[..., None] * n_vec + jnp.einsum('bsh,bshd->bhd', w_state, k)
    return (c_new, n_new, m_new), h


def mlstm(q, k, v, ig, lf, c0, n0, m0):
    bsz, s = q.shape[:2]
    L = min(CHUNK, s)
    nc = s // L

    def to_chunks(a):
        return jnp.swapaxes(a.reshape((bsz, nc, L) + a.shape[2:]), 0, 1)

    carry, h = lax.scan(mlstm_chunk, (c0, n0, m0),
                        (to_chunks(q), to_chunks(k), to_chunks(v), to_chunks(ig), to_chunks(lf)))
    h = jnp.swapaxes(h, 0, 1).reshape((bsz, s) + h.shape[3:])
    return h, carry


def causal_conv(u, past, w, b):
    full = jnp.concatenate([past.astype(u.dtype), u], axis=1)
    s = u.shape[1]
    y = b + full[:, 0:s] * w[0]
    for j in range(1, CONV_W):
        y = y + full[:, j:j + s] * w[j]
    return y, full[:, -(CONV_W - 1):]


def moe_ffn(h, w_router, b_router, w_gu, b_gu, w_down, b_down):
    t, d = h.shape
    logits = h.astype(jnp.float32) @ w_router.astype(jnp.float32) + b_router.astype(jnp.float32)
    top_val, top_idx = lax.top_k(logits, TOP_K)
    gates = jax.nn.softmax(top_val, axis=-1)
    tk = t * TOP_K
    flat_e = top_idx.reshape(-1).astype(jnp.int32)
    flat_tok = jnp.repeat(jnp.arange(t, dtype=jnp.int32), TOP_K)
    flat_gate = gates.reshape(-1)
    order = jnp.argsort(flat_e)
    se = flat_e[order]
    counts = jnp.bincount(flat_e, length=N_EXPERTS).astype(jnp.int32)
    padded = (counts + MOE_BLK - 1) // MOE_BLK * MOE_BLK
    pad_end = jnp.cumsum(padded)
    pad_start = pad_end - padded
    grp_start = jnp.cumsum(counts) - counts
    dest = pad_start[se] + jnp.arange(tk, dtype=jnp.int32) - grp_start[se]
    n_blocks = (tk + N_EXPERTS * (MOE_BLK - 1) + MOE_BLK - 1) // MOE_BLK
    n_rows = n_blocks * MOE_BLK
    row_tok = jnp.zeros((n_rows,), jnp.int32).at[dest].set(flat_tok[order])
    row_gate = jnp.zeros((n_rows,), jnp.float32).at[dest].set(flat_gate[order])
    block_exp = jnp.minimum(jnp.searchsorted(pad_end, jnp.arange(n_blocks, dtype=jnp.int32) * MOE_BLK, side='right'),
                            N_EXPERTS - 1)

    def step(acc, blk):
        tok, gate, e = blk
        xb = h[tok]
        gu = xb @ w_gu[e] + b_gu[e]
        x_glu, x_lin = jnp.split(gu, 2, axis=-1)
        x_glu = jnp.minimum(x_glu, SWIGLU_LIMIT)
        x_lin = jnp.clip(x_lin, -SWIGLU_LIMIT, SWIGLU_LIMIT)
        act = x_glu * jax.nn.sigmoid(SWIGLU_ALPHA * x_glu) * (x_lin + 1.0)
        out = (act @ w_down[e] + b_down[e]).astype(jnp.float32) * gate[:, None]
        return acc.at[tok].add(out), None

    acc, _ = lax.scan(step, jnp.zeros((t, d), jnp.float32),
                      (row_tok.reshape(n_blocks, MOE_BLK), row_gate.reshape(n_blocks, MOE_BLK), block_exp))
    return acc.astype(h.dtype)


def trunk_layer(x, c, k_past, v_past, conv_past, c0, n0, m0, w_ada, b_ada, g_norm1, w_in, g_q, g_k,
                w_conv, b_conv, b_gate, g_head, w_out, g_norm2, w_router, b_router, w_gu, b_gu, w_down, b_down):
    bsz, s, d = x.shape
    past = k_past.shape[1]
    mod = jax.nn.silu(c) @ w_ada + b_ada
    sh1, sc1, ga1, sh2, sc2, ga2 = [m[:, None, :] for m in jnp.split(mod, 6, axis=-1)]
    h = rmsnorm(x, g_norm1) * (1.0 + sc1) + sh1
    u = h @ w_in
    sb_q, sb_k, sb_v, ml_qk, ml_v, ml_gate, ml_o = jnp.split(u, SPLITS, axis=-1)
    q = rmsnorm(sb_q.reshape(bsz, s, SB_HEADS, SB_HEAD_DIM), g_q)
    k = rmsnorm(sb_k.reshape(bsz, s, SB_HEADS, SB_HEAD_DIM), g_k)
    v = sb_v.reshape(bsz, s, SB_HEADS, SB_HEAD_DIM)
    k_all = jnp.concatenate([k_past.astype(k.dtype), k], axis=1)
    v_all = jnp.concatenate([v_past.astype(v.dtype), v], axis=1)
    o_sb = stick_breaking_attention(q, k_all, v_all, past).reshape(bsz, s, SB_WIDTH)
    qk_c, conv_new = causal_conv(ml_qk, conv_past, w_conv, b_conv)
    mq, mk = jnp.split(jax.nn.silu(qk_c).astype(jnp.float32), 2, axis=-1)
    mq = mq.reshape(bsz, s, ML_HEADS, ML_HEAD_DIM)
    mk = mk.reshape(bsz, s, ML_HEADS, ML_HEAD_DIM) * (ML_HEAD_DIM ** -0.5)
    mv = ml_v.astype(jnp.float32).reshape(bsz, s, ML_HEADS, ML_HEAD_DIM)
    ig, fg = jnp.split(ml_gate.astype(jnp.float32) + b_gate.astype(jnp.float32), 2, axis=-1)
    h_ml, (c_new, n_new, m_new) = mlstm(mq, mk, mv, ig, jax.nn.log_sigmoid(fg),
                                        c0.astype(jnp.float32), n0.astype(jnp.float32), m0.astype(jnp.float32))
    h_ml = rmsnorm(h_ml, g_head).reshape(bsz, s, ML_WIDTH).astype(x.dtype)
    o_ml = jax.nn.sigmoid(ml_o) * h_ml
    mix = jnp.concatenate([o_sb, o_ml], axis=-1) @ w_out
    x = x + ga1 * mix
    h2 = rmsnorm(x, g_norm2) * (1.0 + sc2) + sh2
    ff = moe_ffn(h2.reshape(bsz * s, d), w_router, b_router, w_gu, b_gu, w_down, b_down).reshape(bsz, s, d)
    x = x + ga2 * ff
    return x, k, v, conv_new, c_new, n_new, m_new


def setup_inputs(seed: int = 0) -> dict:
    key = jax.random.key(seed)
    ks = jax.random.split(key, 32)
    f32 = jnp.float32

    def nrm(k, shape, scale):
        return jax.random.normal(k, shape, f32) * scale

    b_gate = jnp.concatenate([nrm(ks[14], (DEPTH, ML_HEADS), 0.1),
                              jnp.linspace(3.0, 6.0, ML_HEADS, dtype=f32)[None, :] + nrm(ks[15], (DEPTH, ML_HEADS), 0.1)],
                             axis=-1)
    return {
        "x_prompt": nrm(ks[0], (BATCH, SEQ, D_MODEL), 1.0),
        "x_sample": nrm(ks[1], (DEC_BATCH, DEC_SEQ, D_MODEL), 1.0),
        "c_prompt": nrm(ks[2], (BATCH, D_MODEL), 1.0),
        "c_sample": nrm(ks[3], (DEC_BATCH, D_MODEL), 1.0),
        "cache_sb_k": nrm(ks[4], (DEPTH, DEC_BATCH, PAST_LEN, SB_HEADS, SB_HEAD_DIM), 1.0),
        "cache_sb_v": nrm(ks[5], (DEPTH, DEC_BATCH, PAST_LEN, SB_HEADS, SB_HEAD_DIM), 1.0),
        "state_conv": nrm(ks[6], (DEPTH, DEC_BATCH, CONV_W - 1, 2 * ML_WIDTH), 1.0),
        "state_mlstm_c": nrm(ks[7], (DEPTH, DEC_BATCH, ML_HEADS, ML_HEAD_DIM, ML_HEAD_DIM), 0.1),
        "state_mlstm_n": nrm(ks[8], (DEPTH, DEC_BATCH, ML_HEADS, ML_HEAD_DIM), 0.1),
        "state_mlstm_m": nrm(ks[9], (DEPTH, DEC_BATCH, ML_HEADS), 1.0),
        "w_ada": nrm(ks[10], (DEPTH, D_MODEL, 6 * D_MODEL), 0.5 * D_MODEL ** -0.5),
        "b_ada": nrm(ks[11], (DEPTH, 6 * D_MODEL), 0.02),
        "g_norm1": 1.0 + nrm(ks[12], (DEPTH, D_MODEL), 0.05),
        "w_in": nrm(ks[13], (DEPTH, D_MODEL, IN_COLS), D_MODEL ** -0.5),
        "g_q": 1.0 + nrm(ks[16], (DEPTH, SB_HEAD_DIM), 0.05),
        "g_k": 1.0 + nrm(ks[17], (DEPTH, SB_HEAD_DIM), 0.05),
        "w_conv": nrm(ks[18], (DEPTH, CONV_W, 2 * ML_WIDTH), CONV_W ** -0.5),
        "b_conv": nrm(ks[19], (DEPTH, 2 * ML_WIDTH), 0.02),
        "b_gate": b_gate,
        "g_head": 1.0 + nrm(ks[20], (DEPTH, ML_HEAD_DIM), 0.05),
        "w_out": nrm(ks[21], (DEPTH, MIX_WIDTH, D_MODEL), MIX_WIDTH ** -0.5),
        "g_norm2": 1.0 + nrm(ks[22], (DEPTH, D_MODEL), 0.05),
        "w_router": nrm(ks[23], (DEPTH, D_MODEL, N_EXPERTS), D_MODEL ** -0.5),
        "b_router": nrm(ks[24], (DEPTH, N_EXPERTS), 0.01),
        "w_gu": nrm(ks[25], (DEPTH, N_EXPERTS, D_MODEL, 2 * D_FF), D_MODEL ** -0.5),
        "b_gu": nrm(ks[26], (DEPTH, N_EXPERTS, 2 * D_FF), 0.02),
        "w_down": nrm(ks[27], (DEPTH, N_EXPERTS, D_FF, D_MODEL), D_FF ** -0.5),
        "b_down": nrm(ks[28], (DEPTH, N_EXPERTS, D_MODEL), 0.02),
    }


def reference(x_prompt, x_sample, c_prompt, c_sample, cache_sb_k, cache_sb_v, state_conv, state_mlstm_c,
              state_mlstm_n, state_mlstm_m, w_ada, b_ada, g_norm1, w_in, g_q, g_k, w_conv, b_conv, b_gate,
              g_head, w_out, g_norm2, w_router, b_router, w_gu, b_gu, w_down, b_down):
    bsz = x_prompt.shape[0]
    dt = x_prompt.dtype
    kv0 = jnp.zeros((bsz, 0, SB_HEADS, SB_HEAD_DIM), dt)
    conv0 = jnp.zeros((bsz, CONV_W - 1, 2 * ML_WIDTH), dt)
    c0 = jnp.zeros((bsz, ML_HEADS, ML_HEAD_DIM, ML_HEAD_DIM), jnp.float32)
    n0 = jnp.zeros((bsz, ML_HEADS, ML_HEAD_DIM), jnp.float32)
    m0 = jnp.zeros((bsz, ML_HEADS), jnp.float32)
    yp, ys = x_prompt, x_sample
    sp = [[] for _ in range(6)]
    ss = [[] for _ in range(6)]
    for l in range(DEPTH):
        yp, *st_p = trunk_layer(yp, c_prompt, kv0, kv0, conv0, c0, n0, m0, w_ada[l], b_ada[l], g_norm1[l],
                                w_in[l], g_q[l], g_k[l], w_conv[l], b_conv[l], b_gate[l], g_head[l], w_out[l],
                                g_norm2[l], w_router[l], b_router[l], w_gu[l], b_gu[l], w_down[l], b_down[l])
        ys, *st_s = trunk_layer(ys, c_sample, cache_sb_k[l], cache_sb_v[l], state_conv[l], state_mlstm_c[l],
                                state_mlstm_n[l], state_mlstm_m[l], w_ada[l], b_ada[l], g_norm1[l],
                                w_in[l], g_q[l], g_k[l], w_conv[l], b_conv[l], b_gate[l], g_head[l], w_out[l],
                                g_norm2[l], w_router[l], b_router[l], w_gu[l], b_gu[l], w_down[l], b_down[l])
        for i in range(6):
            sp[i].append(st_p[i])
            ss[i].append(st_s[i])
    kp, vp, convp, cp, np_, mp = [jnp.stack(a, axis=0) for a in sp]
    ks_, vs, convs, cs, ns, ms = [jnp.stack(a, axis=0) for a in ss]
    return (yp, ys, kp, vp, convp, cp, np_, mp, ks_, vs, convs, cs, ns, ms)
```

```python
import functools
import math

import jax
import jax.numpy as jnp
from jax import lax
from jax.experimental import pallas as pl
from jax.experimental.pallas import tpu as pltpu

F32 = jnp.float32
BF16 = jnp.bfloat16
I32 = jnp.int32

EPS = 1e-6
LANES = 128
SB_HEAD_DIM = 64
ML_HEAD_DIM = 128
ML_HEADS = 4
CONV_W = 4
N_EXPERTS = 32
TOP_K = 4
SWIGLU_LIMIT = 7.0
SWIGLU_ALPHA = 1.702
VMEM_LIMIT = 56 * 1024 * 1024

ROW_TILE = 512
MOE_BLK = 512
GATHER_TILE = 256
ML_CHUNK = 256
SB_BLK = 128


def _cparams(sem):
    return pltpu.CompilerParams(dimension_semantics=sem, vmem_limit_bytes=VMEM_LIMIT)


def _split3(x):
    p1 = x.astype(BF16)
    r1 = x - p1.astype(F32)
    p2 = r1.astype(BF16)
    p3 = (r1 - p2.astype(F32)).astype(BF16)
    return p1, p2, p3


def _dot(a, b):
    return jnp.dot(a, b, preferred_element_type=F32)


def _dot_nt(a, b):
    return lax.dot_general(a, b, (((1,), (1,)), ((), ())), preferred_element_type=F32)


def _dot3(x, w_hi, w_lo):
    xh = x.astype(BF16)
    xl = (x - xh.astype(F32)).astype(BF16)
    return _dot(xh, w_hi) + _dot(xl, w_hi) + _dot(xh, w_lo)


def _log_sigmoid(x):
    return jnp.minimum(x, 0.0) - jnp.log1p(jnp.exp(-jnp.abs(x)))


def _ada_kernel(c_ref, w_ref, b_ref, o_ref):
    c = c_ref[...]
    s = c * jax.nn.sigmoid(c)
    o_ref[...] = jnp.dot(s, w_ref[...], precision=lax.Precision.HIGHEST,
                         preferred_element_type=F32) + b_ref[...]


def _ada(c, w_ada, b_ada):
    n, d = c.shape
    cols = w_ada.shape[1]
    tn = cols // 6
    return pl.pallas_call(
        _ada_kernel,
        grid=(cols // tn,),
        in_specs=[pl.BlockSpec((n, d), lambda j: (0, 0)),
                  pl.BlockSpec((d, tn), lambda j: (0, j)),
                  pl.BlockSpec((1, tn), lambda j: (0, j))],
        out_specs=pl.BlockSpec((n, tn), lambda j: (0, j)),
        out_shape=jax.ShapeDtypeStruct((n, cols), F32),
        compiler_params=_cparams(("arbitrary",)),
        name="ada_mod",
    )(c, w_ada, b_ada.reshape(1, cols))


def _inproj_kernel(x_ref, mod_ref, g1_ref, w_ref, wgh_ref, wgl_ref, bg_ref,
                   q_ref, k_ref, v_ref, qk_ref, mv_ref, mo_ref, gate_ref, *, widths):
    x = x_ref[0]
    mod = mod_ref[0]
    sh1 = mod[0:1]
    sc1 = mod[1:2]
    ms = jnp.mean(x * x, axis=-1, keepdims=True)
    h = x * lax.rsqrt(ms + EPS) * g1_ref[...]
    h = h * (1.0 + sc1) + sh1
    hb = h.astype(BF16)
    off = 0
    for ref, wd in zip((q_ref, k_ref, v_ref, qk_ref, mv_ref, mo_ref), widths):
        ref[0] = _dot(hb, w_ref[:, off:off + wd])
        off += wd
    gate_ref[0] = _dot3(h, wgh_ref[...], wgl_ref[...]) + bg_ref[...]


def _inproj(x, mod, g1, w_main, wg_hi, wg_lo, bg, widths):
    b, s, d = x.shape
    tm = min(ROW_TILE, s)
    ncol = w_main.shape[1]
    tok = lambda w: pl.BlockSpec((1, tm, w), lambda bi, i: (bi, i, 0))
    const = lambda shp: pl.BlockSpec(shp, lambda bi, i: (0,) * len(shp))
    return pl.pallas_call(
        functools.partial(_inproj_kernel, widths=widths),
        grid=(b, s // tm),
        in_specs=[tok(d),
                  pl.BlockSpec((1, 6, d), lambda bi, i: (bi, 0, 0)),
                  const((1, d)), const((d, ncol)), const((d, LANES)), const((d, LANES)), const((1, LANES))],
        out_specs=[tok(w) for w in widths] + [tok(LANES)],
        out_shape=[jax.ShapeDtypeStruct((b, s, w), F32) for w in widths]
        + [jax.ShapeDtypeStruct((b, s, LANES), F32)],
        compiler_params=_cparams(("arbitrary", "arbitrary")),
        name="inproj",
    )(x, mod, g1, w_main, wg_hi, wg_lo, bg)


def _sb_kernel(*refs, past, seq, blk):
    if past:
        q_ref, k_ref, v_ref, kp_ref, vp_ref, gq_ref, gk_ref, o_ref, kn_ref, ks_scr, vs_scr = refs
    else:
        q_ref, k_ref, v_ref, gq_ref, gk_ref, o_ref, kn_ref, ks_scr, vs_scr = refs
    i = pl.program_id(2)
    lane = lax.broadcasted_iota(I32, (1, LANES), 1)
    lo_half = lane < SB_HEAD_DIM

    def headnorm(x, g):
        x2 = x * x
        s0 = jnp.sum(jnp.where(lo_half, x2, 0.0), axis=-1, keepdims=True)
        s1 = jnp.sum(jnp.where(lo_half, 0.0, x2), axis=-1, keepdims=True)
        r = jnp.where(lo_half, lax.rsqrt(s0 * (1.0 / SB_HEAD_DIM) + EPS),
                      lax.rsqrt(s1 * (1.0 / SB_HEAD_DIM) + EPS))
        return x * r * g

    @pl.when(i == 0)
    def _():
        ch = min(256, seq)
        for c in range(seq // ch):
            rows = slice(c * ch, (c + 1) * ch)
            kn = headnorm(k_ref[0, rows, :], gk_ref[...])
            kn_ref[0, rows, :] = kn
            ks_scr[past + c * ch:past + (c + 1) * ch, :] = kn.astype(BF16)
            vs_scr[past + c * ch:past + (c + 1) * ch, :] = v_ref[0, rows, :].astype(BF16)
        if past:
            ks_scr[0:past, :] = kp_ref[0].astype(BF16)
            vs_scr[0:past, :] = vp_ref[0].astype(BF16)

    q = headnorm(q_ref[0], gq_ref[...]) * (1.0 / math.sqrt(SB_HEAD_DIM))
    qh = (jnp.where(lo_half, q, 0.0).astype(BF16), jnp.where(lo_half, 0.0, q).astype(BF16))

    rj = lax.broadcasted_iota(I32, (2 * blk, 2 * blk), 0)
    cj = lax.broadcasted_iota(I32, (2 * blk, 2 * blk), 1)
    rjm = jnp.where(rj >= blk, rj - blk, rj)
    w_cum = jnp.where(cj >= blk, 1.0, jnp.where(rjm > cj, 1.0, 0.0)).astype(BF16)
    causal = lax.broadcasted_iota(I32, (blk, blk), 1) < lax.broadcasted_iota(I32, (blk, blk), 0)

    def block(h, j, run, acc, diag):
        row0 = pl.multiple_of(j * blk, blk)
        kblk = ks_scr[pl.ds(row0, blk), :]
        vblk = vs_scr[pl.ds(row0, blk), :]
        z = _dot_nt(qh[h], kblk)
        sp = jnp.maximum(z, 0.0) + jnp.log1p(jnp.exp(-jnp.abs(z)))
        lsm = -sp
        if diag:
            lsm = jnp.where(causal, lsm, 0.0)
        hi = lsm.astype(BF16)
        lo = (lsm - hi.astype(F32)).astype(BF16)
        t = _dot(jnp.concatenate([hi, lo], axis=1), w_cum)
        p = jnp.exp((z - sp) + t[:, :blk] + run)
        if diag:
            p = jnp.where(causal, p, 0.0)
        acc = acc + _dot(p.astype(BF16), vblk)
        return run + t[:, blk:], acc

    jd = past // blk + i
    zero_run = jnp.zeros((blk, blk), F32)
    zero_acc = jnp.zeros((blk, LANES), F32)
    carry = block(0, jd, zero_run, zero_acc, True) + block(1, jd, zero_run, zero_acc, True)

    def body(t, c):
        j = jd - 1 - t
        return block(0, j, c[0], c[1], False) + block(1, j, c[2], c[3], False)

    c = lax.fori_loop(0, jd, body, carry)
    o_ref[0] = jnp.where(lo_half, c[1], c[3])


def _sb_attention(q_raw, k_raw, v, k_past, v_past, g_q, g_k):
    b, s, w = q_raw.shape
    past = 0 if k_past is None else k_past.shape[1]
    blk = min(SB_BLK, s)
    assert s % blk == 0 and past % blk == 0 and w % LANES == 0
    npair = w // LANES
    qblk = pl.BlockSpec((1, blk, LANES), lambda bi, hp, i: (bi, i, hp))
    full = lambda n: pl.BlockSpec((1, n, LANES), lambda bi, hp, i: (bi, 0, hp))
    gspec = pl.BlockSpec((1, LANES), lambda bi, hp, i: (0, 0))
    in_specs = [qblk, full(s), full(s)]
    args = [q_raw, k_raw, v]
    if past:
        in_specs += [full(past), full(past)]
        args += [k_past, v_past]
    in_specs += [gspec, gspec]
    args += [g_q, g_k]
    return pl.pallas_call(
        functools.partial(_sb_kernel, past=past, seq=s, blk=blk),
        grid=(b, npair, s // blk),
        in_specs=in_specs,
        out_specs=[qblk, full(s)],
        out_shape=[jax.ShapeDtypeStruct((b, s, w), F32), jax.ShapeDtypeStruct((b, s, w), F32)],
        scratch_shapes=[pltpu.VMEM((past + s, LANES), BF16), pltpu.VMEM((past + s, LANES), BF16)],
        compiler_params=_cparams(("arbitrary", "arbitrary", "arbitrary")),
        name="sb_attention",
    )(*args)


def _mlstm_kernel(qk_ref, v_ref, og_ref, g_ref, cp_ref, c0_ref, n0_ref, m0_ref, wc_ref, bc_ref, gh_ref,
                  out_ref, cout_ref, nout_ref, mout_ref, convout_ref,
                  prev_scr, c_scr, n_scr, m_scr, *, chunk):
    ci = pl.program_id(1)
    nh, dh = ML_HEADS, ML_HEAD_DIM
    width = nh * dh
    L = chunk

    @pl.when(ci == 0)
    def _():
        prev_scr[...] = jnp.zeros_like(prev_scr)
        prev_scr[8 - (CONV_W - 1):8, :] = cp_ref[0]
        c_scr[...] = c0_ref[0]
        n_scr[...] = n0_ref[0]
        m_scr[...] = m0_ref[0]

    u = qk_ref[0]
    xfull = jnp.concatenate([prev_scr[...], u], axis=0)
    wc = wc_ref[...]
    acc = bc_ref[...] + xfull[8 - (CONV_W - 1):8 - (CONV_W - 1) + L] * wc[0:1]
    for j in range(1, CONV_W):
        o = 8 - (CONV_W - 1) + j
        acc = acc + xfull[o:o + L] * wc[j:j + 1]
    qk = acc * jax.nn.sigmoid(acc)
    prev_scr[...] = u[L - 8:L, :]

    gt = g_ref[0]
    lf = _log_sigmoid(gt)
    gt_t = gt.T
    lf_t = lf.T
    row = lax.broadcasted_iota(I32, (L, L), 0)
    col = lax.broadcasted_iota(I32, (L, L), 1)
    tri = row >= col
    tril = jnp.where(tri, 1.0, 0.0).astype(BF16)
    triu = jnp.where(row <= col, 1.0, 0.0).astype(BF16)
    p1, p2, p3 = _split3(lf)
    b_col = _dot(tril, p1) + _dot(tril, p2) + _dot(tril, p3)
    q1, q2, q3 = _split3(lf_t[0:8, :])
    b_row = _dot(q1, triu) + _dot(q2, triu) + _dot(q3, triu)

    lane = lax.broadcasted_iota(I32, (1, LANES), 1)
    m_vec = m_scr[...]
    m_out = m_vec
    for h in range(nh):
        hs = slice(h * dh, (h + 1) * dh)
        qh = qk[:, h * dh:(h + 1) * dh]
        kh = qk[:, width + h * dh:width + (h + 1) * dh] * (dh ** -0.5)
        vh = v_ref[0, :, hs]
        b_c = b_col[:, nh + h:nh + h + 1]
        ig_c = gt[:, h:h + 1]
        b_r = b_row[nh + h:nh + h + 1, :]
        ig_r = gt_t[h:h + 1, :]
        m_prev = m_vec[:, h:h + 1]

        d = jnp.where(tri, (b_c - b_r) + ig_r, -jnp.inf)
        inter = b_c + m_prev
        m_t = jnp.maximum(inter, jnp.max(d, axis=1, keepdims=True))
        w_intra = jnp.exp(d - m_t)
        w_inter = jnp.exp(inter - m_t)
        qb = qh.astype(BF16)
        kb = kh.astype(BF16)
        s = _dot_nt(qb, kb) * w_intra
        c_h = c_scr[h]
        n_h = n_scr[h:h + 1, :]
        num = _dot(s.astype(BF16), vh.astype(BF16)) + w_inter * _dot_nt(qb, c_h.astype(BF16))
        den = jnp.sum(s, axis=1, keepdims=True) + w_inter * jnp.sum(qh * n_h, axis=1, keepdims=True)
        hh = num / jnp.maximum(jnp.abs(den), jnp.exp(-m_t))

        b_last = b_c[L - 1:L, :]
        g_c = (b_last - b_c) + ig_c
        m_new = jnp.maximum(b_last + m_prev, jnp.max(g_c, axis=0, keepdims=True))
        w_state = jnp.exp(g_c - m_new)
        decay = jnp.exp(b_last + m_prev - m_new)
        vw_t = (vh * w_state).T.astype(BF16)
        c_scr[h] = decay * c_h + _dot(vw_t, kb)
        n_scr[h:h + 1, :] = decay * n_h + jnp.sum(kh * w_state, axis=0, keepdims=True)
        m_out = jnp.where(lane == h, m_new, m_out)

        hn = hh * lax.rsqrt(jnp.mean(hh * hh, axis=-1, keepdims=True) + EPS) * gh_ref[...]
        out_ref[0, :, hs] = jax.nn.sigmoid(og_ref[0, :, hs]) * hn
    m_scr[...] = m_out

    @pl.when(ci == pl.num_programs(1) - 1)
    def _():
        cout_ref[0] = c_scr[...]
        nout_ref[0] = n_scr[...]
        mout_ref[0] = m_scr[...]
        convout_ref[0] = prev_scr[8 - (CONV_W - 1):8, :]


def _mlstm(ml_qk, ml_v, ml_o, gates, conv_past, c0, n0, m0, w_conv, b_conv, g_head):
    b, s, w2 = ml_qk.shape
    width = w2 // 2
    nh, dh = ML_HEADS, ML_HEAD_DIM
    chunk = min(ML_CHUNK, s)
    assert s % chunk == 0 and chunk >= 8
    tok = lambda w: pl.BlockSpec((1, chunk, w), lambda bi, ci: (bi, ci, 0))
    perb = lambda shp: pl.BlockSpec((1,) + shp, lambda bi, ci: (bi,) + (0,) * len(shp))
    const = lambda shp: pl.BlockSpec(shp, lambda bi, ci: (0,) * len(shp))
    return pl.pallas_call(
        functools.partial(_mlstm_kernel, chunk=chunk),
        grid=(b, s // chunk),
        in_specs=[tok(w2), tok(width), tok(width), tok(LANES),
                  perb((CONV_W - 1, w2)), perb((nh, dh, dh)), perb((nh, dh)), perb((1, LANES)),
                  const((CONV_W, w2)), const((1, w2)), const((1, dh))],
        out_specs=[tok(width), perb((nh, dh, dh)), perb((nh, dh)), perb((1, LANES)), perb((CONV_W - 1, w2))],
        out_shape=[jax.ShapeDtypeStruct((b, s, width), F32),
                   jax.ShapeDtypeStruct((b, nh, dh, dh), F32),
                   jax.ShapeDtypeStruct((b, nh, dh), F32),
                   jax.ShapeDtypeStruct((b, 1, LANES), F32),
                   jax.ShapeDtypeStruct((b, CONV_W - 1, w2), F32)],
        scratch_shapes=[pltpu.VMEM((8, w2), F32), pltpu.VMEM((nh, dh, dh), F32),
                        pltpu.VMEM((nh, dh), F32), pltpu.VMEM((1, LANES), F32)],
        compiler_params=_cparams(("arbitrary", "arbitrary")),
        name="mlstm",
    )(ml_qk, ml_v, ml_o, gates, conv_past, c0, n0, m0, w_conv, b_conv, g_head)


def _outproj_kernel(osb_ref, oml_ref, x_ref, mod_ref, w1_ref, w2_ref, g2_ref, wrh_ref, wrl_ref, br_ref,
                    x1_ref, h2_ref, ti_ref, tg_ref):
    mod = mod_ref[0]
    ga1 = mod[2:3]
    sh2 = mod[3:4]
    sc2 = mod[4:5]
    mix = _dot(osb_ref[0].astype(BF16), w1_ref[...]) + _dot(oml_ref[0].astype(BF16), w2_ref[...])
    x1 = x_ref[0] + ga1 * mix
    x1_ref[0] = x1
    ms = jnp.mean(x1 * x1, axis=-1, keepdims=True)
    h2 = x1 * lax.rsqrt(ms + EPS) * g2_ref[...]
    h2 = h2 * (1.0 + sc2) + sh2
    h2_ref[0] = h2
    logits = _dot3(h2, wrh_ref[...], wrl_ref[...]) + br_ref[...]
    lane = lax.broadcasted_iota(I32, logits.shape, 1)
    vals, idxs = [], []
    cur = logits
    for _ in range(TOP_K):
        m = jnp.max(cur, axis=-1, keepdims=True)
        idx = jnp.min(jnp.where(cur == m, lane, LANES), axis=-1, keepdims=True)
        vals.append(m)
        idxs.append(idx)
        cur = jnp.where(lane == idx, -jnp.inf, cur)
    es = [jnp.exp(v - vals[0]) for v in vals]
    tot = es[0] + es[1] + es[2] + es[3]
    ti = jnp.zeros(logits.shape, I32)
    tg = jnp.zeros(logits.shape, F32)
    for k in range(TOP_K):
        ti = jnp.where(lane == k, idxs[k], ti)
        tg = jnp.where(lane == k, es[k] / tot, tg)
    ti_ref[0] = ti
    tg_ref[0] = tg


def _outproj(o_sb, o_ml, x, mod, w1, w2, g2, wr_hi, wr_lo, br):
    b, s, d = x.shape
    hw = o_sb.shape[2]
    tm = min(ROW_TILE, s)
    tok = lambda w: pl.BlockSpec((1, tm, w), lambda bi, i: (bi, i, 0))
    const = lambda shp: pl.BlockSpec(shp, lambda bi, i: (0,) * len(shp))
    return pl.pallas_call(
        _outproj_kernel,
        grid=(b, s // tm),
        in_specs=[tok(hw), tok(hw), tok(d), pl.BlockSpec((1, 6, d), lambda bi, i: (bi, 0, 0)),
                  const((hw, d)), const((hw, d)), const((1, d)),
                  const((d, LANES)), const((d, LANES)), const((1, LANES))],
        out_specs=[tok(d), tok(d), tok(LANES), tok(LANES)],
        out_shape=[jax.ShapeDtypeStruct((b, s, d), F32), jax.ShapeDtypeStruct((b, s, d), F32),
                   jax.ShapeDtypeStruct((b, s, LANES), I32), jax.ShapeDtypeStruct((b, s, LANES), F32)],
        compiler_params=_cparams(("arbitrary", "arbitrary")),
        name="outproj_router",
    )(o_sb, o_ml, x, mod, w1, w2, g2, wr_hi, wr_lo, br)


def _rank_kernel(ti_ref, rank_ref, cnt_ref, carry_scr):
    i = pl.program_id(0)

    @pl.when(i == 0)
    def _():
        carry_scr[...] = jnp.zeros_like(carry_scr)

    ti = ti_ref[...]
    tm = ti.shape[0]
    lane = lax.broadcasted_iota(I32, ti.shape, 1)
    hit = jnp.zeros(ti.shape, F32)
    for k in range(TOP_K):
        hit = hit + jnp.where(lane == ti[:, k:k + 1], 1.0, 0.0)
    row = lax.broadcasted_iota(I32, (tm, tm), 0)
    col = lax.broadcasted_iota(I32, (tm, tm), 1)
    strict = jnp.where(col < row, 1.0, 0.0).astype(BF16)
    excl = _dot(strict, hit.astype(BF16)) + carry_scr[...]
    rank = jnp.zeros(ti.shape, F32)
    for k in range(TOP_K):
        rk = jnp.sum(jnp.where(lane == ti[:, k:k + 1], excl, 0.0), axis=-1, keepdims=True)
        rank = jnp.where(lane == k, rk, rank)
    rank_ref[...] = rank.astype(I32)
    carry_scr[...] = carry_scr[...] + jnp.sum(hit, axis=0, keepdims=True)
    cnt_ref[...] = carry_scr[...].astype(I32)


def _ranks(top_idx):
    t = top_idx.shape[0]
    tm = ROW_TILE
    assert t % tm == 0
    return pl.pallas_call(
        _rank_kernel,
        grid=(t // tm,),
        in_specs=[pl.BlockSpec((tm, LANES), lambda i: (i, 0))],
        out_specs=[pl.BlockSpec((tm, LANES), lambda i: (i, 0)), pl.BlockSpec((1, LANES), lambda i: (0, 0))],
        out_shape=[jax.ShapeDtypeStruct((t, LANES), I32), jax.ShapeDtypeStruct((1, LANES), I32)],
        scratch_shapes=[pltpu.VMEM((1, LANES), F32)],
        compiler_params=_cparams(("arbitrary",)),
        name="expert_ranks",
    )(top_idx)


def _dispatch_kernel(pos_hbm, h_ref, xs_in_hbm, xs_hbm, pos_smem, stage, pos_sem, row_sem, *, tm):
    del xs_in_hbm
    i = pl.program_id(0)
    n = pl.num_programs(0)
    slot = lax.rem(i, 2)
    nidx = tm * TOP_K

    def row_copy(sl, r, p):
        return pltpu.make_async_copy(stage.at[sl, pl.ds(r, 1), :], xs_hbm.at[pl.ds(p, 1), :], row_sem.at[sl])

    def drain(sl):
        def w(_, c):
            row_copy(sl, 0, 0).wait()
            return c
        lax.fori_loop(0, nidx, w, 0)

    cp = pltpu.make_async_copy(pos_hbm.at[pl.ds(pl.multiple_of(i * nidx, nidx), nidx)], pos_smem, pos_sem)
    cp.start()

    @pl.when(i >= 2)
    def _():
        drain(slot)

    stage[slot] = h_ref[...]
    cp.wait()

    def issue(r, c):
        for k in range(TOP_K):
            row_copy(slot, r, pos_smem[TOP_K * r + k]).start()
        return c
    lax.fori_loop(0, tm, issue, 0)

    @pl.when(i == n - 1)
    def _():
        drain(slot)

        @pl.when(n >= 2)
        def _():
            drain(1 - slot)


def _dispatch(pos_flat, h, xs):
    t, d = h.shape
    tm = GATHER_TILE
    assert t % tm == 0
    return pl.pallas_call(
        functools.partial(_dispatch_kernel, tm=tm),
        grid=(t // tm,),
        in_specs=[pl.BlockSpec(memory_space=pl.ANY),
                  pl.BlockSpec((tm, d), lambda i: (i, 0)),
                  pl.BlockSpec(memory_space=pl.ANY)],
        out_specs=pl.BlockSpec(memory_space=pl.ANY),
        out_shape=jax.ShapeDtypeStruct(xs.shape, xs.dtype),
        scratch_shapes=[pltpu.SMEM((tm * TOP_K,), I32), pltpu.VMEM((2, tm, d), F32),
                        pltpu.SemaphoreType.DMA, pltpu.SemaphoreType.DMA((2,))],
        input_output_aliases={2: 0},
        compiler_params=_cparams(("arbitrary",)),
        name="moe_dispatch",
    )(pos_flat, h, xs)


def _moe_kernel(be_ref, nu_ref, xs_ref, wgu_ref, bgu_ref, wd_ref, bd_ref, ys_ref):
    del be_ref
    i = pl.program_id(0)
    ff = wd_ref.shape[1]

    @pl.when(i < nu_ref[0])
    def _():
        x = xs_ref[...].astype(BF16)
        glu = _dot(x, wgu_ref[0, :, 0:ff]) + bgu_ref[0, :, 0:ff]
        lin = _dot(x, wgu_ref[0, :, ff:2 * ff]) + bgu_ref[0, :, ff:2 * ff]
        glu = jnp.minimum(glu, SWIGLU_LIMIT)
        lin = jnp.clip(lin, -SWIGLU_LIMIT, SWIGLU_LIMIT)
        act = glu * jax.nn.sigmoid(SWIGLU_ALPHA * glu) * (lin + 1.0)
        ys_ref[...] = _dot(act.astype(BF16), wd_ref[0]) + bd_ref[0]

    @pl.when(i >= nu_ref[0])
    def _():
        ys_ref[...] = jnp.zeros_like(ys_ref)


def _moe_mlp(block_expert, n_used, xs, w_gu, b_gu, w_down, b_down):
    n_rows, d = xs.shape
    ne, _, ff2 = w_gu.shape
    ff = ff2 // 2
    tb = MOE_BLK
    blk = lambda i, be, nu: jnp.minimum(i, nu[0] - 1)
    grid_spec = pltpu.PrefetchScalarGridSpec(
        num_scalar_prefetch=2,
        grid=(n_rows // tb,),
        in_specs=[pl.BlockSpec((tb, d), lambda i, be, nu: (blk(i, be, nu), 0)),
                  pl.BlockSpec((1, d, ff2), lambda i, be, nu: (be[blk(i, be, nu)], 0, 0)),
                  pl.BlockSpec((1, 1, ff2), lambda i, be, nu: (be[blk(i, be, nu)], 0, 0)),
                  pl.BlockSpec((1, ff, d), lambda i, be, nu: (be[blk(i, be, nu)], 0, 0)),
                  pl.BlockSpec((1, 1, d), lambda i, be, nu: (be[blk(i, be, nu)], 0, 0))],
        out_specs=pl.BlockSpec((tb, d), lambda i, be, nu: (i, 0)),
    )
    return pl.pallas_call(
        _moe_kernel,
        grid_spec=grid_spec,
        out_shape=jax.ShapeDtypeStruct((n_rows, d), F32),
        compiler_params=_cparams(("arbitrary",)),
        name="moe_mlp",
    )(block_expert, n_used, xs, w_gu, b_gu.reshape(ne, 1, ff2), w_down, b_down.reshape(ne, 1, d))


def _combine_kernel(pos_hbm, ys_hbm, x1_ref, tg_ref, mod_ref, y_ref, pos_smem, buf, pos_sem, row_sem, *, tm, nb):
    i = pl.program_id(0)
    nidx = tm * TOP_K

    def row_copy(k, r, p):
        return pltpu.make_async_copy(ys_hbm.at[pl.ds(p, 1), :], buf.at[k, pl.ds(r, 1), :], row_sem)

    cp = pltpu.make_async_copy(pos_hbm.at[pl.ds(pl.multiple_of(i * nidx, nidx), nidx)], pos_smem, pos_sem)
    cp.start()
    cp.wait()

    def issue(r, c):
        for k in range(TOP_K):
            row_copy(k, r, pos_smem[TOP_K * r + k]).start()
        return c
    lax.fori_loop(0, tm, issue, 0)

    def w(_, c):
        row_copy(0, 0, 0).wait()
        return c
    lax.fori_loop(0, nidx, w, 0)

    tg = tg_ref[...]
    ff = tg[:, 0:1] * buf[0]
    for k in range(1, TOP_K):
        ff = ff + tg[:, k:k + 1] * buf[k]
    rows = tm // nb
    for g in range(nb):
        ga2 = mod_ref[g, 5:6, :]
        rs = slice(g * rows, (g + 1) * rows)
        y_ref[rs, :] = x1_ref[rs, :] + ga2 * ff[rs, :]


def _combine(pos_flat, ys, x1, top_gate, mod, tokens_per_batch):
    t, d = x1.shape
    tm = GATHER_TILE
    assert t % tm == 0
    if tokens_per_batch >= tm:
        assert tokens_per_batch % tm == 0
        nb = 1
        per = tokens_per_batch // tm
        mod_spec = pl.BlockSpec((1, 6, d), lambda i: (i // per, 0, 0))
    else:
        assert tm % tokens_per_batch == 0
        nb = tm // tokens_per_batch
        mod_spec = pl.BlockSpec((nb, 6, d), lambda i: (i, 0, 0))
    return pl.pallas_call(
        functools.partial(_combine_kernel, tm=tm, nb=nb),
        grid=(t // tm,),
        in_specs=[pl.BlockSpec(memory_space=pl.ANY), pl.BlockSpec(memory_space=pl.ANY),
                  pl.BlockSpec((tm, d), lambda i: (i, 0)), pl.BlockSpec((tm, LANES), lambda i: (i, 0)), mod_spec],
        out_specs=pl.BlockSpec((tm, d), lambda i: (i, 0)),
        out_shape=jax.ShapeDtypeStruct((t, d), F32),
        scratch_shapes=[pltpu.SMEM((tm * TOP_K,), I32), pltpu.VMEM((TOP_K, tm, d), F32),
                        pltpu.SemaphoreType.DMA, pltpu.SemaphoreType.DMA],
        compiler_params=_cparams(("arbitrary",)),
        name="moe_combine",
    )(pos_flat, ys, x1, top_gate, mod)


def _pad_lanes(a, fill=0.0):
    return jnp.pad(a, [(0, 0)] * (a.ndim - 1) + [(0, LANES - a.shape[-1])], constant_values=fill)


def _hi_lo(w):
    hi = w.astype(BF16)
    return hi, (w - hi.astype(F32)).astype(BF16)


def _mixer(x, mod, k_past, v_past, conv_past, c0, n0, m0, p):
    b, s, d = x.shape
    q_raw, k_raw, v, ml_qk, ml_v, ml_o, gates = _inproj(
        x, mod, p["g1"], p["w_main"], p["wg_hi"], p["wg_lo"], p["bg"], p["widths"])
    o_sb, k_norm = _sb_attention(q_raw, k_raw, v, k_past, v_past, p["g_q"], p["g_k"])
    o_ml, c_new, n_new, m_new, conv_new = _mlstm(
        ml_qk, ml_v, ml_o, gates, conv_past, c0, n0, _pad_lanes(m0)[:, None, :],
        p["w_conv"], p["b_conv"], p["g_head"])
    x1, h2, ti, tg = _outproj(o_sb, o_ml, x, mod, p["w_out1"], p["w_out2"], p["g2"],
                              p["wr_hi"], p["wr_lo"], p["br"])
    nsb = k_norm.shape[-1] // SB_HEAD_DIM
    state = (k_norm.reshape(b, s, nsb, SB_HEAD_DIM), v.reshape(b, s, nsb, SB_HEAD_DIM), conv_new,
             c_new, n_new, m_new[:, 0, :ML_HEADS])
    return x1.reshape(b * s, d), h2.reshape(b * s, d), ti.reshape(b * s, LANES), tg.reshape(b * s, LANES), state


def _layer(xp, xs_, cp, cs, kc, vc, convc, cc, nc, mc, w_ada, b_ada, g_norm1, w_in, g_q, g_k, w_conv, b_conv,
           b_gate, g_head, w_out, g_norm2, w_router, b_router, w_gu, b_gu, w_down, b_down):
    d = xp.shape[-1]
    bp, sp = xp.shape[:2]
    bs, ss = xs_.shape[:2]
    sbw = kc.shape[-2] * kc.shape[-1]
    mlw = ML_HEADS * ML_HEAD_DIM
    ngate = 2 * ML_HEADS
    gate0 = 3 * sbw + 3 * mlw
    w_main = jnp.concatenate([w_in[:, :gate0], w_in[:, gate0 + ngate:]], axis=1).astype(BF16)
    wg_hi, wg_lo = _hi_lo(_pad_lanes(w_in[:, gate0:gate0 + ngate]))
    wr_hi, wr_lo = _hi_lo(_pad_lanes(w_router))
    p = dict(
        widths=(sbw, sbw, sbw, 2 * mlw, mlw, mlw),
        g1=g_norm1[None], w_main=w_main, wg_hi=wg_hi, wg_lo=wg_lo, bg=_pad_lanes(b_gate)[None],
        g_q=jnp.tile(g_q, LANES // SB_HEAD_DIM)[None], g_k=jnp.tile(g_k, LANES // SB_HEAD_DIM)[None],
        w_conv=w_conv, b_conv=b_conv[None], g_head=g_head[None],
        w_out1=w_out[:sbw].astype(BF16), w_out2=w_out[sbw:].astype(BF16), g2=g_norm2[None],
        wr_hi=wr_hi, wr_lo=wr_lo, br=_pad_lanes(b_router, -1e30)[None],
    )
    mod = _ada(jnp.concatenate([cp, cs], axis=0), w_ada, b_ada)
    mod_p = mod[:bp].reshape(bp, 6, d)
    mod_s = mod[bp:].reshape(bs, 6, d)

    zeros = lambda *shp: jnp.zeros(shp, F32)
    x1p, h2p, tip, tgp, st_p = _mixer(
        xp, mod_p, None, None, zeros(bp, CONV_W - 1, 2 * mlw), zeros(bp, ML_HEADS, ML_HEAD_DIM, ML_HEAD_DIM),
        zeros(bp, ML_HEADS, ML_HEAD_DIM), zeros(bp, ML_HEADS), p)
    x1s, h2s, tis, tgs, st_s = _mixer(
        xs_, mod_s, kc.reshape(bs, -1, sbw), vc.reshape(bs, -1, sbw), convc, cc, nc, mc, p)

    tp, ts = bp * sp, bs * ss
    ti_all = jnp.concatenate([tip, tis], axis=0)
    rank, counts = _ranks(ti_all)
    counts = counts[0, :N_EXPERTS]
    padded = (counts + MOE_BLK - 1) // MOE_BLK * MOE_BLK
    pad_end = jnp.cumsum(padded)
    pad_start = pad_end - padded
    top_idx = ti_all[:, :TOP_K]
    pos = (pad_start[top_idx] + rank[:, :TOP_K]).astype(I32)
    n_blocks = -(-((tp + ts) * TOP_K + N_EXPERTS * (MOE_BLK - 1)) // MOE_BLK)
    n_used = (pad_end[-1] // MOE_BLK).astype(I32).reshape(1)
    block_expert = jnp.minimum(
        jnp.searchsorted(pad_end, jnp.arange(n_blocks, dtype=I32) * MOE_BLK, side="right"),
        N_EXPERTS - 1).astype(I32)
    pos_p = pos[:tp].reshape(-1)
    pos_s = pos[tp:].reshape(-1)

    xs_buf = jnp.zeros((n_blocks * MOE_BLK, d), F32)
    xs_buf = _dispatch(pos_p, h2p, xs_buf)
    xs_buf = _dispatch(pos_s, h2s, xs_buf)
    ys = _moe_mlp(block_expert, n_used, xs_buf, w_gu.astype(BF16), b_gu, w_down.astype(BF16), b_down)
    yp = _combine(pos_p, ys, x1p, tgp, mod_p, sp).reshape(bp, sp, d)
    ysm = _combine(pos_s, ys, x1s, tgs, mod_s, ss).reshape(bs, ss, d)
    return yp, ysm, st_p, st_s


def kernel(x_prompt, x_sample, c_prompt, c_sample, cache_sb_k, cache_sb_v, state_conv, state_mlstm_c, state_mlstm_n, state_mlstm_m, w_ada, b_ada, g_norm1, w_in, g_q, g_k, w_conv, b_conv, b_gate, g_head, w_out, g_norm2, w_router, b_router, w_gu, b_gu, w_down, b_down):
    assert w_ada.shape[0] == 1, "single-layer step"
    yp, ys, st_p, st_s = _layer(
        x_prompt, x_sample, c_prompt, c_sample, cache_sb_k[0], cache_sb_v[0], state_conv[0], state_mlstm_c[0],
        state_mlstm_n[0], state_mlstm_m[0], w_ada[0], b_ada[0], g_norm1[0], w_in[0], g_q[0], g_k[0], w_conv[0],
        b_conv[0], b_gate[0], g_head[0], w_out[0], g_norm2[0], w_router[0], b_router[0], w_gu[0], b_gu[0],
        w_down[0], b_down[0])
    return (yp, ys) + tuple(a[None] for a in st_p) + tuple(a[None] for a in st_s)
```

```python
import functools
import math

import jax
import jax.numpy as jnp
from jax import lax
from jax.experimental import pallas as pl
from jax.experimental.pallas import tpu as pltpu

F32 = jnp.float32
BF16 = jnp.bfloat16
I32 = jnp.int32

EPS = 1e-6
LANES = 128
SB_HEAD_DIM = 64
ML_HEAD_DIM = 128
ML_HEADS = 4
CONV_W = 4
N_EXPERTS = 32
TOP_K = 4
SWIGLU_LIMIT = 7.0
SWIGLU_ALPHA = 1.702
VMEM_LIMIT = 56 * 1024 * 1024

ROW_TILE = 512
MOE_BLK = 512
GATHER_TILE = 256
ML_CHUNK = 256
SB_BLK = 128
SB_UNDERFLOW = -105.0


def _cparams(sem):
    return pltpu.CompilerParams(dimension_semantics=sem, vmem_limit_bytes=VMEM_LIMIT)


def _split3(x):
    p1 = x.astype(BF16)
    r1 = x - p1.astype(F32)
    p2 = r1.astype(BF16)
    p3 = (r1 - p2.astype(F32)).astype(BF16)
    return p1, p2, p3


def _dot(a, b):
    return jnp.dot(a, b, preferred_element_type=F32)


def _dot_nt(a, b):
    return lax.dot_general(a, b, (((1,), (1,)), ((), ())), preferred_element_type=F32)


def _dot3(x, w_hi, w_lo):
    xh = x.astype(BF16)
    xl = (x - xh.astype(F32)).astype(BF16)
    return _dot(xh, w_hi) + _dot(xl, w_hi) + _dot(xh, w_lo)


def _log_sigmoid(x):
    return jnp.minimum(x, 0.0) - jnp.log1p(jnp.exp(-jnp.abs(x)))


def _ada_kernel(c_ref, w_ref, b_ref, o_ref):
    c = c_ref[...]
    s = c * jax.nn.sigmoid(c)
    o_ref[...] = jnp.dot(s, w_ref[...], precision=lax.Precision.HIGHEST,
                         preferred_element_type=F32) + b_ref[...]


def _ada(c, w_ada, b_ada):
    n, d = c.shape
    cols = w_ada.shape[1]
    tn = cols // 6
    return pl.pallas_call(
        _ada_kernel,
        grid=(cols // tn,),
        in_specs=[pl.BlockSpec((n, d), lambda j: (0, 0)),
                  pl.BlockSpec((d, tn), lambda j: (0, j)),
                  pl.BlockSpec((1, tn), lambda j: (0, j))],
        out_specs=pl.BlockSpec((n, tn), lambda j: (0, j)),
        out_shape=jax.ShapeDtypeStruct((n, cols), F32),
        compiler_params=_cparams(("arbitrary",)),
        name="ada_mod",
    )(c, w_ada, b_ada.reshape(1, cols))


def _inproj_kernel(x_ref, mod_ref, g1_ref, w_ref, wgh_ref, wgl_ref, bg_ref,
                   q_ref, k_ref, v_ref, qk_ref, mv_ref, mo_ref, gate_ref, *, widths):
    x = x_ref[0]
    mod = mod_ref[0]
    sh1 = mod[0:1]
    sc1 = mod[1:2]
    ms = jnp.mean(x * x, axis=-1, keepdims=True)
    h = x * lax.rsqrt(ms + EPS) * g1_ref[...]
    h = h * (1.0 + sc1) + sh1
    hb = h.astype(BF16)
    off = 0
    for ref, wd in zip((q_ref, k_ref, v_ref, qk_ref, mv_ref, mo_ref), widths):
        ref[0] = _dot(hb, w_ref[:, off:off + wd])
        off += wd
    gate_ref[0] = _dot3(h, wgh_ref[...], wgl_ref[...]) + bg_ref[...]


def _inproj(x, mod, g1, w_main, wg_hi, wg_lo, bg, widths):
    b, s, d = x.shape
    tm = min(ROW_TILE, s)
    ncol = w_main.shape[1]
    tok = lambda w: pl.BlockSpec((1, tm, w), lambda bi, i: (bi, i, 0))
    const = lambda shp: pl.BlockSpec(shp, lambda bi, i: (0,) * len(shp))
    return pl.pallas_call(
        functools.partial(_inproj_kernel, widths=widths),
        grid=(b, s // tm),
        in_specs=[tok(d),
                  pl.BlockSpec((1, 6, d), lambda bi, i: (bi, 0, 0)),
                  const((1, d)), const((d, ncol)), const((d, LANES)), const((d, LANES)), const((1, LANES))],
        out_specs=[tok(w) for w in widths] + [tok(LANES)],
        out_shape=[jax.ShapeDtypeStruct((b, s, w), F32) for w in widths]
        + [jax.ShapeDtypeStruct((b, s, LANES), F32)],
        compiler_params=_cparams(("arbitrary", "arbitrary")),
        name="inproj",
    )(x, mod, g1, w_main, wg_hi, wg_lo, bg)


def _sb_kernel(*refs, past, seq, tq, kb, npair):
    if past:
        q_ref, k_ref, v_ref, kp_ref, vp_ref, gq_ref, gk_ref, o_ref, kn_ref, ks_scr, vs_scr, run_scr, acc_scr = refs
    else:
        q_ref, k_ref, v_ref, gq_ref, gk_ref, o_ref, kn_ref, ks_scr, vs_scr, run_scr, acc_scr = refs
    i = pl.program_id(1)
    lane = lax.broadcasted_iota(I32, (1, LANES), 1)
    lo_half = lane < SB_HEAD_DIM
    pairs = [slice(hp * LANES, (hp + 1) * LANES) for hp in range(npair)]

    def headnorm(x, g):
        x2 = x * x
        s0 = jnp.sum(jnp.where(lo_half, x2, 0.0), axis=-1, keepdims=True)
        s1 = jnp.sum(jnp.where(lo_half, 0.0, x2), axis=-1, keepdims=True)
        r = jnp.where(lo_half, lax.rsqrt(s0 * (1.0 / SB_HEAD_DIM) + EPS),
                      lax.rsqrt(s1 * (1.0 / SB_HEAD_DIM) + EPS))
        return x * r * g

    @pl.when(i == 0)
    def _():
        ch = min(256, seq)
        for c in range(seq // ch):
            rows = slice(c * ch, (c + 1) * ch)
            dst = slice(past + c * ch, past + (c + 1) * ch)
            for ps in pairs:
                kn = headnorm(k_ref[0, rows, ps], gk_ref[...])
                kn_ref[0, rows, ps] = kn
                ks_scr[dst, ps] = kn.astype(BF16)
            vs_scr[dst, :] = v_ref[0, rows, :].astype(BF16)
        if past:
            ks_scr[0:past, :] = kp_ref[0].astype(BF16)
            vs_scr[0:past, :] = vp_ref[0].astype(BF16)

    qm = []
    for ps in pairs:
        q = headnorm(q_ref[0, :, ps], gq_ref[...]) * (1.0 / math.sqrt(SB_HEAD_DIM))
        qm.append(jnp.concatenate([jnp.where(lo_half, q, 0.0), jnp.where(lo_half, 0.0, q)], axis=0).astype(BF16))
    nrow = 2 * npair * tq

    def neg_cum(nk):
        rj = lax.broadcasted_iota(I32, (2 * nk, LANES + nk), 0)
        cj = lax.broadcasted_iota(I32, (2 * nk, LANES + nk), 1)
        rjm = jnp.where(rj >= nk, rj - nk, rj)
        return jnp.where(cj < LANES, -1.0, jnp.where(rjm >= cj - LANES, -1.0, 0.0)).astype(BF16)

    w_diag = neg_cum(tq)
    w_full = w_diag if kb == tq else neg_cum(kb)
    causal = (lax.broadcasted_iota(I32, (nrow, tq), 1)
              < jnp.bitwise_and(lax.broadcasted_iota(I32, (nrow, tq), 0), tq - 1))

    def sweep(row0, nk, w_neg, diag):
        kv = [(ks_scr[pl.ds(row0, nk), ps], vs_scr[pl.ds(row0, nk), ps]) for ps in pairs]
        z = jnp.concatenate([_dot_nt(qm[hp], kv[hp][0]) for hp in range(npair)], axis=0)
        sp = jnp.maximum(z, 0.0) + jnp.log(1.0 + jnp.exp(-jnp.abs(z)))
        if diag:
            sp = jnp.where(causal, sp, 0.0)
        hi = sp.astype(BF16)
        lo = (sp - hi.astype(F32)).astype(BF16)
        t = _dot(jnp.concatenate([hi, lo], axis=1), w_neg)
        if diag:
            arg = z + t[:, LANES:]
            run = t[:, :LANES]
        else:
            old = run_scr[...]
            arg = z + t[:, LANES:] + old[:, :nk]
            run = old + t[:, :LANES]
        run_scr[...] = run
        p = jnp.exp(arg)
        if diag:
            p = jnp.where(causal, p, 0.0)
        p = p.astype(BF16)
        for hp in range(npair):
            r0 = 2 * hp * tq
            vblk = kv[hp][1]
            vm = jnp.concatenate([jnp.where(lo_half, vblk, 0.0), jnp.where(lo_half, 0.0, vblk)], axis=0)
            pv = _dot(jnp.concatenate([p[r0:r0 + tq], p[r0 + tq:r0 + 2 * tq]], axis=1), vm.astype(BF16))
            acc_scr[hp] = pv if diag else acc_scr[hp] + pv
        return jnp.max(run)

    q0 = past + i * tq
    worst0 = sweep(pl.multiple_of(q0, tq), tq, w_diag, True)
    n_before = q0 // kb

    def cond(c):
        return jnp.logical_and(c[0] < n_before, c[1] > SB_UNDERFLOW)

    def body(c):
        j = n_before - 1 - c[0]
        return c[0] + 1, sweep(pl.multiple_of(j * kb, kb), kb, w_full, False)

    lax.while_loop(cond, body, (jnp.int32(0), worst0))
    for hp, ps in enumerate(pairs):
        o_ref[0, :, ps] = acc_scr[hp]


def _sb_attention(q_raw, k_raw, v, k_past, v_past, g_q, g_k):
    b, s, w = q_raw.shape
    past = 0 if k_past is None else k_past.shape[1]
    tq = min(SB_BLK, s)
    kb = SB_BLK
    assert s % tq == 0 and past % kb == 0 and w % LANES == 0 and (tq == kb or s == tq)
    npair = w // LANES
    qblk = pl.BlockSpec((1, tq, w), lambda bi, i: (bi, i, 0))
    full = lambda n: pl.BlockSpec((1, n, w), lambda bi, i: (bi, 0, 0))
    gspec = pl.BlockSpec((1, LANES), lambda bi, i: (0, 0))
    in_specs = [qblk, full(s), full(s)]
    args = [q_raw, k_raw, v]
    if past:
        in_specs += [full(past), full(past)]
        args += [k_past, v_past]
    in_specs += [gspec, gspec]
    args += [g_q, g_k]
    return pl.pallas_call(
        functools.partial(_sb_kernel, past=past, seq=s, tq=tq, kb=kb, npair=npair),
        grid=(b, s // tq),
        in_specs=in_specs,
        out_specs=[qblk, full(s)],
        out_shape=[jax.ShapeDtypeStruct((b, s, w), F32), jax.ShapeDtypeStruct((b, s, w), F32)],
        scratch_shapes=[pltpu.VMEM((past + s, w), BF16), pltpu.VMEM((past + s, w), BF16),
                        pltpu.VMEM((2 * npair * tq, LANES), F32), pltpu.VMEM((npair, tq, LANES), F32)],
        compiler_params=_cparams(("arbitrary", "arbitrary")),
        name="sb_attention",
    )(*args)


def _mlstm_kernel(qk_ref, v_ref, og_ref, g_ref, cp_ref, c0_ref, n0_ref, m0_ref, wc_ref, bc_ref, gh_ref,
                  out_ref, cout_ref, nout_ref, mout_ref, convout_ref,
                  prev_scr, c_scr, n_scr, m_scr, *, chunk):
    ci = pl.program_id(1)
    nh, dh = ML_HEADS, ML_HEAD_DIM
    width = nh * dh
    L = chunk

    @pl.when(ci == 0)
    def _():
        prev_scr[...] = jnp.zeros_like(prev_scr)
        prev_scr[8 - (CONV_W - 1):8, :] = cp_ref[0]
        c_scr[...] = c0_ref[0]
        n_scr[...] = n0_ref[0]
        m_scr[...] = m0_ref[0]

    u = qk_ref[0]
    xfull = jnp.concatenate([prev_scr[...], u], axis=0)
    wc = wc_ref[...]
    acc = bc_ref[...] + xfull[8 - (CONV_W - 1):8 - (CONV_W - 1) + L] * wc[0:1]
    for j in range(1, CONV_W):
        o = 8 - (CONV_W - 1) + j
        acc = acc + xfull[o:o + L] * wc[j:j + 1]
    qk = acc * jax.nn.sigmoid(acc)
    prev_scr[...] = u[L - 8:L, :]

    gt = g_ref[0]
    lf = _log_sigmoid(gt)
    gt_t = gt.T
    lf_t = lf.T
    row = lax.broadcasted_iota(I32, (L, L), 0)
    col = lax.broadcasted_iota(I32, (L, L), 1)
    tri = row >= col
    tril = jnp.where(tri, 1.0, 0.0).astype(BF16)
    triu = jnp.where(row <= col, 1.0, 0.0).astype(BF16)
    p1, p2, p3 = _split3(lf)
    b_col = _dot(tril, p1) + _dot(tril, p2) + _dot(tril, p3)
    q1, q2, q3 = _split3(lf_t[0:8, :])
    b_row = _dot(q1, triu) + _dot(q2, triu) + _dot(q3, triu)

    lane = lax.broadcasted_iota(I32, (1, LANES), 1)
    m_vec = m_scr[...]
    m_out = m_vec
    for h in range(nh):
        hs = slice(h * dh, (h + 1) * dh)
        qh = qk[:, h * dh:(h + 1) * dh]
        kh = qk[:, width + h * dh:width + (h + 1) * dh] * (dh ** -0.5)
        vh = v_ref[0, :, hs]
        b_c = b_col[:, nh + h:nh + h + 1]
        ig_c = gt[:, h:h + 1]
        b_r = b_row[nh + h:nh + h + 1, :]
        ig_r = gt_t[h:h + 1, :]
        m_prev = m_vec[:, h:h + 1]

        d = jnp.where(tri, (b_c - b_r) + ig_r, -jnp.inf)
        inter = b_c + m_prev
        m_t = jnp.maximum(inter, jnp.max(d, axis=1, keepdims=True))
        w_intra = jnp.exp(d - m_t)
        w_inter = jnp.exp(inter - m_t)
        qb = qh.astype(BF16)
        kb = kh.astype(BF16)
        s = _dot_nt(qb, kb) * w_intra
        c_h = c_scr[h]
        n_h = n_scr[h:h + 1, :]
        num = _dot(s.astype(BF16), vh.astype(BF16)) + w_inter * _dot_nt(qb, c_h.astype(BF16))
        den = jnp.sum(s, axis=1, keepdims=True) + w_inter * jnp.sum(qh * n_h, axis=1, keepdims=True)
        hh = num / jnp.maximum(jnp.abs(den), jnp.exp(-m_t))

        b_last = b_c[L - 1:L, :]
        g_c = (b_last - b_c) + ig_c
        m_new = jnp.maximum(b_last + m_prev, jnp.max(g_c, axis=0, keepdims=True))
        w_state = jnp.exp(g_c - m_new)
        decay = jnp.exp(b_last + m_prev - m_new)
        vw_t = (vh * w_state).T.astype(BF16)
        c_scr[h] = decay * c_h + _dot(vw_t, kb)
        n_scr[h:h + 1, :] = decay * n_h + jnp.sum(kh * w_state, axis=0, keepdims=True)
        m_out = jnp.where(lane == h, m_new, m_out)

        hn = hh * lax.rsqrt(jnp.mean(hh * hh, axis=-1, keepdims=True) + EPS) * gh_ref[...]
        out_ref[0, :, hs] = jax.nn.sigmoid(og_ref[0, :, hs]) * hn
    m_scr[...] = m_out

    @pl.when(ci == pl.num_programs(1) - 1)
    def _():
        cout_ref[0] = c_scr[...]
        nout_ref[0] = n_scr[...]
        mout_ref[0] = m_scr[...]
        convout_ref[0] = prev_scr[8 - (CONV_W - 1):8, :]


def _mlstm(ml_qk, ml_v, ml_o, gates, conv_past, c0, n0, m0, w_conv, b_conv, g_head):
    b, s, w2 = ml_qk.shape
    width = w2 // 2
    nh, dh = ML_HEADS, ML_HEAD_DIM
    chunk = min(ML_CHUNK, s)
    assert s % chunk == 0 and chunk >= 8
    tok = lambda w: pl.BlockSpec((1, chunk, w), lambda bi, ci: (bi, ci, 0))
    perb = lambda shp: pl.BlockSpec((1,) + shp, lambda bi, ci: (bi,) + (0,) * len(shp))
    const = lambda shp: pl.BlockSpec(shp, lambda bi, ci: (0,) * len(shp))
    return pl.pallas_call(
        functools.partial(_mlstm_kernel, chunk=chunk),
        grid=(b, s // chunk),
        in_specs=[tok(w2), tok(width), tok(width), tok(LANES),
                  perb((CONV_W - 1, w2)), perb((nh, dh, dh)), perb((nh, dh)), perb((1, LANES)),
                  const((CONV_W, w2)), const((1, w2)), const((1, dh))],
        out_specs=[tok(width), perb((nh, dh, dh)), perb((nh, dh)), perb((1, LANES)), perb((CONV_W - 1, w2))],
        out_shape=[jax.ShapeDtypeStruct((b, s, width), F32),
                   jax.ShapeDtypeStruct((b, nh, dh, dh), F32),
                   jax.ShapeDtypeStruct((b, nh, dh), F32),
                   jax.ShapeDtypeStruct((b, 1, LANES), F32),
                   jax.ShapeDtypeStruct((b, CONV_W - 1, w2), F32)],
        scratch_shapes=[pltpu.VMEM((8, w2), F32), pltpu.VMEM((nh, dh, dh), F32),
                        pltpu.VMEM((nh, dh), F32), pltpu.VMEM((1, LANES), F32)],
        compiler_params=_cparams(("arbitrary", "arbitrary")),
        name="mlstm",
    )(ml_qk, ml_v, ml_o, gates, conv_past, c0, n0, m0, w_conv, b_conv, g_head)


def _outproj_kernel(osb_ref, oml_ref, x_ref, mod_ref, w1_ref, w2_ref, g2_ref, wrh_ref, wrl_ref, br_ref,
                    x1_ref, h2_ref, ti_ref, tg_ref):
    mod = mod_ref[0]
    ga1 = mod[2:3]
    sh2 = mod[3:4]
    sc2 = mod[4:5]
    mix = _dot(osb_ref[0].astype(BF16), w1_ref[...]) + _dot(oml_ref[0].astype(BF16), w2_ref[...])
    x1 = x_ref[0] + ga1 * mix
    x1_ref[0] = x1
    ms = jnp.mean(x1 * x1, axis=-1, keepdims=True)
    h2 = x1 * lax.rsqrt(ms + EPS) * g2_ref[...]
    h2 = h2 * (1.0 + sc2) + sh2
    h2_ref[0] = h2
    logits = _dot3(h2, wrh_ref[...], wrl_ref[...]) + br_ref[...]
    lane = lax.broadcasted_iota(I32, logits.shape, 1)
    vals, idxs = [], []
    cur = logits
    for _ in range(TOP_K):
        m = jnp.max(cur, axis=-1, keepdims=True)
        idx = jnp.min(jnp.where(cur == m, lane, LANES), axis=-1, keepdims=True)
        vals.append(m)
        idxs.append(idx)
        cur = jnp.where(lane == idx, -jnp.inf, cur)
    es = [jnp.exp(v - vals[0]) for v in vals]
    tot = es[0] + es[1] + es[2] + es[3]
    ti = jnp.zeros(logits.shape, I32)
    tg = jnp.zeros(logits.shape, F32)
    for k in range(TOP_K):
        ti = jnp.where(lane == k, idxs[k], ti)
        tg = jnp.where(lane == k, es[k] / tot, tg)
    ti_ref[0] = ti
    tg_ref[0] = tg


def _outproj(o_sb, o_ml, x, mod, w1, w2, g2, wr_hi, wr_lo, br):
    b, s, d = x.shape
    hw = o_sb.shape[2]
    tm = min(ROW_TILE, s)
    tok = lambda w: pl.BlockSpec((1, tm, w), lambda bi, i: (bi, i, 0))
    const = lambda shp: pl.BlockSpec(shp, lambda bi, i: (0,) * len(shp))
    return pl.pallas_call(
        _outproj_kernel,
        grid=(b, s // tm),
        in_specs=[tok(hw), tok(hw), tok(d), pl.BlockSpec((1, 6, d), lambda bi, i: (bi, 0, 0)),
                  const((hw, d)), const((hw, d)), const((1, d)),
                  const((d, LANES)), const((d, LANES)), const((1, LANES))],
        out_specs=[tok(d), tok(d), tok(LANES), tok(LANES)],
        out_shape=[jax.ShapeDtypeStruct((b, s, d), F32), jax.ShapeDtypeStruct((b, s, d), F32),
                   jax.ShapeDtypeStruct((b, s, LANES), I32), jax.ShapeDtypeStruct((b, s, LANES), F32)],
        compiler_params=_cparams(("arbitrary", "arbitrary")),
        name="outproj_router",
    )(o_sb, o_ml, x, mod, w1, w2, g2, wr_hi, wr_lo, br)


def _rank_kernel(ti_ref, rank_ref, cnt_ref, carry_scr):
    i = pl.program_id(0)

    @pl.when(i == 0)
    def _():
        carry_scr[...] = jnp.zeros_like(carry_scr)

    ti = ti_ref[...]
    tm = ti.shape[0]
    lane = lax.broadcasted_iota(I32, ti.shape, 1)
    hit = jnp.zeros(ti.shape, F32)
    for k in range(TOP_K):
        hit = hit + jnp.where(lane == ti[:, k:k + 1], 1.0, 0.0)
    row = lax.broadcasted_iota(I32, (tm, tm), 0)
    col = lax.broadcasted_iota(I32, (tm, tm), 1)
    strict = jnp.where(col < row, 1.0, 0.0).astype(BF16)
    excl = _dot(strict, hit.astype(BF16)) + carry_scr[...]
    rank = jnp.zeros(ti.shape, F32)
    for k in range(TOP_K):
        rk = jnp.sum(jnp.where(lane == ti[:, k:k + 1], excl, 0.0), axis=-1, keepdims=True)
        rank = jnp.where(lane == k, rk, rank)
    rank_ref[...] = rank.astype(I32)
    carry_scr[...] = carry_scr[...] + jnp.sum(hit, axis=0, keepdims=True)
    cnt_ref[...] = carry_scr[...].astype(I32)


def _ranks(top_idx):
    t = top_idx.shape[0]
    tm = ROW_TILE
    assert t % tm == 0
    return pl.pallas_call(
        _rank_kernel,
        grid=(t // tm,),
        in_specs=[pl.BlockSpec((tm, LANES), lambda i: (i, 0))],
        out_specs=[pl.BlockSpec((tm, LANES), lambda i: (i, 0)), pl.BlockSpec((1, LANES), lambda i: (0, 0))],
        out_shape=[jax.ShapeDtypeStruct((t, LANES), I32), jax.ShapeDtypeStruct((1, LANES), I32)],
        scratch_shapes=[pltpu.VMEM((1, LANES), F32)],
        compiler_params=_cparams(("arbitrary",)),
        name="expert_ranks",
    )(top_idx)


def _dispatch_kernel(pos_hbm, h_ref, xs_in_hbm, xs_hbm, pos_smem, stage, pos_sem, row_sem, *, tm):
    del xs_in_hbm
    i = pl.program_id(0)
    n = pl.num_programs(0)
    slot = lax.rem(i, 2)
    nidx = tm * TOP_K

    def row_copy(sl, r, p):
        return pltpu.make_async_copy(stage.at[sl, pl.ds(r, 1), :], xs_hbm.at[pl.ds(p, 1), :], row_sem.at[sl])

    def drain(sl):
        def w(_, c):
            row_copy(sl, 0, 0).wait()
            return c
        lax.fori_loop(0, nidx, w, 0)

    cp = pltpu.make_async_copy(pos_hbm.at[pl.ds(pl.multiple_of(i * nidx, nidx), nidx)], pos_smem, pos_sem)
    cp.start()

    @pl.when(i >= 2)
    def _():
        drain(slot)

    stage[slot] = h_ref[...]
    cp.wait()

    def issue(r, c):
        for k in range(TOP_K):
            row_copy(slot, r, pos_smem[TOP_K * r + k]).start()
        return c
    lax.fori_loop(0, tm, issue, 0)

    @pl.when(i == n - 1)
    def _():
        drain(slot)

        @pl.when(n >= 2)
        def _():
            drain(1 - slot)


def _dispatch(pos_flat, h, xs):
    t, d = h.shape
    tm = GATHER_TILE
    assert t % tm == 0
    return pl.pallas_call(
        functools.partial(_dispatch_kernel, tm=tm),
        grid=(t // tm,),
        in_specs=[pl.BlockSpec(memory_space=pl.ANY),
                  pl.BlockSpec((tm, d), lambda i: (i, 0)),
                  pl.BlockSpec(memory_space=pl.ANY)],
        out_specs=pl.BlockSpec(memory_space=pl.ANY),
        out_shape=jax.ShapeDtypeStruct(xs.shape, xs.dtype),
        scratch_shapes=[pltpu.SMEM((tm * TOP_K,), I32), pltpu.VMEM((2, tm, d), F32),
                        pltpu.SemaphoreType.DMA, pltpu.SemaphoreType.DMA((2,))],
        input_output_aliases={2: 0},
        compiler_params=_cparams(("arbitrary",)),
        name="moe_dispatch",
    )(pos_flat, h, xs)


def _moe_kernel(be_ref, nu_ref, xs_ref, wgu_ref, bgu_ref, wd_ref, bd_ref, ys_ref):
    del be_ref
    i = pl.program_id(0)
    ff = wd_ref.shape[1]

    @pl.when(i < nu_ref[0])
    def _():
        x = xs_ref[...].astype(BF16)
        glu = _dot(x, wgu_ref[0, :, 0:ff]) + bgu_ref[0, :, 0:ff]
        lin = _dot(x, wgu_ref[0, :, ff:2 * ff]) + bgu_ref[0, :, ff:2 * ff]
        glu = jnp.minimum(glu, SWIGLU_LIMIT)
        lin = jnp.clip(lin, -SWIGLU_LIMIT, SWIGLU_LIMIT)
        act = glu * jax.nn.sigmoid(SWIGLU_ALPHA * glu) * (lin + 1.0)
        ys_ref[...] = _dot(act.astype(BF16), wd_ref[0]) + bd_ref[0]

    @pl.when(i >= nu_ref[0])
    def _():
        ys_ref[...] = jnp.zeros_like(ys_ref)


def _moe_mlp(block_expert, n_used, xs, w_gu, b_gu, w_down, b_down):
    n_rows, d = xs.shape
    ne, _, ff2 = w_gu.shape
    ff = ff2 // 2
    tb = MOE_BLK
    blk = lambda i, be, nu: jnp.minimum(i, nu[0] - 1)
    grid_spec = pltpu.PrefetchScalarGridSpec(
        num_scalar_prefetch=2,
        grid=(n_rows // tb,),
        in_specs=[pl.BlockSpec((tb, d), lambda i, be, nu: (blk(i, be, nu), 0)),
                  pl.BlockSpec((1, d, ff2), lambda i, be, nu: (be[blk(i, be, nu)], 0, 0)),
                  pl.BlockSpec((1, 1, ff2), lambda i, be, nu: (be[blk(i, be, nu)], 0, 0)),
                  pl.BlockSpec((1, ff, d), lambda i, be, nu: (be[blk(i, be, nu)], 0, 0)),
                  pl.BlockSpec((1, 1, d), lambda i, be, nu: (be[blk(i, be, nu)], 0, 0))],
        out_specs=pl.BlockSpec((tb, d), lambda i, be, nu: (i, 0)),
    )
    return pl.pallas_call(
        _moe_kernel,
        grid_spec=grid_spec,
        out_shape=jax.ShapeDtypeStruct((n_rows, d), F32),
        compiler_params=_cparams(("arbitrary",)),
        name="moe_mlp",
    )(block_expert, n_used, xs, w_gu, b_gu.reshape(ne, 1, ff2), w_down, b_down.reshape(ne, 1, d))


def _combine_kernel(pos_hbm, ys_hbm, x1_ref, tg_ref, mod_ref, y_ref, pos_smem, buf, pos_sem, row_sem, *, tm, nb):
    i = pl.program_id(0)
    nidx = tm * TOP_K

    def row_copy(k, r, p):
        return pltpu.make_async_copy(ys_hbm.at[pl.ds(p, 1), :], buf.at[k, pl.ds(r, 1), :], row_sem)

    cp = pltpu.make_async_copy(pos_hbm.at[pl.ds(pl.multiple_of(i * nidx, nidx), nidx)], pos_smem, pos_sem)
    cp.start()
    cp.wait()

    def issue(r, c):
        for k in range(TOP_K):
            row_copy(k, r, pos_smem[TOP_K * r + k]).start()
        return c
    lax.fori_loop(0, tm, issue, 0)

    def w(_, c):
        row_copy(0, 0, 0).wait()
        return c
    lax.fori_loop(0, nidx, w, 0)

    tg = tg_ref[...]
    ff = tg[:, 0:1] * buf[0]
    for k in range(1, TOP_K):
        ff = ff + tg[:, k:k + 1] * buf[k]
    rows = tm // nb
    for g in range(nb):
        ga2 = mod_ref[g, 5:6, :]
        rs = slice(g * rows, (g + 1) * rows)
        y_ref[rs, :] = x1_ref[rs, :] + ga2 * ff[rs, :]


def _combine(pos_flat, ys, x1, top_gate, mod, tokens_per_batch):
    t, d = x1.shape
    tm = GATHER_TILE
    assert t % tm == 0
    if tokens_per_batch >= tm:
        assert tokens_per_batch % tm == 0
        nb = 1
        per = tokens_per_batch // tm
        mod_spec = pl.BlockSpec((1, 6, d), lambda i: (i // per, 0, 0))
    else:
        assert tm % tokens_per_batch == 0
        nb = tm // tokens_per_batch
        mod_spec = pl.BlockSpec((nb, 6, d), lambda i: (i, 0, 0))
    return pl.pallas_call(
        functools.partial(_combine_kernel, tm=tm, nb=nb),
        grid=(t // tm,),
        in_specs=[pl.BlockSpec(memory_space=pl.ANY), pl.BlockSpec(memory_space=pl.ANY),
                  pl.BlockSpec((tm, d), lambda i: (i, 0)), pl.BlockSpec((tm, LANES), lambda i: (i, 0)), mod_spec],
        out_specs=pl.BlockSpec((tm, d), lambda i: (i, 0)),
        out_shape=jax.ShapeDtypeStruct((t, d), F32),
        scratch_shapes=[pltpu.SMEM((tm * TOP_K,), I32), pltpu.VMEM((TOP_K, tm, d), F32),
                        pltpu.SemaphoreType.DMA, pltpu.SemaphoreType.DMA],
        compiler_params=_cparams(("arbitrary",)),
        name="moe_combine",
    )(pos_flat, ys, x1, top_gate, mod)


def _pad_lanes(a, fill=0.0):
    return jnp.pad(a, [(0, 0)] * (a.ndim - 1) + [(0, LANES - a.shape[-1])], constant_values=fill)


def _hi_lo(w):
    hi = w.astype(BF16)
    return hi, (w - hi.astype(F32)).astype(BF16)


def _mixer(x, mod, k_past, v_past, conv_past, c0, n0, m0, p):
    b, s, d = x.shape
    q_raw, k_raw, v, ml_qk, ml_v, ml_o, gates = _inproj(
        x, mod, p["g1"], p["w_main"], p["wg_hi"], p["wg_lo"], p["bg"], p["widths"])
    o_sb, k_norm = _sb_attention(q_raw, k_raw, v, k_past, v_past, p["g_q"], p["g_k"])
    o_ml, c_new, n_new, m_new, conv_new = _mlstm(
        ml_qk, ml_v, ml_o, gates, conv_past, c0, n0, _pad_lanes(m0)[:, None, :],
        p["w_conv"], p["b_conv"], p["g_head"])
    x1, h2, ti, tg = _outproj(o_sb, o_ml, x, mod, p["w_out1"], p["w_out2"], p["g2"],
                              p["wr_hi"], p["wr_lo"], p["br"])
    nsb = k_norm.shape[-1] // SB_HEAD_DIM
    state = (k_norm.reshape(b, s, nsb, SB_HEAD_DIM), v.reshape(b, s, nsb, SB_HEAD_DIM), conv_new,
             c_new, n_new, m_new[:, 0, :ML_HEADS])
    return x1.reshape(b * s, d), h2.reshape(b * s, d), ti.reshape(b * s, LANES), tg.reshape(b * s, LANES), state


def _layer(xp, xs_, cp, cs, kc, vc, convc, cc, nc, mc, w_ada, b_ada, g_norm1, w_in, g_q, g_k, w_conv, b_conv,
           b_gate, g_head, w_out, g_norm2, w_router, b_router, w_gu, b_gu, w_down, b_down):
    d = xp.shape[-1]
    bp, sp = xp.shape[:2]
    bs, ss = xs_.shape[:2]
    sbw = kc.shape[-2] * kc.shape[-1]
    mlw = ML_HEADS * ML_HEAD_DIM
    ngate = 2 * ML_HEADS
    gate0 = 3 * sbw + 3 * mlw
    w_main = jnp.concatenate([w_in[:, :gate0], w_in[:, gate0 + ngate:]], axis=1).astype(BF16)
    wg_hi, wg_lo = _hi_lo(_pad_lanes(w_in[:, gate0:gate0 + ngate]))
    wr_hi, wr_lo = _hi_lo(_pad_lanes(w_router))
    p = dict(
        widths=(sbw, sbw, sbw, 2 * mlw, mlw, mlw),
        g1=g_norm1[None], w_main=w_main, wg_hi=wg_hi, wg_lo=wg_lo, bg=_pad_lanes(b_gate)[None],
        g_q=jnp.tile(g_q, LANES // SB_HEAD_DIM)[None], g_k=jnp.tile(g_k, LANES // SB_HEAD_DIM)[None],
        w_conv=w_conv, b_conv=b_conv[None], g_head=g_head[None],
        w_out1=w_out[:sbw].astype(BF16), w_out2=w_out[sbw:].astype(BF16), g2=g_norm2[None],
        wr_hi=wr_hi, wr_lo=wr_lo, br=_pad_lanes(b_router, -1e30)[None],
    )
    mod = _ada(jnp.concatenate([cp, cs], axis=0), w_ada, b_ada)
    mod_p = mod[:bp].reshape(bp, 6, d)
    mod_s = mod[bp:].reshape(bs, 6, d)

    zeros = lambda *shp: jnp.zeros(shp, F32)
    x1p, h2p, tip, tgp, st_p = _mixer(
        xp, mod_p, None, None, zeros(bp, CONV_W - 1, 2 * mlw), zeros(bp, ML_HEADS, ML_HEAD_DIM, ML_HEAD_DIM),
        zeros(bp, ML_HEADS, ML_HEAD_DIM), zeros(bp, ML_HEADS), p)
    x1s, h2s, tis, tgs, st_s = _mixer(
        xs_, mod_s, kc.reshape(bs, -1, sbw), vc.reshape(bs, -1, sbw), convc, cc, nc, mc, p)

    tp, ts = bp * sp, bs * ss
    ti_all = jnp.concatenate([tip, tis], axis=0)
    rank, counts = _ranks(ti_all)
    counts = counts[0, :N_EXPERTS]
    padded = (counts + MOE_BLK - 1) // MOE_BLK * MOE_BLK
    pad_end = jnp.cumsum(padded)
    pad_start = pad_end - padded
    top_idx = ti_all[:, :TOP_K]
    pos = (pad_start[top_idx] + rank[:, :TOP_K]).astype(I32)
    n_blocks = -(-((tp + ts) * TOP_K + N_EXPERTS * (MOE_BLK - 1)) // MOE_BLK)
    n_used = (pad_end[-1] // MOE_BLK).astype(I32).reshape(1)
    block_expert = jnp.minimum(
        jnp.searchsorted(pad_end, jnp.arange(n_blocks, dtype=I32) * MOE_BLK, side="right"),
        N_EXPERTS - 1).astype(I32)
    pos_p = pos[:tp].reshape(-1)
    pos_s = pos[tp:].reshape(-1)

    xs_buf = jnp.zeros((n_blocks * MOE_BLK, d), F32)
    xs_buf = _dispatch(pos_p, h2p, xs_buf)
    xs_buf = _dispatch(pos_s, h2s, xs_buf)
    ys = _moe_mlp(block_expert, n_used, xs_buf, w_gu.astype(BF16), b_gu, w_down.astype(BF16), b_down)
    yp = _combine(pos_p, ys, x1p, tgp, mod_p, sp).reshape(bp, sp, d)
    ysm = _combine(pos_s, ys, x1s, tgs, mod_s, ss).reshape(bs, ss, d)
    return yp, ysm, st_p, st_s


def kernel(x_prompt, x_sample, c_prompt, c_sample, cache_sb_k, cache_sb_v, state_conv, state_mlstm_c, state_mlstm_n, state_mlstm_m, w_ada, b_ada, g_norm1, w_in, g_q, g_k, w_conv, b_conv, b_gate, g_head, w_out, g_norm2, w_router, b_router, w_gu, b_gu, w_down, b_down):
    assert w_ada.shape[0] == 1, "single-layer step"
    yp, ys, st_p, st_s = _layer(
        x_prompt, x_sample, c_prompt, c_sample, cache_sb_k[0], cache_sb_v[0], state_conv[0], state_mlstm_c[0],
        state_mlstm_n[0], state_mlstm_m[0], w_ada[0], b_ada[0], g_norm1[0], w_in[0], g_q[0], g_k[0], w_conv[0],
        b_conv[0], b_gate[0], g_head[0], w_out[0], g_norm2[0], w_router[0], b_router[0], w_gu[0], b_gu[0],
        w_down[0], b_down[0])
    return (yp, ys) + tuple(a[None] for a in st_p) + tuple(a[None] for a in st_s)
```

```python
import functools
import math

import jax
import jax.numpy as jnp
from jax import lax
from jax.experimental import pallas as pl
from jax.experimental.pallas import tpu as pltpu

F32 = jnp.float32
BF16 = jnp.bfloat16
I32 = jnp.int32

EPS = 1e-6
LANES = 128
SUBLANES = 8
SB_HEAD_DIM = 64
ML_HEAD_DIM = 128
ML_HEADS = 4
CONV_W = 4
N_EXPERTS = 32
TOP_K = 4
SWIGLU_LIMIT = 7.0
SWIGLU_ALPHA = 1.702
VMEM_LIMIT = 56 * 1024 * 1024

ROW_TILE = 512
MOE_BLK = 512
GATHER_TILE = 256
ISSUE_UNROLL = 8
ML_CHUNK = 256
SB_BLK = 128
SB_UNDERFLOW = -105.0


def _cparams(sem):
    return pltpu.CompilerParams(dimension_semantics=sem, vmem_limit_bytes=VMEM_LIMIT)


def _split3(x):
    p1 = x.astype(BF16)
    r1 = x - p1.astype(F32)
    p2 = r1.astype(BF16)
    p3 = (r1 - p2.astype(F32)).astype(BF16)
    return p1, p2, p3


def _dot(a, b):
    return jnp.dot(a, b, preferred_element_type=F32)


def _dot_nt(a, b):
    return lax.dot_general(a, b, (((1,), (1,)), ((), ())), preferred_element_type=F32)


def _dot3(x, w_hi, w_lo):
    xh = x.astype(BF16)
    xl = (x - xh.astype(F32)).astype(BF16)
    return _dot(xh, w_hi) + _dot(xl, w_hi) + _dot(xh, w_lo)


def _store_row_tiles(ref, x):
    n, d = x.shape
    assert d == SUBLANES * LANES
    for c in range(SUBLANES):
        ref[pl.ds(c, n, stride=SUBLANES), :] = x[:, c * LANES:(c + 1) * LANES]


def _load_row_tiles(ref, n):
    return jnp.concatenate([ref[pl.ds(c, n, stride=SUBLANES), :] for c in range(SUBLANES)], axis=1)


def _log_sigmoid(x):
    return jnp.minimum(x, 0.0) - jnp.log1p(jnp.exp(-jnp.abs(x)))


def _ada_kernel(c_ref, w_ref, b_ref, o_ref):
    c = c_ref[...]
    s = c * jax.nn.sigmoid(c)
    o_ref[...] = jnp.dot(s, w_ref[...], precision=lax.Precision.HIGHEST,
                         preferred_element_type=F32) + b_ref[...]


def _ada(c, w_ada, b_ada):
    n, d = c.shape
    cols = w_ada.shape[1]
    tn = cols // 6
    return pl.pallas_call(
        _ada_kernel,
        grid=(cols // tn,),
        in_specs=[pl.BlockSpec((n, d), lambda j: (0, 0)),
                  pl.BlockSpec((d, tn), lambda j: (0, j)),
                  pl.BlockSpec((1, tn), lambda j: (0, j))],
        out_specs=pl.BlockSpec((n, tn), lambda j: (0, j)),
        out_shape=jax.ShapeDtypeStruct((n, cols), F32),
        compiler_params=_cparams(("arbitrary",)),
        name="ada_mod",
    )(c, w_ada, b_ada.reshape(1, cols))


def _inproj_kernel(x_ref, mod_ref, g1_ref, w_ref, wgh_ref, wgl_ref, bg_ref,
                   q_ref, k_ref, v_ref, qk_ref, mv_ref, mo_ref, gate_ref, *, widths):
    x = x_ref[0]
    mod = mod_ref[0]
    sh1 = mod[0:1]
    sc1 = mod[1:2]
    ms = jnp.mean(x * x, axis=-1, keepdims=True)
    h = x * lax.rsqrt(ms + EPS) * g1_ref[...]
    h = h * (1.0 + sc1) + sh1
    hb = h.astype(BF16)
    off = 0
    for ref, wd in zip((q_ref, k_ref, v_ref, qk_ref, mv_ref, mo_ref), widths):
        ref[0] = _dot(hb, w_ref[:, off:off + wd])
        off += wd
    gate_ref[0] = _dot3(h, wgh_ref[...], wgl_ref[...]) + bg_ref[...]


def _inproj(x, mod, g1, w_main, wg_hi, wg_lo, bg, widths):
    b, s, d = x.shape
    tm = min(ROW_TILE, s)
    ncol = w_main.shape[1]
    tok = lambda w: pl.BlockSpec((1, tm, w), lambda bi, i: (bi, i, 0))
    const = lambda shp: pl.BlockSpec(shp, lambda bi, i: (0,) * len(shp))
    return pl.pallas_call(
        functools.partial(_inproj_kernel, widths=widths),
        grid=(b, s // tm),
        in_specs=[tok(d),
                  pl.BlockSpec((1, 6, d), lambda bi, i: (bi, 0, 0)),
                  const((1, d)), const((d, ncol)), const((d, LANES)), const((d, LANES)), const((1, LANES))],
        out_specs=[tok(w) for w in widths] + [tok(LANES)],
        out_shape=[jax.ShapeDtypeStruct((b, s, w), F32) for w in widths]
        + [jax.ShapeDtypeStruct((b, s, LANES), F32)],
        compiler_params=_cparams(("arbitrary", "arbitrary")),
        name="inproj",
    )(x, mod, g1, w_main, wg_hi, wg_lo, bg)


def _sb_kernel(*refs, past, seq, tq, kb, npair):
    if past:
        q_ref, k_ref, v_ref, kp_ref, vp_ref, gq_ref, gk_ref, o_ref, kn_ref, ks_scr, vs_scr, run_scr, acc_scr = refs
    else:
        q_ref, k_ref, v_ref, gq_ref, gk_ref, o_ref, kn_ref, ks_scr, vs_scr, run_scr, acc_scr = refs
    i = pl.program_id(1)
    lane = lax.broadcasted_iota(I32, (1, LANES), 1)
    lo_half = lane < SB_HEAD_DIM
    pairs = [slice(hp * LANES, (hp + 1) * LANES) for hp in range(npair)]

    def headnorm(x, g):
        x2 = x * x
        s0 = jnp.sum(jnp.where(lo_half, x2, 0.0), axis=-1, keepdims=True)
        s1 = jnp.sum(jnp.where(lo_half, 0.0, x2), axis=-1, keepdims=True)
        r = jnp.where(lo_half, lax.rsqrt(s0 * (1.0 / SB_HEAD_DIM) + EPS),
                      lax.rsqrt(s1 * (1.0 / SB_HEAD_DIM) + EPS))
        return x * r * g

    @pl.when(i == 0)
    def _():
        ch = min(256, seq)
        for c in range(seq // ch):
            rows = slice(c * ch, (c + 1) * ch)
            dst = slice(past + c * ch, past + (c + 1) * ch)
            for ps in pairs:
                kn = headnorm(k_ref[0, rows, ps], gk_ref[...])
                kn_ref[0, rows, ps] = kn
                ks_scr[dst, ps] = kn.astype(BF16)
            vs_scr[dst, :] = v_ref[0, rows, :].astype(BF16)
        if past:
            ks_scr[0:past, :] = kp_ref[0].astype(BF16)
            vs_scr[0:past, :] = vp_ref[0].astype(BF16)

    qm = []
    for ps in pairs:
        q = headnorm(q_ref[0, :, ps], gq_ref[...]) * (1.0 / math.sqrt(SB_HEAD_DIM))
        qm.append(jnp.concatenate([jnp.where(lo_half, q, 0.0), jnp.where(lo_half, 0.0, q)], axis=0).astype(BF16))
    nrow = 2 * npair * tq

    def neg_cum(nk):
        rj = lax.broadcasted_iota(I32, (2 * nk, LANES + nk), 0)
        cj = lax.broadcasted_iota(I32, (2 * nk, LANES + nk), 1)
        rjm = jnp.where(rj >= nk, rj - nk, rj)
        return jnp.where(cj < LANES, -1.0, jnp.where(rjm >= cj - LANES, -1.0, 0.0)).astype(BF16)

    w_diag = neg_cum(tq)
    w_full = w_diag if kb == tq else neg_cum(kb)
    causal = (lax.broadcasted_iota(I32, (nrow, tq), 1)
              < jnp.bitwise_and(lax.broadcasted_iota(I32, (nrow, tq), 0), tq - 1))

    def sweep(row0, nk, w_neg, diag):
        kv = [(ks_scr[pl.ds(row0, nk), ps], vs_scr[pl.ds(row0, nk), ps]) for ps in pairs]
        z = jnp.concatenate([_dot_nt(qm[hp], kv[hp][0]) for hp in range(npair)], axis=0)
        sp = jnp.maximum(z, 0.0) + jnp.log(1.0 + jnp.exp(-jnp.abs(z)))
        if diag:
            sp = jnp.where(causal, sp, 0.0)
        hi = sp.astype(BF16)
        lo = (sp - hi.astype(F32)).astype(BF16)
        t = _dot(jnp.concatenate([hi, lo], axis=1), w_neg)
        if diag:
            arg = z + t[:, LANES:]
            run = t[:, :LANES]
        else:
            old = run_scr[...]
            arg = z + t[:, LANES:] + old[:, :nk]
            run = old + t[:, :LANES]
        run_scr[...] = run
        p = jnp.exp(arg)
        if diag:
            p = jnp.where(causal, p, 0.0)
        p = p.astype(BF16)
        for hp in range(npair):
            r0 = 2 * hp * tq
            vblk = kv[hp][1]
            vm = jnp.concatenate([jnp.where(lo_half, vblk, 0.0), jnp.where(lo_half, 0.0, vblk)], axis=0)
            pv = _dot(jnp.concatenate([p[r0:r0 + tq], p[r0 + tq:r0 + 2 * tq]], axis=1), vm.astype(BF16))
            acc_scr[hp] = pv if diag else acc_scr[hp] + pv
        return jnp.max(run)

    q0 = past + i * tq
    worst0 = sweep(pl.multiple_of(q0, tq), tq, w_diag, True)
    n_before = q0 // kb

    def cond(c):
        return jnp.logical_and(c[0] < n_before, c[1] > SB_UNDERFLOW)

    def body(c):
        j = n_before - 1 - c[0]
        return c[0] + 1, sweep(pl.multiple_of(j * kb, kb), kb, w_full, False)

    lax.while_loop(cond, body, (jnp.int32(0), worst0))
    for hp, ps in enumerate(pairs):
        o_ref[0, :, ps] = acc_scr[hp]


def _sb_attention(q_raw, k_raw, v, k_past, v_past, g_q, g_k):
    b, s, w = q_raw.shape
    past = 0 if k_past is None else k_past.shape[1]
    tq = min(SB_BLK, s)
    kb = SB_BLK
    assert s % tq == 0 and past % kb == 0 and w % LANES == 0 and (tq == kb or s == tq)
    npair = w // LANES
    qblk = pl.BlockSpec((1, tq, w), lambda bi, i: (bi, i, 0))
    full = lambda n: pl.BlockSpec((1, n, w), lambda bi, i: (bi, 0, 0))
    gspec = pl.BlockSpec((1, LANES), lambda bi, i: (0, 0))
    in_specs = [qblk, full(s), full(s)]
    args = [q_raw, k_raw, v]
    if past:
        in_specs += [full(past), full(past)]
        args += [k_past, v_past]
    in_specs += [gspec, gspec]
    args += [g_q, g_k]
    return pl.pallas_call(
        functools.partial(_sb_kernel, past=past, seq=s, tq=tq, kb=kb, npair=npair),
        grid=(b, s // tq),
        in_specs=in_specs,
        out_specs=[qblk, full(s)],
        out_shape=[jax.ShapeDtypeStruct((b, s, w), F32), jax.ShapeDtypeStruct((b, s, w), F32)],
        scratch_shapes=[pltpu.VMEM((past + s, w), BF16), pltpu.VMEM((past + s, w), BF16),
                        pltpu.VMEM((2 * npair * tq, LANES), F32), pltpu.VMEM((npair, tq, LANES), F32)],
        compiler_params=_cparams(("arbitrary", "arbitrary")),
        name="sb_attention",
    )(*args)


def _mlstm_kernel(qk_ref, v_ref, og_ref, g_ref, cp_ref, c0_ref, n0_ref, m0_ref, wc_ref, bc_ref, gh_ref,
                  out_ref, cout_ref, nout_ref, mout_ref, convout_ref,
                  prev_scr, c_scr, n_scr, m_scr, *, chunk):
    ci = pl.program_id(1)
    nh, dh = ML_HEADS, ML_HEAD_DIM
    width = nh * dh
    L = chunk

    @pl.when(ci == 0)
    def _():
        prev_scr[...] = jnp.zeros_like(prev_scr)
        prev_scr[8 - (CONV_W - 1):8, :] = cp_ref[0]
        c_scr[...] = c0_ref[0]
        n_scr[...] = n0_ref[0]
        m_scr[...] = m0_ref[0]

    u = qk_ref[0]
    xfull = jnp.concatenate([prev_scr[...], u], axis=0)
    wc = wc_ref[...]
    acc = bc_ref[...] + xfull[8 - (CONV_W - 1):8 - (CONV_W - 1) + L] * wc[0:1]
    for j in range(1, CONV_W):
        o = 8 - (CONV_W - 1) + j
        acc = acc + xfull[o:o + L] * wc[j:j + 1]
    qk = acc * jax.nn.sigmoid(acc)
    prev_scr[...] = u[L - 8:L, :]

    gt = g_ref[0]
    lf = _log_sigmoid(gt)
    gt_t = gt.T
    lf_t = lf.T
    row = lax.broadcasted_iota(I32, (L, L), 0)
    col = lax.broadcasted_iota(I32, (L, L), 1)
    tri = row >= col
    tril = jnp.where(tri, 1.0, 0.0).astype(BF16)
    triu = jnp.where(row <= col, 1.0, 0.0).astype(BF16)
    p1, p2, p3 = _split3(lf)
    b_col = _dot(tril, p1) + _dot(tril, p2) + _dot(tril, p3)
    q1, q2, q3 = _split3(lf_t[0:8, :])
    b_row = _dot(q1, triu) + _dot(q2, triu) + _dot(q3, triu)

    lane = lax.broadcasted_iota(I32, (1, LANES), 1)
    m_vec = m_scr[...]
    m_out = m_vec
    for h in range(nh):
        hs = slice(h * dh, (h + 1) * dh)
        qh = qk[:, h * dh:(h + 1) * dh]
        kh = qk[:, width + h * dh:width + (h + 1) * dh] * (dh ** -0.5)
        vh = v_ref[0, :, hs]
        b_c = b_col[:, nh + h:nh + h + 1]
        ig_c = gt[:, h:h + 1]
        b_r = b_row[nh + h:nh + h + 1, :]
        ig_r = gt_t[h:h + 1, :]
        m_prev = m_vec[:, h:h + 1]

        d = jnp.where(tri, (b_c - b_r) + ig_r, -jnp.inf)
        inter = b_c + m_prev
        m_t = jnp.maximum(inter, jnp.max(d, axis=1, keepdims=True))
        w_intra = jnp.exp(d - m_t)
        w_inter = jnp.exp(inter - m_t)
        qb = qh.astype(BF16)
        kb = kh.astype(BF16)
        s = _dot_nt(qb, kb) * w_intra
        c_h = c_scr[h]
        n_h = n_scr[h:h + 1, :]
        num = _dot(s.astype(BF16), vh.astype(BF16)) + w_inter * _dot_nt(qb, c_h.astype(BF16))
        den = jnp.sum(s, axis=1, keepdims=True) + w_inter * jnp.sum(qh * n_h, axis=1, keepdims=True)
        hh = num / jnp.maximum(jnp.abs(den), jnp.exp(-m_t))

        b_last = b_c[L - 1:L, :]
        g_c = (b_last - b_c) + ig_c
        m_new = jnp.maximum(b_last + m_prev, jnp.max(g_c, axis=0, keepdims=True))
        w_state = jnp.exp(g_c - m_new)
        decay = jnp.exp(b_last + m_prev - m_new)
        vw_t = (vh * w_state).T.astype(BF16)
        c_scr[h] = decay * c_h + _dot(vw_t, kb)
        n_scr[h:h + 1, :] = decay * n_h + jnp.sum(kh * w_state, axis=0, keepdims=True)
        m_out = jnp.where(lane == h, m_new, m_out)

        hn = hh * lax.rsqrt(jnp.mean(hh * hh, axis=-1, keepdims=True) + EPS) * gh_ref[...]
        out_ref[0, :, hs] = jax.nn.sigmoid(og_ref[0, :, hs]) * hn
    m_scr[...] = m_out

    @pl.when(ci == pl.num_programs(1) - 1)
    def _():
        cout_ref[0] = c_scr[...]
        nout_ref[0] = n_scr[...]
        mout_ref[0] = m_scr[...]
        convout_ref[0] = prev_scr[8 - (CONV_W - 1):8, :]


def _mlstm(ml_qk, ml_v, ml_o, gates, conv_past, c0, n0, m0, w_conv, b_conv, g_head):
    b, s, w2 = ml_qk.shape
    width = w2 // 2
    nh, dh = ML_HEADS, ML_HEAD_DIM
    chunk = min(ML_CHUNK, s)
    assert s % chunk == 0 and chunk >= 8
    tok = lambda w: pl.BlockSpec((1, chunk, w), lambda bi, ci: (bi, ci, 0))
    perb = lambda shp: pl.BlockSpec((1,) + shp, lambda bi, ci: (bi,) + (0,) * len(shp))
    const = lambda shp: pl.BlockSpec(shp, lambda bi, ci: (0,) * len(shp))
    return pl.pallas_call(
        functools.partial(_mlstm_kernel, chunk=chunk),
        grid=(b, s // chunk),
        in_specs=[tok(w2), tok(width), tok(width), tok(LANES),
                  perb((CONV_W - 1, w2)), perb((nh, dh, dh)), perb((nh, dh)), perb((1, LANES)),
                  const((CONV_W, w2)), const((1, w2)), const((1, dh))],
        out_specs=[tok(width), perb((nh, dh, dh)), perb((nh, dh)), perb((1, LANES)), perb((CONV_W - 1, w2))],
        out_shape=[jax.ShapeDtypeStruct((b, s, width), F32),
                   jax.ShapeDtypeStruct((b, nh, dh, dh), F32),
                   jax.ShapeDtypeStruct((b, nh, dh), F32),
                   jax.ShapeDtypeStruct((b, 1, LANES), F32),
                   jax.ShapeDtypeStruct((b, CONV_W - 1, w2), F32)],
        scratch_shapes=[pltpu.VMEM((8, w2), F32), pltpu.VMEM((nh, dh, dh), F32),
                        pltpu.VMEM((nh, dh), F32), pltpu.VMEM((1, LANES), F32)],
        compiler_params=_cparams(("arbitrary", "arbitrary")),
        name="mlstm",
    )(ml_qk, ml_v, ml_o, gates, conv_past, c0, n0, m0, w_conv, b_conv, g_head)


def _outproj_kernel(osb_ref, oml_ref, x_ref, mod_ref, w1_ref, w2_ref, g2_ref, wrh_ref, wrl_ref, br_ref,
                    x1_ref, h2_ref, ti_ref, tg_ref):
    mod = mod_ref[0]
    ga1 = mod[2:3]
    sh2 = mod[3:4]
    sc2 = mod[4:5]
    mix = _dot(osb_ref[0].astype(BF16), w1_ref[...]) + _dot(oml_ref[0].astype(BF16), w2_ref[...])
    x1 = x_ref[0] + ga1 * mix
    x1_ref[0] = x1
    ms = jnp.mean(x1 * x1, axis=-1, keepdims=True)
    h2 = x1 * lax.rsqrt(ms + EPS) * g2_ref[...]
    h2 = h2 * (1.0 + sc2) + sh2
    _store_row_tiles(h2_ref, h2)
    logits = _dot3(h2, wrh_ref[...], wrl_ref[...]) + br_ref[...]
    lane = lax.broadcasted_iota(I32, logits.shape, 1)
    vals, idxs = [], []
    cur = logits
    for _ in range(TOP_K):
        m = jnp.max(cur, axis=-1, keepdims=True)
        idx = jnp.min(jnp.where(cur == m, lane, LANES), axis=-1, keepdims=True)
        vals.append(m)
        idxs.append(idx)
        cur = jnp.where(lane == idx, -jnp.inf, cur)
    es = [jnp.exp(v - vals[0]) for v in vals]
    tot = es[0] + es[1] + es[2] + es[3]
    ti = jnp.zeros(logits.shape, I32)
    tg = jnp.zeros(logits.shape, F32)
    for k in range(TOP_K):
        ti = jnp.where(lane == k, idxs[k], ti)
        tg = jnp.where(lane == k, es[k] / tot, tg)
    ti_ref[0] = ti
    tg_ref[0] = tg


def _outproj(o_sb, o_ml, x, mod, w1, w2, g2, wr_hi, wr_lo, br):
    b, s, d = x.shape
    hw = o_sb.shape[2]
    tm = min(ROW_TILE, s)
    tok = lambda w: pl.BlockSpec((1, tm, w), lambda bi, i: (bi, i, 0))
    const = lambda shp: pl.BlockSpec(shp, lambda bi, i: (0,) * len(shp))
    return pl.pallas_call(
        _outproj_kernel,
        grid=(b, s // tm),
        in_specs=[tok(hw), tok(hw), tok(d), pl.BlockSpec((1, 6, d), lambda bi, i: (bi, 0, 0)),
                  const((hw, d)), const((hw, d)), const((1, d)),
                  const((d, LANES)), const((d, LANES)), const((1, LANES))],
        out_specs=[tok(d), pl.BlockSpec((tm * SUBLANES, LANES), lambda bi, i: (bi * (s // tm) + i, 0)),
                   tok(LANES), tok(LANES)],
        out_shape=[jax.ShapeDtypeStruct((b, s, d), F32), jax.ShapeDtypeStruct((b * s * SUBLANES, LANES), F32),
                   jax.ShapeDtypeStruct((b, s, LANES), I32), jax.ShapeDtypeStruct((b, s, LANES), F32)],
        compiler_params=_cparams(("arbitrary", "arbitrary")),
        name="outproj_router",
    )(o_sb, o_ml, x, mod, w1, w2, g2, wr_hi, wr_lo, br)


def _rank_kernel(ti_ref, rank_ref, cnt_ref, carry_scr):
    i = pl.program_id(0)

    @pl.when(i == 0)
    def _():
        carry_scr[...] = jnp.zeros_like(carry_scr)

    ti = ti_ref[...]
    tm = ti.shape[0]
    lane = lax.broadcasted_iota(I32, ti.shape, 1)
    hit = jnp.zeros(ti.shape, F32)
    for k in range(TOP_K):
        hit = hit + jnp.where(lane == ti[:, k:k + 1], 1.0, 0.0)
    row = lax.broadcasted_iota(I32, (tm, tm), 0)
    col = lax.broadcasted_iota(I32, (tm, tm), 1)
    strict = jnp.where(col < row, 1.0, 0.0).astype(BF16)
    excl = _dot(strict, hit.astype(BF16)) + carry_scr[...]
    rank = jnp.zeros(ti.shape, F32)
    for k in range(TOP_K):
        rk = jnp.sum(jnp.where(lane == ti[:, k:k + 1], excl, 0.0), axis=-1, keepdims=True)
        rank = jnp.where(lane == k, rk, rank)
    rank_ref[...] = rank.astype(I32)
    carry_scr[...] = carry_scr[...] + jnp.sum(hit, axis=0, keepdims=True)
    cnt_ref[...] = carry_scr[...].astype(I32)


def _ranks(top_idx):
    t = top_idx.shape[0]
    tm = ROW_TILE
    assert t % tm == 0
    return pl.pallas_call(
        _rank_kernel,
        grid=(t // tm,),
        in_specs=[pl.BlockSpec((tm, LANES), lambda i: (i, 0))],
        out_specs=[pl.BlockSpec((tm, LANES), lambda i: (i, 0)), pl.BlockSpec((1, LANES), lambda i: (0, 0))],
        out_shape=[jax.ShapeDtypeStruct((t, LANES), I32), jax.ShapeDtypeStruct((1, LANES), I32)],
        scratch_shapes=[pltpu.VMEM((1, LANES), F32)],
        compiler_params=_cparams(("arbitrary",)),
        name="expert_ranks",
    )(top_idx)


def _row_tile(ref, r):
    return ref.at[pl.ds(pl.multiple_of(r * SUBLANES, SUBLANES), SUBLANES), :]


def _dispatch_kernel(pos_hbm, h_ref, xs_in_hbm, xs_hbm, pos_smem, stage, pos_sem, row_sem, *, tm):
    del xs_in_hbm
    i = pl.program_id(0)
    n = pl.num_programs(0)
    slot = lax.rem(i, 2)
    nidx = tm * TOP_K

    def drain(sl):
        for _ in range(TOP_K):
            pltpu.make_async_copy(stage.at[sl], stage.at[sl], row_sem.at[sl]).wait()

    cp = pltpu.make_async_copy(pos_hbm.at[pl.ds(pl.multiple_of(i * nidx, nidx), nidx)], pos_smem, pos_sem)
    cp.start()

    @pl.when(i >= 2)
    def _():
        drain(slot)

    stage[slot] = h_ref[...]
    cp.wait()

    def issue(c, carry):
        for rr in range(ISSUE_UNROLL):
            r = c * ISSUE_UNROLL + rr
            for k in range(TOP_K):
                pltpu.make_async_copy(_row_tile(stage.at[slot], r), _row_tile(xs_hbm, pos_smem[TOP_K * r + k]),
                                      row_sem.at[slot]).start()
        return carry
    lax.fori_loop(0, tm // ISSUE_UNROLL, issue, 0)

    @pl.when(i == n - 1)
    def _():
        drain(slot)

        @pl.when(n >= 2)
        def _():
            drain(1 - slot)


def _dispatch(pos_flat, h_tiles, xs):
    t = h_tiles.shape[0] // SUBLANES
    tm = GATHER_TILE
    assert t % tm == 0 and tm % ISSUE_UNROLL == 0
    return pl.pallas_call(
        functools.partial(_dispatch_kernel, tm=tm),
        grid=(t // tm,),
        in_specs=[pl.BlockSpec(memory_space=pl.ANY),
                  pl.BlockSpec((tm * SUBLANES, LANES), lambda i: (i, 0)),
                  pl.BlockSpec(memory_space=pl.ANY)],
        out_specs=pl.BlockSpec(memory_space=pl.ANY),
        out_shape=jax.ShapeDtypeStruct(xs.shape, xs.dtype),
        scratch_shapes=[pltpu.SMEM((tm * TOP_K,), I32), pltpu.VMEM((2, tm * SUBLANES, LANES), F32),
                        pltpu.SemaphoreType.DMA, pltpu.SemaphoreType.DMA((2,))],
        input_output_aliases={2: 0},
        compiler_params=_cparams(("arbitrary",)),
        name="moe_dispatch",
    )(pos_flat, h_tiles, xs)


def _moe_kernel(be_ref, nu_ref, xs_ref, wgu_ref, bgu_ref, wd_ref, bd_ref, ys_ref):
    del be_ref
    i = pl.program_id(0)
    ff = wd_ref.shape[1]

    @pl.when(i < nu_ref[0])
    def _():
        x = _load_row_tiles(xs_ref, MOE_BLK).astype(BF16)
        glu = _dot(x, wgu_ref[0, :, 0:ff]) + bgu_ref[0, :, 0:ff]
        lin = _dot(x, wgu_ref[0, :, ff:2 * ff]) + bgu_ref[0, :, ff:2 * ff]
        glu = jnp.minimum(glu, SWIGLU_LIMIT)
        lin = jnp.clip(lin, -SWIGLU_LIMIT, SWIGLU_LIMIT)
        act = glu * jax.nn.sigmoid(SWIGLU_ALPHA * glu) * (lin + 1.0)
        _store_row_tiles(ys_ref, _dot(act.astype(BF16), wd_ref[0]) + bd_ref[0])

    @pl.when(i >= nu_ref[0])
    def _():
        ys_ref[...] = jnp.zeros_like(ys_ref)


def _moe_mlp(block_expert, n_used, xs, w_gu, b_gu, w_down, b_down):
    ne, d, ff2 = w_gu.shape
    ff = ff2 // 2
    tb = MOE_BLK
    n_rows = xs.shape[0] // SUBLANES
    blk = lambda i, be, nu: jnp.minimum(i, nu[0] - 1)
    grid_spec = pltpu.PrefetchScalarGridSpec(
        num_scalar_prefetch=2,
        grid=(n_rows // tb,),
        in_specs=[pl.BlockSpec((tb * SUBLANES, LANES), lambda i, be, nu: (blk(i, be, nu), 0)),
                  pl.BlockSpec((1, d, ff2), lambda i, be, nu: (be[blk(i, be, nu)], 0, 0)),
                  pl.BlockSpec((1, 1, ff2), lambda i, be, nu: (be[blk(i, be, nu)], 0, 0)),
                  pl.BlockSpec((1, ff, d), lambda i, be, nu: (be[blk(i, be, nu)], 0, 0)),
                  pl.BlockSpec((1, 1, d), lambda i, be, nu: (be[blk(i, be, nu)], 0, 0))],
        out_specs=pl.BlockSpec((tb * SUBLANES, LANES), lambda i, be, nu: (i, 0)),
    )
    return pl.pallas_call(
        _moe_kernel,
        grid_spec=grid_spec,
        out_shape=jax.ShapeDtypeStruct(xs.shape, F32),
        compiler_params=_cparams(("arbitrary",)),
        name="moe_mlp",
    )(block_expert, n_used, xs, w_gu, b_gu.reshape(ne, 1, ff2), w_down, b_down.reshape(ne, 1, d))


def _combine_kernel(pos_hbm, ys_hbm, x1_ref, tg_ref, mod_ref, y_ref, pos_smem, buf, pos_sem, row_sem, *, tm, nb):
    i = pl.program_id(0)
    n = pl.num_programs(0)
    slot = lax.rem(i, 2)
    nidx = tm * TOP_K

    def gather(tile, sl):
        cp = pltpu.make_async_copy(pos_hbm.at[pl.ds(pl.multiple_of(tile * nidx, nidx), nidx)],
                                   pos_smem.at[sl], pos_sem)
        cp.start()
        cp.wait()

        def issue(c, carry):
            for rr in range(ISSUE_UNROLL):
                r = c * ISSUE_UNROLL + rr
                for k in range(TOP_K):
                    pltpu.make_async_copy(_row_tile(ys_hbm, pos_smem[sl, TOP_K * r + k]),
                                          _row_tile(buf.at[sl, k], r), row_sem.at[sl]).start()
            return carry
        lax.fori_loop(0, tm // ISSUE_UNROLL, issue, 0)

    @pl.when(i == 0)
    def _():
        gather(0, 0)

    @pl.when(i + 1 < n)
    def _():
        gather(i + 1, 1 - slot)

    for k in range(TOP_K):
        pltpu.make_async_copy(buf.at[slot, k], buf.at[slot, k], row_sem.at[slot]).wait()

    tg = tg_ref[...]
    rows = tm // nb
    for c in range(SUBLANES):
        cols = slice(c * LANES, (c + 1) * LANES)
        ff = tg[:, 0:1] * buf[slot, 0, pl.ds(c, tm, stride=SUBLANES), :]
        for k in range(1, TOP_K):
            ff = ff + tg[:, k:k + 1] * buf[slot, k, pl.ds(c, tm, stride=SUBLANES), :]
        for g in range(nb):
            rs = slice(g * rows, (g + 1) * rows)
            y_ref[rs, cols] = x1_ref[rs, cols] + mod_ref[g, 5:6, cols] * ff[rs, :]


def _combine(pos_flat, ys, x1, top_gate, mod, tokens_per_batch):
    t, d = x1.shape
    tm = GATHER_TILE
    assert t % tm == 0 and tm % ISSUE_UNROLL == 0 and d == SUBLANES * LANES
    if tokens_per_batch >= tm:
        assert tokens_per_batch % tm == 0
        nb = 1
        per = tokens_per_batch // tm
        mod_spec = pl.BlockSpec((1, 6, d), lambda i: (i // per, 0, 0))
    else:
        assert tm % tokens_per_batch == 0
        nb = tm // tokens_per_batch
        mod_spec = pl.BlockSpec((nb, 6, d), lambda i: (i, 0, 0))
    return pl.pallas_call(
        functools.partial(_combine_kernel, tm=tm, nb=nb),
        grid=(t // tm,),
        in_specs=[pl.BlockSpec(memory_space=pl.ANY), pl.BlockSpec(memory_space=pl.ANY),
                  pl.BlockSpec((tm, d), lambda i: (i, 0)), pl.BlockSpec((tm, LANES), lambda i: (i, 0)), mod_spec],
        out_specs=pl.BlockSpec((tm, d), lambda i: (i, 0)),
        out_shape=jax.ShapeDtypeStruct((t, d), F32),
        scratch_shapes=[pltpu.SMEM((2, tm * TOP_K), I32), pltpu.VMEM((2, TOP_K, tm * SUBLANES, LANES), F32),
                        pltpu.SemaphoreType.DMA, pltpu.SemaphoreType.DMA((2,))],
        compiler_params=_cparams(("arbitrary",)),
        name="moe_combine",
    )(pos_flat, ys, x1, top_gate, mod)


def _pad_lanes(a, fill=0.0):
    return jnp.pad(a, [(0, 0)] * (a.ndim - 1) + [(0, LANES - a.shape[-1])], constant_values=fill)


def _hi_lo(w):
    hi = w.astype(BF16)
    return hi, (w - hi.astype(F32)).astype(BF16)


def _mixer(x, mod, k_past, v_past, conv_past, c0, n0, m0, p):
    b, s, d = x.shape
    q_raw, k_raw, v, ml_qk, ml_v, ml_o, gates = _inproj(
        x, mod, p["g1"], p["w_main"], p["wg_hi"], p["wg_lo"], p["bg"], p["widths"])
    o_sb, k_norm = _sb_attention(q_raw, k_raw, v, k_past, v_past, p["g_q"], p["g_k"])
    o_ml, c_new, n_new, m_new, conv_new = _mlstm(
        ml_qk, ml_v, ml_o, gates, conv_past, c0, n0, _pad_lanes(m0)[:, None, :],
        p["w_conv"], p["b_conv"], p["g_head"])
    x1, h2, ti, tg = _outproj(o_sb, o_ml, x, mod, p["w_out1"], p["w_out2"], p["g2"],
                              p["wr_hi"], p["wr_lo"], p["br"])
    nsb = k_norm.shape[-1] // SB_HEAD_DIM
    state = (k_norm.reshape(b, s, nsb, SB_HEAD_DIM), v.reshape(b, s, nsb, SB_HEAD_DIM), conv_new,
             c_new, n_new, m_new[:, 0, :ML_HEADS])
    return x1.reshape(b * s, d), h2, ti.reshape(b * s, LANES), tg.reshape(b * s, LANES), state


def _layer(xp, xs_, cp, cs, kc, vc, convc, cc, nc, mc, w_ada, b_ada, g_norm1, w_in, g_q, g_k, w_conv, b_conv,
           b_gate, g_head, w_out, g_norm2, w_router, b_router, w_gu, b_gu, w_down, b_down):
    d = xp.shape[-1]
    bp, sp = xp.shape[:2]
    bs, ss = xs_.shape[:2]
    sbw = kc.shape[-2] * kc.shape[-1]
    mlw = ML_HEADS * ML_HEAD_DIM
    ngate = 2 * ML_HEADS
    gate0 = 3 * sbw + 3 * mlw
    w_main = jnp.concatenate([w_in[:, :gate0], w_in[:, gate0 + ngate:]], axis=1).astype(BF16)
    wg_hi, wg_lo = _hi_lo(_pad_lanes(w_in[:, gate0:gate0 + ngate]))
    wr_hi, wr_lo = _hi_lo(_pad_lanes(w_router))
    p = dict(
        widths=(sbw, sbw, sbw, 2 * mlw, mlw, mlw),
        g1=g_norm1[None], w_main=w_main, wg_hi=wg_hi, wg_lo=wg_lo, bg=_pad_lanes(b_gate)[None],
        g_q=jnp.tile(g_q, LANES // SB_HEAD_DIM)[None], g_k=jnp.tile(g_k, LANES // SB_HEAD_DIM)[None],
        w_conv=w_conv, b_conv=b_conv[None], g_head=g_head[None],
        w_out1=w_out[:sbw].astype(BF16), w_out2=w_out[sbw:].astype(BF16), g2=g_norm2[None],
        wr_hi=wr_hi, wr_lo=wr_lo, br=_pad_lanes(b_router, -1e30)[None],
    )
    mod = _ada(jnp.concatenate([cp, cs], axis=0), w_ada, b_ada)
    mod_p = mod[:bp].reshape(bp, 6, d)
    mod_s = mod[bp:].reshape(bs, 6, d)

    zeros = lambda *shp: jnp.zeros(shp, F32)
    x1p, h2p, tip, tgp, st_p = _mixer(
        xp, mod_p, None, None, zeros(bp, CONV_W - 1, 2 * mlw), zeros(bp, ML_HEADS, ML_HEAD_DIM, ML_HEAD_DIM),
        zeros(bp, ML_HEADS, ML_HEAD_DIM), zeros(bp, ML_HEADS), p)
    x1s, h2s, tis, tgs, st_s = _mixer(
        xs_, mod_s, kc.reshape(bs, -1, sbw), vc.reshape(bs, -1, sbw), convc, cc, nc, mc, p)

    tp, ts = bp * sp, bs * ss
    ti_all = jnp.concatenate([tip, tis], axis=0)
    rank, counts = _ranks(ti_all)
    counts = counts[0, :N_EXPERTS]
    padded = (counts + MOE_BLK - 1) // MOE_BLK * MOE_BLK
    pad_end = jnp.cumsum(padded)
    pad_start = pad_end - padded
    top_idx = ti_all[:, :TOP_K]
    pos = (pad_start[top_idx] + rank[:, :TOP_K]).astype(I32)
    n_blocks = -(-((tp + ts) * TOP_K + N_EXPERTS * (MOE_BLK - 1)) // MOE_BLK)
    n_used = (pad_end[-1] // MOE_BLK).astype(I32).reshape(1)
    block_start = jnp.arange(n_blocks, dtype=I32) * MOE_BLK
    block_expert = jnp.minimum(jnp.sum(block_start[:, None] >= pad_end[None, :], axis=1), N_EXPERTS - 1).astype(I32)
    pos_p = pos[:tp].reshape(-1)
    pos_s = pos[tp:].reshape(-1)

    xs_buf = jnp.zeros((n_blocks * MOE_BLK * SUBLANES, LANES), F32)
    xs_buf = _dispatch(pos_p, h2p, xs_buf)
    xs_buf = _dispatch(pos_s, h2s, xs_buf)
    ys = _moe_mlp(block_expert, n_used, xs_buf, w_gu.astype(BF16), b_gu, w_down.astype(BF16), b_down)
    yp = _combine(pos_p, ys, x1p, tgp, mod_p, sp).reshape(bp, sp, d)
    ysm = _combine(pos_s, ys, x1s, tgs, mod_s, ss).reshape(bs, ss, d)
    return yp, ysm, st_p, st_s


def kernel(x_prompt, x_sample, c_prompt, c_sample, cache_sb_k, cache_sb_v, state_conv, state_mlstm_c, state_mlstm_n, state_mlstm_m, w_ada, b_ada, g_norm1, w_in, g_q, g_k, w_conv, b_conv, b_gate, g_head, w_out, g_norm2, w_router, b_router, w_gu, b_gu, w_down, b_down):
    assert w_ada.shape[0] == 1, "single-layer step"
    yp, ys, st_p, st_s = _layer(
        x_prompt, x_sample, c_prompt, c_sample, cache_sb_k[0], cache_sb_v[0], state_conv[0], state_mlstm_c[0],
        state_mlstm_n[0], state_mlstm_m[0], w_ada[0], b_ada[0], g_norm1[0], w_in[0], g_q[0], g_k[0], w_conv[0],
        b_conv[0], b_gate[0], g_head[0], w_out[0], g_norm2[0], w_router[0], b_router[0], w_gu[0], b_gu[0],
        w_down[0], b_down[0])
    return (yp, ys) + tuple(a[None] for a in st_p) + tuple(a[None] for a in st_s)
```

```python
import functools
import math

import jax
import jax.numpy as jnp
from jax import lax
from jax.experimental import pallas as pl
from jax.experimental.pallas import tpu as pltpu

F32 = jnp.float32
BF16 = jnp.bfloat16
I32 = jnp.int32

EPS = 1e-6
LANES = 128
SUBLANES = 8
SB_HEAD_DIM = 64
ML_HEAD_DIM = 128
ML_HEADS = 4
CONV_W = 4
N_EXPERTS = 32
TOP_K = 4
SWIGLU_LIMIT = 7.0
SWIGLU_ALPHA = 1.702
VMEM_LIMIT = 56 * 1024 * 1024

ROW_TILE = 512
MOE_BLK = 512
GATHER_TILE = 256
ISSUE_UNROLL = 8
ML_CHUNK = 256
SB_BLK = 128
SB_UNDERFLOW = -105.0


def _cparams(sem):
    return pltpu.CompilerParams(dimension_semantics=sem, vmem_limit_bytes=VMEM_LIMIT)


def _split3(x):
    p1 = x.astype(BF16)
    r1 = x - p1.astype(F32)
    p2 = r1.astype(BF16)
    p3 = (r1 - p2.astype(F32)).astype(BF16)
    return p1, p2, p3


def _dot(a, b):
    return jnp.dot(a, b, preferred_element_type=F32)


def _dot_nt(a, b):
    return lax.dot_general(a, b, (((1,), (1,)), ((), ())), preferred_element_type=F32)


def _dot3(x, w_hi, w_lo):
    xh = x.astype(BF16)
    xl = (x - xh.astype(F32)).astype(BF16)
    return _dot(xh, w_hi) + _dot(xl, w_hi) + _dot(xh, w_lo)


def _store_row_tiles(ref, x):
    n, d = x.shape
    assert d == SUBLANES * LANES
    for c in range(SUBLANES):
        ref[pl.ds(c, n, stride=SUBLANES), :] = x[:, c * LANES:(c + 1) * LANES]


def _load_row_tiles(ref, n):
    return jnp.concatenate([ref[pl.ds(c, n, stride=SUBLANES), :] for c in range(SUBLANES)], axis=1)


def _log_sigmoid(x):
    return jnp.minimum(x, 0.0) - jnp.log1p(jnp.exp(-jnp.abs(x)))


def _ada_kernel(c_ref, w_ref, b_ref, o_ref):
    c = c_ref[...]
    s = c * jax.nn.sigmoid(c)
    o_ref[...] = jnp.dot(s, w_ref[...], precision=lax.Precision.HIGHEST,
                         preferred_element_type=F32) + b_ref[...]


def _ada(c, w_ada, b_ada):
    n, d = c.shape
    cols = w_ada.shape[1]
    tn = cols // 6
    return pl.pallas_call(
        _ada_kernel,
        grid=(cols // tn,),
        in_specs=[pl.BlockSpec((n, d), lambda j: (0, 0)),
                  pl.BlockSpec((d, tn), lambda j: (0, j)),
                  pl.BlockSpec((1, tn), lambda j: (0, j))],
        out_specs=pl.BlockSpec((n, tn), lambda j: (0, j)),
        out_shape=jax.ShapeDtypeStruct((n, cols), F32),
        compiler_params=_cparams(("arbitrary",)),
        name="ada_mod",
    )(c, w_ada, b_ada.reshape(1, cols))


def _inproj_kernel(x_ref, mod_ref, g1_ref, w_ref, wgh_ref, wgl_ref, bg_ref,
                   q_ref, k_ref, v_ref, qk_ref, mv_ref, mo_ref, gate_ref, *, widths):
    x = x_ref[0]
    mod = mod_ref[0]
    sh1 = mod[0:1]
    sc1 = mod[1:2]
    ms = jnp.mean(x * x, axis=-1, keepdims=True)
    h = x * lax.rsqrt(ms + EPS) * g1_ref[...]
    h = h * (1.0 + sc1) + sh1
    hb = h.astype(BF16)
    off = 0
    for ref, wd in zip((q_ref, k_ref, v_ref, qk_ref, mv_ref, mo_ref), widths):
        ref[0] = _dot(hb, w_ref[:, off:off + wd])
        off += wd
    gate_ref[0] = _dot3(h, wgh_ref[...], wgl_ref[...]) + bg_ref[...]


def _inproj(x, mod, g1, w_main, wg_hi, wg_lo, bg, widths):
    b, s, d = x.shape
    tm = min(ROW_TILE, s)
    ncol = w_main.shape[1]
    tok = lambda w: pl.BlockSpec((1, tm, w), lambda bi, i: (bi, i, 0))
    const = lambda shp: pl.BlockSpec(shp, lambda bi, i: (0,) * len(shp))
    return pl.pallas_call(
        functools.partial(_inproj_kernel, widths=widths),
        grid=(b, s // tm),
        in_specs=[tok(d),
                  pl.BlockSpec((1, 6, d), lambda bi, i: (bi, 0, 0)),
                  const((1, d)), const((d, ncol)), const((d, LANES)), const((d, LANES)), const((1, LANES))],
        out_specs=[tok(w) for w in widths] + [tok(LANES)],
        out_shape=[jax.ShapeDtypeStruct((b, s, w), F32) for w in widths]
        + [jax.ShapeDtypeStruct((b, s, LANES), F32)],
        compiler_params=_cparams(("arbitrary", "arbitrary")),
        name="inproj",
    )(x, mod, g1, w_main, wg_hi, wg_lo, bg)


def _sb_kernel(*refs, past, seq, tq, kb, npair):
    if past:
        q_ref, k_ref, v_ref, kp_ref, vp_ref, gq_ref, gk_ref, o_ref, kn_ref, ks_scr, vs_scr, run_scr, acc_scr = refs
    else:
        q_ref, k_ref, v_ref, gq_ref, gk_ref, o_ref, kn_ref, ks_scr, vs_scr, run_scr, acc_scr = refs
    i = pl.program_id(1)
    lane = lax.broadcasted_iota(I32, (1, LANES), 1)
    lo_half = lane < SB_HEAD_DIM
    pairs = [slice(hp * LANES, (hp + 1) * LANES) for hp in range(npair)]

    def headnorm(x, g):
        x2 = x * x
        s0 = jnp.sum(jnp.where(lo_half, x2, 0.0), axis=-1, keepdims=True)
        s1 = jnp.sum(jnp.where(lo_half, 0.0, x2), axis=-1, keepdims=True)
        r = jnp.where(lo_half, lax.rsqrt(s0 * (1.0 / SB_HEAD_DIM) + EPS),
                      lax.rsqrt(s1 * (1.0 / SB_HEAD_DIM) + EPS))
        return x * r * g

    @pl.when(i == 0)
    def _():
        ch = min(256, seq)
        for c in range(seq // ch):
            rows = slice(c * ch, (c + 1) * ch)
            dst = slice(past + c * ch, past + (c + 1) * ch)
            for ps in pairs:
                kn = headnorm(k_ref[0, rows, ps], gk_ref[...])
                kn_ref[0, rows, ps] = kn
                ks_scr[dst, ps] = kn.astype(BF16)
            vs_scr[dst, :] = v_ref[0, rows, :].astype(BF16)
        if past:
            ks_scr[0:past, :] = kp_ref[0].astype(BF16)
            vs_scr[0:past, :] = vp_ref[0].astype(BF16)

    qm = []
    for ps in pairs:
        q = headnorm(q_ref[0, :, ps], gq_ref[...]) * (1.0 / math.sqrt(SB_HEAD_DIM))
        qm.append(jnp.concatenate([jnp.where(lo_half, q, 0.0), jnp.where(lo_half, 0.0, q)], axis=0).astype(BF16))
    nrow = 2 * npair * tq

    def neg_cum(nk):
        rj = lax.broadcasted_iota(I32, (2 * nk, LANES + nk), 0)
        cj = lax.broadcasted_iota(I32, (2 * nk, LANES + nk), 1)
        rjm = jnp.where(rj >= nk, rj - nk, rj)
        return jnp.where(cj < LANES, -1.0, jnp.where(rjm >= cj - LANES, -1.0, 0.0)).astype(BF16)

    w_diag = neg_cum(tq)
    w_full = w_diag if kb == tq else neg_cum(kb)
    causal = (lax.broadcasted_iota(I32, (nrow, tq), 1)
              < jnp.bitwise_and(lax.broadcasted_iota(I32, (nrow, tq), 0), tq - 1))

    def sweep(row0, nk, w_neg, diag):
        kv = [(ks_scr[pl.ds(row0, nk), ps], vs_scr[pl.ds(row0, nk), ps]) for ps in pairs]
        z = jnp.concatenate([_dot_nt(qm[hp], kv[hp][0]) for hp in range(npair)], axis=0)
        sp = jnp.maximum(z, 0.0) + jnp.log(1.0 + jnp.exp(-jnp.abs(z)))
        if diag:
            sp = jnp.where(causal, sp, 0.0)
        hi = sp.astype(BF16)
        lo = (sp - hi.astype(F32)).astype(BF16)
        t = _dot(jnp.concatenate([hi, lo], axis=1), w_neg)
        if diag:
            arg = z + t[:, LANES:]
            run = t[:, :LANES]
        else:
            old = run_scr[...]
            arg = z + t[:, LANES:] + old[:, :nk]
            run = old + t[:, :LANES]
        run_scr[...] = run
        p = jnp.exp(arg)
        if diag:
            p = jnp.where(causal, p, 0.0)
        p = p.astype(BF16)
        for hp in range(npair):
            r0 = 2 * hp * tq
            vblk = kv[hp][1]
            vm = jnp.concatenate([jnp.where(lo_half, vblk, 0.0), jnp.where(lo_half, 0.0, vblk)], axis=0)
            pv = _dot(jnp.concatenate([p[r0:r0 + tq], p[r0 + tq:r0 + 2 * tq]], axis=1), vm.astype(BF16))
            acc_scr[hp] = pv if diag else acc_scr[hp] + pv
        return jnp.max(run)

    q0 = past + i * tq
    worst0 = sweep(pl.multiple_of(q0, tq), tq, w_diag, True)
    n_before = q0 // kb

    def cond(c):
        return jnp.logical_and(c[0] < n_before, c[1] > SB_UNDERFLOW)

    def body(c):
        j = n_before - 1 - c[0]
        return c[0] + 1, sweep(pl.multiple_of(j * kb, kb), kb, w_full, False)

    lax.while_loop(cond, body, (jnp.int32(0), worst0))
    for hp, ps in enumerate(pairs):
        o_ref[0, :, ps] = acc_scr[hp]


def _sb_attention(q_raw, k_raw, v, k_past, v_past, g_q, g_k):
    b, s, w = q_raw.shape
    past = 0 if k_past is None else k_past.shape[1]
    tq = min(SB_BLK, s)
    kb = SB_BLK
    assert s % tq == 0 and past % kb == 0 and w % LANES == 0 and (tq == kb or s == tq)
    npair = w // LANES
    qblk = pl.BlockSpec((1, tq, w), lambda bi, i: (bi, i, 0))
    full = lambda n: pl.BlockSpec((1, n, w), lambda bi, i: (bi, 0, 0))
    gspec = pl.BlockSpec((1, LANES), lambda bi, i: (0, 0))
    in_specs = [qblk, full(s), full(s)]
    args = [q_raw, k_raw, v]
    if past:
        in_specs += [full(past), full(past)]
        args += [k_past, v_past]
    in_specs += [gspec, gspec]
    args += [g_q, g_k]
    return pl.pallas_call(
        functools.partial(_sb_kernel, past=past, seq=s, tq=tq, kb=kb, npair=npair),
        grid=(b, s // tq),
        in_specs=in_specs,
        out_specs=[qblk, full(s)],
        out_shape=[jax.ShapeDtypeStruct((b, s, w), F32), jax.ShapeDtypeStruct((b, s, w), F32)],
        scratch_shapes=[pltpu.VMEM((past + s, w), BF16), pltpu.VMEM((past + s, w), BF16),
                        pltpu.VMEM((2 * npair * tq, LANES), F32), pltpu.VMEM((npair, tq, LANES), F32)],
        compiler_params=_cparams(("arbitrary", "arbitrary")),
        name="sb_attention",
    )(*args)


def _mlstm_kernel(qk_ref, v_ref, og_ref, g_ref, cp_ref, c0_ref, n0_ref, m0_ref, wc_ref, bc_ref, gh_ref,
                  out_ref, cout_ref, nout_ref, mout_ref, convout_ref,
                  prev_scr, c_scr, n_scr, m_scr, *, chunk):
    ci = pl.program_id(1)
    nh, dh = ML_HEADS, ML_HEAD_DIM
    width = nh * dh
    L = chunk

    @pl.when(ci == 0)
    def _():
        prev_scr[...] = jnp.zeros_like(prev_scr)
        prev_scr[8 - (CONV_W - 1):8, :] = cp_ref[0]
        c_scr[...] = c0_ref[0]
        n_scr[...] = n0_ref[0]
        m_scr[...] = m0_ref[0]

    u = qk_ref[0]
    xfull = jnp.concatenate([prev_scr[...], u], axis=0)
    wc = wc_ref[...]
    acc = bc_ref[...] + xfull[8 - (CONV_W - 1):8 - (CONV_W - 1) + L] * wc[0:1]
    for j in range(1, CONV_W):
        o = 8 - (CONV_W - 1) + j
        acc = acc + xfull[o:o + L] * wc[j:j + 1]
    qk = acc * jax.nn.sigmoid(acc)
    prev_scr[...] = u[L - 8:L, :]

    gt = g_ref[0]
    lf = _log_sigmoid(gt)
    gt_t = gt.T
    lf_t = lf.T
    row = lax.broadcasted_iota(I32, (L, L), 0)
    col = lax.broadcasted_iota(I32, (L, L), 1)
    tri = row >= col
    tril = jnp.where(tri, 1.0, 0.0).astype(BF16)
    triu = jnp.where(row <= col, 1.0, 0.0).astype(BF16)
    p1, p2, p3 = _split3(lf)
    b_col = _dot(tril, p1) + _dot(tril, p2) + _dot(tril, p3)
    q1, q2, q3 = _split3(lf_t[0:8, :])
    b_row = _dot(q1, triu) + _dot(q2, triu) + _dot(q3, triu)

    lane = lax.broadcasted_iota(I32, (1, LANES), 1)
    m_vec = m_scr[...]
    m_out = m_vec
    for h in range(nh):
        hs = slice(h * dh, (h + 1) * dh)
        qh = qk[:, h * dh:(h + 1) * dh]
        kh = qk[:, width + h * dh:width + (h + 1) * dh] * (dh ** -0.5)
        vh = v_ref[0, :, hs]
        b_c = b_col[:, nh + h:nh + h + 1]
        ig_c = gt[:, h:h + 1]
        b_r = b_row[nh + h:nh + h + 1, :]
        ig_r = gt_t[h:h + 1, :]
        m_prev = m_vec[:, h:h + 1]

        d = jnp.where(tri, (b_c - b_r) + ig_r, -jnp.inf)
        inter = b_c + m_prev
        m_t = jnp.maximum(inter, jnp.max(d, axis=1, keepdims=True))
        w_intra = jnp.exp(d - m_t)
        w_inter = jnp.exp(inter - m_t)
        qb = qh.astype(BF16)
        kb = kh.astype(BF16)
        s = _dot_nt(qb, kb) * w_intra
        c_h = c_scr[h]
        n_h = n_scr[h:h + 1, :]
        num = _dot(s.astype(BF16), vh.astype(BF16)) + w_inter * _dot_nt(qb, c_h.astype(BF16))
        den = jnp.sum(s, axis=1, keepdims=True) + w_inter * jnp.sum(qh * n_h, axis=1, keepdims=True)
        hh = num / jnp.maximum(jnp.abs(den), jnp.exp(-m_t))

        b_last = b_c[L - 1:L, :]
        g_c = (b_last - b_c) + ig_c
        m_new = jnp.maximum(b_last + m_prev, jnp.max(g_c, axis=0, keepdims=True))
        w_state = jnp.exp(g_c - m_new)
        decay = jnp.exp(b_last + m_prev - m_new)
        vw_t = (vh * w_state).T.astype(BF16)
        c_scr[h] = decay * c_h + _dot(vw_t, kb)
        n_scr[h:h + 1, :] = decay * n_h + jnp.sum(kh * w_state, axis=0, keepdims=True)
        m_out = jnp.where(lane == h, m_new, m_out)

        hn = hh * lax.rsqrt(jnp.mean(hh * hh, axis=-1, keepdims=True) + EPS) * gh_ref[...]
        out_ref[0, :, hs] = jax.nn.sigmoid(og_ref[0, :, hs]) * hn
    m_scr[...] = m_out

    @pl.when(ci == pl.num_programs(1) - 1)
    def _():
        cout_ref[0] = c_scr[...]
        nout_ref[0] = n_scr[...]
        mout_ref[0] = m_scr[...]
        convout_ref[0] = prev_scr[8 - (CONV_W - 1):8, :]


def _mlstm(ml_qk, ml_v, ml_o, gates, conv_past, c0, n0, m0, w_conv, b_conv, g_head):
    b, s, w2 = ml_qk.shape
    width = w2 // 2
    nh, dh = ML_HEADS, ML_HEAD_DIM
    chunk = min(ML_CHUNK, s)
    assert s % chunk == 0 and chunk >= 8
    tok = lambda w: pl.BlockSpec((1, chunk, w), lambda bi, ci: (bi, ci, 0))
    perb = lambda shp: pl.BlockSpec((1,) + shp, lambda bi, ci: (bi,) + (0,) * len(shp))
    const = lambda shp: pl.BlockSpec(shp, lambda bi, ci: (0,) * len(shp))
    return pl.pallas_call(
        functools.partial(_mlstm_kernel, chunk=chunk),
        grid=(b, s // chunk),
        in_specs=[tok(w2), tok(width), tok(width), tok(LANES),
                  perb((CONV_W - 1, w2)), perb((nh, dh, dh)), perb((nh, dh)), perb((1, LANES)),
                  const((CONV_W, w2)), const((1, w2)), const((1, dh))],
        out_specs=[tok(width), perb((nh, dh, dh)), perb((nh, dh)), perb((1, LANES)), perb((CONV_W - 1, w2))],
        out_shape=[jax.ShapeDtypeStruct((b, s, width), F32),
                   jax.ShapeDtypeStruct((b, nh, dh, dh), F32),
                   jax.ShapeDtypeStruct((b, nh, dh), F32),
                   jax.ShapeDtypeStruct((b, 1, LANES), F32),
                   jax.ShapeDtypeStruct((b, CONV_W - 1, w2), F32)],
        scratch_shapes=[pltpu.VMEM((8, w2), F32), pltpu.VMEM((nh, dh, dh), F32),
                        pltpu.VMEM((nh, dh), F32), pltpu.VMEM((1, LANES), F32)],
        compiler_params=_cparams(("arbitrary", "arbitrary")),
        name="mlstm",
    )(ml_qk, ml_v, ml_o, gates, conv_past, c0, n0, m0, w_conv, b_conv, g_head)


def _outproj_kernel(osb_ref, oml_ref, x_ref, mod_ref, w1_ref, w2_ref, g2_ref, wrh_ref, wrl_ref, br_ref,
                    x1_ref, h2_ref, ti_ref, tg_ref):
    mod = mod_ref[0]
    ga1 = mod[2:3]
    sh2 = mod[3:4]
    sc2 = mod[4:5]
    mix = _dot(osb_ref[0].astype(BF16), w1_ref[...]) + _dot(oml_ref[0].astype(BF16), w2_ref[...])
    x1 = x_ref[0] + ga1 * mix
    x1_ref[0] = x1
    ms = jnp.mean(x1 * x1, axis=-1, keepdims=True)
    h2 = x1 * lax.rsqrt(ms + EPS) * g2_ref[...]
    h2 = h2 * (1.0 + sc2) + sh2
    _store_row_tiles(h2_ref, h2)
    logits = _dot3(h2, wrh_ref[...], wrl_ref[...]) + br_ref[...]
    lane = lax.broadcasted_iota(I32, logits.shape, 1)
    vals, idxs = [], []
    cur = logits
    for _ in range(TOP_K):
        m = jnp.max(cur, axis=-1, keepdims=True)
        idx = jnp.min(jnp.where(cur == m, lane, LANES), axis=-1, keepdims=True)
        vals.append(m)
        idxs.append(idx)
        cur = jnp.where(lane == idx, -jnp.inf, cur)
    es = [jnp.exp(v - vals[0]) for v in vals]
    tot = es[0] + es[1] + es[2] + es[3]
    ti = jnp.zeros(logits.shape, I32)
    tg = jnp.zeros(logits.shape, F32)
    for k in range(TOP_K):
        ti = jnp.where(lane == k, idxs[k], ti)
        tg = jnp.where(lane == k, es[k] / tot, tg)
    ti_ref[0] = ti
    tg_ref[0] = tg


def _outproj(o_sb, o_ml, x, mod, w1, w2, g2, wr_hi, wr_lo, br):
    b, s, d = x.shape
    hw = o_sb.shape[2]
    tm = min(ROW_TILE, s)
    tok = lambda w: pl.BlockSpec((1, tm, w), lambda bi, i: (bi, i, 0))
    const = lambda shp: pl.BlockSpec(shp, lambda bi, i: (0,) * len(shp))
    return pl.pallas_call(
        _outproj_kernel,
        grid=(b, s // tm),
        in_specs=[tok(hw), tok(hw), tok(d), pl.BlockSpec((1, 6, d), lambda bi, i: (bi, 0, 0)),
                  const((hw, d)), const((hw, d)), const((1, d)),
                  const((d, LANES)), const((d, LANES)), const((1, LANES))],
        out_specs=[tok(d), pl.BlockSpec((tm * SUBLANES, LANES), lambda bi, i: (bi * (s // tm) + i, 0)),
                   tok(LANES), tok(LANES)],
        out_shape=[jax.ShapeDtypeStruct((b, s, d), F32), jax.ShapeDtypeStruct((b * s * SUBLANES, LANES), F32),
                   jax.ShapeDtypeStruct((b, s, LANES), I32), jax.ShapeDtypeStruct((b, s, LANES), F32)],
        compiler_params=_cparams(("arbitrary", "arbitrary")),
        name="outproj_router",
    )(o_sb, o_ml, x, mod, w1, w2, g2, wr_hi, wr_lo, br)


def _rank_kernel(ti_ref, rank_ref, cnt_ref, carry_scr):
    i = pl.program_id(0)

    @pl.when(i == 0)
    def _():
        carry_scr[...] = jnp.zeros_like(carry_scr)

    ti = ti_ref[...]
    tm = ti.shape[0]
    lane = lax.broadcasted_iota(I32, ti.shape, 1)
    hit = jnp.zeros(ti.shape, F32)
    for k in range(TOP_K):
        hit = hit + jnp.where(lane == ti[:, k:k + 1], 1.0, 0.0)
    row = lax.broadcasted_iota(I32, (tm, tm), 0)
    col = lax.broadcasted_iota(I32, (tm, tm), 1)
    strict = jnp.where(col < row, 1.0, 0.0).astype(BF16)
    excl = _dot(strict, hit.astype(BF16)) + carry_scr[...]
    rank = jnp.zeros(ti.shape, F32)
    for k in range(TOP_K):
        rk = jnp.sum(jnp.where(lane == ti[:, k:k + 1], excl, 0.0), axis=-1, keepdims=True)
        rank = jnp.where(lane == k, rk, rank)
    rank_ref[...] = rank.astype(I32)
    carry_scr[...] = carry_scr[...] + jnp.sum(hit, axis=0, keepdims=True)
    cnt_ref[...] = carry_scr[...].astype(I32)


def _ranks(top_idx):
    t = top_idx.shape[0]
    tm = ROW_TILE
    assert t % tm == 0
    return pl.pallas_call(
        _rank_kernel,
        grid=(t // tm,),
        in_specs=[pl.BlockSpec((tm, LANES), lambda i: (i, 0))],
        out_specs=[pl.BlockSpec((tm, LANES), lambda i: (i, 0)), pl.BlockSpec((1, LANES), lambda i: (0, 0))],
        out_shape=[jax.ShapeDtypeStruct((t, LANES), I32), jax.ShapeDtypeStruct((1, LANES), I32)],
        scratch_shapes=[pltpu.VMEM((1, LANES), F32)],
        compiler_params=_cparams(("arbitrary",)),
        name="expert_ranks",
    )(top_idx)


def _row_tile(ref, r):
    return ref.at[pl.ds(pl.multiple_of(r * SUBLANES, SUBLANES), SUBLANES), :]


def _dispatch_kernel(*refs, tm):
    pos_hbm, h_ref = refs[:2]
    xs_hbm, pos_smem, stage, pos_sem, row_sem = refs[-5:]
    i = pl.program_id(0)
    n = pl.num_programs(0)
    slot = lax.rem(i, 2)
    nidx = tm * TOP_K

    def drain(sl):
        for _ in range(TOP_K):
            pltpu.make_async_copy(stage.at[sl], stage.at[sl], row_sem.at[sl]).wait()

    cp = pltpu.make_async_copy(pos_hbm.at[pl.ds(pl.multiple_of(i * nidx, nidx), nidx)], pos_smem, pos_sem)
    cp.start()

    @pl.when(i >= 2)
    def _():
        drain(slot)

    stage[slot] = h_ref[...]
    cp.wait()

    def issue(c, carry):
        for rr in range(ISSUE_UNROLL):
            r = c * ISSUE_UNROLL + rr
            for k in range(TOP_K):
                pltpu.make_async_copy(_row_tile(stage.at[slot], r), _row_tile(xs_hbm, pos_smem[TOP_K * r + k]),
                                      row_sem.at[slot]).start()
        return carry
    lax.fori_loop(0, tm // ISSUE_UNROLL, issue, 0)

    @pl.when(i == n - 1)
    def _():
        drain(slot)

        @pl.when(n >= 2)
        def _():
            drain(1 - slot)


def _dispatch(pos_flat, h_tiles, xs=None, n_rows=None):
    t = h_tiles.shape[0] // SUBLANES
    tm = GATHER_TILE
    assert t % tm == 0 and tm % ISSUE_UNROLL == 0
    in_specs = [pl.BlockSpec(memory_space=pl.ANY), pl.BlockSpec((tm * SUBLANES, LANES), lambda i: (i, 0))]
    args = [pos_flat, h_tiles]
    if xs is not None:
        in_specs.append(pl.BlockSpec(memory_space=pl.ANY))
        args.append(xs)
    return pl.pallas_call(
        functools.partial(_dispatch_kernel, tm=tm),
        grid=(t // tm,),
        in_specs=in_specs,
        out_specs=pl.BlockSpec(memory_space=pl.ANY),
        out_shape=jax.ShapeDtypeStruct((n_rows * SUBLANES, LANES) if xs is None else xs.shape, F32),
        scratch_shapes=[pltpu.SMEM((tm * TOP_K,), I32), pltpu.VMEM((2, tm * SUBLANES, LANES), F32),
                        pltpu.SemaphoreType.DMA, pltpu.SemaphoreType.DMA((2,))],
        input_output_aliases={} if xs is None else {2: 0},
        compiler_params=_cparams(("arbitrary",)),
        name="moe_dispatch",
    )(*args)


def _pad_fill_kernel(pos_hbm, xs_in_hbm, xs_hbm, pos_smem, zero_tile, pos_sem, row_sem, *, nidx):
    del xs_in_hbm
    i = pl.program_id(0)
    cp = pltpu.make_async_copy(pos_hbm.at[pl.ds(pl.multiple_of(i * nidx, nidx), nidx)], pos_smem, pos_sem)
    cp.start()
    zero_tile[...] = jnp.zeros_like(zero_tile)
    cp.wait()

    def issue(c, carry):
        for rr in range(TOP_K * ISSUE_UNROLL):
            r = c * (TOP_K * ISSUE_UNROLL) + rr
            pltpu.make_async_copy(zero_tile, _row_tile(xs_hbm, pos_smem[r]), row_sem).start()
        return carry
    lax.fori_loop(0, nidx // (TOP_K * ISSUE_UNROLL), issue, 0)
    whole = xs_hbm.at[pl.ds(0, nidx * SUBLANES), :]
    pltpu.make_async_copy(whole, whole, row_sem).wait()


def _pad_fill(pad_pos, xs):
    nidx = GATHER_TILE * TOP_K
    assert pad_pos.shape[0] % nidx == 0
    return pl.pallas_call(
        functools.partial(_pad_fill_kernel, nidx=nidx),
        grid=(pad_pos.shape[0] // nidx,),
        in_specs=[pl.BlockSpec(memory_space=pl.ANY), pl.BlockSpec(memory_space=pl.ANY)],
        out_specs=pl.BlockSpec(memory_space=pl.ANY),
        out_shape=jax.ShapeDtypeStruct(xs.shape, xs.dtype),
        scratch_shapes=[pltpu.SMEM((nidx,), I32), pltpu.VMEM((SUBLANES, LANES), F32),
                        pltpu.SemaphoreType.DMA, pltpu.SemaphoreType.DMA],
        input_output_aliases={1: 0},
        compiler_params=_cparams(("arbitrary",)),
        name="moe_pad_fill",
    )(pad_pos, xs)


def _moe_kernel(be_ref, nu_ref, xs_ref, wgu_ref, bgu_ref, wd_ref, bd_ref, ys_ref, wgu_bf, wd_bf):
    i = pl.program_id(0)
    d, ff = wgu_ref.shape[1], wd_ref.shape[1]

    @pl.when(jnp.logical_and(i < nu_ref[0],
                             jnp.logical_or(i == 0, be_ref[i] != be_ref[jnp.maximum(i - 1, 0)])))
    def _():
        for r in range(0, d, LANES):
            wgu_bf[r:r + LANES, :] = wgu_ref[0, r:r + LANES, :].astype(BF16)
        for r in range(0, ff, LANES):
            wd_bf[r:r + LANES, :] = wd_ref[0, r:r + LANES, :].astype(BF16)

    @pl.when(i < nu_ref[0])
    def _():
        x = _load_row_tiles(xs_ref, MOE_BLK).astype(BF16)
        glu = _dot(x, wgu_bf[:, 0:ff]) + bgu_ref[0, :, 0:ff]
        lin = _dot(x, wgu_bf[:, ff:2 * ff]) + bgu_ref[0, :, ff:2 * ff]
        glu = jnp.minimum(glu, SWIGLU_LIMIT)
        lin = jnp.clip(lin, -SWIGLU_LIMIT, SWIGLU_LIMIT)
        act = glu * jax.nn.sigmoid(SWIGLU_ALPHA * glu) * (lin + 1.0)
        _store_row_tiles(ys_ref, _dot(act.astype(BF16), wd_bf[...]) + bd_ref[0])

    @pl.when(i >= nu_ref[0])
    def _():
        ys_ref[...] = jnp.zeros_like(ys_ref)


def _moe_mlp(block_expert, n_used, xs, w_gu, b_gu, w_down, b_down):
    ne, d, ff2 = w_gu.shape
    ff = ff2 // 2
    tb = MOE_BLK
    n_rows = block_expert.shape[0] * tb
    assert xs.shape[0] >= n_rows * SUBLANES
    blk = lambda i, be, nu: jnp.minimum(i, nu[0] - 1)
    grid_spec = pltpu.PrefetchScalarGridSpec(
        num_scalar_prefetch=2,
        grid=(n_rows // tb,),
        in_specs=[pl.BlockSpec((tb * SUBLANES, LANES), lambda i, be, nu: (blk(i, be, nu), 0)),
                  pl.BlockSpec((1, d, ff2), lambda i, be, nu: (be[blk(i, be, nu)], 0, 0)),
                  pl.BlockSpec((1, 1, ff2), lambda i, be, nu: (be[blk(i, be, nu)], 0, 0)),
                  pl.BlockSpec((1, ff, d), lambda i, be, nu: (be[blk(i, be, nu)], 0, 0)),
                  pl.BlockSpec((1, 1, d), lambda i, be, nu: (be[blk(i, be, nu)], 0, 0))],
        out_specs=pl.BlockSpec((tb * SUBLANES, LANES), lambda i, be, nu: (i, 0)),
        scratch_shapes=[pltpu.VMEM((d, ff2), BF16), pltpu.VMEM((ff, d), BF16)],
    )
    return pl.pallas_call(
        _moe_kernel,
        grid_spec=grid_spec,
        out_shape=jax.ShapeDtypeStruct((n_rows * SUBLANES, LANES), F32),
        compiler_params=_cparams(("arbitrary",)),
        name="moe_mlp",
    )(block_expert, n_used, xs, w_gu, b_gu.reshape(ne, 1, ff2), w_down, b_down.reshape(ne, 1, d))


def _combine_kernel(pos_hbm, ys_hbm, x1_ref, tg_ref, mod_ref, y_ref, pos_smem0, pos_smem1, buf, pos_sem, row_sem,
                    *, tm, nb):
    i = pl.program_id(0)
    n = pl.num_programs(0)
    slot = lax.rem(i, 2)
    nidx = tm * TOP_K

    def gather(tile, sl):
        pos_smem = (pos_smem0, pos_smem1)[sl]
        cp = pltpu.make_async_copy(pos_hbm.at[pl.ds(pl.multiple_of(tile * nidx, nidx), nidx)], pos_smem, pos_sem)
        cp.start()
        cp.wait()

        def issue(c, carry):
            for rr in range(ISSUE_UNROLL):
                r = c * ISSUE_UNROLL + rr
                for k in range(TOP_K):
                    pltpu.make_async_copy(_row_tile(ys_hbm, pos_smem[TOP_K * r + k]),
                                          _row_tile(buf.at[sl, k], r), row_sem.at[sl]).start()
            return carry
        lax.fori_loop(0, tm // ISSUE_UNROLL, issue, 0)

    @pl.when(i == 0)
    def _():
        gather(0, 0)

    for sl in range(2):
        @pl.when(jnp.logical_and(i + 1 < n, slot == 1 - sl))
        def _():
            gather(i + 1, sl)

    for k in range(TOP_K):
        pltpu.make_async_copy(buf.at[slot, k], buf.at[slot, k], row_sem.at[slot]).wait()

    tg = tg_ref[...]
    rows = tm // nb
    for c in range(SUBLANES):
        cols = slice(c * LANES, (c + 1) * LANES)
        ff = tg[:, 0:1] * buf[slot, 0, pl.ds(c, tm, stride=SUBLANES), :]
        for k in range(1, TOP_K):
            ff = ff + tg[:, k:k + 1] * buf[slot, k, pl.ds(c, tm, stride=SUBLANES), :]
        for g in range(nb):
            rs = slice(g * rows, (g + 1) * rows)
            y_ref[rs, cols] = x1_ref[rs, cols] + mod_ref[g, 5:6, cols] * ff[rs, :]


def _combine(pos_flat, ys, x1, top_gate, mod, tokens_per_batch):
    t, d = x1.shape
    tm = GATHER_TILE
    assert t % tm == 0 and tm % ISSUE_UNROLL == 0 and d == SUBLANES * LANES
    if tokens_per_batch >= tm:
        assert tokens_per_batch % tm == 0
        nb = 1
        per = tokens_per_batch // tm
        mod_spec = pl.BlockSpec((1, 6, d), lambda i: (i // per, 0, 0))
    else:
        assert tm % tokens_per_batch == 0
        nb = tm // tokens_per_batch
        mod_spec = pl.BlockSpec((nb, 6, d), lambda i: (i, 0, 0))
    return pl.pallas_call(
        functools.partial(_combine_kernel, tm=tm, nb=nb),
        grid=(t // tm,),
        in_specs=[pl.BlockSpec(memory_space=pl.ANY), pl.BlockSpec(memory_space=pl.ANY),
                  pl.BlockSpec((tm, d), lambda i: (i, 0)), pl.BlockSpec((tm, LANES), lambda i: (i, 0)), mod_spec],
        out_specs=pl.BlockSpec((tm, d), lambda i: (i, 0)),
        out_shape=jax.ShapeDtypeStruct((t, d), F32),
        scratch_shapes=[pltpu.SMEM((tm * TOP_K,), I32), pltpu.SMEM((tm * TOP_K,), I32),
                        pltpu.VMEM((2, TOP_K, tm * SUBLANES, LANES), F32),
                        pltpu.SemaphoreType.DMA, pltpu.SemaphoreType.DMA((2,))],
        compiler_params=_cparams(("arbitrary",)),
        name="moe_combine",
    )(pos_flat, ys, x1, top_gate, mod)


def _pad_lanes(a, fill=0.0):
    return jnp.pad(a, [(0, 0)] * (a.ndim - 1) + [(0, LANES - a.shape[-1])], constant_values=fill)


def _hi_lo(w):
    hi = w.astype(BF16)
    return hi, (w - hi.astype(F32)).astype(BF16)


def _mixer(x, mod, k_past, v_past, conv_past, c0, n0, m0, p):
    b, s, d = x.shape
    q_raw, k_raw, v, ml_qk, ml_v, ml_o, gates = _inproj(
        x, mod, p["g1"], p["w_main"], p["wg_hi"], p["wg_lo"], p["bg"], p["widths"])
    o_sb, k_norm = _sb_attention(q_raw, k_raw, v, k_past, v_past, p["g_q"], p["g_k"])
    o_ml, c_new, n_new, m_new, conv_new = _mlstm(
        ml_qk, ml_v, ml_o, gates, conv_past, c0, n0, _pad_lanes(m0)[:, None, :],
        p["w_conv"], p["b_conv"], p["g_head"])
    x1, h2, ti, tg = _outproj(o_sb, o_ml, x, mod, p["w_out1"], p["w_out2"], p["g2"],
                              p["wr_hi"], p["wr_lo"], p["br"])
    nsb = k_norm.shape[-1] // SB_HEAD_DIM
    state = (k_norm.reshape(b, s, nsb, SB_HEAD_DIM), v.reshape(b, s, nsb, SB_HEAD_DIM), conv_new,
             c_new, n_new, m_new[:, 0, :ML_HEADS])
    return x1.reshape(b * s, d), h2, ti.reshape(b * s, LANES), tg.reshape(b * s, LANES), state


def _layer(xp, xs_, cp, cs, kc, vc, convc, cc, nc, mc, w_ada, b_ada, g_norm1, w_in, g_q, g_k, w_conv, b_conv,
           b_gate, g_head, w_out, g_norm2, w_router, b_router, w_gu, b_gu, w_down, b_down):
    d = xp.shape[-1]
    bp, sp = xp.shape[:2]
    bs, ss = xs_.shape[:2]
    sbw = kc.shape[-2] * kc.shape[-1]
    mlw = ML_HEADS * ML_HEAD_DIM
    ngate = 2 * ML_HEADS
    gate0 = 3 * sbw + 3 * mlw
    w_main = jnp.concatenate([w_in[:, :gate0], w_in[:, gate0 + ngate:]], axis=1).astype(BF16)
    wg_hi, wg_lo = _hi_lo(_pad_lanes(w_in[:, gate0:gate0 + ngate]))
    wr_hi, wr_lo = _hi_lo(_pad_lanes(w_router))
    p = dict(
        widths=(sbw, sbw, sbw, 2 * mlw, mlw, mlw),
        g1=g_norm1[None], w_main=w_main, wg_hi=wg_hi, wg_lo=wg_lo, bg=_pad_lanes(b_gate)[None],
        g_q=jnp.tile(g_q, LANES // SB_HEAD_DIM)[None], g_k=jnp.tile(g_k, LANES // SB_HEAD_DIM)[None],
        w_conv=w_conv, b_conv=b_conv[None], g_head=g_head[None],
        w_out1=w_out[:sbw].astype(BF16), w_out2=w_out[sbw:].astype(BF16), g2=g_norm2[None],
        wr_hi=wr_hi, wr_lo=wr_lo, br=_pad_lanes(b_router, -1e30)[None],
    )
    mod = _ada(jnp.concatenate([cp, cs], axis=0), w_ada, b_ada)
    mod_p = mod[:bp].reshape(bp, 6, d)
    mod_s = mod[bp:].reshape(bs, 6, d)

    zeros = lambda *shp: jnp.zeros(shp, F32)
    x1p, h2p, tip, tgp, st_p = _mixer(
        xp, mod_p, None, None, zeros(bp, CONV_W - 1, 2 * mlw), zeros(bp, ML_HEADS, ML_HEAD_DIM, ML_HEAD_DIM),
        zeros(bp, ML_HEADS, ML_HEAD_DIM), zeros(bp, ML_HEADS), p)
    x1s, h2s, tis, tgs, st_s = _mixer(
        xs_, mod_s, kc.reshape(bs, -1, sbw), vc.reshape(bs, -1, sbw), convc, cc, nc, mc, p)

    tp, ts = bp * sp, bs * ss
    ti_all = jnp.concatenate([tip, tis], axis=0)
    rank, counts = _ranks(ti_all)
    counts = counts[0, :N_EXPERTS]
    padded = (counts + MOE_BLK - 1) // MOE_BLK * MOE_BLK
    pad_end = jnp.cumsum(padded)
    pad_start = pad_end - padded
    top_idx = ti_all[:, :TOP_K]
    pos = (pad_start[top_idx] + rank[:, :TOP_K]).astype(I32)
    n_blocks = -(-((tp + ts) * TOP_K + N_EXPERTS * (MOE_BLK - 1)) // MOE_BLK)
    n_used = (pad_end[-1] // MOE_BLK).astype(I32).reshape(1)
    block_start = jnp.arange(n_blocks, dtype=I32) * MOE_BLK
    block_expert = jnp.minimum(jnp.sum(block_start[:, None] >= pad_end[None, :], axis=1), N_EXPERTS - 1).astype(I32)
    pos_p = pos[:tp].reshape(-1)
    pos_s = pos[tp:].reshape(-1)

    n_rows = n_blocks * MOE_BLK
    slot = jnp.arange(MOE_BLK, dtype=I32)[None, :]
    spare = n_rows + jnp.arange(N_EXPERTS * MOE_BLK, dtype=I32).reshape(N_EXPERTS, MOE_BLK)
    pad_pos = jnp.where(slot < (padded - counts)[:, None], (pad_start + counts)[:, None] + slot, spare).reshape(-1)

    xs_buf = _dispatch(pos_p, h2p, n_rows=n_rows + N_EXPERTS * MOE_BLK)
    xs_buf = _dispatch(pos_s, h2s, xs_buf)
    xs_buf = _pad_fill(pad_pos.astype(I32), xs_buf)
    ys = _moe_mlp(block_expert, n_used, xs_buf, w_gu, b_gu, w_down, b_down)
    yp = _combine(pos_p, ys, x1p, tgp, mod_p, sp).reshape(bp, sp, d)
    ysm = _combine(pos_s, ys, x1s, tgs, mod_s, ss).reshape(bs, ss, d)
    return yp, ysm, st_p, st_s


def kernel(x_prompt, x_sample, c_prompt, c_sample, cache_sb_k, cache_sb_v, state_conv, state_mlstm_c, state_mlstm_n, state_mlstm_m, w_ada, b_ada, g_norm1, w_in, g_q, g_k, w_conv, b_conv, b_gate, g_head, w_out, g_norm2, w_router, b_router, w_gu, b_gu, w_down, b_down):
    assert w_ada.shape[0] == 1, "single-layer step"
    yp, ys, st_p, st_s = _layer(
        x_prompt, x_sample, c_prompt, c_sample, cache_sb_k[0], cache_sb_v[0], state_conv[0], state_mlstm_c[0],
        state_mlstm_n[0], state_mlstm_m[0], w_ada[0], b_ada[0], g_norm1[0], w_in[0], g_q[0], g_k[0], w_conv[0],
        b_conv[0], b_gate[0], g_head[0], w_out[0], g_norm2[0], w_router[0], b_router[0], w_gu[0], b_gu[0],
        w_down[0], b_down[0])
    return (yp, ys) + tuple(a[None] for a in st_p) + tuple(a[None] for a in st_s)
```

```python
import functools
import math

import jax
import jax.numpy as jnp
from jax import lax
from jax.experimental import pallas as pl
from jax.experimental.pallas import tpu as pltpu

F32 = jnp.float32
BF16 = jnp.bfloat16
I32 = jnp.int32

EPS = 1e-6
LANES = 128
SUBLANES = 8
SB_HEAD_DIM = 64
ML_HEAD_DIM = 128
ML_HEADS = 4
CONV_W = 4
N_EXPERTS = 32
TOP_K = 4
SWIGLU_LIMIT = 7.0
SWIGLU_ALPHA = 1.702
VMEM_LIMIT = 56 * 1024 * 1024

ROW_TILE = 512
MOE_BLK = 512
GATHER_TILE = 256
ISSUE_UNROLL = 8
DMA_QUEUES = 2
ML_CHUNK = 256
SB_BLK = 128
SB_SWEEP = 256
SB_UNDERFLOW = -105.0


def _cparams(sem):
    return pltpu.CompilerParams(dimension_semantics=sem, vmem_limit_bytes=VMEM_LIMIT)


def _split3(x):
    p1 = x.astype(BF16)
    r1 = x - p1.astype(F32)
    p2 = r1.astype(BF16)
    p3 = (r1 - p2.astype(F32)).astype(BF16)
    return p1, p2, p3


def _dot(a, b):
    return jnp.dot(a, b, preferred_element_type=F32)


def _dot_nt(a, b):
    return lax.dot_general(a, b, (((1,), (1,)), ((), ())), preferred_element_type=F32)


def _dot3(x, w_hi, w_lo):
    xh = x.astype(BF16)
    xl = (x - xh.astype(F32)).astype(BF16)
    return _dot(xh, w_hi) + _dot(xl, w_hi) + _dot(xh, w_lo)


def _store_row_tiles(ref, x):
    n, d = x.shape
    assert d == SUBLANES * LANES
    for c in range(SUBLANES):
        ref[pl.ds(c, n, stride=SUBLANES), :] = x[:, c * LANES:(c + 1) * LANES]


def _load_row_tiles(ref, n):
    return jnp.concatenate([ref[pl.ds(c, n, stride=SUBLANES), :] for c in range(SUBLANES)], axis=1)


def _log_sigmoid(x):
    return jnp.minimum(x, 0.0) - jnp.log1p(jnp.exp(-jnp.abs(x)))


def _ada_kernel(c_ref, w_ref, b_ref, o_ref):
    c = c_ref[...]
    s = c * jax.nn.sigmoid(c)
    o_ref[...] = jnp.dot(s, w_ref[...], precision=lax.Precision.HIGHEST,
                         preferred_element_type=F32) + b_ref[...]


def _ada(c, w_ada, b_ada):
    n, d = c.shape
    cols = w_ada.shape[1]
    tn = cols // 6
    return pl.pallas_call(
        _ada_kernel,
        grid=(cols // tn,),
        in_specs=[pl.BlockSpec((n, d), lambda j: (0, 0)),
                  pl.BlockSpec((d, tn), lambda j: (0, j)),
                  pl.BlockSpec((1, tn), lambda j: (0, j))],
        out_specs=pl.BlockSpec((n, tn), lambda j: (0, j)),
        out_shape=jax.ShapeDtypeStruct((n, cols), F32),
        compiler_params=_cparams(("arbitrary",)),
        name="ada_mod",
    )(c, w_ada, b_ada.reshape(1, cols))


def _inproj_kernel(x_ref, mod_ref, g1_ref, w_ref, wgh_ref, wgl_ref, bg_ref,
                   q_ref, k_ref, v_ref, qk_ref, mv_ref, mo_ref, gate_ref, *, widths):
    x = x_ref[0]
    mod = mod_ref[0]
    sh1 = mod[0:1]
    sc1 = mod[1:2]
    ms = jnp.mean(x * x, axis=-1, keepdims=True)
    h = x * lax.rsqrt(ms + EPS) * g1_ref[...]
    h = h * (1.0 + sc1) + sh1
    hb = h.astype(BF16)
    off = 0
    for ref, wd in zip((q_ref, k_ref, v_ref, qk_ref, mv_ref, mo_ref), widths):
        ref[0] = _dot(hb, w_ref[:, off:off + wd])
        off += wd
    gate_ref[0] = _dot3(h, wgh_ref[...], wgl_ref[...]) + bg_ref[...]


def _inproj(x, mod, g1, w_main, wg_hi, wg_lo, bg, widths):
    b, s, d = x.shape
    tm = min(ROW_TILE, s)
    ncol = w_main.shape[1]
    tok = lambda w: pl.BlockSpec((1, tm, w), lambda bi, i: (bi, i, 0))
    const = lambda shp: pl.BlockSpec(shp, lambda bi, i: (0,) * len(shp))
    return pl.pallas_call(
        functools.partial(_inproj_kernel, widths=widths),
        grid=(b, s // tm),
        in_specs=[tok(d),
                  pl.BlockSpec((1, 6, d), lambda bi, i: (bi, 0, 0)),
                  const((1, d)), const((d, ncol)), const((d, LANES)), const((d, LANES)), const((1, LANES))],
        out_specs=[tok(w) for w in widths] + [tok(LANES)],
        out_shape=[jax.ShapeDtypeStruct((b, s, w), F32) for w in widths]
        + [jax.ShapeDtypeStruct((b, s, LANES), F32)],
        compiler_params=_cparams(("arbitrary", "arbitrary")),
        name="inproj",
    )(x, mod, g1, w_main, wg_hi, wg_lo, bg)


def _sb_kernel(*refs, past, seq, tq, kb, npair):
    if past:
        q_ref, k_ref, v_ref, kp_ref, vp_ref, gq_ref, gk_ref, o_ref, kn_ref, ks_scr, vs_scr, run_scr, acc_scr = refs
    else:
        q_ref, k_ref, v_ref, gq_ref, gk_ref, o_ref, kn_ref, ks_scr, vs_scr, run_scr, acc_scr = refs
    i = pl.program_id(1)
    lane = lax.broadcasted_iota(I32, (1, LANES), 1)
    lo_half = lane < SB_HEAD_DIM
    pairs = [slice(hp * LANES, (hp + 1) * LANES) for hp in range(npair)]

    def headnorm(x, g):
        x2 = x * x
        s0 = jnp.sum(jnp.where(lo_half, x2, 0.0), axis=-1, keepdims=True)
        s1 = jnp.sum(jnp.where(lo_half, 0.0, x2), axis=-1, keepdims=True)
        r = jnp.where(lo_half, lax.rsqrt(s0 * (1.0 / SB_HEAD_DIM) + EPS),
                      lax.rsqrt(s1 * (1.0 / SB_HEAD_DIM) + EPS))
        return x * r * g

    @pl.when(i == 0)
    def _():
        ch = min(256, seq)
        for c in range(seq // ch):
            rows = slice(c * ch, (c + 1) * ch)
            dst = slice(past + c * ch, past + (c + 1) * ch)
            for ps in pairs:
                kn = headnorm(k_ref[0, rows, ps], gk_ref[...])
                kn_ref[0, rows, ps] = kn
                ks_scr[dst, ps] = kn.astype(BF16)
            vs_scr[dst, :] = v_ref[0, rows, :].astype(BF16)
        if past:
            ks_scr[0:past, :] = kp_ref[0].astype(BF16)
            vs_scr[0:past, :] = vp_ref[0].astype(BF16)

    qm = []
    for ps in pairs:
        q = headnorm(q_ref[0, :, ps], gq_ref[...]) * (1.0 / math.sqrt(SB_HEAD_DIM))
        qm.append(jnp.concatenate([jnp.where(lo_half, q, 0.0), jnp.where(lo_half, 0.0, q)], axis=0).astype(BF16))
    nrow = 2 * npair * tq

    def neg_cum(nk):
        rj = lax.broadcasted_iota(I32, (2 * nk, LANES + nk), 0)
        cj = lax.broadcasted_iota(I32, (2 * nk, LANES + nk), 1)
        rjm = jnp.where(rj >= nk, rj - nk, rj)
        return jnp.where(cj < LANES, -1.0, jnp.where(rjm >= cj - LANES, -1.0, 0.0)).astype(BF16)

    w_diag = neg_cum(tq)
    w_full = neg_cum(kb)
    causal = (lax.broadcasted_iota(I32, (nrow, tq), 1)
              < jnp.bitwise_and(lax.broadcasted_iota(I32, (nrow, tq), 0), tq - 1))

    def sweep(row0, nk, w_neg, diag):
        kv = [(ks_scr[pl.ds(row0, nk), ps], vs_scr[pl.ds(row0, nk), ps]) for ps in pairs]
        z = jnp.concatenate([_dot_nt(qm[hp], kv[hp][0]) for hp in range(npair)], axis=0)
        sp = jnp.maximum(z, 0.0) + jnp.log(1.0 + jnp.exp(-jnp.abs(z)))
        if diag:
            sp = jnp.where(causal, sp, 0.0)
        hi = sp.astype(BF16)
        lo = (sp - hi.astype(F32)).astype(BF16)
        t = _dot(jnp.concatenate([hi, lo], axis=1), w_neg)
        if diag:
            arg = z + t[:, LANES:]
            run = t[:, :LANES]
        else:
            old = run_scr[...]
            arg = z + t[:, LANES:] + jnp.concatenate([old] * (nk // LANES), axis=1)
            run = old + t[:, :LANES]
        run_scr[...] = run
        p = jnp.exp(arg)
        if diag:
            p = jnp.where(causal, p, 0.0)
        p = p.astype(BF16)
        for hp in range(npair):
            r0 = 2 * hp * tq
            vblk = kv[hp][1]
            vm = jnp.concatenate([jnp.where(lo_half, vblk, 0.0), jnp.where(lo_half, 0.0, vblk)], axis=0)
            pv = _dot(jnp.concatenate([p[r0:r0 + tq], p[r0 + tq:r0 + 2 * tq]], axis=1), vm.astype(BF16))
            acc_scr[hp] = pv if diag else acc_scr[hp] + pv
        return jnp.max(run)

    q0 = past + i * tq
    worst0 = sweep(pl.multiple_of(q0, tq), tq, w_diag, True)
    n_before = q0 // kb

    def cond(c):
        return jnp.logical_and(c[0] < n_before, c[1] > SB_UNDERFLOW)

    def body(c):
        row0 = q0 - (c[0] + 1) * kb
        return c[0] + 1, sweep(pl.multiple_of(row0, tq), kb, w_full, False)

    _, worst = lax.while_loop(cond, body, (jnp.int32(0), worst0))
    if tq < kb and seq > tq:
        @pl.when(jnp.logical_and(q0 - n_before * kb > 0, worst > SB_UNDERFLOW))
        def _():
            sweep(0, tq, w_diag, False)
    for hp, ps in enumerate(pairs):
        o_ref[0, :, ps] = acc_scr[hp]


def _sb_attention(q_raw, k_raw, v, k_past, v_past, g_q, g_k):
    b, s, w = q_raw.shape
    past = 0 if k_past is None else k_past.shape[1]
    tq = min(SB_BLK, s)
    kb = SB_SWEEP
    assert s % tq == 0 and past % kb == 0 and w % LANES == 0 and kb % LANES == 0 and (kb == 2 * tq or s == tq)
    npair = w // LANES
    qblk = pl.BlockSpec((1, tq, w), lambda bi, i: (bi, i, 0))
    full = lambda n: pl.BlockSpec((1, n, w), lambda bi, i: (bi, 0, 0))
    gspec = pl.BlockSpec((1, LANES), lambda bi, i: (0, 0))
    in_specs = [qblk, full(s), full(s)]
    args = [q_raw, k_raw, v]
    if past:
        in_specs += [full(past), full(past)]
        args += [k_past, v_past]
    in_specs += [gspec, gspec]
    args += [g_q, g_k]
    return pl.pallas_call(
        functools.partial(_sb_kernel, past=past, seq=s, tq=tq, kb=kb, npair=npair),
        grid=(b, s // tq),
        in_specs=in_specs,
        out_specs=[qblk, full(s)],
        out_shape=[jax.ShapeDtypeStruct((b, s, w), F32), jax.ShapeDtypeStruct((b, s, w), F32)],
        scratch_shapes=[pltpu.VMEM((past + s, w), BF16), pltpu.VMEM((past + s, w), BF16),
                        pltpu.VMEM((2 * npair * tq, LANES), F32), pltpu.VMEM((npair, tq, LANES), F32)],
        compiler_params=_cparams(("arbitrary", "arbitrary")),
        name="sb_attention",
    )(*args)


def _mlstm_kernel(qk_ref, v_ref, og_ref, g_ref, cp_ref, c0_ref, n0_ref, m0_ref, wc_ref, bc_ref, gh_ref,
                  out_ref, cout_ref, nout_ref, mout_ref, convout_ref,
                  prev_scr, c_scr, n_scr, m_scr, *, chunk):
    ci = pl.program_id(1)
    nh, dh = ML_HEADS, ML_HEAD_DIM
    width = nh * dh
    L = chunk

    @pl.when(ci == 0)
    def _():
        prev_scr[...] = jnp.zeros_like(prev_scr)
        prev_scr[8 - (CONV_W - 1):8, :] = cp_ref[0]
        c_scr[...] = c0_ref[0]
        n_scr[...] = n0_ref[0]
        m_scr[...] = m0_ref[0]

    u = qk_ref[0]
    xfull = jnp.concatenate([prev_scr[...], u], axis=0)
    wc = wc_ref[...]
    acc = bc_ref[...] + xfull[8 - (CONV_W - 1):8 - (CONV_W - 1) + L] * wc[0:1]
    for j in range(1, CONV_W):
        o = 8 - (CONV_W - 1) + j
        acc = acc + xfull[o:o + L] * wc[j:j + 1]
    qk = acc * jax.nn.sigmoid(acc)
    prev_scr[...] = u[L - 8:L, :]

    gt = g_ref[0]
    lf = _log_sigmoid(gt)
    gt_t = gt.T
    lf_t = lf.T
    row = lax.broadcasted_iota(I32, (L, L), 0)
    col = lax.broadcasted_iota(I32, (L, L), 1)
    tri = row >= col
    tril = jnp.where(tri, 1.0, 0.0).astype(BF16)
    triu = jnp.where(row <= col, 1.0, 0.0).astype(BF16)
    p1, p2, p3 = _split3(lf)
    b_col = _dot(tril, p1) + _dot(tril, p2) + _dot(tril, p3)
    q1, q2, q3 = _split3(lf_t[0:8, :])
    b_row = _dot(q1, triu) + _dot(q2, triu) + _dot(q3, triu)

    lane = lax.broadcasted_iota(I32, (1, LANES), 1)
    m_vec = m_scr[...]
    m_out = m_vec
    for h in range(nh):
        hs = slice(h * dh, (h + 1) * dh)
        qh = qk[:, h * dh:(h + 1) * dh]
        kh = qk[:, width + h * dh:width + (h + 1) * dh] * (dh ** -0.5)
        vh = v_ref[0, :, hs]
        b_c = b_col[:, nh + h:nh + h + 1]
        ig_c = gt[:, h:h + 1]
        b_r = b_row[nh + h:nh + h + 1, :]
        ig_r = gt_t[h:h + 1, :]
        m_prev = m_vec[:, h:h + 1]

        d = jnp.where(tri, (b_c - b_r) + ig_r, -jnp.inf)
        inter = b_c + m_prev
        m_t = jnp.maximum(inter, jnp.max(d, axis=1, keepdims=True))
        w_intra = jnp.exp(d - m_t)
        w_inter = jnp.exp(inter - m_t)
        qb = qh.astype(BF16)
        kb = kh.astype(BF16)
        s = _dot_nt(qb, kb) * w_intra
        c_h = c_scr[h]
        n_h = n_scr[h:h + 1, :]
        num = _dot(s.astype(BF16), vh.astype(BF16)) + w_inter * _dot_nt(qb, c_h.astype(BF16))
        den = jnp.sum(s, axis=1, keepdims=True) + w_inter * jnp.sum(qh * n_h, axis=1, keepdims=True)
        hh = num / jnp.maximum(jnp.abs(den), jnp.exp(-m_t))

        b_last = b_c[L - 1:L, :]
        g_c = (b_last - b_c) + ig_c
        m_new = jnp.maximum(b_last + m_prev, jnp.max(g_c, axis=0, keepdims=True))
        w_state = jnp.exp(g_c - m_new)
        decay = jnp.exp(b_last + m_prev - m_new)
        vw_t = (vh * w_state).T.astype(BF16)
        c_scr[h] = decay * c_h + _dot(vw_t, kb)
        n_scr[h:h + 1, :] = decay * n_h + jnp.sum(kh * w_state, axis=0, keepdims=True)
        m_out = jnp.where(lane == h, m_new, m_out)

        hn = hh * lax.rsqrt(jnp.mean(hh * hh, axis=-1, keepdims=True) + EPS) * gh_ref[...]
        out_ref[0, :, hs] = jax.nn.sigmoid(og_ref[0, :, hs]) * hn
    m_scr[...] = m_out

    @pl.when(ci == pl.num_programs(1) - 1)
    def _():
        cout_ref[0] = c_scr[...]
        nout_ref[0] = n_scr[...]
        mout_ref[0] = m_scr[...]
        convout_ref[0] = prev_scr[8 - (CONV_W - 1):8, :]


def _mlstm(ml_qk, ml_v, ml_o, gates, conv_past, c0, n0, m0, w_conv, b_conv, g_head):
    b, s, w2 = ml_qk.shape
    width = w2 // 2
    nh, dh = ML_HEADS, ML_HEAD_DIM
    chunk = min(ML_CHUNK, s)
    assert s % chunk == 0 and chunk >= 8
    tok = lambda w: pl.BlockSpec((1, chunk, w), lambda bi, ci: (bi, ci, 0))
    perb = lambda shp: pl.BlockSpec((1,) + shp, lambda bi, ci: (bi,) + (0,) * len(shp))
    const = lambda shp: pl.BlockSpec(shp, lambda bi, ci: (0,) * len(shp))
    return pl.pallas_call(
        functools.partial(_mlstm_kernel, chunk=chunk),
        grid=(b, s // chunk),
        in_specs=[tok(w2), tok(width), tok(width), tok(LANES),
                  perb((CONV_W - 1, w2)), perb((nh, dh, dh)), perb((nh, dh)), perb((1, LANES)),
                  const((CONV_W, w2)), const((1, w2)), const((1, dh))],
        out_specs=[tok(width), perb((nh, dh, dh)), perb((nh, dh)), perb((1, LANES)), perb((CONV_W - 1, w2))],
        out_shape=[jax.ShapeDtypeStruct((b, s, width), F32),
                   jax.ShapeDtypeStruct((b, nh, dh, dh), F32),
                   jax.ShapeDtypeStruct((b, nh, dh), F32),
                   jax.ShapeDtypeStruct((b, 1, LANES), F32),
                   jax.ShapeDtypeStruct((b, CONV_W - 1, w2), F32)],
        scratch_shapes=[pltpu.VMEM((8, w2), F32), pltpu.VMEM((nh, dh, dh), F32),
                        pltpu.VMEM((nh, dh), F32), pltpu.VMEM((1, LANES), F32)],
        compiler_params=_cparams(("arbitrary", "arbitrary")),
        name="mlstm",
    )(ml_qk, ml_v, ml_o, gates, conv_past, c0, n0, m0, w_conv, b_conv, g_head)


def _outproj_kernel(osb_ref, oml_ref, x_ref, mod_ref, w1_ref, w2_ref, g2_ref, wrh_ref, wrl_ref, br_ref,
                    x1_ref, h2_ref, ti_ref, tg_ref):
    mod = mod_ref[0]
    ga1 = mod[2:3]
    sh2 = mod[3:4]
    sc2 = mod[4:5]
    mix = _dot(osb_ref[0].astype(BF16), w1_ref[...]) + _dot(oml_ref[0].astype(BF16), w2_ref[...])
    x1 = x_ref[0] + ga1 * mix
    x1_ref[0] = x1
    ms = jnp.mean(x1 * x1, axis=-1, keepdims=True)
    h2 = x1 * lax.rsqrt(ms + EPS) * g2_ref[...]
    h2 = h2 * (1.0 + sc2) + sh2
    _store_row_tiles(h2_ref, h2)
    logits = _dot3(h2, wrh_ref[...], wrl_ref[...]) + br_ref[...]
    lane = lax.broadcasted_iota(I32, logits.shape, 1)
    vals, idxs = [], []
    cur = logits
    for _ in range(TOP_K):
        m = jnp.max(cur, axis=-1, keepdims=True)
        idx = jnp.min(jnp.where(cur == m, lane, LANES), axis=-1, keepdims=True)
        vals.append(m)
        idxs.append(idx)
        cur = jnp.where(lane == idx, -jnp.inf, cur)
    es = [jnp.exp(v - vals[0]) for v in vals]
    tot = es[0] + es[1] + es[2] + es[3]
    ti = jnp.zeros(logits.shape, I32)
    tg = jnp.zeros(logits.shape, F32)
    for k in range(TOP_K):
        ti = jnp.where(lane == k, idxs[k], ti)
        tg = jnp.where(lane == k, es[k] / tot, tg)
    ti_ref[0] = ti
    tg_ref[0] = tg


def _outproj(o_sb, o_ml, x, mod, w1, w2, g2, wr_hi, wr_lo, br):
    b, s, d = x.shape
    hw = o_sb.shape[2]
    tm = min(ROW_TILE, s)
    tok = lambda w: pl.BlockSpec((1, tm, w), lambda bi, i: (bi, i, 0))
    const = lambda shp: pl.BlockSpec(shp, lambda bi, i: (0,) * len(shp))
    return pl.pallas_call(
        _outproj_kernel,
        grid=(b, s // tm),
        in_specs=[tok(hw), tok(hw), tok(d), pl.BlockSpec((1, 6, d), lambda bi, i: (bi, 0, 0)),
                  const((hw, d)), const((hw, d)), const((1, d)),
                  const((d, LANES)), const((d, LANES)), const((1, LANES))],
        out_specs=[tok(d), pl.BlockSpec((tm * SUBLANES, LANES), lambda bi, i: (bi * (s // tm) + i, 0)),
                   tok(LANES), tok(LANES)],
        out_shape=[jax.ShapeDtypeStruct((b, s, d), F32), jax.ShapeDtypeStruct((b * s * SUBLANES, LANES), F32),
                   jax.ShapeDtypeStruct((b, s, LANES), I32), jax.ShapeDtypeStruct((b, s, LANES), F32)],
        compiler_params=_cparams(("arbitrary", "arbitrary")),
        name="outproj_router",
    )(o_sb, o_ml, x, mod, w1, w2, g2, wr_hi, wr_lo, br)


def _rank_kernel(ti_ref, rank_ref, cnt_ref, carry_scr):
    i = pl.program_id(0)

    @pl.when(i == 0)
    def _():
        carry_scr[...] = jnp.zeros_like(carry_scr)

    ti = ti_ref[...]
    tm = ti.shape[0]
    lane = lax.broadcasted_iota(I32, ti.shape, 1)
    hit = jnp.zeros(ti.shape, F32)
    for k in range(TOP_K):
        hit = hit + jnp.where(lane == ti[:, k:k + 1], 1.0, 0.0)
    row = lax.broadcasted_iota(I32, (tm, tm), 0)
    col = lax.broadcasted_iota(I32, (tm, tm), 1)
    strict = jnp.where(col < row, 1.0, 0.0).astype(BF16)
    excl = _dot(strict, hit.astype(BF16)) + carry_scr[...]
    rank = jnp.zeros(ti.shape, F32)
    for k in range(TOP_K):
        rk = jnp.sum(jnp.where(lane == ti[:, k:k + 1], excl, 0.0), axis=-1, keepdims=True)
        rank = jnp.where(lane == k, rk, rank)
    rank_ref[...] = rank.astype(I32)
    carry_scr[...] = carry_scr[...] + jnp.sum(hit, axis=0, keepdims=True)
    cnt_ref[...] = carry_scr[...].astype(I32)


def _ranks(top_idx):
    t = top_idx.shape[0]
    tm = ROW_TILE
    assert t % tm == 0
    return pl.pallas_call(
        _rank_kernel,
        grid=(t // tm,),
        in_specs=[pl.BlockSpec((tm, LANES), lambda i: (i, 0))],
        out_specs=[pl.BlockSpec((tm, LANES), lambda i: (i, 0)), pl.BlockSpec((1, LANES), lambda i: (0, 0))],
        out_shape=[jax.ShapeDtypeStruct((t, LANES), I32), jax.ShapeDtypeStruct((1, LANES), I32)],
        scratch_shapes=[pltpu.VMEM((1, LANES), F32)],
        compiler_params=_cparams(("arbitrary",)),
        name="expert_ranks",
    )(top_idx)


def _row_tile(ref, r):
    return ref.at[pl.ds(pl.multiple_of(r * SUBLANES, SUBLANES), SUBLANES), :]


def _dispatch_kernel(*refs, tm):
    pos_hbm, h_ref = refs[:2]
    xs_hbm, pos_smem, stage, pos_sem, row_sem = refs[-5:]
    i = pl.program_id(0)
    n = pl.num_programs(0)
    slot = lax.rem(i, 2)
    nidx = tm * TOP_K

    def drain(sl):
        for _ in range(TOP_K):
            pltpu.make_async_copy(stage.at[sl], stage.at[sl], row_sem.at[sl]).wait()

    cp = pltpu.make_async_copy(pos_hbm.at[pl.ds(pl.multiple_of(i * nidx, nidx), nidx)], pos_smem, pos_sem)
    cp.start()

    @pl.when(i >= 2)
    def _():
        drain(slot)

    stage[slot] = h_ref[...]
    cp.wait()

    def issue(c, carry):
        for rr in range(ISSUE_UNROLL):
            r = c * ISSUE_UNROLL + rr
            for k in range(TOP_K):
                pltpu.make_async_copy(_row_tile(stage.at[slot], r), _row_tile(xs_hbm, pos_smem[TOP_K * r + k]),
                                      row_sem.at[slot]).start(priority=k % DMA_QUEUES)
        return carry
    lax.fori_loop(0, tm // ISSUE_UNROLL, issue, 0)

    @pl.when(i == n - 1)
    def _():
        drain(slot)

        @pl.when(n >= 2)
        def _():
            drain(1 - slot)


def _dispatch(pos_flat, h_tiles, xs=None, n_rows=None):
    t = h_tiles.shape[0] // SUBLANES
    tm = GATHER_TILE
    assert t % tm == 0 and tm % ISSUE_UNROLL == 0
    in_specs = [pl.BlockSpec(memory_space=pl.ANY), pl.BlockSpec((tm * SUBLANES, LANES), lambda i: (i, 0))]
    args = [pos_flat, h_tiles]
    if xs is not None:
        in_specs.append(pl.BlockSpec(memory_space=pl.ANY))
        args.append(xs)
    return pl.pallas_call(
        functools.partial(_dispatch_kernel, tm=tm),
        grid=(t // tm,),
        in_specs=in_specs,
        out_specs=pl.BlockSpec(memory_space=pl.ANY),
        out_shape=jax.ShapeDtypeStruct((n_rows * SUBLANES, LANES) if xs is None else xs.shape, F32),
        scratch_shapes=[pltpu.SMEM((tm * TOP_K,), I32), pltpu.VMEM((2, tm * SUBLANES, LANES), F32),
                        pltpu.SemaphoreType.DMA, pltpu.SemaphoreType.DMA((2,))],
        input_output_aliases={} if xs is None else {2: 0},
        compiler_params=_cparams(("arbitrary",)),
        name="moe_dispatch",
    )(*args)


def _pad_fill_kernel(pos_hbm, xs_in_hbm, xs_hbm, pos_smem, zero_tile, pos_sem, row_sem, *, nidx):
    del xs_in_hbm
    i = pl.program_id(0)
    cp = pltpu.make_async_copy(pos_hbm.at[pl.ds(pl.multiple_of(i * nidx, nidx), nidx)], pos_smem, pos_sem)
    cp.start()
    zero_tile[...] = jnp.zeros_like(zero_tile)
    cp.wait()

    def issue(c, carry):
        for rr in range(TOP_K * ISSUE_UNROLL):
            r = c * (TOP_K * ISSUE_UNROLL) + rr
            pltpu.make_async_copy(zero_tile, _row_tile(xs_hbm, pos_smem[r]), row_sem).start(
                priority=rr % DMA_QUEUES)
        return carry
    lax.fori_loop(0, nidx // (TOP_K * ISSUE_UNROLL), issue, 0)
    whole = xs_hbm.at[pl.ds(0, nidx * SUBLANES), :]
    pltpu.make_async_copy(whole, whole, row_sem).wait()


def _pad_fill(pad_pos, xs):
    nidx = GATHER_TILE * TOP_K
    assert pad_pos.shape[0] % nidx == 0
    return pl.pallas_call(
        functools.partial(_pad_fill_kernel, nidx=nidx),
        grid=(pad_pos.shape[0] // nidx,),
        in_specs=[pl.BlockSpec(memory_space=pl.ANY), pl.BlockSpec(memory_space=pl.ANY)],
        out_specs=pl.BlockSpec(memory_space=pl.ANY),
        out_shape=jax.ShapeDtypeStruct(xs.shape, xs.dtype),
        scratch_shapes=[pltpu.SMEM((nidx,), I32), pltpu.VMEM((SUBLANES, LANES), F32),
                        pltpu.SemaphoreType.DMA, pltpu.SemaphoreType.DMA],
        input_output_aliases={1: 0},
        compiler_params=_cparams(("arbitrary",)),
        name="moe_pad_fill",
    )(pad_pos, xs)


def _moe_kernel(be_ref, nu_ref, xs_ref, wgu_ref, bgu_ref, wd_ref, bd_ref, ys_ref, wgu_bf, wd_bf):
    i = pl.program_id(0)
    d, ff = wgu_ref.shape[1], wd_ref.shape[1]

    @pl.when(jnp.logical_and(i < nu_ref[0],
                             jnp.logical_or(i == 0, be_ref[i] != be_ref[jnp.maximum(i - 1, 0)])))
    def _():
        for r in range(0, d, LANES):
            wgu_bf[r:r + LANES, :] = wgu_ref[0, r:r + LANES, :].astype(BF16)
        for r in range(0, ff, LANES):
            wd_bf[r:r + LANES, :] = wd_ref[0, r:r + LANES, :].astype(BF16)

    @pl.when(i < nu_ref[0])
    def _():
        x = _load_row_tiles(xs_ref, MOE_BLK).astype(BF16)
        glu = _dot(x, wgu_bf[:, 0:ff]) + bgu_ref[0, :, 0:ff]
        lin = _dot(x, wgu_bf[:, ff:2 * ff]) + bgu_ref[0, :, ff:2 * ff]
        glu = jnp.minimum(glu, SWIGLU_LIMIT)
        lin = jnp.clip(lin, -SWIGLU_LIMIT, SWIGLU_LIMIT)
        act = glu * jax.nn.sigmoid(SWIGLU_ALPHA * glu) * (lin + 1.0)
        _store_row_tiles(ys_ref, _dot(act.astype(BF16), wd_bf[...]) + bd_ref[0])

    @pl.when(i >= nu_ref[0])
    def _():
        ys_ref[...] = jnp.zeros_like(ys_ref)


def _moe_mlp(block_expert, n_used, xs, w_gu, b_gu, w_down, b_down):
    ne, d, ff2 = w_gu.shape
    ff = ff2 // 2
    tb = MOE_BLK
    n_rows = block_expert.shape[0] * tb
    assert xs.shape[0] >= n_rows * SUBLANES
    blk = lambda i, be, nu: jnp.minimum(i, nu[0] - 1)
    grid_spec = pltpu.PrefetchScalarGridSpec(
        num_scalar_prefetch=2,
        grid=(n_rows // tb,),
        in_specs=[pl.BlockSpec((tb * SUBLANES, LANES), lambda i, be, nu: (blk(i, be, nu), 0)),
                  pl.BlockSpec((1, d, ff2), lambda i, be, nu: (be[blk(i, be, nu)], 0, 0)),
                  pl.BlockSpec((1, 1, ff2), lambda i, be, nu: (be[blk(i, be, nu)], 0, 0)),
                  pl.BlockSpec((1, ff, d), lambda i, be, nu: (be[blk(i, be, nu)], 0, 0)),
                  pl.BlockSpec((1, 1, d), lambda i, be, nu: (be[blk(i, be, nu)], 0, 0))],
        out_specs=pl.BlockSpec((tb * SUBLANES, LANES), lambda i, be, nu: (i, 0)),
        scratch_shapes=[pltpu.VMEM((d, ff2), BF16), pltpu.VMEM((ff, d), BF16)],
    )
    return pl.pallas_call(
        _moe_kernel,
        grid_spec=grid_spec,
        out_shape=jax.ShapeDtypeStruct((n_rows * SUBLANES, LANES), F32),
        compiler_params=_cparams(("arbitrary",)),
        name="moe_mlp",
    )(block_expert, n_used, xs, w_gu, b_gu.reshape(ne, 1, ff2), w_down, b_down.reshape(ne, 1, d))


def _combine_kernel(pos_hbm, ys_hbm, x1_ref, tg_ref, mod_ref, y_ref, pos_smem0, pos_smem1, buf, pos_sem, row_sem,
                    *, tm, nb):
    i = pl.program_id(0)
    n = pl.num_programs(0)
    slot = lax.rem(i, 2)
    nidx = tm * TOP_K

    def gather(tile, sl):
        pos_smem = (pos_smem0, pos_smem1)[sl]
        cp = pltpu.make_async_copy(pos_hbm.at[pl.ds(pl.multiple_of(tile * nidx, nidx), nidx)], pos_smem, pos_sem)
        cp.start()
        cp.wait()

        def issue(c, carry):
            for rr in range(ISSUE_UNROLL):
                r = c * ISSUE_UNROLL + rr
                for k in range(TOP_K):
                    pltpu.make_async_copy(_row_tile(ys_hbm, pos_smem[TOP_K * r + k]),
                                          _row_tile(buf.at[sl, k], r), row_sem.at[sl]).start(
                                              priority=k % DMA_QUEUES)
            return carry
        lax.fori_loop(0, tm // ISSUE_UNROLL, issue, 0)

    @pl.when(i == 0)
    def _():
        gather(0, 0)

    for sl in range(2):
        @pl.when(jnp.logical_and(i + 1 < n, slot == 1 - sl))
        def _():
            gather(i + 1, sl)

    for k in range(TOP_K):
        pltpu.make_async_copy(buf.at[slot, k], buf.at[slot, k], row_sem.at[slot]).wait()

    tg = tg_ref[...]
    rows = tm // nb
    for c in range(SUBLANES):
        cols = slice(c * LANES, (c + 1) * LANES)
        ff = tg[:, 0:1] * buf[slot, 0, pl.ds(c, tm, stride=SUBLANES), :]
        for k in range(1, TOP_K):
            ff = ff + tg[:, k:k + 1] * buf[slot, k, pl.ds(c, tm, stride=SUBLANES), :]
        for g in range(nb):
            rs = slice(g * rows, (g + 1) * rows)
            y_ref[rs, cols] = x1_ref[rs, cols] + mod_ref[g, 5:6, cols] * ff[rs, :]


def _combine(pos_flat, ys, x1, top_gate, mod, tokens_per_batch):
    t, d = x1.shape
    tm = GATHER_TILE
    assert t % tm == 0 and tm % ISSUE_UNROLL == 0 and d == SUBLANES * LANES
    if tokens_per_batch >= tm:
        assert tokens_per_batch % tm == 0
        nb = 1
        per = tokens_per_batch // tm
        mod_spec = pl.BlockSpec((1, 6, d), lambda i: (i // per, 0, 0))
    else:
        assert tm % tokens_per_batch == 0
        nb = tm // tokens_per_batch
        mod_spec = pl.BlockSpec((nb, 6, d), lambda i: (i, 0, 0))
    return pl.pallas_call(
        functools.partial(_combine_kernel, tm=tm, nb=nb),
        grid=(t // tm,),
        in_specs=[pl.BlockSpec(memory_space=pl.ANY), pl.BlockSpec(memory_space=pl.ANY),
                  pl.BlockSpec((tm, d), lambda i: (i, 0)), pl.BlockSpec((tm, LANES), lambda i: (i, 0)), mod_spec],
        out_specs=pl.BlockSpec((tm, d), lambda i: (i, 0)),
        out_shape=jax.ShapeDtypeStruct((t, d), F32),
        scratch_shapes=[pltpu.SMEM((tm * TOP_K,), I32), pltpu.SMEM((tm * TOP_K,), I32),
                        pltpu.VMEM((2, TOP_K, tm * SUBLANES, LANES), F32),
                        pltpu.SemaphoreType.DMA, pltpu.SemaphoreType.DMA((2,))],
        compiler_params=_cparams(("arbitrary",)),
        name="moe_combine",
    )(pos_flat, ys, x1, top_gate, mod)


def _pad_lanes(a, fill=0.0):
    return jnp.pad(a, [(0, 0)] * (a.ndim - 1) + [(0, LANES - a.shape[-1])], constant_values=fill)


def _hi_lo(w):
    hi = w.astype(BF16)
    return hi, (w - hi.astype(F32)).astype(BF16)


def _mixer(x, mod, k_past, v_past, conv_past, c0, n0, m0, p):
    b, s, d = x.shape
    q_raw, k_raw, v, ml_qk, ml_v, ml_o, gates = _inproj(
        x, mod, p["g1"], p["w_main"], p["wg_hi"], p["wg_lo"], p["bg"], p["widths"])
    o_sb, k_norm = _sb_attention(q_raw, k_raw, v, k_past, v_past, p["g_q"], p["g_k"])
    o_ml, c_new, n_new, m_new, conv_new = _mlstm(
        ml_qk, ml_v, ml_o, gates, conv_past, c0, n0, _pad_lanes(m0)[:, None, :],
        p["w_conv"], p["b_conv"], p["g_head"])
    x1, h2, ti, tg = _outproj(o_sb, o_ml, x, mod, p["w_out1"], p["w_out2"], p["g2"],
                              p["wr_hi"], p["wr_lo"], p["br"])
    nsb = k_norm.shape[-1] // SB_HEAD_DIM
    state = (k_norm.reshape(b, s, nsb, SB_HEAD_DIM), v.reshape(b, s, nsb, SB_HEAD_DIM), conv_new,
             c_new, n_new, m_new[:, 0, :ML_HEADS])
    return x1.reshape(b * s, d), h2, ti.reshape(b * s, LANES), tg.reshape(b * s, LANES), state


def _layer(xp, xs_, cp, cs, kc, vc, convc, cc, nc, mc, w_ada, b_ada, g_norm1, w_in, g_q, g_k, w_conv, b_conv,
           b_gate, g_head, w_out, g_norm2, w_router, b_router, w_gu, b_gu, w_down, b_down):
    d = xp.shape[-1]
    bp, sp = xp.shape[:2]
    bs, ss = xs_.shape[:2]
    sbw = kc.shape[-2] * kc.shape[-1]
    mlw = ML_HEADS * ML_HEAD_DIM
    ngate = 2 * ML_HEADS
    gate0 = 3 * sbw + 3 * mlw
    w_main = jnp.concatenate([w_in[:, :gate0], w_in[:, gate0 + ngate:]], axis=1).astype(BF16)
    wg_hi, wg_lo = _hi_lo(_pad_lanes(w_in[:, gate0:gate0 + ngate]))
    wr_hi, wr_lo = _hi_lo(_pad_lanes(w_router))
    p = dict(
        widths=(sbw, sbw, sbw, 2 * mlw, mlw, mlw),
        g1=g_norm1[None], w_main=w_main, wg_hi=wg_hi, wg_lo=wg_lo, bg=_pad_lanes(b_gate)[None],
        g_q=jnp.tile(g_q, LANES // SB_HEAD_DIM)[None], g_k=jnp.tile(g_k, LANES // SB_HEAD_DIM)[None],
        w_conv=w_conv, b_conv=b_conv[None], g_head=g_head[None],
        w_out1=w_out[:sbw].astype(BF16), w_out2=w_out[sbw:].astype(BF16), g2=g_norm2[None],
        wr_hi=wr_hi, wr_lo=wr_lo, br=_pad_lanes(b_router, -1e30)[None],
    )
    mod = _ada(jnp.concatenate([cp, cs], axis=0), w_ada, b_ada)
    mod_p = mod[:bp].reshape(bp, 6, d)
    mod_s = mod[bp:].reshape(bs, 6, d)

    zeros = lambda *shp: jnp.zeros(shp, F32)
    x1p, h2p, tip, tgp, st_p = _mixer(
        xp, mod_p, None, None, zeros(bp, CONV_W - 1, 2 * mlw), zeros(bp, ML_HEADS, ML_HEAD_DIM, ML_HEAD_DIM),
        zeros(bp, ML_HEADS, ML_HEAD_DIM), zeros(bp, ML_HEADS), p)
    x1s, h2s, tis, tgs, st_s = _mixer(
        xs_, mod_s, kc.reshape(bs, -1, sbw), vc.reshape(bs, -1, sbw), convc, cc, nc, mc, p)

    tp, ts = bp * sp, bs * ss
    ti_all = jnp.concatenate([tip, tis], axis=0)
    rank, counts = _ranks(ti_all)
    counts = counts[0, :N_EXPERTS]
    padded = (counts + MOE_BLK - 1) // MOE_BLK * MOE_BLK
    pad_end = jnp.cumsum(padded)
    pad_start = pad_end - padded
    top_idx = ti_all[:, :TOP_K]
    pos = (pad_start[top_idx] + rank[:, :TOP_K]).astype(I32)
    n_blocks = -(-((tp + ts) * TOP_K + N_EXPERTS * (MOE_BLK - 1)) // MOE_BLK)
    n_used = (pad_end[-1] // MOE_BLK).astype(I32).reshape(1)
    block_start = jnp.arange(n_blocks, dtype=I32) * MOE_BLK
    block_expert = jnp.minimum(jnp.sum(block_start[:, None] >= pad_end[None, :], axis=1), N_EXPERTS - 1).astype(I32)
    pos_p = pos[:tp].reshape(-1)
    pos_s = pos[tp:].reshape(-1)

    n_rows = n_blocks * MOE_BLK
    slot = jnp.arange(MOE_BLK, dtype=I32)[None, :]
    spare = n_rows + jnp.arange(N_EXPERTS * MOE_BLK, dtype=I32).reshape(N_EXPERTS, MOE_BLK)
    pad_pos = jnp.where(slot < (padded - counts)[:, None], (pad_start + counts)[:, None] + slot, spare).reshape(-1)

    xs_buf = _dispatch(pos_p, h2p, n_rows=n_rows + N_EXPERTS * MOE_BLK)
    xs_buf = _dispatch(pos_s, h2s, xs_buf)
    xs_buf = _pad_fill(pad_pos.astype(I32), xs_buf)
    ys = _moe_mlp(block_expert, n_used, xs_buf, w_gu, b_gu, w_down, b_down)
    yp = _combine(pos_p, ys, x1p, tgp, mod_p, sp).reshape(bp, sp, d)
    ysm = _combine(pos_s, ys, x1s, tgs, mod_s, ss).reshape(bs, ss, d)
    return yp, ysm, st_p, st_s


def kernel(x_prompt, x_sample, c_prompt, c_sample, cache_sb_k, cache_sb_v, state_conv, state_mlstm_c, state_mlstm_n, state_mlstm_m, w_ada, b_ada, g_norm1, w_in, g_q, g_k, w_conv, b_conv, b_gate, g_head, w_out, g_norm2, w_router, b_router, w_gu, b_gu, w_down, b_down):
    assert w_ada.shape[0] == 1, "single-layer step"
    yp, ys, st_p, st_s = _layer(
        x_prompt, x_sample, c_prompt, c_sample, cache_sb_k[0], cache_sb_v[0], state_conv[0], state_mlstm_c[0],
        state_mlstm_n[0], state_mlstm_m[0], w_ada[0], b_ada[0], g_norm1[0], w_in[0], g_q[0], g_k[0], w_conv[0],
        b_conv[0], b_gate[0], g_head[0], w_out[0], g_norm2[0], w_router[0], b_router[0], w_gu[0], b_gu[0],
        w_down[0], b_down[0])
    return (yp, ys) + tuple(a[None] for a in st_p) + tuple(a[None] for a in st_s)
```

```python
import functools
import math

import jax
import jax.numpy as jnp
from jax import lax
from jax.experimental import pallas as pl
from jax.experimental.pallas import tpu as pltpu

F32 = jnp.float32
BF16 = jnp.bfloat16
I32 = jnp.int32

EPS = 1e-6
LANES = 128
SUBLANES = 8
SB_HEAD_DIM = 64
ML_HEAD_DIM = 128
ML_HEADS = 4
CONV_W = 4
N_EXPERTS = 32
TOP_K = 4
SWIGLU_LIMIT = 7.0
SWIGLU_ALPHA = 1.702
VMEM_LIMIT = 56 * 1024 * 1024

ROW_TILE = 512
MOE_BLK = 512
GATHER_TILE = 256
ISSUE_UNROLL = 8
DMA_QUEUES = 2
ML_CHUNK = 256
SB_BLK = 128
SB_SWEEP = 256
SB_UNDERFLOW = -105.0


def _cparams(sem):
    return pltpu.CompilerParams(dimension_semantics=sem, vmem_limit_bytes=VMEM_LIMIT)


def _split3(x):
    p1 = x.astype(BF16)
    r1 = x - p1.astype(F32)
    p2 = r1.astype(BF16)
    p3 = (r1 - p2.astype(F32)).astype(BF16)
    return p1, p2, p3


def _dot(a, b):
    return jnp.dot(a, b, preferred_element_type=F32)


def _dot_nt(a, b):
    return lax.dot_general(a, b, (((1,), (1,)), ((), ())), preferred_element_type=F32)


def _dot3(x, w_hi, w_lo):
    xh = x.astype(BF16)
    xl = (x - xh.astype(F32)).astype(BF16)
    return _dot(xh, w_hi) + _dot(xl, w_hi) + _dot(xh, w_lo)


def _store_row_tiles(ref, x):
    n, d = x.shape
    assert d == SUBLANES * LANES
    for c in range(SUBLANES):
        ref[pl.ds(c, n, stride=SUBLANES), :] = x[:, c * LANES:(c + 1) * LANES]


def _load_row_tiles(ref, n):
    return jnp.concatenate([ref[pl.ds(c, n, stride=SUBLANES), :] for c in range(SUBLANES)], axis=1)


def _log_sigmoid(x):
    return jnp.minimum(x, 0.0) - jnp.log1p(jnp.exp(-jnp.abs(x)))


def _ada_kernel(c_ref, w_ref, b_ref, o_ref):
    c = c_ref[...]
    s = c * jax.nn.sigmoid(c)
    o_ref[...] = jnp.dot(s, w_ref[...], precision=lax.Precision.HIGHEST,
                         preferred_element_type=F32) + b_ref[...]


def _ada(c, w_ada, b_ada):
    n, d = c.shape
    cols = w_ada.shape[1]
    tn = cols // 6
    return pl.pallas_call(
        _ada_kernel,
        grid=(cols // tn,),
        in_specs=[pl.BlockSpec((n, d), lambda j: (0, 0)),
                  pl.BlockSpec((d, tn), lambda j: (0, j)),
                  pl.BlockSpec((1, tn), lambda j: (0, j))],
        out_specs=pl.BlockSpec((n, tn), lambda j: (0, j)),
        out_shape=jax.ShapeDtypeStruct((n, cols), F32),
        compiler_params=_cparams(("arbitrary",)),
        name="ada_mod",
    )(c, w_ada, b_ada.reshape(1, cols))


def _inproj_kernel(x_ref, mod_ref, g1_ref, w_ref, wgh_ref, wgl_ref, bg_ref,
                   q_ref, k_ref, v_ref, qk_ref, mv_ref, mo_ref, gate_ref, *, widths):
    x = x_ref[0]
    mod = mod_ref[0]
    sh1 = mod[0:1]
    sc1 = mod[1:2]
    ms = jnp.mean(x * x, axis=-1, keepdims=True)
    h = x * lax.rsqrt(ms + EPS) * g1_ref[...]
    h = h * (1.0 + sc1) + sh1
    hb = h.astype(BF16)
    off = 0
    for ref, wd in zip((q_ref, k_ref, v_ref, qk_ref, mv_ref, mo_ref), widths):
        ref[0] = _dot(hb, w_ref[:, off:off + wd])
        off += wd
    gate_ref[0] = _dot3(h, wgh_ref[...], wgl_ref[...]) + bg_ref[...]


def _inproj(x, mod, g1, w_main, wg_hi, wg_lo, bg, widths):
    b, s, d = x.shape
    tm = min(ROW_TILE, s)
    ncol = w_main.shape[1]
    tok = lambda w: pl.BlockSpec((1, tm, w), lambda bi, i: (bi, i, 0))
    const = lambda shp: pl.BlockSpec(shp, lambda bi, i: (0,) * len(shp))
    return pl.pallas_call(
        functools.partial(_inproj_kernel, widths=widths),
        grid=(b, s // tm),
        in_specs=[tok(d),
                  pl.BlockSpec((1, 6, d), lambda bi, i: (bi, 0, 0)),
                  const((1, d)), const((d, ncol)), const((d, LANES)), const((d, LANES)), const((1, LANES))],
        out_specs=[tok(w) for w in widths] + [tok(LANES)],
        out_shape=[jax.ShapeDtypeStruct((b, s, w), F32) for w in widths]
        + [jax.ShapeDtypeStruct((b, s, LANES), F32)],
        compiler_params=_cparams(("arbitrary", "arbitrary")),
        name="inproj",
    )(x, mod, g1, w_main, wg_hi, wg_lo, bg)


def _sb_kernel(*refs, past, seq, tq, kb, npair):
    if past:
        q_ref, k_ref, v_ref, kp_ref, vp_ref, gq_ref, gk_ref, o_ref, kn_ref, ks_scr, vs_scr, run_scr, acc_scr = refs
    else:
        q_ref, k_ref, v_ref, gq_ref, gk_ref, o_ref, kn_ref, ks_scr, vs_scr, run_scr, acc_scr = refs
    i = pl.program_id(1)
    lane = lax.broadcasted_iota(I32, (1, LANES), 1)
    lo_half = lane < SB_HEAD_DIM
    pairs = [slice(hp * LANES, (hp + 1) * LANES) for hp in range(npair)]

    def headnorm(x, g):
        x2 = x * x
        s0 = jnp.sum(jnp.where(lo_half, x2, 0.0), axis=-1, keepdims=True)
        s1 = jnp.sum(jnp.where(lo_half, 0.0, x2), axis=-1, keepdims=True)
        r = jnp.where(lo_half, lax.rsqrt(s0 * (1.0 / SB_HEAD_DIM) + EPS),
                      lax.rsqrt(s1 * (1.0 / SB_HEAD_DIM) + EPS))
        return x * r * g

    @pl.when(i == 0)
    def _():
        ch = min(256, seq)
        for c in range(seq // ch):
            rows = slice(c * ch, (c + 1) * ch)
            dst = slice(past + c * ch, past + (c + 1) * ch)
            for ps in pairs:
                kn = headnorm(k_ref[0, rows, ps], gk_ref[...])
                kn_ref[0, rows, ps] = kn
                ks_scr[dst, ps] = kn.astype(BF16)
            vs_scr[dst, :] = v_ref[0, rows, :].astype(BF16)
        if past:
            ks_scr[0:past, :] = kp_ref[0].astype(BF16)
            vs_scr[0:past, :] = vp_ref[0].astype(BF16)

    qm = []
    for ps in pairs:
        q = headnorm(q_ref[0, :, ps], gq_ref[...]) * (1.0 / math.sqrt(SB_HEAD_DIM))
        qm.append(jnp.concatenate([jnp.where(lo_half, q, 0.0), jnp.where(lo_half, 0.0, q)], axis=0).astype(BF16))
    nrow = 2 * npair * tq

    def neg_cum(nk):
        rj = lax.broadcasted_iota(I32, (2 * nk, LANES + nk), 0)
        cj = lax.broadcasted_iota(I32, (2 * nk, LANES + nk), 1)
        rjm = jnp.where(rj >= nk, rj - nk, rj)
        return jnp.where(cj < LANES, -1.0, jnp.where(rjm >= cj - LANES, -1.0, 0.0)).astype(BF16)

    w_diag = neg_cum(tq)
    w_full = neg_cum(kb)
    causal = (lax.broadcasted_iota(I32, (nrow, tq), 1)
              < jnp.bitwise_and(lax.broadcasted_iota(I32, (nrow, tq), 0), tq - 1))

    def sweep(row0, nk, w_neg, diag):
        kv = [(ks_scr[pl.ds(row0, nk), ps], vs_scr[pl.ds(row0, nk), ps]) for ps in pairs]
        z = jnp.concatenate([_dot_nt(qm[hp], kv[hp][0]) for hp in range(npair)], axis=0)
        sp = jnp.maximum(z, 0.0) + jnp.log(1.0 + jnp.exp(-jnp.abs(z)))
        if diag:
            sp = jnp.where(causal, sp, 0.0)
        hi = sp.astype(BF16)
        lo = (sp - hi.astype(F32)).astype(BF16)
        t = _dot(jnp.concatenate([hi, lo], axis=1), w_neg)
        if diag:
            arg = z + t[:, LANES:]
            run = t[:, :LANES]
        else:
            old = run_scr[...]
            arg = z + t[:, LANES:] + jnp.concatenate([old] * (nk // LANES), axis=1)
            run = old + t[:, :LANES]
        run_scr[...] = run
        p = jnp.exp(arg)
        if diag:
            p = jnp.where(causal, p, 0.0)
        p = p.astype(BF16)
        for hp in range(npair):
            r0 = 2 * hp * tq
            vblk = kv[hp][1]
            vm = jnp.concatenate([jnp.where(lo_half, vblk, 0.0), jnp.where(lo_half, 0.0, vblk)], axis=0)
            pv = _dot(jnp.concatenate([p[r0:r0 + tq], p[r0 + tq:r0 + 2 * tq]], axis=1), vm.astype(BF16))
            acc_scr[hp] = pv if diag else acc_scr[hp] + pv
        return jnp.max(run)

    q0 = past + i * tq
    worst0 = sweep(pl.multiple_of(q0, tq), tq, w_diag, True)
    n_before = q0 // kb

    def cond(c):
        return jnp.logical_and(c[0] < n_before, c[1] > SB_UNDERFLOW)

    def body(c):
        row0 = q0 - (c[0] + 1) * kb
        return c[0] + 1, sweep(pl.multiple_of(row0, tq), kb, w_full, False)

    _, worst = lax.while_loop(cond, body, (jnp.int32(0), worst0))
    if tq < kb and seq > tq:
        @pl.when(jnp.logical_and(q0 - n_before * kb > 0, worst > SB_UNDERFLOW))
        def _():
            sweep(0, tq, w_diag, False)
    for hp, ps in enumerate(pairs):
        o_ref[0, :, ps] = acc_scr[hp]


def _sb_attention(q_raw, k_raw, v, k_past, v_past, g_q, g_k):
    b, s, w = q_raw.shape
    past = 0 if k_past is None else k_past.shape[1]
    tq = min(SB_BLK, s)
    kb = SB_SWEEP
    assert s % tq == 0 and past % kb == 0 and w % LANES == 0 and kb % LANES == 0 and (kb == 2 * tq or s == tq)
    npair = w // LANES
    qblk = pl.BlockSpec((1, tq, w), lambda bi, i: (bi, i, 0))
    full = lambda n: pl.BlockSpec((1, n, w), lambda bi, i: (bi, 0, 0))
    gspec = pl.BlockSpec((1, LANES), lambda bi, i: (0, 0))
    in_specs = [qblk, full(s), full(s)]
    args = [q_raw, k_raw, v]
    if past:
        in_specs += [full(past), full(past)]
        args += [k_past, v_past]
    in_specs += [gspec, gspec]
    args += [g_q, g_k]
    return pl.pallas_call(
        functools.partial(_sb_kernel, past=past, seq=s, tq=tq, kb=kb, npair=npair),
        grid=(b, s // tq),
        in_specs=in_specs,
        out_specs=[qblk, full(s)],
        out_shape=[jax.ShapeDtypeStruct((b, s, w), F32), jax.ShapeDtypeStruct((b, s, w), F32)],
        scratch_shapes=[pltpu.VMEM((past + s, w), BF16), pltpu.VMEM((past + s, w), BF16),
                        pltpu.VMEM((2 * npair * tq, LANES), F32), pltpu.VMEM((npair, tq, LANES), F32)],
        compiler_params=_cparams(("arbitrary", "arbitrary")),
        name="sb_attention",
    )(*args)


def _mlstm_kernel(qk_ref, v_ref, og_ref, g_ref, cp_ref, c0_ref, n0_ref, m0_ref, wc_ref, bc_ref, gh_ref,
                  out_ref, cout_ref, nout_ref, mout_ref, convout_ref,
                  prev_scr, c_scr, n_scr, m_scr, *, chunk):
    ci = pl.program_id(1)
    nh, dh = ML_HEADS, ML_HEAD_DIM
    width = nh * dh
    L = chunk

    @pl.when(ci == 0)
    def _():
        prev_scr[0:8, :] = jnp.zeros((8, prev_scr.shape[1]), F32)
        prev_scr[8 - (CONV_W - 1):8, :] = cp_ref[0]
        c_scr[...] = c0_ref[0]
        n_scr[...] = n0_ref[0]
        m_scr[...] = m0_ref[0]

    prev_scr[8:8 + L, :] = qk_ref[0]
    wc = wc_ref[...]
    acc = bc_ref[...] + prev_scr[8 - (CONV_W - 1):8 - (CONV_W - 1) + L, :] * wc[0:1]
    for j in range(1, CONV_W):
        o = 8 - (CONV_W - 1) + j
        acc = acc + prev_scr[o:o + L, :] * wc[j:j + 1]
    qk = acc * jax.nn.sigmoid(acc)
    prev_scr[0:8, :] = prev_scr[L:L + 8, :]

    gt = g_ref[0]
    lf = _log_sigmoid(gt)
    gt_t = gt.T
    lf_t = lf.T
    row = lax.broadcasted_iota(I32, (L, L), 0)
    col = lax.broadcasted_iota(I32, (L, L), 1)
    tri = row >= col
    tril = jnp.where(tri, 1.0, 0.0).astype(BF16)
    triu = jnp.where(row <= col, 1.0, 0.0).astype(BF16)
    p1, p2, p3 = _split3(lf)
    b_col = _dot(tril, p1) + _dot(tril, p2) + _dot(tril, p3)
    q1, q2, q3 = _split3(lf_t[0:8, :])
    b_row = _dot(q1, triu) + _dot(q2, triu) + _dot(q3, triu)

    lane = lax.broadcasted_iota(I32, (1, LANES), 1)
    m_vec = m_scr[...]
    m_out = m_vec
    for h in range(nh):
        hs = slice(h * dh, (h + 1) * dh)
        qh = qk[:, h * dh:(h + 1) * dh]
        kh = qk[:, width + h * dh:width + (h + 1) * dh] * (dh ** -0.5)
        vh = v_ref[0, :, hs]
        b_c = b_col[:, nh + h:nh + h + 1]
        ig_c = gt[:, h:h + 1]
        b_r = b_row[nh + h:nh + h + 1, :]
        ig_r = gt_t[h:h + 1, :]
        m_prev = m_vec[:, h:h + 1]

        d = jnp.where(tri, (b_c - b_r) + ig_r, -jnp.inf)
        inter = b_c + m_prev
        m_t = jnp.maximum(inter, jnp.max(d, axis=1, keepdims=True))
        w_intra = jnp.exp(d - m_t)
        w_inter = jnp.exp(inter - m_t)
        qb = qh.astype(BF16)
        kb = kh.astype(BF16)
        s = _dot_nt(qb, kb) * w_intra
        c_h = c_scr[h]
        n_h = n_scr[h:h + 1, :]
        num = _dot(s.astype(BF16), vh.astype(BF16)) + w_inter * _dot_nt(qb, c_h.astype(BF16))
        den = jnp.sum(s, axis=1, keepdims=True) + w_inter * jnp.sum(qh * n_h, axis=1, keepdims=True)
        hh = num / jnp.maximum(jnp.abs(den), jnp.exp(-m_t))

        b_last = b_c[L - 1:L, :]
        g_c = (b_last - b_c) + ig_c
        m_new = jnp.maximum(b_last + m_prev, jnp.max(g_c, axis=0, keepdims=True))
        w_state = jnp.exp(g_c - m_new)
        decay = jnp.exp(b_last + m_prev - m_new)
        vw_t = (vh * w_state).T.astype(BF16)
        c_scr[h] = decay * c_h + _dot(vw_t, kb)
        n_scr[h:h + 1, :] = decay * n_h + jnp.sum(kh * w_state, axis=0, keepdims=True)
        m_out = jnp.where(lane == h, m_new, m_out)

        hn = hh * lax.rsqrt(jnp.mean(hh * hh, axis=-1, keepdims=True) + EPS) * gh_ref[...]
        out_ref[0, :, hs] = jax.nn.sigmoid(og_ref[0, :, hs]) * hn
    m_scr[...] = m_out

    @pl.when(ci == pl.num_programs(1) - 1)
    def _():
        cout_ref[0] = c_scr[...]
        nout_ref[0] = n_scr[...]
        mout_ref[0] = m_scr[...]
        convout_ref[0] = prev_scr[8 - (CONV_W - 1):8, :]


def _mlstm(ml_qk, ml_v, ml_o, gates, conv_past, c0, n0, m0, w_conv, b_conv, g_head):
    b, s, w2 = ml_qk.shape
    width = w2 // 2
    nh, dh = ML_HEADS, ML_HEAD_DIM
    chunk = min(ML_CHUNK, s)
    assert s % chunk == 0 and chunk >= 8
    tok = lambda w: pl.BlockSpec((1, chunk, w), lambda bi, ci: (bi, ci, 0))
    perb = lambda shp: pl.BlockSpec((1,) + shp, lambda bi, ci: (bi,) + (0,) * len(shp))
    const = lambda shp: pl.BlockSpec(shp, lambda bi, ci: (0,) * len(shp))
    return pl.pallas_call(
        functools.partial(_mlstm_kernel, chunk=chunk),
        grid=(b, s // chunk),
        in_specs=[tok(w2), tok(width), tok(width), tok(LANES),
                  perb((CONV_W - 1, w2)), perb((nh, dh, dh)), perb((nh, dh)), perb((1, LANES)),
                  const((CONV_W, w2)), const((1, w2)), const((1, dh))],
        out_specs=[tok(width), perb((nh, dh, dh)), perb((nh, dh)), perb((1, LANES)), perb((CONV_W - 1, w2))],
        out_shape=[jax.ShapeDtypeStruct((b, s, width), F32),
                   jax.ShapeDtypeStruct((b, nh, dh, dh), F32),
                   jax.ShapeDtypeStruct((b, nh, dh), F32),
                   jax.ShapeDtypeStruct((b, 1, LANES), F32),
                   jax.ShapeDtypeStruct((b, CONV_W - 1, w2), F32)],
        scratch_shapes=[pltpu.VMEM((8 + chunk, w2), F32), pltpu.VMEM((nh, dh, dh), F32),
                        pltpu.VMEM((nh, dh), F32), pltpu.VMEM((1, LANES), F32)],
        compiler_params=_cparams(("arbitrary", "arbitrary")),
        name="mlstm",
    )(ml_qk, ml_v, ml_o, gates, conv_past, c0, n0, m0, w_conv, b_conv, g_head)


def _outproj_kernel(osb_ref, oml_ref, x_ref, mod_ref, w1_ref, w2_ref, g2_ref, wrh_ref, wrl_ref, br_ref,
                    x1_ref, h2_ref, ti_ref, tg_ref):
    mod = mod_ref[0]
    ga1 = mod[2:3]
    sh2 = mod[3:4]
    sc2 = mod[4:5]
    mix = _dot(osb_ref[0].astype(BF16), w1_ref[...]) + _dot(oml_ref[0].astype(BF16), w2_ref[...])
    x1 = x_ref[0] + ga1 * mix
    x1_ref[0] = x1
    ms = jnp.mean(x1 * x1, axis=-1, keepdims=True)
    h2 = x1 * lax.rsqrt(ms + EPS) * g2_ref[...]
    h2 = h2 * (1.0 + sc2) + sh2
    _store_row_tiles(h2_ref, h2)
    logits = _dot3(h2, wrh_ref[...], wrl_ref[...]) + br_ref[...]
    lane = lax.broadcasted_iota(I32, logits.shape, 1)
    vals, idxs = [], []
    cur = logits
    for _ in range(TOP_K):
        m = jnp.max(cur, axis=-1, keepdims=True)
        idx = jnp.min(jnp.where(cur == m, lane, LANES), axis=-1, keepdims=True)
        vals.append(m)
        idxs.append(idx)
        cur = jnp.where(lane == idx, -jnp.inf, cur)
    es = [jnp.exp(v - vals[0]) for v in vals]
    tot = es[0] + es[1] + es[2] + es[3]
    ti = jnp.zeros(logits.shape, I32)
    tg = jnp.zeros(logits.shape, F32)
    for k in range(TOP_K):
        ti = jnp.where(lane == k, idxs[k], ti)
        tg = jnp.where(lane == k, es[k] / tot, tg)
    ti_ref[0] = ti
    tg_ref[0] = tg


def _outproj(o_sb, o_ml, x, mod, w1, w2, g2, wr_hi, wr_lo, br):
    b, s, d = x.shape
    hw = o_sb.shape[2]
    tm = min(ROW_TILE, s)
    tok = lambda w: pl.BlockSpec((1, tm, w), lambda bi, i: (bi, i, 0))
    const = lambda shp: pl.BlockSpec(shp, lambda bi, i: (0,) * len(shp))
    return pl.pallas_call(
        _outproj_kernel,
        grid=(b, s // tm),
        in_specs=[tok(hw), tok(hw), tok(d), pl.BlockSpec((1, 6, d), lambda bi, i: (bi, 0, 0)),
                  const((hw, d)), const((hw, d)), const((1, d)),
                  const((d, LANES)), const((d, LANES)), const((1, LANES))],
        out_specs=[tok(d), pl.BlockSpec((tm * SUBLANES, LANES), lambda bi, i: (bi * (s // tm) + i, 0)),
                   tok(LANES), tok(LANES)],
        out_shape=[jax.ShapeDtypeStruct((b, s, d), F32), jax.ShapeDtypeStruct((b * s * SUBLANES, LANES), F32),
                   jax.ShapeDtypeStruct((b, s, LANES), I32), jax.ShapeDtypeStruct((b, s, LANES), F32)],
        compiler_params=_cparams(("arbitrary", "arbitrary")),
        name="outproj_router",
    )(o_sb, o_ml, x, mod, w1, w2, g2, wr_hi, wr_lo, br)


def _rank_kernel(ti_ref, rank_ref, cnt_ref, carry_scr):
    i = pl.program_id(0)

    @pl.when(i == 0)
    def _():
        carry_scr[...] = jnp.zeros_like(carry_scr)

    ti = ti_ref[...]
    tm = ti.shape[0]
    lane = lax.broadcasted_iota(I32, ti.shape, 1)
    hit = jnp.zeros(ti.shape, F32)
    for k in range(TOP_K):
        hit = hit + jnp.where(lane == ti[:, k:k + 1], 1.0, 0.0)
    row = lax.broadcasted_iota(I32, (tm, tm), 0)
    col = lax.broadcasted_iota(I32, (tm, tm), 1)
    strict = jnp.where(col < row, 1.0, 0.0).astype(BF16)
    excl = _dot(strict, hit.astype(BF16)) + carry_scr[...]
    rank = jnp.zeros(ti.shape, F32)
    for k in range(TOP_K):
        rk = jnp.sum(jnp.where(lane == ti[:, k:k + 1], excl, 0.0), axis=-1, keepdims=True)
        rank = jnp.where(lane == k, rk, rank)
    rank_ref[...] = rank.astype(I32)
    carry_scr[...] = carry_scr[...] + jnp.sum(hit, axis=0, keepdims=True)
    cnt_ref[...] = carry_scr[...].astype(I32)


def _ranks(top_idx):
    t = top_idx.shape[0]
    tm = ROW_TILE
    assert t % tm == 0
    return pl.pallas_call(
        _rank_kernel,
        grid=(t // tm,),
        in_specs=[pl.BlockSpec((tm, LANES), lambda i: (i, 0))],
        out_specs=[pl.BlockSpec((tm, LANES), lambda i: (i, 0)), pl.BlockSpec((1, LANES), lambda i: (0, 0))],
        out_shape=[jax.ShapeDtypeStruct((t, LANES), I32), jax.ShapeDtypeStruct((1, LANES), I32)],
        scratch_shapes=[pltpu.VMEM((1, LANES), F32)],
        compiler_params=_cparams(("arbitrary",)),
        name="expert_ranks",
    )(top_idx)


def _row_tile(ref, r):
    return ref.at[pl.ds(pl.multiple_of(r * SUBLANES, SUBLANES), SUBLANES), :]


def _dispatch_kernel(*refs, tm):
    pos_hbm, h_ref = refs[:2]
    xs_hbm, pos_smem0, pos_smem1, stage, pos_sem, row_sem = refs[-6:]
    i = pl.program_id(0)
    n = pl.num_programs(0)
    slot = lax.rem(i, 2)
    nidx = tm * TOP_K

    def pos_copy(tile, sl):
        return pltpu.make_async_copy(pos_hbm.at[pl.ds(pl.multiple_of(tile * nidx, nidx), nidx)],
                                     (pos_smem0, pos_smem1)[sl], pos_sem.at[sl])

    def drain(sl):
        for _ in range(TOP_K):
            pltpu.make_async_copy(stage.at[sl], stage.at[sl], row_sem.at[sl]).wait()

    @pl.when(i == 0)
    def _():
        pos_copy(0, 0).start()

    for sl in range(2):
        @pl.when(jnp.logical_and(i + 1 < n, slot == 1 - sl))
        def _():
            pos_copy(i + 1, sl).start()

    @pl.when(i >= 2)
    def _():
        drain(slot)

    stage[slot] = h_ref[...]

    for sl in range(2):
        @pl.when(slot == sl)
        def _():
            pos_copy(i, sl).wait()
            pos_smem = (pos_smem0, pos_smem1)[sl]

            def issue(c, carry):
                for rr in range(ISSUE_UNROLL):
                    r = c * ISSUE_UNROLL + rr
                    for k in range(TOP_K):
                        pltpu.make_async_copy(_row_tile(stage.at[sl], r), _row_tile(xs_hbm, pos_smem[TOP_K * r + k]),
                                              row_sem.at[sl]).start(priority=k % DMA_QUEUES)
                return carry
            lax.fori_loop(0, tm // ISSUE_UNROLL, issue, 0)

    @pl.when(i == n - 1)
    def _():
        drain(slot)

        @pl.when(n >= 2)
        def _():
            drain(1 - slot)


def _dispatch(pos_flat, h_tiles, xs=None, n_rows=None):
    t = h_tiles.shape[0] // SUBLANES
    tm = GATHER_TILE
    assert t % tm == 0 and tm % ISSUE_UNROLL == 0
    in_specs = [pl.BlockSpec(memory_space=pl.ANY), pl.BlockSpec((tm * SUBLANES, LANES), lambda i: (i, 0))]
    args = [pos_flat, h_tiles]
    if xs is not None:
        in_specs.append(pl.BlockSpec(memory_space=pl.ANY))
        args.append(xs)
    return pl.pallas_call(
        functools.partial(_dispatch_kernel, tm=tm),
        grid=(t // tm,),
        in_specs=in_specs,
        out_specs=pl.BlockSpec(memory_space=pl.ANY),
        out_shape=jax.ShapeDtypeStruct((n_rows * SUBLANES, LANES) if xs is None else xs.shape, F32),
        scratch_shapes=[pltpu.SMEM((tm * TOP_K,), I32), pltpu.SMEM((tm * TOP_K,), I32),
                        pltpu.VMEM((2, tm * SUBLANES, LANES), F32),
                        pltpu.SemaphoreType.DMA((2,)), pltpu.SemaphoreType.DMA((2,))],
        input_output_aliases={} if xs is None else {2: 0},
        compiler_params=_cparams(("arbitrary",)),
        name="moe_dispatch",
    )(*args)


def _pad_fill_kernel(pos_hbm, xs_in_hbm, xs_hbm, pos_smem, zero_tile, pos_sem, row_sem, *, nidx):
    del xs_in_hbm
    i = pl.program_id(0)
    cp = pltpu.make_async_copy(pos_hbm.at[pl.ds(pl.multiple_of(i * nidx, nidx), nidx)], pos_smem, pos_sem)
    cp.start()
    zero_tile[...] = jnp.zeros_like(zero_tile)
    cp.wait()

    def issue(c, carry):
        for rr in range(TOP_K * ISSUE_UNROLL):
            r = c * (TOP_K * ISSUE_UNROLL) + rr
            pltpu.make_async_copy(zero_tile, _row_tile(xs_hbm, pos_smem[r]), row_sem).start(
                priority=rr % DMA_QUEUES)
        return carry
    lax.fori_loop(0, nidx // (TOP_K * ISSUE_UNROLL), issue, 0)
    whole = xs_hbm.at[pl.ds(0, nidx * SUBLANES), :]
    pltpu.make_async_copy(whole, whole, row_sem).wait()


def _pad_fill(pad_pos, xs):
    nidx = GATHER_TILE * TOP_K
    assert pad_pos.shape[0] % nidx == 0
    return pl.pallas_call(
        functools.partial(_pad_fill_kernel, nidx=nidx),
        grid=(pad_pos.shape[0] // nidx,),
        in_specs=[pl.BlockSpec(memory_space=pl.ANY), pl.BlockSpec(memory_space=pl.ANY)],
        out_specs=pl.BlockSpec(memory_space=pl.ANY),
        out_shape=jax.ShapeDtypeStruct(xs.shape, xs.dtype),
        scratch_shapes=[pltpu.SMEM((nidx,), I32), pltpu.VMEM((SUBLANES, LANES), F32),
                        pltpu.SemaphoreType.DMA, pltpu.SemaphoreType.DMA],
        input_output_aliases={1: 0},
        compiler_params=_cparams(("arbitrary",)),
        name="moe_pad_fill",
    )(pad_pos, xs)


def _moe_kernel(be_ref, nu_ref, xs_ref, wgu_ref, bgu_ref, wd_ref, bd_ref, ys_ref, wgu_bf, wd_bf):
    i = pl.program_id(0)
    d, ff = wgu_ref.shape[1], wd_ref.shape[1]

    @pl.when(jnp.logical_and(i < nu_ref[0],
                             jnp.logical_or(i == 0, be_ref[i] != be_ref[jnp.maximum(i - 1, 0)])))
    def _():
        for r in range(0, d, LANES):
            wgu_bf[r:r + LANES, :] = wgu_ref[0, r:r + LANES, :].astype(BF16)
        for r in range(0, ff, LANES):
            wd_bf[r:r + LANES, :] = wd_ref[0, r:r + LANES, :].astype(BF16)

    @pl.when(i < nu_ref[0])
    def _():
        x = _load_row_tiles(xs_ref, MOE_BLK).astype(BF16)
        glu = _dot(x, wgu_bf[:, 0:ff]) + bgu_ref[0, :, 0:ff]
        lin = _dot(x, wgu_bf[:, ff:2 * ff]) + bgu_ref[0, :, ff:2 * ff]
        glu = jnp.minimum(glu, SWIGLU_LIMIT)
        lin = jnp.clip(lin, -SWIGLU_LIMIT, SWIGLU_LIMIT)
        act = glu * jax.nn.sigmoid(SWIGLU_ALPHA * glu) * (lin + 1.0)
        _store_row_tiles(ys_ref, _dot(act.astype(BF16), wd_bf[...]) + bd_ref[0])

    @pl.when(i >= nu_ref[0])
    def _():
        ys_ref[...] = jnp.zeros_like(ys_ref)


def _moe_mlp(block_expert, n_used, xs, w_gu, b_gu, w_down, b_down):
    ne, d, ff2 = w_gu.shape
    ff = ff2 // 2
    tb = MOE_BLK
    n_rows = block_expert.shape[0] * tb
    assert xs.shape[0] >= n_rows * SUBLANES
    blk = lambda i, be, nu: jnp.minimum(i, nu[0] - 1)
    grid_spec = pltpu.PrefetchScalarGridSpec(
        num_scalar_prefetch=2,
        grid=(n_rows // tb,),
        in_specs=[pl.BlockSpec((tb * SUBLANES, LANES), lambda i, be, nu: (blk(i, be, nu), 0)),
                  pl.BlockSpec((1, d, ff2), lambda i, be, nu: (be[blk(i, be, nu)], 0, 0)),
                  pl.BlockSpec((1, 1, ff2), lambda i, be, nu: (be[blk(i, be, nu)], 0, 0)),
                  pl.BlockSpec((1, ff, d), lambda i, be, nu: (be[blk(i, be, nu)], 0, 0)),
                  pl.BlockSpec((1, 1, d), lambda i, be, nu: (be[blk(i, be, nu)], 0, 0))],
        out_specs=pl.BlockSpec((tb * SUBLANES, LANES), lambda i, be, nu: (i, 0)),
        scratch_shapes=[pltpu.VMEM((d, ff2), BF16), pltpu.VMEM((ff, d), BF16)],
    )
    return pl.pallas_call(
        _moe_kernel,
        grid_spec=grid_spec,
        out_shape=jax.ShapeDtypeStruct((n_rows * SUBLANES, LANES), F32),
        compiler_params=_cparams(("arbitrary",)),
        name="moe_mlp",
    )(block_expert, n_used, xs, w_gu, b_gu.reshape(ne, 1, ff2), w_down, b_down.reshape(ne, 1, d))


def _combine_kernel(pos_hbm, ys_hbm, x1_ref, tg_ref, mod_ref, y_ref, pos_smem0, pos_smem1, buf, pos_sem, row_sem,
                    *, tm, nb):
    i = pl.program_id(0)
    n = pl.num_programs(0)
    slot = lax.rem(i, 2)
    nidx = tm * TOP_K

    def pos_copy(tile, sl):
        return pltpu.make_async_copy(pos_hbm.at[pl.ds(pl.multiple_of(tile * nidx, nidx), nidx)],
                                     (pos_smem0, pos_smem1)[sl], pos_sem.at[sl])

    def gather(tile, sl):
        pos_copy(tile, sl).wait()
        pos_smem = (pos_smem0, pos_smem1)[sl]

        def issue(c, carry):
            for rr in range(ISSUE_UNROLL):
                r = c * ISSUE_UNROLL + rr
                for k in range(TOP_K):
                    pltpu.make_async_copy(_row_tile(ys_hbm, pos_smem[TOP_K * r + k]),
                                          _row_tile(buf.at[sl, k], r), row_sem.at[sl]).start(
                                              priority=k % DMA_QUEUES)
            return carry
        lax.fori_loop(0, tm // ISSUE_UNROLL, issue, 0)

        @pl.when(tile + 1 < n)
        def _():
            pos_copy(tile + 1, 1 - sl).start()

    @pl.when(i == 0)
    def _():
        pos_copy(0, 0).start()
        gather(0, 0)

    for sl in range(2):
        @pl.when(jnp.logical_and(i + 1 < n, slot == 1 - sl))
        def _():
            gather(i + 1, sl)

    for k in range(TOP_K):
        pltpu.make_async_copy(buf.at[slot, k], buf.at[slot, k], row_sem.at[slot]).wait()

    tg = tg_ref[...]
    rows = tm // nb
    for c in range(SUBLANES):
        cols = slice(c * LANES, (c + 1) * LANES)
        ff = tg[:, 0:1] * buf[slot, 0, pl.ds(c, tm, stride=SUBLANES), :]
        for k in range(1, TOP_K):
            ff = ff + tg[:, k:k + 1] * buf[slot, k, pl.ds(c, tm, stride=SUBLANES), :]
        for g in range(nb):
            rs = slice(g * rows, (g + 1) * rows)
            y_ref[rs, cols] = x1_ref[rs, cols] + mod_ref[g, 5:6, cols] * ff[rs, :]


def _combine(pos_flat, ys, x1, top_gate, mod, tokens_per_batch):
    t, d = x1.shape
    tm = GATHER_TILE
    assert t % tm == 0 and tm % ISSUE_UNROLL == 0 and d == SUBLANES * LANES
    if tokens_per_batch >= tm:
        assert tokens_per_batch % tm == 0
        nb = 1
        per = tokens_per_batch // tm
        mod_spec = pl.BlockSpec((1, 6, d), lambda i: (i // per, 0, 0))
    else:
        assert tm % tokens_per_batch == 0
        nb = tm // tokens_per_batch
        mod_spec = pl.BlockSpec((nb, 6, d), lambda i: (i, 0, 0))
    return pl.pallas_call(
        functools.partial(_combine_kernel, tm=tm, nb=nb),
        grid=(t // tm,),
        in_specs=[pl.BlockSpec(memory_space=pl.ANY), pl.BlockSpec(memory_space=pl.ANY),
                  pl.BlockSpec((tm, d), lambda i: (i, 0)), pl.BlockSpec((tm, LANES), lambda i: (i, 0)), mod_spec],
        out_specs=pl.BlockSpec((tm, d), lambda i: (i, 0)),
        out_shape=jax.ShapeDtypeStruct((t, d), F32),
        scratch_shapes=[pltpu.SMEM((tm * TOP_K,), I32), pltpu.SMEM((tm * TOP_K,), I32),
                        pltpu.VMEM((2, TOP_K, tm * SUBLANES, LANES), F32),
                        pltpu.SemaphoreType.DMA((2,)), pltpu.SemaphoreType.DMA((2,))],
        compiler_params=_cparams(("arbitrary",)),
        name="moe_combine",
    )(pos_flat, ys, x1, top_gate, mod)


def _pad_lanes(a, fill=0.0):
    return jnp.pad(a, [(0, 0)] * (a.ndim - 1) + [(0, LANES - a.shape[-1])], constant_values=fill)


def _hi_lo(w):
    hi = w.astype(BF16)
    return hi, (w - hi.astype(F32)).astype(BF16)


def _mixer(x, mod, k_past, v_past, conv_past, c0, n0, m0, p):
    b, s, d = x.shape
    q_raw, k_raw, v, ml_qk, ml_v, ml_o, gates = _inproj(
        x, mod, p["g1"], p["w_main"], p["wg_hi"], p["wg_lo"], p["bg"], p["widths"])
    o_sb, k_norm = _sb_attention(q_raw, k_raw, v, k_past, v_past, p["g_q"], p["g_k"])
    o_ml, c_new, n_new, m_new, conv_new = _mlstm(
        ml_qk, ml_v, ml_o, gates, conv_past, c0, n0, _pad_lanes(m0)[:, None, :],
        p["w_conv"], p["b_conv"], p["g_head"])
    x1, h2, ti, tg = _outproj(o_sb, o_ml, x, mod, p["w_out1"], p["w_out2"], p["g2"],
                              p["wr_hi"], p["wr_lo"], p["br"])
    nsb = k_norm.shape[-1] // SB_HEAD_DIM
    state = (k_norm.reshape(b, s, nsb, SB_HEAD_DIM), v.reshape(b, s, nsb, SB_HEAD_DIM), conv_new,
             c_new, n_new, m_new[:, 0, :ML_HEADS])
    return x1.reshape(b * s, d), h2, ti.reshape(b * s, LANES), tg.reshape(b * s, LANES), state


def _layer(xp, xs_, cp, cs, kc, vc, convc, cc, nc, mc, w_ada, b_ada, g_norm1, w_in, g_q, g_k, w_conv, b_conv,
           b_gate, g_head, w_out, g_norm2, w_router, b_router, w_gu, b_gu, w_down, b_down):
    d = xp.shape[-1]
    bp, sp = xp.shape[:2]
    bs, ss = xs_.shape[:2]
    sbw = kc.shape[-2] * kc.shape[-1]
    mlw = ML_HEADS * ML_HEAD_DIM
    ngate = 2 * ML_HEADS
    gate0 = 3 * sbw + 3 * mlw
    w_main = jnp.concatenate([w_in[:, :gate0], w_in[:, gate0 + ngate:]], axis=1).astype(BF16)
    wg_hi, wg_lo = _hi_lo(_pad_lanes(w_in[:, gate0:gate0 + ngate]))
    wr_hi, wr_lo = _hi_lo(_pad_lanes(w_router))
    p = dict(
        widths=(sbw, sbw, sbw, 2 * mlw, mlw, mlw),
        g1=g_norm1[None], w_main=w_main, wg_hi=wg_hi, wg_lo=wg_lo, bg=_pad_lanes(b_gate)[None],
        g_q=jnp.tile(g_q, LANES // SB_HEAD_DIM)[None], g_k=jnp.tile(g_k, LANES // SB_HEAD_DIM)[None],
        w_conv=w_conv, b_conv=b_conv[None], g_head=g_head[None],
        w_out1=w_out[:sbw].astype(BF16), w_out2=w_out[sbw:].astype(BF16), g2=g_norm2[None],
        wr_hi=wr_hi, wr_lo=wr_lo, br=_pad_lanes(b_router, -1e30)[None],
    )
    mod = _ada(jnp.concatenate([cp, cs], axis=0), w_ada, b_ada)
    mod_p = mod[:bp].reshape(bp, 6, d)
    mod_s = mod[bp:].reshape(bs, 6, d)

    zeros = lambda *shp: jnp.zeros(shp, F32)
    x1p, h2p, tip, tgp, st_p = _mixer(
        xp, mod_p, None, None, zeros(bp, CONV_W - 1, 2 * mlw), zeros(bp, ML_HEADS, ML_HEAD_DIM, ML_HEAD_DIM),
        zeros(bp, ML_HEADS, ML_HEAD_DIM), zeros(bp, ML_HEADS), p)
    x1s, h2s, tis, tgs, st_s = _mixer(
        xs_, mod_s, kc.reshape(bs, -1, sbw), vc.reshape(bs, -1, sbw), convc, cc, nc, mc, p)

    tp, ts = bp * sp, bs * ss
    ti_all = jnp.concatenate([tip, tis], axis=0)
    rank, counts = _ranks(ti_all)
    counts = counts[0, :N_EXPERTS]
    padded = (counts + MOE_BLK - 1) // MOE_BLK * MOE_BLK
    pad_end = jnp.cumsum(padded)
    pad_start = pad_end - padded
    top_idx = ti_all[:, :TOP_K]
    pos = (pad_start[top_idx] + rank[:, :TOP_K]).astype(I32)
    n_blocks = -(-((tp + ts) * TOP_K + N_EXPERTS * (MOE_BLK - 1)) // MOE_BLK)
    n_used = (pad_end[-1] // MOE_BLK).astype(I32).reshape(1)
    block_start = jnp.arange(n_blocks, dtype=I32) * MOE_BLK
    block_expert = jnp.minimum(jnp.sum(block_start[:, None] >= pad_end[None, :], axis=1), N_EXPERTS - 1).astype(I32)
    pos_p = pos[:tp].reshape(-1)
    pos_s = pos[tp:].reshape(-1)

    n_rows = n_blocks * MOE_BLK
    slot = jnp.arange(MOE_BLK, dtype=I32)[None, :]
    spare = n_rows + jnp.arange(N_EXPERTS * MOE_BLK, dtype=I32).reshape(N_EXPERTS, MOE_BLK)
    pad_pos = jnp.where(slot < (padded - counts)[:, None], (pad_start + counts)[:, None] + slot, spare).reshape(-1)

    xs_buf = _dispatch(pos_p, h2p, n_rows=n_rows + N_EXPERTS * MOE_BLK)
    xs_buf = _dispatch(pos_s, h2s, xs_buf)
    xs_buf = _pad_fill(pad_pos.astype(I32), xs_buf)
    ys = _moe_mlp(block_expert, n_used, xs_buf, w_gu, b_gu, w_down, b_down)
    yp = _combine(pos_p, ys, x1p, tgp, mod_p, sp).reshape(bp, sp, d)
    ysm = _combine(pos_s, ys, x1s, tgs, mod_s, ss).reshape(bs, ss, d)
    return yp, ysm, st_p, st_s


def kernel(x_prompt, x_sample, c_prompt, c_sample, cache_sb_k, cache_sb_v, state_conv, state_mlstm_c, state_mlstm_n, state_mlstm_m, w_ada, b_ada, g_norm1, w_in, g_q, g_k, w_conv, b_conv, b_gate, g_head, w_out, g_norm2, w_router, b_router, w_gu, b_gu, w_down, b_down):
    assert w_ada.shape[0] == 1, "single-layer step"
    yp, ys, st_p, st_s = _layer(
        x_prompt, x_sample, c_prompt, c_sample, cache_sb_k[0], cache_sb_v[0], state_conv[0], state_mlstm_c[0],
        state_mlstm_n[0], state_mlstm_m[0], w_ada[0], b_ada[0], g_norm1[0], w_in[0], g_q[0], g_k[0], w_conv[0],
        b_conv[0], b_gate[0], g_head[0], w_out[0], g_norm2[0], w_router[0], b_router[0], w_gu[0], b_gu[0],
        w_down[0], b_down[0])
    return (yp, ys) + tuple(a[None] for a in st_p) + tuple(a[None] for a in st_s)
```

```python
import functools
import math

import jax
import jax.numpy as jnp
from jax import lax
from jax.experimental import pallas as pl
from jax.experimental.pallas import tpu as pltpu

F32 = jnp.float32
BF16 = jnp.bfloat16
I32 = jnp.int32

EPS = 1e-6
LANES = 128
SUBLANES = 8
SB_HEAD_DIM = 64
ML_HEAD_DIM = 128
ML_HEADS = 4
CONV_W = 4
N_EXPERTS = 32
TOP_K = 4
SWIGLU_LIMIT = 7.0
SWIGLU_ALPHA = 1.702
VMEM_LIMIT = 56 * 1024 * 1024

ROW_TILE = 512
MOE_BLK = 512
GATHER_TILE = 256
ISSUE_UNROLL = 8
DMA_QUEUES = 2
ML_CHUNK = 256
SB_BLK = 128
SB_SWEEP = 256
SB_UNDERFLOW = -105.0


def _cparams(sem):
    return pltpu.CompilerParams(dimension_semantics=sem, vmem_limit_bytes=VMEM_LIMIT)


def _split3(x):
    p1 = x.astype(BF16)
    r1 = x - p1.astype(F32)
    p2 = r1.astype(BF16)
    p3 = (r1 - p2.astype(F32)).astype(BF16)
    return p1, p2, p3


def _dot(a, b):
    return jnp.dot(a, b, preferred_element_type=F32)


def _dot_nt(a, b):
    return lax.dot_general(a, b, (((1,), (1,)), ((), ())), preferred_element_type=F32)


def _dot3(x, w_hi, w_lo):
    xh = x.astype(BF16)
    xl = (x - xh.astype(F32)).astype(BF16)
    return _dot(xh, w_hi) + _dot(xl, w_hi) + _dot(xh, w_lo)


def _store_row_tiles(ref, x):
    n, d = x.shape
    assert d == SUBLANES * LANES
    for c in range(SUBLANES):
        ref[pl.ds(c, n, stride=SUBLANES), :] = x[:, c * LANES:(c + 1) * LANES]


def _load_row_tiles(ref, n):
    return jnp.concatenate([ref[pl.ds(c, n, stride=SUBLANES), :] for c in range(SUBLANES)], axis=1)


def _store_head_tiles(ref, row0, head0, x):
    n = x.shape[0]
    for j in range(x.shape[1] // SB_HEAD_DIM):
        ref[pl.ds(row0 * SUBLANES + head0 + j, n, stride=SUBLANES), :] = x[:, j * SB_HEAD_DIM:(j + 1) * SB_HEAD_DIM]


def _headnorm(x, g):
    lo_half = lax.broadcasted_iota(I32, (1, LANES), 1) < SB_HEAD_DIM
    x2 = x * x
    s0 = jnp.sum(jnp.where(lo_half, x2, 0.0), axis=-1, keepdims=True)
    s1 = jnp.sum(jnp.where(lo_half, 0.0, x2), axis=-1, keepdims=True)
    r = jnp.where(lo_half, lax.rsqrt(s0 * (1.0 / SB_HEAD_DIM) + EPS), lax.rsqrt(s1 * (1.0 / SB_HEAD_DIM) + EPS))
    return x * r * g


def _log_sigmoid(x):
    return jnp.minimum(x, 0.0) - jnp.log1p(jnp.exp(-jnp.abs(x)))


def _ada_kernel(c_ref, w_ref, b_ref, o_ref):
    c = c_ref[...]
    s = c * jax.nn.sigmoid(c)
    o_ref[...] = jnp.dot(s, w_ref[...], precision=lax.Precision.HIGHEST,
                         preferred_element_type=F32) + b_ref[...]


def _ada(c, w_ada, b_ada):
    n, d = c.shape
    cols = w_ada.shape[1]
    tn = cols // 6
    return pl.pallas_call(
        _ada_kernel,
        grid=(cols // tn,),
        in_specs=[pl.BlockSpec((n, d), lambda j: (0, 0)),
                  pl.BlockSpec((d, tn), lambda j: (0, j)),
                  pl.BlockSpec((1, tn), lambda j: (0, j))],
        out_specs=pl.BlockSpec((n, tn), lambda j: (0, j)),
        out_shape=jax.ShapeDtypeStruct((n, cols), F32),
        compiler_params=_cparams(("arbitrary",)),
        name="ada_mod",
    )(c, w_ada, b_ada.reshape(1, cols))


def _inproj_kernel(x_ref, mod_ref, g1_ref, w_ref, wgh_ref, wgl_ref, bg_ref, gk_ref,
                   q_ref, k_ref, v_ref, qk_ref, mv_ref, mo_ref, gate_ref, kt_ref, vt_ref, *, widths):
    x = x_ref[0]
    mod = mod_ref[0]
    sh1 = mod[0:1]
    sc1 = mod[1:2]
    ms = jnp.mean(x * x, axis=-1, keepdims=True)
    h = x * lax.rsqrt(ms + EPS) * g1_ref[...]
    h = h * (1.0 + sc1) + sh1
    hb = h.astype(BF16)
    off = 0
    for ref, wd in zip((q_ref, k_ref, v_ref, qk_ref, mv_ref, mo_ref), widths):
        u = _dot(hb, w_ref[:, off:off + wd])
        if ref is k_ref:
            u = jnp.concatenate([_headnorm(u[:, c:c + LANES], gk_ref[...]) for c in range(0, wd, LANES)], axis=1)
            _store_head_tiles(kt_ref, 0, 0, u)
        if ref is v_ref:
            _store_head_tiles(vt_ref, 0, 0, u)
        ref[0] = u
        off += wd
    gate_ref[0] = _dot3(h, wgh_ref[...], wgl_ref[...]) + bg_ref[...]


def _inproj(x, mod, g1, w_main, wg_hi, wg_lo, bg, g_k, widths):
    b, s, d = x.shape
    tm = min(ROW_TILE, s)
    ncol = w_main.shape[1]
    tok = lambda w: pl.BlockSpec((1, tm, w), lambda bi, i: (bi, i, 0))
    const = lambda shp: pl.BlockSpec(shp, lambda bi, i: (0,) * len(shp))
    head_tiles = pl.BlockSpec((tm * SUBLANES, SB_HEAD_DIM), lambda bi, i: (bi * (s // tm) + i, 0))
    return pl.pallas_call(
        functools.partial(_inproj_kernel, widths=widths),
        grid=(b, s // tm),
        in_specs=[tok(d),
                  pl.BlockSpec((1, 6, d), lambda bi, i: (bi, 0, 0)),
                  const((1, d)), const((d, ncol)), const((d, LANES)), const((d, LANES)), const((1, LANES)),
                  const((1, LANES))],
        out_specs=[tok(w) for w in widths] + [tok(LANES), head_tiles, head_tiles],
        out_shape=[jax.ShapeDtypeStruct((b, s, w), F32) for w in widths]
        + [jax.ShapeDtypeStruct((b, s, LANES), F32)]
        + [jax.ShapeDtypeStruct((b * s * SUBLANES, SB_HEAD_DIM), F32)] * 2,
        compiler_params=_cparams(("arbitrary", "arbitrary")),
        name="inproj",
    )(x, mod, g1, w_main, wg_hi, wg_lo, bg, g_k)


def _sb_kernel(*refs, past, seq, tq, kb, npair):
    if past:
        q_ref, k_ref, v_ref, kp_ref, vp_ref, gq_ref, o_ref, ks_scr, vs_scr, run_scr, acc_scr = refs
    else:
        q_ref, k_ref, v_ref, gq_ref, o_ref, ks_scr, vs_scr, run_scr, acc_scr = refs
    i = pl.program_id(1)
    lane = lax.broadcasted_iota(I32, (1, LANES), 1)
    lo_half = lane < SB_HEAD_DIM
    pairs = [slice(hp * LANES, (hp + 1) * LANES) for hp in range(npair)]

    @pl.when(i == 0)
    def _():
        ch = min(256, seq)
        for c in range(seq // ch):
            rows = slice(c * ch, (c + 1) * ch)
            dst = slice(past + c * ch, past + (c + 1) * ch)
            ks_scr[dst, :] = k_ref[0, rows, :].astype(BF16)
            vs_scr[dst, :] = v_ref[0, rows, :].astype(BF16)
        if past:
            ks_scr[0:past, :] = kp_ref[0].astype(BF16)
            vs_scr[0:past, :] = vp_ref[0].astype(BF16)

    qm = []
    for ps in pairs:
        q = _headnorm(q_ref[0, :, ps], gq_ref[...]) * (1.0 / math.sqrt(SB_HEAD_DIM))
        qm.append(jnp.concatenate([jnp.where(lo_half, q, 0.0), jnp.where(lo_half, 0.0, q)], axis=0).astype(BF16))
    nrow = 2 * npair * tq

    def neg_cum(nk):
        rj = lax.broadcasted_iota(I32, (2 * nk, LANES + nk), 0)
        cj = lax.broadcasted_iota(I32, (2 * nk, LANES + nk), 1)
        rjm = jnp.where(rj >= nk, rj - nk, rj)
        return jnp.where(cj < LANES, -1.0, jnp.where(rjm >= cj - LANES, -1.0, 0.0)).astype(BF16)

    w_diag = neg_cum(tq)
    w_sub = w_diag if tq == LANES else neg_cum(LANES)
    causal = (lax.broadcasted_iota(I32, (nrow, tq), 1)
              < jnp.bitwise_and(lax.broadcasted_iota(I32, (nrow, tq), 0), tq - 1))

    def sweep(row0, nk, diag):
        kv = [(ks_scr[pl.ds(row0, nk), ps], vs_scr[pl.ds(row0, nk), ps]) for ps in pairs]
        z = jnp.concatenate([_dot_nt(qm[hp], kv[hp][0]) for hp in range(npair)], axis=0)
        sp = jnp.maximum(z, 0.0) + jnp.log(1.0 + jnp.exp(-jnp.abs(z)))
        if diag:
            sp = jnp.where(causal, sp, 0.0)
        hi = sp.astype(BF16)
        lo = (sp - hi.astype(F32)).astype(BF16)
        sub = min(nk, LANES)
        w_neg = w_diag if sub == tq else w_sub
        run = None if diag else run_scr[...]
        args = [None] * (nk // sub)
        for j in reversed(range(nk // sub)):
            cols = slice(j * sub, (j + 1) * sub)
            t = _dot(jnp.concatenate([hi[:, cols], lo[:, cols]], axis=1), w_neg)
            args[j] = z[:, cols] + t[:, LANES:] if run is None else z[:, cols] + t[:, LANES:] + run[:, :sub]
            run = t[:, :LANES] if run is None else run + t[:, :LANES]
        run_scr[...] = run
        p = jnp.exp(args[0] if len(args) == 1 else jnp.concatenate(args, axis=1))
        if diag:
            p = jnp.where(causal, p, 0.0)
        p = p.astype(BF16)
        for hp in range(npair):
            r0 = 2 * hp * tq
            vblk = kv[hp][1]
            vm = jnp.concatenate([jnp.where(lo_half, vblk, 0.0), jnp.where(lo_half, 0.0, vblk)], axis=0)
            pv = _dot(jnp.concatenate([p[r0:r0 + tq], p[r0 + tq:r0 + 2 * tq]], axis=1), vm.astype(BF16))
            acc_scr[hp] = pv if diag else acc_scr[hp] + pv
        return jnp.max(run)

    q0 = past + i * tq
    worst0 = sweep(pl.multiple_of(q0, tq), tq, True)
    n_before = q0 // kb

    def cond(c):
        return jnp.logical_and(c[0] < n_before, c[1] > SB_UNDERFLOW)

    def body(c):
        row0 = q0 - (c[0] + 1) * kb
        return c[0] + 1, sweep(pl.multiple_of(row0, tq), kb, False)

    _, worst = lax.while_loop(cond, body, (jnp.int32(0), worst0))
    if tq < kb and seq > tq:
        @pl.when(jnp.logical_and(q0 - n_before * kb > 0, worst > SB_UNDERFLOW))
        def _():
            sweep(0, tq, False)
    for hp, ps in enumerate(pairs):
        o_ref[0, :, ps] = acc_scr[hp]


def _sb_attention(q_raw, k_norm, v, k_past, v_past, g_q):
    b, s, w = q_raw.shape
    past = 0 if k_past is None else k_past.shape[1]
    tq = min(SB_BLK, s)
    kb = SB_SWEEP
    assert s % tq == 0 and past % kb == 0 and w % LANES == 0 and kb % LANES == 0 and (kb == 2 * tq or s == tq)
    npair = w // LANES
    qblk = pl.BlockSpec((1, tq, w), lambda bi, i: (bi, i, 0))
    full = lambda n: pl.BlockSpec((1, n, w), lambda bi, i: (bi, 0, 0))
    gspec = pl.BlockSpec((1, LANES), lambda bi, i: (0, 0))
    in_specs = [qblk, full(s), full(s)]
    args = [q_raw, k_norm, v]
    if past:
        in_specs += [full(past), full(past)]
        args += [k_past, v_past]
    in_specs += [gspec]
    args += [g_q]
    return pl.pallas_call(
        functools.partial(_sb_kernel, past=past, seq=s, tq=tq, kb=kb, npair=npair),
        grid=(b, s // tq),
        in_specs=in_specs,
        out_specs=qblk,
        out_shape=jax.ShapeDtypeStruct((b, s, w), F32),
        scratch_shapes=[pltpu.VMEM((past + s, w), BF16), pltpu.VMEM((past + s, w), BF16),
                        pltpu.VMEM((2 * npair * tq, LANES), F32), pltpu.VMEM((npair, tq, LANES), F32)],
        compiler_params=_cparams(("arbitrary", "arbitrary")),
        name="sb_attention",
    )(*args)


def _mlstm_kernel(qk_ref, v_ref, og_ref, g_ref, cp_ref, c0_ref, n0_ref, m0_ref, wc_ref, bc_ref, gh_ref,
                  out_ref, cout_ref, nout_ref, mout_ref, convout_ref,
                  prev_scr, c_scr, n_scr, m_scr, *, chunk):
    ci = pl.program_id(1)
    nh, dh = ML_HEADS, ML_HEAD_DIM
    width = nh * dh
    L = chunk

    @pl.when(ci == 0)
    def _():
        prev_scr[0:8, :] = jnp.zeros((8, prev_scr.shape[1]), F32)
        prev_scr[8 - (CONV_W - 1):8, :] = cp_ref[0]
        c_scr[...] = c0_ref[0]
        n_scr[...] = n0_ref[0]
        m_scr[...] = m0_ref[0]

    prev_scr[8:8 + L, :] = qk_ref[0]
    wc = wc_ref[...]
    acc = bc_ref[...] + prev_scr[8 - (CONV_W - 1):8 - (CONV_W - 1) + L, :] * wc[0:1]
    for j in range(1, CONV_W):
        o = 8 - (CONV_W - 1) + j
        acc = acc + prev_scr[o:o + L, :] * wc[j:j + 1]
    qk = acc * jax.nn.sigmoid(acc)
    prev_scr[0:8, :] = prev_scr[L:L + 8, :]

    gt = g_ref[0]
    lf = _log_sigmoid(gt)
    gt_t = gt.T
    lf_t = lf.T
    row = lax.broadcasted_iota(I32, (L, L), 0)
    col = lax.broadcasted_iota(I32, (L, L), 1)
    tri = row >= col
    tril = jnp.where(tri, 1.0, 0.0).astype(BF16)
    triu = jnp.where(row <= col, 1.0, 0.0).astype(BF16)
    p1, p2, p3 = _split3(lf)
    b_col = _dot(tril, p1) + _dot(tril, p2) + _dot(tril, p3)
    q1, q2, q3 = _split3(lf_t[0:8, :])
    b_row = _dot(q1, triu) + _dot(q2, triu) + _dot(q3, triu)

    lane = lax.broadcasted_iota(I32, (1, LANES), 1)
    m_vec = m_scr[...]
    m_out = m_vec
    for h in range(nh):
        hs = slice(h * dh, (h + 1) * dh)
        qh = qk[:, h * dh:(h + 1) * dh]
        kh = qk[:, width + h * dh:width + (h + 1) * dh] * (dh ** -0.5)
        vh = v_ref[0, :, hs]
        b_c = b_col[:, nh + h:nh + h + 1]
        ig_c = gt[:, h:h + 1]
        b_r = b_row[nh + h:nh + h + 1, :]
        ig_r = gt_t[h:h + 1, :]
        m_prev = m_vec[:, h:h + 1]

        d = jnp.where(tri, (b_c - b_r) + ig_r, -jnp.inf)
        inter = b_c + m_prev
        m_t = jnp.maximum(inter, jnp.max(d, axis=1, keepdims=True))
        w_intra = jnp.exp(d - m_t)
        w_inter = jnp.exp(inter - m_t)
        qb = qh.astype(BF16)
        kb = kh.astype(BF16)
        s = _dot_nt(qb, kb) * w_intra
        c_h = c_scr[h]
        n_h = n_scr[h:h + 1, :]
        num = _dot(s.astype(BF16), vh.astype(BF16)) + w_inter * _dot_nt(qb, c_h.astype(BF16))
        den = jnp.sum(s, axis=1, keepdims=True) + w_inter * jnp.sum(qh * n_h, axis=1, keepdims=True)
        hh = num / jnp.maximum(jnp.abs(den), jnp.exp(-m_t))

        b_last = b_c[L - 1:L, :]
        g_c = (b_last - b_c) + ig_c
        m_new = jnp.maximum(b_last + m_prev, jnp.max(g_c, axis=0, keepdims=True))
        w_state = jnp.exp(g_c - m_new)
        decay = jnp.exp(b_last + m_prev - m_new)
        vw_t = (vh * w_state).T.astype(BF16)
        c_scr[h] = decay * c_h + _dot(vw_t, kb)
        n_scr[h:h + 1, :] = decay * n_h + jnp.sum(kh * w_state, axis=0, keepdims=True)
        m_out = jnp.where(lane == h, m_new, m_out)

        hn = hh * lax.rsqrt(jnp.mean(hh * hh, axis=-1, keepdims=True) + EPS) * gh_ref[...]
        out_ref[0, :, hs] = jax.nn.sigmoid(og_ref[0, :, hs]) * hn
    m_scr[...] = m_out

    @pl.when(ci == pl.num_programs(1) - 1)
    def _():
        cout_ref[0] = c_scr[...]
        nout_ref[0] = n_scr[...]
        mout_ref[0] = m_scr[...]
        convout_ref[0] = prev_scr[8 - (CONV_W - 1):8, :]


def _mlstm(ml_qk, ml_v, ml_o, gates, conv_past, c0, n0, m0, w_conv, b_conv, g_head):
    b, s, w2 = ml_qk.shape
    width = w2 // 2
    nh, dh = ML_HEADS, ML_HEAD_DIM
    chunk = min(ML_CHUNK, s)
    assert s % chunk == 0 and chunk >= 8
    tok = lambda w: pl.BlockSpec((1, chunk, w), lambda bi, ci: (bi, ci, 0))
    perb = lambda shp: pl.BlockSpec((1,) + shp, lambda bi, ci: (bi,) + (0,) * len(shp))
    const = lambda shp: pl.BlockSpec(shp, lambda bi, ci: (0,) * len(shp))
    return pl.pallas_call(
        functools.partial(_mlstm_kernel, chunk=chunk),
        grid=(b, s // chunk),
        in_specs=[tok(w2), tok(width), tok(width), tok(LANES),
                  perb((CONV_W - 1, w2)), perb((nh, dh, dh)), perb((nh, dh)), perb((1, LANES)),
                  const((CONV_W, w2)), const((1, w2)), const((1, dh))],
        out_specs=[tok(width), perb((nh, dh, dh)), perb((nh, dh)), perb((1, LANES)), perb((CONV_W - 1, w2))],
        out_shape=[jax.ShapeDtypeStruct((b, s, width), F32),
                   jax.ShapeDtypeStruct((b, nh, dh, dh), F32),
                   jax.ShapeDtypeStruct((b, nh, dh), F32),
                   jax.ShapeDtypeStruct((b, 1, LANES), F32),
                   jax.ShapeDtypeStruct((b, CONV_W - 1, w2), F32)],
        scratch_shapes=[pltpu.VMEM((8 + chunk, w2), F32), pltpu.VMEM((nh, dh, dh), F32),
                        pltpu.VMEM((nh, dh), F32), pltpu.VMEM((1, LANES), F32)],
        compiler_params=_cparams(("arbitrary", "arbitrary")),
        name="mlstm",
    )(ml_qk, ml_v, ml_o, gates, conv_past, c0, n0, m0, w_conv, b_conv, g_head)


def _outproj_kernel(osb_ref, oml_ref, x_ref, mod_ref, w1_ref, w2_ref, g2_ref, wrh_ref, wrl_ref, br_ref,
                    x1_ref, h2_ref, ti_ref, tg_ref):
    mod = mod_ref[0]
    ga1 = mod[2:3]
    sh2 = mod[3:4]
    sc2 = mod[4:5]
    mix = _dot(osb_ref[0].astype(BF16), w1_ref[...]) + _dot(oml_ref[0].astype(BF16), w2_ref[...])
    x1 = x_ref[0] + ga1 * mix
    x1_ref[0] = x1
    ms = jnp.mean(x1 * x1, axis=-1, keepdims=True)
    h2 = x1 * lax.rsqrt(ms + EPS) * g2_ref[...]
    h2 = h2 * (1.0 + sc2) + sh2
    _store_row_tiles(h2_ref, h2)
    logits = _dot3(h2, wrh_ref[...], wrl_ref[...]) + br_ref[...]
    lane = lax.broadcasted_iota(I32, logits.shape, 1)
    vals, idxs = [], []
    cur = logits
    for _ in range(TOP_K):
        m = jnp.max(cur, axis=-1, keepdims=True)
        idx = jnp.min(jnp.where(cur == m, lane, LANES), axis=-1, keepdims=True)
        vals.append(m)
        idxs.append(idx)
        cur = jnp.where(lane == idx, -jnp.inf, cur)
    es = [jnp.exp(v - vals[0]) for v in vals]
    tot = es[0] + es[1] + es[2] + es[3]
    ti = jnp.zeros(logits.shape, I32)
    tg = jnp.zeros(logits.shape, F32)
    for k in range(TOP_K):
        ti = jnp.where(lane == k, idxs[k], ti)
        tg = jnp.where(lane == k, es[k] / tot, tg)
    ti_ref[0] = ti
    tg_ref[0] = tg


def _outproj(o_sb, o_ml, x, mod, w1, w2, g2, wr_hi, wr_lo, br):
    b, s, d = x.shape
    hw = o_sb.shape[2]
    tm = min(ROW_TILE, s)
    tok = lambda w: pl.BlockSpec((1, tm, w), lambda bi, i: (bi, i, 0))
    const = lambda shp: pl.BlockSpec(shp, lambda bi, i: (0,) * len(shp))
    return pl.pallas_call(
        _outproj_kernel,
        grid=(b, s // tm),
        in_specs=[tok(hw), tok(hw), tok(d), pl.BlockSpec((1, 6, d), lambda bi, i: (bi, 0, 0)),
                  const((hw, d)), const((hw, d)), const((1, d)),
                  const((d, LANES)), const((d, LANES)), const((1, LANES))],
        out_specs=[tok(d), pl.BlockSpec((tm * SUBLANES, LANES), lambda bi, i: (bi * (s // tm) + i, 0)),
                   tok(LANES), tok(LANES)],
        out_shape=[jax.ShapeDtypeStruct((b, s, d), F32), jax.ShapeDtypeStruct((b * s * SUBLANES, LANES), F32),
                   jax.ShapeDtypeStruct((b, s, LANES), I32), jax.ShapeDtypeStruct((b, s, LANES), F32)],
        compiler_params=_cparams(("arbitrary", "arbitrary")),
        name="outproj_router",
    )(o_sb, o_ml, x, mod, w1, w2, g2, wr_hi, wr_lo, br)


def _rank_kernel(ti_ref, rank_ref, cnt_ref, carry_scr):
    i = pl.program_id(0)

    @pl.when(i == 0)
    def _():
        carry_scr[...] = jnp.zeros_like(carry_scr)

    ti = ti_ref[...]
    tm = ti.shape[0]
    lane = lax.broadcasted_iota(I32, ti.shape, 1)
    hit = jnp.zeros(ti.shape, F32)
    for k in range(TOP_K):
        hit = hit + jnp.where(lane == ti[:, k:k + 1], 1.0, 0.0)
    row = lax.broadcasted_iota(I32, (tm, tm), 0)
    col = lax.broadcasted_iota(I32, (tm, tm), 1)
    strict = jnp.where(col < row, 1.0, 0.0).astype(BF16)
    excl = _dot(strict, hit.astype(BF16)) + carry_scr[...]
    rank = jnp.zeros(ti.shape, F32)
    for k in range(TOP_K):
        rk = jnp.sum(jnp.where(lane == ti[:, k:k + 1], excl, 0.0), axis=-1, keepdims=True)
        rank = jnp.where(lane == k, rk, rank)
    rank_ref[...] = rank.astype(I32)
    carry_scr[...] = carry_scr[...] + jnp.sum(hit, axis=0, keepdims=True)
    cnt_ref[...] = carry_scr[...].astype(I32)


def _ranks(top_idx):
    t = top_idx.shape[0]
    tm = ROW_TILE
    assert t % tm == 0
    return pl.pallas_call(
        _rank_kernel,
        grid=(t // tm,),
        in_specs=[pl.BlockSpec((tm, LANES), lambda i: (i, 0))],
        out_specs=[pl.BlockSpec((tm, LANES), lambda i: (i, 0)), pl.BlockSpec((1, LANES), lambda i: (0, 0))],
        out_shape=[jax.ShapeDtypeStruct((t, LANES), I32), jax.ShapeDtypeStruct((1, LANES), I32)],
        scratch_shapes=[pltpu.VMEM((1, LANES), F32)],
        compiler_params=_cparams(("arbitrary",)),
        name="expert_ranks",
    )(top_idx)


def _row_tile(ref, r):
    return ref.at[pl.ds(pl.multiple_of(r * SUBLANES, SUBLANES), SUBLANES), :]


def _dispatch_kernel(*refs, tm):
    pos_hbm, h_ref = refs[:2]
    xs_hbm, pos_smem0, pos_smem1, stage, pos_sem, row_sem = refs[-6:]
    i = pl.program_id(0)
    n = pl.num_programs(0)
    slot = lax.rem(i, 2)
    nidx = tm * TOP_K

    def pos_copy(tile, sl):
        return pltpu.make_async_copy(pos_hbm.at[pl.ds(pl.multiple_of(tile * nidx, nidx), nidx)],
                                     (pos_smem0, pos_smem1)[sl], pos_sem.at[sl])

    def drain(sl):
        for _ in range(TOP_K):
            pltpu.make_async_copy(stage.at[sl], stage.at[sl], row_sem.at[sl]).wait()

    @pl.when(i == 0)
    def _():
        pos_copy(0, 0).start()

    for sl in range(2):
        @pl.when(jnp.logical_and(i + 1 < n, slot == 1 - sl))
        def _():
            pos_copy(i + 1, sl).start()

    @pl.when(i >= 2)
    def _():
        drain(slot)

    stage[slot] = h_ref[...]

    for sl in range(2):
        @pl.when(slot == sl)
        def _():
            pos_copy(i, sl).wait()
            pos_smem = (pos_smem0, pos_smem1)[sl]

            def issue(c, carry):
                for rr in range(ISSUE_UNROLL):
                    r = c * ISSUE_UNROLL + rr
                    for k in range(TOP_K):
                        pltpu.make_async_copy(_row_tile(stage.at[sl], r), _row_tile(xs_hbm, pos_smem[TOP_K * r + k]),
                                              row_sem.at[sl]).start(priority=k % DMA_QUEUES)
                return carry
            lax.fori_loop(0, tm // ISSUE_UNROLL, issue, 0)

    @pl.when(i == n - 1)
    def _():
        drain(slot)

        @pl.when(n >= 2)
        def _():
            drain(1 - slot)


def _dispatch(pos_flat, h_tiles, xs=None, n_rows=None):
    t = h_tiles.shape[0] // SUBLANES
    tm = GATHER_TILE
    assert t % tm == 0 and tm % ISSUE_UNROLL == 0
    in_specs = [pl.BlockSpec(memory_space=pl.ANY), pl.BlockSpec((tm * SUBLANES, LANES), lambda i: (i, 0))]
    args = [pos_flat, h_tiles]
    if xs is not None:
        in_specs.append(pl.BlockSpec(memory_space=pl.ANY))
        args.append(xs)
    return pl.pallas_call(
        functools.partial(_dispatch_kernel, tm=tm),
        grid=(t // tm,),
        in_specs=in_specs,
        out_specs=pl.BlockSpec(memory_space=pl.ANY),
        out_shape=jax.ShapeDtypeStruct((n_rows * SUBLANES, LANES) if xs is None else xs.shape, F32),
        scratch_shapes=[pltpu.SMEM((tm * TOP_K,), I32), pltpu.SMEM((tm * TOP_K,), I32),
                        pltpu.VMEM((2, tm * SUBLANES, LANES), F32),
                        pltpu.SemaphoreType.DMA((2,)), pltpu.SemaphoreType.DMA((2,))],
        input_output_aliases={} if xs is None else {2: 0},
        compiler_params=_cparams(("arbitrary",)),
        name="moe_dispatch",
    )(*args)


def _pad_fill_kernel(pos_hbm, xs_in_hbm, xs_hbm, pos_smem, zero_tile, pos_sem, row_sem, *, nidx):
    del xs_in_hbm
    i = pl.program_id(0)
    cp = pltpu.make_async_copy(pos_hbm.at[pl.ds(pl.multiple_of(i * nidx, nidx), nidx)], pos_smem, pos_sem)
    cp.start()
    zero_tile[...] = jnp.zeros_like(zero_tile)
    cp.wait()

    def issue(c, carry):
        for rr in range(TOP_K * ISSUE_UNROLL):
            r = c * (TOP_K * ISSUE_UNROLL) + rr
            pltpu.make_async_copy(zero_tile, _row_tile(xs_hbm, pos_smem[r]), row_sem).start(
                priority=rr % DMA_QUEUES)
        return carry
    lax.fori_loop(0, nidx // (TOP_K * ISSUE_UNROLL), issue, 0)
    whole = xs_hbm.at[pl.ds(0, nidx * SUBLANES), :]
    pltpu.make_async_copy(whole, whole, row_sem).wait()


def _pad_fill(pad_pos, xs):
    nidx = GATHER_TILE * TOP_K
    assert pad_pos.shape[0] % nidx == 0
    return pl.pallas_call(
        functools.partial(_pad_fill_kernel, nidx=nidx),
        grid=(pad_pos.shape[0] // nidx,),
        in_specs=[pl.BlockSpec(memory_space=pl.ANY), pl.BlockSpec(memory_space=pl.ANY)],
        out_specs=pl.BlockSpec(memory_space=pl.ANY),
        out_shape=jax.ShapeDtypeStruct(xs.shape, xs.dtype),
        scratch_shapes=[pltpu.SMEM((nidx,), I32), pltpu.VMEM((SUBLANES, LANES), F32),
                        pltpu.SemaphoreType.DMA, pltpu.SemaphoreType.DMA],
        input_output_aliases={1: 0},
        compiler_params=_cparams(("arbitrary",)),
        name="moe_pad_fill",
    )(pad_pos, xs)


def _moe_kernel(be_ref, nu_ref, xs_ref, wgu_ref, bgu_ref, wd_ref, bd_ref, ys_ref, wgu_bf, wd_bf):
    i = pl.program_id(0)
    d, ff = wgu_ref.shape[1], wd_ref.shape[1]

    @pl.when(jnp.logical_and(i < nu_ref[0],
                             jnp.logical_or(i == 0, be_ref[i] != be_ref[jnp.maximum(i - 1, 0)])))
    def _():
        for r in range(0, d, LANES):
            wgu_bf[r:r + LANES, :] = wgu_ref[0, r:r + LANES, :].astype(BF16)
        for r in range(0, ff, LANES):
            wd_bf[r:r + LANES, :] = wd_ref[0, r:r + LANES, :].astype(BF16)

    @pl.when(i < nu_ref[0])
    def _():
        x = _load_row_tiles(xs_ref, MOE_BLK).astype(BF16)
        glu = _dot(x, wgu_bf[:, 0:ff]) + bgu_ref[0, :, 0:ff]
        lin = _dot(x, wgu_bf[:, ff:2 * ff]) + bgu_ref[0, :, ff:2 * ff]
        glu = jnp.minimum(glu, SWIGLU_LIMIT)
        lin = jnp.clip(lin, -SWIGLU_LIMIT, SWIGLU_LIMIT)
        act = glu * jax.nn.sigmoid(SWIGLU_ALPHA * glu) * (lin + 1.0)
        _store_row_tiles(ys_ref, _dot(act.astype(BF16), wd_bf[...]) + bd_ref[0])

    @pl.when(i >= nu_ref[0])
    def _():
        ys_ref[...] = jnp.zeros_like(ys_ref)


def _moe_mlp(block_expert, n_used, xs, w_gu, b_gu, w_down, b_down):
    ne, d, ff2 = w_gu.shape
    ff = ff2 // 2
    tb = MOE_BLK
    n_rows = block_expert.shape[0] * tb
    assert xs.shape[0] >= n_rows * SUBLANES
    blk = lambda i, be, nu: jnp.minimum(i, nu[0] - 1)
    grid_spec = pltpu.PrefetchScalarGridSpec(
        num_scalar_prefetch=2,
        grid=(n_rows // tb,),
        in_specs=[pl.BlockSpec((tb * SUBLANES, LANES), lambda i, be, nu: (blk(i, be, nu), 0)),
                  pl.BlockSpec((1, d, ff2), lambda i, be, nu: (be[blk(i, be, nu)], 0, 0)),
                  pl.BlockSpec((1, 1, ff2), lambda i, be, nu: (be[blk(i, be, nu)], 0, 0)),
                  pl.BlockSpec((1, ff, d), lambda i, be, nu: (be[blk(i, be, nu)], 0, 0)),
                  pl.BlockSpec((1, 1, d), lambda i, be, nu: (be[blk(i, be, nu)], 0, 0))],
        out_specs=pl.BlockSpec((tb * SUBLANES, LANES), lambda i, be, nu: (i, 0)),
        scratch_shapes=[pltpu.VMEM((d, ff2), BF16), pltpu.VMEM((ff, d), BF16)],
    )
    return pl.pallas_call(
        _moe_kernel,
        grid_spec=grid_spec,
        out_shape=jax.ShapeDtypeStruct((n_rows * SUBLANES, LANES), F32),
        compiler_params=_cparams(("arbitrary",)),
        name="moe_mlp",
    )(block_expert, n_used, xs, w_gu, b_gu.reshape(ne, 1, ff2), w_down, b_down.reshape(ne, 1, d))


def _combine_kernel(pos_hbm, ys_hbm, x1_ref, tg_ref, mod_ref, y_ref, pos_smem0, pos_smem1, buf, pos_sem, row_sem,
                    *, tm, nb):
    i = pl.program_id(0)
    n = pl.num_programs(0)
    slot = lax.rem(i, 2)
    nidx = tm * TOP_K

    def pos_copy(tile, sl):
        return pltpu.make_async_copy(pos_hbm.at[pl.ds(pl.multiple_of(tile * nidx, nidx), nidx)],
                                     (pos_smem0, pos_smem1)[sl], pos_sem.at[sl])

    def gather(tile, sl):
        pos_copy(tile, sl).wait()
        pos_smem = (pos_smem0, pos_smem1)[sl]

        def issue(c, carry):
            for rr in range(ISSUE_UNROLL):
                r = c * ISSUE_UNROLL + rr
                for k in range(TOP_K):
                    pltpu.make_async_copy(_row_tile(ys_hbm, pos_smem[TOP_K * r + k]),
                                          _row_tile(buf.at[sl, k], r), row_sem.at[sl]).start(
                                              priority=k % DMA_QUEUES)
            return carry
        lax.fori_loop(0, tm // ISSUE_UNROLL, issue, 0)

        @pl.when(tile + 1 < n)
        def _():
            pos_copy(tile + 1, 1 - sl).start()

    @pl.when(i == 0)
    def _():
        pos_copy(0, 0).start()
        gather(0, 0)

    for sl in range(2):
        @pl.when(jnp.logical_and(i + 1 < n, slot == 1 - sl))
        def _():
            gather(i + 1, sl)

    for k in range(TOP_K):
        pltpu.make_async_copy(buf.at[slot, k], buf.at[slot, k], row_sem.at[slot]).wait()

    tg = tg_ref[...]
    rows = tm // nb
    for c in range(SUBLANES):
        cols = slice(c * LANES, (c + 1) * LANES)
        ff = tg[:, 0:1] * buf[slot, 0, pl.ds(c, tm, stride=SUBLANES), :]
        for k in range(1, TOP_K):
            ff = ff + tg[:, k:k + 1] * buf[slot, k, pl.ds(c, tm, stride=SUBLANES), :]
        for g in range(nb):
            rs = slice(g * rows, (g + 1) * rows)
            y_ref[rs, cols] = x1_ref[rs, cols] + mod_ref[g, 5:6, cols] * ff[rs, :]


def _combine(pos_flat, ys, x1, top_gate, mod, tokens_per_batch):
    t, d = x1.shape
    tm = GATHER_TILE
    assert t % tm == 0 and tm % ISSUE_UNROLL == 0 and d == SUBLANES * LANES
    if tokens_per_batch >= tm:
        assert tokens_per_batch % tm == 0
        nb = 1
        per = tokens_per_batch // tm
        mod_spec = pl.BlockSpec((1, 6, d), lambda i: (i // per, 0, 0))
    else:
        assert tm % tokens_per_batch == 0
        nb = tm // tokens_per_batch
        mod_spec = pl.BlockSpec((nb, 6, d), lambda i: (i, 0, 0))
    return pl.pallas_call(
        functools.partial(_combine_kernel, tm=tm, nb=nb),
        grid=(t // tm,),
        in_specs=[pl.BlockSpec(memory_space=pl.ANY), pl.BlockSpec(memory_space=pl.ANY),
                  pl.BlockSpec((tm, d), lambda i: (i, 0)), pl.BlockSpec((tm, LANES), lambda i: (i, 0)), mod_spec],
        out_specs=pl.BlockSpec((tm, d), lambda i: (i, 0)),
        out_shape=jax.ShapeDtypeStruct((t, d), F32),
        scratch_shapes=[pltpu.SMEM((tm * TOP_K,), I32), pltpu.SMEM((tm * TOP_K,), I32),
                        pltpu.VMEM((2, TOP_K, tm * SUBLANES, LANES), F32),
                        pltpu.SemaphoreType.DMA((2,)), pltpu.SemaphoreType.DMA((2,))],
        compiler_params=_cparams(("arbitrary",)),
        name="moe_combine",
    )(pos_flat, ys, x1, top_gate, mod)


def _pad_lanes(a, fill=0.0):
    return jnp.pad(a, [(0, 0)] * (a.ndim - 1) + [(0, LANES - a.shape[-1])], constant_values=fill)


def _hi_lo(w):
    hi = w.astype(BF16)
    return hi, (w - hi.astype(F32)).astype(BF16)


def _mixer(x, mod, k_past, v_past, conv_past, c0, n0, m0, p):
    b, s, d = x.shape
    q_raw, k_norm, v, ml_qk, ml_v, ml_o, gates, k_tiles, v_tiles = _inproj(
        x, mod, p["g1"], p["w_main"], p["wg_hi"], p["wg_lo"], p["bg"], p["g_k"], p["widths"])
    o_sb = _sb_attention(q_raw, k_norm, v, k_past, v_past, p["g_q"])
    o_ml, c_new, n_new, m_new, conv_new = _mlstm(
        ml_qk, ml_v, ml_o, gates, conv_past, c0, n0, _pad_lanes(m0)[:, None, :],
        p["w_conv"], p["b_conv"], p["g_head"])
    x1, h2, ti, tg = _outproj(o_sb, o_ml, x, mod, p["w_out1"], p["w_out2"], p["g2"],
                              p["wr_hi"], p["wr_lo"], p["br"])
    assert v.shape[-1] == SUBLANES * SB_HEAD_DIM
    state = (k_tiles.reshape(b, s, SUBLANES, SB_HEAD_DIM), v_tiles.reshape(b, s, SUBLANES, SB_HEAD_DIM), conv_new,
             c_new, n_new, m_new[:, 0, :ML_HEADS])
    return x1.reshape(b * s, d), h2, ti.reshape(b * s, LANES), tg.reshape(b * s, LANES), state


def _layer(xp, xs_, cp, cs, kc, vc, convc, cc, nc, mc, w_ada, b_ada, g_norm1, w_in, g_q, g_k, w_conv, b_conv,
           b_gate, g_head, w_out, g_norm2, w_router, b_router, w_gu, b_gu, w_down, b_down):
    d = xp.shape[-1]
    bp, sp = xp.shape[:2]
    bs, ss = xs_.shape[:2]
    sbw = kc.shape[-2] * kc.shape[-1]
    mlw = ML_HEADS * ML_HEAD_DIM
    ngate = 2 * ML_HEADS
    gate0 = 3 * sbw + 3 * mlw
    w_main = jnp.concatenate([w_in[:, :gate0], w_in[:, gate0 + ngate:]], axis=1).astype(BF16)
    wg_hi, wg_lo = _hi_lo(_pad_lanes(w_in[:, gate0:gate0 + ngate]))
    wr_hi, wr_lo = _hi_lo(_pad_lanes(w_router))
    p = dict(
        widths=(sbw, sbw, sbw, 2 * mlw, mlw, mlw),
        g1=g_norm1[None], w_main=w_main, wg_hi=wg_hi, wg_lo=wg_lo, bg=_pad_lanes(b_gate)[None],
        g_q=jnp.tile(g_q, LANES // SB_HEAD_DIM)[None], g_k=jnp.tile(g_k, LANES // SB_HEAD_DIM)[None],
        w_conv=w_conv, b_conv=b_conv[None], g_head=g_head[None],
        w_out1=w_out[:sbw].astype(BF16), w_out2=w_out[sbw:].astype(BF16), g2=g_norm2[None],
        wr_hi=wr_hi, wr_lo=wr_lo, br=_pad_lanes(b_router, -1e30)[None],
    )
    mod = _ada(jnp.concatenate([cp, cs], axis=0), w_ada, b_ada)
    mod_p = mod[:bp].reshape(bp, 6, d)
    mod_s = mod[bp:].reshape(bs, 6, d)

    zeros = lambda *shp: jnp.zeros(shp, F32)
    x1p, h2p, tip, tgp, st_p = _mixer(
        xp, mod_p, None, None, zeros(bp, CONV_W - 1, 2 * mlw), zeros(bp, ML_HEADS, ML_HEAD_DIM, ML_HEAD_DIM),
        zeros(bp, ML_HEADS, ML_HEAD_DIM), zeros(bp, ML_HEADS), p)
    x1s, h2s, tis, tgs, st_s = _mixer(
        xs_, mod_s, kc.reshape(bs, -1, sbw), vc.reshape(bs, -1, sbw), convc, cc, nc, mc, p)

    tp, ts = bp * sp, bs * ss
    ti_all = jnp.concatenate([tip, tis], axis=0)
    rank, counts = _ranks(ti_all)
    counts = counts[0, :N_EXPERTS]
    padded = (counts + MOE_BLK - 1) // MOE_BLK * MOE_BLK
    pad_end = jnp.cumsum(padded)
    pad_start = pad_end - padded
    top_idx = ti_all[:, :TOP_K]
    pos = (pad_start[top_idx] + rank[:, :TOP_K]).astype(I32)
    n_blocks = -(-((tp + ts) * TOP_K + N_EXPERTS * (MOE_BLK - 1)) // MOE_BLK)
    n_used = (pad_end[-1] // MOE_BLK).astype(I32).reshape(1)
    block_start = jnp.arange(n_blocks, dtype=I32) * MOE_BLK
    block_expert = jnp.minimum(jnp.sum(block_start[:, None] >= pad_end[None, :], axis=1), N_EXPERTS - 1).astype(I32)
    pos_p = pos[:tp].reshape(-1)
    pos_s = pos[tp:].reshape(-1)

    n_rows = n_blocks * MOE_BLK
    slot = jnp.arange(MOE_BLK, dtype=I32)[None, :]
    spare = n_rows + jnp.arange(N_EXPERTS * MOE_BLK, dtype=I32).reshape(N_EXPERTS, MOE_BLK)
    pad_pos = jnp.where(slot < (padded - counts)[:, None], (pad_start + counts)[:, None] + slot, spare).reshape(-1)

    xs_buf = _dispatch(pos_p, h2p, n_rows=n_rows + N_EXPERTS * MOE_BLK)
    xs_buf = _dispatch(pos_s, h2s, xs_buf)
    xs_buf = _pad_fill(pad_pos.astype(I32), xs_buf)
    ys = _moe_mlp(block_expert, n_used, xs_buf, w_gu, b_gu, w_down, b_down)
    yp = _combine(pos_p, ys, x1p, tgp, mod_p, sp).reshape(bp, sp, d)
    ysm = _combine(pos_s, ys, x1s, tgs, mod_s, ss).reshape(bs, ss, d)
    return yp, ysm, st_p, st_s


def kernel(x_prompt, x_sample, c_prompt, c_sample, cache_sb_k, cache_sb_v, state_conv, state_mlstm_c, state_mlstm_n, state_mlstm_m, w_ada, b_ada, g_norm1, w_in, g_q, g_k, w_conv, b_conv, b_gate, g_head, w_out, g_norm2, w_router, b_router, w_gu, b_gu, w_down, b_down):
    assert w_ada.shape[0] == 1, "single-layer step"
    yp, ys, st_p, st_s = _layer(
        x_prompt, x_sample, c_prompt, c_sample, cache_sb_k[0], cache_sb_v[0], state_conv[0], state_mlstm_c[0],
        state_mlstm_n[0], state_mlstm_m[0], w_ada[0], b_ada[0], g_norm1[0], w_in[0], g_q[0], g_k[0], w_conv[0],
        b_conv[0], b_gate[0], g_head[0], w_out[0], g_norm2[0], w_router[0], b_router[0], w_gu[0], b_gu[0],
        w_down[0], b_down[0])
    return (yp, ys) + tuple(a[None] for a in st_p) + tuple(a[None] for a in st_s)
```

```python
import functools
import math

import jax
import jax.numpy as jnp
from jax import lax
from jax.experimental import pallas as pl
from jax.experimental.pallas import tpu as pltpu

F32 = jnp.float32
BF16 = jnp.bfloat16
I32 = jnp.int32

EPS = 1e-6
LANES = 128
SUBLANES = 8
SB_HEAD_DIM = 64
ML_HEAD_DIM = 128
ML_HEADS = 4
CONV_W = 4
N_EXPERTS = 32
TOP_K = 4
SWIGLU_LIMIT = 7.0
SWIGLU_ALPHA = 1.702
VMEM_LIMIT = 56 * 1024 * 1024

ROW_TILE = 512
MOE_BLK = 512
GATHER_TILE = 256
ISSUE_UNROLL = 8
DMA_QUEUES = 2
ML_CHUNK = 256
SB_BLK = 128
SB_SWEEP = 256
SB_UNDERFLOW = -105.0


def _cparams(sem):
    return pltpu.CompilerParams(dimension_semantics=sem, vmem_limit_bytes=VMEM_LIMIT)


def _split3(x):
    p1 = x.astype(BF16)
    r1 = x - p1.astype(F32)
    p2 = r1.astype(BF16)
    p3 = (r1 - p2.astype(F32)).astype(BF16)
    return p1, p2, p3


def _dot(a, b):
    return jnp.dot(a, b, preferred_element_type=F32)


def _dot_nt(a, b):
    return lax.dot_general(a, b, (((1,), (1,)), ((), ())), preferred_element_type=F32)


def _dot3(x, w_hi, w_lo):
    xh = x.astype(BF16)
    xl = (x - xh.astype(F32)).astype(BF16)
    return _dot(xh, w_hi) + _dot(xl, w_hi) + _dot(xh, w_lo)


def _store_row_tiles(ref, x):
    n, d = x.shape
    assert d == SUBLANES * LANES
    for c in range(SUBLANES):
        ref[pl.ds(c, n, stride=SUBLANES), :] = x[:, c * LANES:(c + 1) * LANES]


def _load_row_tiles(ref, n):
    return jnp.concatenate([ref[pl.ds(c, n, stride=SUBLANES), :] for c in range(SUBLANES)], axis=1)


def _store_head_tiles(ref, row0, head0, x):
    n = x.shape[0]
    for j in range(x.shape[1] // SB_HEAD_DIM):
        ref[pl.ds(row0 * SUBLANES + head0 + j, n, stride=SUBLANES), :] = x[:, j * SB_HEAD_DIM:(j + 1) * SB_HEAD_DIM]


def _headnorm(x, g):
    lo_half = lax.broadcasted_iota(I32, (1, LANES), 1) < SB_HEAD_DIM
    x2 = x * x
    s0 = jnp.sum(jnp.where(lo_half, x2, 0.0), axis=-1, keepdims=True)
    s1 = jnp.sum(jnp.where(lo_half, 0.0, x2), axis=-1, keepdims=True)
    r = jnp.where(lo_half, lax.rsqrt(s0 * (1.0 / SB_HEAD_DIM) + EPS), lax.rsqrt(s1 * (1.0 / SB_HEAD_DIM) + EPS))
    return x * r * g


def _log_sigmoid(x):
    return jnp.minimum(x, 0.0) - jnp.log1p(jnp.exp(-jnp.abs(x)))


def _ada_kernel(c_ref, w_ref, b_ref, o_ref):
    c = c_ref[...]
    s = c * jax.nn.sigmoid(c)
    o_ref[...] = jnp.dot(s, w_ref[...], precision=lax.Precision.HIGHEST,
                         preferred_element_type=F32) + b_ref[...]


def _ada(c, w_ada, b_ada):
    n, d = c.shape
    cols = w_ada.shape[1]
    tn = cols // 6
    return pl.pallas_call(
        _ada_kernel,
        grid=(cols // tn,),
        in_specs=[pl.BlockSpec((n, d), lambda j: (0, 0)),
                  pl.BlockSpec((d, tn), lambda j: (0, j)),
                  pl.BlockSpec((1, tn), lambda j: (0, j))],
        out_specs=pl.BlockSpec((n, tn), lambda j: (0, j)),
        out_shape=jax.ShapeDtypeStruct((n, cols), F32),
        compiler_params=_cparams(("arbitrary",)),
        name="ada_mod",
    )(c, w_ada, b_ada.reshape(1, cols))


def _inproj_kernel(x_ref, mod_ref, g1_ref, w_ref, wgh_ref, wgl_ref, bg_ref, gk_ref,
                   q_ref, k_ref, v_ref, qk_ref, mv_ref, mo_ref, gate_ref, kt_ref, vt_ref, *, widths):
    x = x_ref[0]
    mod = mod_ref[0]
    sh1 = mod[0:1]
    sc1 = mod[1:2]
    ms = jnp.mean(x * x, axis=-1, keepdims=True)
    h = x * lax.rsqrt(ms + EPS) * g1_ref[...]
    h = h * (1.0 + sc1) + sh1
    hb = h.astype(BF16)
    off = 0
    for ref, wd in zip((q_ref, k_ref, v_ref, qk_ref, mv_ref, mo_ref), widths):
        u = _dot(hb, w_ref[:, off:off + wd])
        if ref is k_ref:
            u = jnp.concatenate([_headnorm(u[:, c:c + LANES], gk_ref[...]) for c in range(0, wd, LANES)], axis=1)
            _store_head_tiles(kt_ref, 0, 0, u)
        if ref is v_ref:
            _store_head_tiles(vt_ref, 0, 0, u)
        ref[0] = u
        off += wd
    gate_ref[0] = _dot3(h, wgh_ref[...], wgl_ref[...]) + bg_ref[...]


def _inproj(x, mod, g1, w_main, wg_hi, wg_lo, bg, g_k, widths):
    b, s, d = x.shape
    tm = min(ROW_TILE, s)
    ncol = w_main.shape[1]
    tok = lambda w: pl.BlockSpec((1, tm, w), lambda bi, i: (bi, i, 0))
    const = lambda shp: pl.BlockSpec(shp, lambda bi, i: (0,) * len(shp))
    head_tiles = pl.BlockSpec((tm * SUBLANES, SB_HEAD_DIM), lambda bi, i: (bi * (s // tm) + i, 0))
    return pl.pallas_call(
        functools.partial(_inproj_kernel, widths=widths),
        grid=(b, s // tm),
        in_specs=[tok(d),
                  pl.BlockSpec((1, 6, d), lambda bi, i: (bi, 0, 0)),
                  const((1, d)), const((d, ncol)), const((d, LANES)), const((d, LANES)), const((1, LANES)),
                  const((1, LANES))],
        out_specs=[tok(w) for w in widths] + [tok(LANES), head_tiles, head_tiles],
        out_shape=[jax.ShapeDtypeStruct((b, s, w), F32) for w in widths]
        + [jax.ShapeDtypeStruct((b, s, LANES), F32)]
        + [jax.ShapeDtypeStruct((b * s * SUBLANES, SB_HEAD_DIM), F32)] * 2,
        compiler_params=_cparams(("arbitrary", "arbitrary")),
        name="inproj",
    )(x, mod, g1, w_main, wg_hi, wg_lo, bg, g_k)


def _sb_kernel(*refs, past, seq, tq, kb, npair):
    if past:
        q_ref, k_ref, v_ref, kp_ref, vp_ref, gq_ref, o_ref, ks_scr, vs_scr, run_scr, acc_scr = refs
    else:
        q_ref, k_ref, v_ref, gq_ref, o_ref, ks_scr, vs_scr, run_scr, acc_scr = refs
    i = pl.program_id(1)
    lane = lax.broadcasted_iota(I32, (1, LANES), 1)
    lo_half = lane < SB_HEAD_DIM
    pairs = [slice(hp * LANES, (hp + 1) * LANES) for hp in range(npair)]

    @pl.when(i == 0)
    def _():
        ch = min(256, seq)
        for c in range(seq // ch):
            rows = slice(c * ch, (c + 1) * ch)
            dst = slice(past + c * ch, past + (c + 1) * ch)
            ks_scr[dst, :] = k_ref[0, rows, :].astype(BF16)
            vs_scr[dst, :] = v_ref[0, rows, :].astype(BF16)
        if past:
            for src, dst_scr in ((kp_ref, ks_scr), (vp_ref, vs_scr)):
                for c in range(past // ch):
                    heads = [src[pl.ds(c * ch * SUBLANES + h, ch, stride=SUBLANES), :] for h in range(2 * npair)]
                    dst_scr[c * ch:(c + 1) * ch, :] = jnp.concatenate(heads, axis=1).astype(BF16)

    qm = []
    for ps in pairs:
        q = _headnorm(q_ref[0, :, ps], gq_ref[...]) * (1.0 / math.sqrt(SB_HEAD_DIM))
        qm.append(jnp.concatenate([jnp.where(lo_half, q, 0.0), jnp.where(lo_half, 0.0, q)], axis=0).astype(BF16))
    nrow = 2 * npair * tq

    def neg_cum(nk):
        rj = lax.broadcasted_iota(I32, (2 * nk, LANES + nk), 0)
        cj = lax.broadcasted_iota(I32, (2 * nk, LANES + nk), 1)
        rjm = jnp.where(rj >= nk, rj - nk, rj)
        return jnp.where(cj < LANES, -1.0, jnp.where(rjm >= cj - LANES, -1.0, 0.0)).astype(BF16)

    w_diag = neg_cum(tq)
    w_sub = w_diag if tq == LANES else neg_cum(LANES)
    causal = (lax.broadcasted_iota(I32, (nrow, tq), 1)
              < jnp.bitwise_and(lax.broadcasted_iota(I32, (nrow, tq), 0), tq - 1))

    def sweep(row0, nk, diag):
        kv = [(ks_scr[pl.ds(row0, nk), ps], vs_scr[pl.ds(row0, nk), ps]) for ps in pairs]
        z = jnp.concatenate([_dot_nt(qm[hp], kv[hp][0]) for hp in range(npair)], axis=0)
        sp = jnp.maximum(z, 0.0) + jnp.log(1.0 + jnp.exp(-jnp.abs(z)))
        if diag:
            sp = jnp.where(causal, sp, 0.0)
        hi = sp.astype(BF16)
        lo = (sp - hi.astype(F32)).astype(BF16)
        sub = min(nk, LANES)
        w_neg = w_diag if sub == tq else w_sub
        run = None if diag else run_scr[...]
        args = [None] * (nk // sub)
        for j in reversed(range(nk // sub)):
            cols = slice(j * sub, (j + 1) * sub)
            t = _dot(jnp.concatenate([hi[:, cols], lo[:, cols]], axis=1), w_neg)
            args[j] = z[:, cols] + t[:, LANES:] if run is None else z[:, cols] + t[:, LANES:] + run[:, :sub]
            run = t[:, :LANES] if run is None else run + t[:, :LANES]
        run_scr[...] = run
        p = jnp.exp(args[0] if len(args) == 1 else jnp.concatenate(args, axis=1))
        if diag:
            p = jnp.where(causal, p, 0.0)
        p = p.astype(BF16)
        for hp in range(npair):
            r0 = 2 * hp * tq
            vblk = kv[hp][1]
            vm = jnp.concatenate([jnp.where(lo_half, vblk, 0.0), jnp.where(lo_half, 0.0, vblk)], axis=0)
            pv = _dot(jnp.concatenate([p[r0:r0 + tq], p[r0 + tq:r0 + 2 * tq]], axis=1), vm.astype(BF16))
            acc_scr[hp] = pv if diag else acc_scr[hp] + pv
        return jnp.max(run)

    q0 = past + i * tq
    worst0 = sweep(pl.multiple_of(q0, tq), tq, True)
    n_before = q0 // kb

    def cond(c):
        return jnp.logical_and(c[0] < n_before, c[1] > SB_UNDERFLOW)

    def body(c):
        row0 = q0 - (c[0] + 1) * kb
        return c[0] + 1, sweep(pl.multiple_of(row0, tq), kb, False)

    _, worst = lax.while_loop(cond, body, (jnp.int32(0), worst0))
    if tq < kb and seq > tq:
        @pl.when(jnp.logical_and(q0 - n_before * kb > 0, worst > SB_UNDERFLOW))
        def _():
            sweep(0, tq, False)
    for hp, ps in enumerate(pairs):
        o_ref[0, :, ps] = acc_scr[hp]


def _sb_attention(q_raw, k_norm, v, k_past, v_past, g_q):
    b, s, w = q_raw.shape
    past = 0 if k_past is None else k_past.shape[0] // (b * SUBLANES)
    tq = min(SB_BLK, s)
    kb = SB_SWEEP
    assert w == SUBLANES * SB_HEAD_DIM and past % min(256, s) == 0
    assert s % tq == 0 and past % kb == 0 and w % LANES == 0 and kb % LANES == 0 and (kb == 2 * tq or s == tq)
    npair = w // LANES
    qblk = pl.BlockSpec((1, tq, w), lambda bi, i: (bi, i, 0))
    full = lambda n: pl.BlockSpec((1, n, w), lambda bi, i: (bi, 0, 0))
    gspec = pl.BlockSpec((1, LANES), lambda bi, i: (0, 0))
    in_specs = [qblk, full(s), full(s)]
    args = [q_raw, k_norm, v]
    if past:
        cache = pl.BlockSpec((past * SUBLANES, SB_HEAD_DIM), lambda bi, i: (bi, 0))
        in_specs += [cache, cache]
        args += [k_past, v_past]
    in_specs += [gspec]
    args += [g_q]
    return pl.pallas_call(
        functools.partial(_sb_kernel, past=past, seq=s, tq=tq, kb=kb, npair=npair),
        grid=(b, s // tq),
        in_specs=in_specs,
        out_specs=qblk,
        out_shape=jax.ShapeDtypeStruct((b, s, w), F32),
        scratch_shapes=[pltpu.VMEM((past + s, w), BF16), pltpu.VMEM((past + s, w), BF16),
                        pltpu.VMEM((2 * npair * tq, LANES), F32), pltpu.VMEM((npair, tq, LANES), F32)],
        compiler_params=_cparams(("arbitrary", "arbitrary")),
        name="sb_attention",
    )(*args)


def _mlstm_kernel(qk_ref, v_ref, og_ref, g_ref, cp_ref, c0_ref, n0_ref, m0_ref, wc_ref, bc_ref, gh_ref,
                  out_ref, cout_ref, nout_ref, mout_ref, convout_ref,
                  prev_scr, c_scr, n_scr, m_scr, *, chunk):
    ci = pl.program_id(1)
    nh, dh = ML_HEADS, ML_HEAD_DIM
    width = nh * dh
    L = chunk

    @pl.when(ci == 0)
    def _():
        prev_scr[0:8, :] = jnp.zeros((8, prev_scr.shape[1]), F32)
        prev_scr[8 - (CONV_W - 1):8, :] = cp_ref[0]
        c_scr[...] = c0_ref[0]
        n_scr[...] = n0_ref[0]
        m_scr[...] = m0_ref[0]

    prev_scr[8:8 + L, :] = qk_ref[0]
    wc = wc_ref[...]
    acc = bc_ref[...] + prev_scr[8 - (CONV_W - 1):8 - (CONV_W - 1) + L, :] * wc[0:1]
    for j in range(1, CONV_W):
        o = 8 - (CONV_W - 1) + j
        acc = acc + prev_scr[o:o + L, :] * wc[j:j + 1]
    qk = acc * jax.nn.sigmoid(acc)
    prev_scr[0:8, :] = prev_scr[L:L + 8, :]

    gt = g_ref[0]
    lf = _log_sigmoid(gt)
    gt_t = gt.T
    lf_t = lf.T
    row = lax.broadcasted_iota(I32, (L, L), 0)
    col = lax.broadcasted_iota(I32, (L, L), 1)
    tri = row >= col
    tril = jnp.where(tri, 1.0, 0.0).astype(BF16)
    triu = jnp.where(row <= col, 1.0, 0.0).astype(BF16)
    p1, p2, p3 = _split3(lf)
    b_col = _dot(tril, p1) + _dot(tril, p2) + _dot(tril, p3)
    q1, q2, q3 = _split3(lf_t[0:8, :])
    b_row = _dot(q1, triu) + _dot(q2, triu) + _dot(q3, triu)

    lane = lax.broadcasted_iota(I32, (1, LANES), 1)
    m_vec = m_scr[...]
    m_out = m_vec
    for h in range(nh):
        hs = slice(h * dh, (h + 1) * dh)
        qb = qk[:, h * dh:(h + 1) * dh].astype(BF16)
        kh = qk[:, width + h * dh:width + (h + 1) * dh] * (dh ** -0.5)
        kb = kh.astype(BF16)
        v_t = v_ref[0, :, hs].T
        b_r = b_row[nh + h:nh + h + 1, :]
        ig_r = gt_t[h:h + 1, :]
        c_col = gt[:, h:h + 1] - b_col[:, nh + h:nh + h + 1]
        m_prev = m_vec[:, h:h + 1]

        d_t = jnp.where(row <= col, b_r + c_col, -jnp.inf)
        inter = b_r + m_prev
        m_t = jnp.maximum(inter, jnp.max(d_t, axis=0, keepdims=True))
        w_inter = jnp.exp(inter - m_t)
        s_t = _dot_nt(kb, qb) * jnp.exp(d_t - m_t)
        c_h = c_scr[h]
        n_h = n_scr[h:h + 1, :]
        n_hi = n_h.astype(BF16)
        n_lo = (n_h - n_hi.astype(F32)).astype(BF16)
        num_t = _dot(v_t.astype(BF16), s_t.astype(BF16)) + w_inter * _dot_nt(c_h.astype(BF16), qb)
        den = jnp.sum(s_t, axis=0, keepdims=True) + w_inter * (_dot_nt(n_hi, qb) + _dot_nt(n_lo, qb))
        hh_t = num_t / jnp.maximum(jnp.abs(den), jnp.exp(-m_t))

        b_last = b_r[:, L - 1:L]
        g_r = (b_last - b_r) + ig_r
        m_new = jnp.maximum(b_last + m_prev, jnp.max(g_r, axis=1, keepdims=True))
        w_state = jnp.exp(g_r - m_new)
        decay = jnp.exp(b_last + m_prev - m_new)
        c_scr[h] = decay * c_h + _dot((v_t * w_state).astype(BF16), kb)
        w_hi = w_state.astype(BF16)
        w_lo = (w_state - w_hi.astype(F32)).astype(BF16)
        n_scr[h:h + 1, :] = decay * n_h + (_dot(w_hi, kb) + _dot(w_lo, kb))
        m_out = jnp.where(lane == h, m_new, m_out)

        hn_t = hh_t * lax.rsqrt(jnp.mean(hh_t * hh_t, axis=0, keepdims=True) + EPS) * gh_ref[...]
        out_ref[0, :, hs] = jax.nn.sigmoid(og_ref[0, :, hs]) * hn_t.T
    m_scr[...] = m_out

    @pl.when(ci == pl.num_programs(1) - 1)
    def _():
        cout_ref[0] = c_scr[...]
        nout_ref[0] = n_scr[...]
        mout_ref[0] = m_scr[...]
        convout_ref[0] = prev_scr[8 - (CONV_W - 1):8, :]


def _mlstm(ml_qk, ml_v, ml_o, gates, conv_past, c0, n0, m0, w_conv, b_conv, g_head):
    b, s, w2 = ml_qk.shape
    width = w2 // 2
    nh, dh = ML_HEADS, ML_HEAD_DIM
    chunk = min(ML_CHUNK, s)
    assert s % chunk == 0 and chunk >= 8
    tok = lambda w: pl.BlockSpec((1, chunk, w), lambda bi, ci: (bi, ci, 0))
    perb = lambda shp: pl.BlockSpec((1,) + shp, lambda bi, ci: (bi,) + (0,) * len(shp))
    const = lambda shp: pl.BlockSpec(shp, lambda bi, ci: (0,) * len(shp))
    return pl.pallas_call(
        functools.partial(_mlstm_kernel, chunk=chunk),
        grid=(b, s // chunk),
        in_specs=[tok(w2), tok(width), tok(width), tok(LANES),
                  perb((CONV_W - 1, w2)), perb((nh, dh, dh)), perb((nh, dh)), perb((1, LANES)),
                  const((CONV_W, w2)), const((1, w2)), const((dh, 1))],
        out_specs=[tok(width), perb((nh, dh, dh)), perb((nh, dh)), perb((1, LANES)), perb((CONV_W - 1, w2))],
        out_shape=[jax.ShapeDtypeStruct((b, s, width), F32),
                   jax.ShapeDtypeStruct((b, nh, dh, dh), F32),
                   jax.ShapeDtypeStruct((b, nh, dh), F32),
                   jax.ShapeDtypeStruct((b, 1, LANES), F32),
                   jax.ShapeDtypeStruct((b, CONV_W - 1, w2), F32)],
        scratch_shapes=[pltpu.VMEM((8 + chunk, w2), F32), pltpu.VMEM((nh, dh, dh), F32),
                        pltpu.VMEM((nh, dh), F32), pltpu.VMEM((1, LANES), F32)],
        compiler_params=_cparams(("arbitrary", "arbitrary")),
        name="mlstm",
    )(ml_qk, ml_v, ml_o, gates, conv_past, c0, n0, m0, w_conv, b_conv, g_head)


def _outproj_kernel(osb_ref, oml_ref, x_ref, mod_ref, w1_ref, w2_ref, g2_ref, wrh_ref, wrl_ref, br_ref,
                    x1_ref, h2_ref, ti_ref, tg_ref):
    mod = mod_ref[0]
    ga1 = mod[2:3]
    sh2 = mod[3:4]
    sc2 = mod[4:5]
    mix = _dot(osb_ref[0].astype(BF16), w1_ref[...]) + _dot(oml_ref[0].astype(BF16), w2_ref[...])
    x1 = x_ref[0] + ga1 * mix
    x1_ref[0] = x1
    ms = jnp.mean(x1 * x1, axis=-1, keepdims=True)
    h2 = x1 * lax.rsqrt(ms + EPS) * g2_ref[...]
    h2 = h2 * (1.0 + sc2) + sh2
    _store_row_tiles(h2_ref, h2)
    logits = _dot3(h2, wrh_ref[...], wrl_ref[...]) + br_ref[...]
    lane = lax.broadcasted_iota(I32, logits.shape, 1)
    vals, idxs = [], []
    cur = logits
    for _ in range(TOP_K):
        m = jnp.max(cur, axis=-1, keepdims=True)
        idx = jnp.min(jnp.where(cur == m, lane, LANES), axis=-1, keepdims=True)
        vals.append(m)
        idxs.append(idx)
        cur = jnp.where(lane == idx, -jnp.inf, cur)
    es = [jnp.exp(v - vals[0]) for v in vals]
    tot = es[0] + es[1] + es[2] + es[3]
    ti = jnp.zeros(logits.shape, I32)
    tg = jnp.zeros(logits.shape, F32)
    for k in range(TOP_K):
        ti = jnp.where(lane == k, idxs[k], ti)
        tg = jnp.where(lane == k, es[k] / tot, tg)
    ti_ref[0] = ti
    tg_ref[0] = tg


def _outproj(o_sb, o_ml, x, mod, w1, w2, g2, wr_hi, wr_lo, br):
    b, s, d = x.shape
    hw = o_sb.shape[2]
    tm = min(ROW_TILE, s)
    tok = lambda w: pl.BlockSpec((1, tm, w), lambda bi, i: (bi, i, 0))
    const = lambda shp: pl.BlockSpec(shp, lambda bi, i: (0,) * len(shp))
    return pl.pallas_call(
        _outproj_kernel,
        grid=(b, s // tm),
        in_specs=[tok(hw), tok(hw), tok(d), pl.BlockSpec((1, 6, d), lambda bi, i: (bi, 0, 0)),
                  const((hw, d)), const((hw, d)), const((1, d)),
                  const((d, LANES)), const((d, LANES)), const((1, LANES))],
        out_specs=[tok(d), pl.BlockSpec((tm * SUBLANES, LANES), lambda bi, i: (bi * (s // tm) + i, 0)),
                   tok(LANES), tok(LANES)],
        out_shape=[jax.ShapeDtypeStruct((b, s, d), F32), jax.ShapeDtypeStruct((b * s * SUBLANES, LANES), F32),
                   jax.ShapeDtypeStruct((b, s, LANES), I32), jax.ShapeDtypeStruct((b, s, LANES), F32)],
        compiler_params=_cparams(("arbitrary", "arbitrary")),
        name="outproj_router",
    )(o_sb, o_ml, x, mod, w1, w2, g2, wr_hi, wr_lo, br)


def _rank_kernel(ti_ref, rank_ref, cnt_ref, carry_scr):
    i = pl.program_id(0)

    @pl.when(i == 0)
    def _():
        carry_scr[...] = jnp.zeros_like(carry_scr)

    ti = ti_ref[...]
    tm = ti.shape[0]
    lane = lax.broadcasted_iota(I32, ti.shape, 1)
    hit = jnp.zeros(ti.shape, F32)
    for k in range(TOP_K):
        hit = hit + jnp.where(lane == ti[:, k:k + 1], 1.0, 0.0)
    row = lax.broadcasted_iota(I32, (tm, tm), 0)
    col = lax.broadcasted_iota(I32, (tm, tm), 1)
    strict = jnp.where(col < row, 1.0, 0.0).astype(BF16)
    excl = _dot(strict, hit.astype(BF16)) + carry_scr[...]
    rank = jnp.zeros(ti.shape, F32)
    for k in range(TOP_K):
        rk = jnp.sum(jnp.where(lane == ti[:, k:k + 1], excl, 0.0), axis=-1, keepdims=True)
        rank = jnp.where(lane == k, rk, rank)
    rank_ref[...] = rank.astype(I32)
    carry_scr[...] = carry_scr[...] + jnp.sum(hit, axis=0, keepdims=True)
    cnt_ref[...] = carry_scr[...].astype(I32)


def _ranks(top_idx):
    t = top_idx.shape[0]
    tm = ROW_TILE
    assert t % tm == 0
    return pl.pallas_call(
        _rank_kernel,
        grid=(t // tm,),
        in_specs=[pl.BlockSpec((tm, LANES), lambda i: (i, 0))],
        out_specs=[pl.BlockSpec((tm, LANES), lambda i: (i, 0)), pl.BlockSpec((1, LANES), lambda i: (0, 0))],
        out_shape=[jax.ShapeDtypeStruct((t, LANES), I32), jax.ShapeDtypeStruct((1, LANES), I32)],
        scratch_shapes=[pltpu.VMEM((1, LANES), F32)],
        compiler_params=_cparams(("arbitrary",)),
        name="expert_ranks",
    )(top_idx)


def _row_tile(ref, r):
    return ref.at[pl.ds(pl.multiple_of(r * SUBLANES, SUBLANES), SUBLANES), :]


def _dispatch_kernel(*refs, tm):
    pos_hbm, h_ref = refs[:2]
    xs_hbm, pos_smem0, pos_smem1, stage, pos_sem, row_sem = refs[-6:]
    i = pl.program_id(0)
    n = pl.num_programs(0)
    slot = lax.rem(i, 2)
    nidx = tm * TOP_K

    def pos_copy(tile, sl):
        return pltpu.make_async_copy(pos_hbm.at[pl.ds(pl.multiple_of(tile * nidx, nidx), nidx)],
                                     (pos_smem0, pos_smem1)[sl], pos_sem.at[sl])

    def drain(sl):
        for _ in range(TOP_K):
            pltpu.make_async_copy(stage.at[sl], stage.at[sl], row_sem.at[sl]).wait()

    @pl.when(i == 0)
    def _():
        pos_copy(0, 0).start()

    for sl in range(2):
        @pl.when(jnp.logical_and(i + 1 < n, slot == 1 - sl))
        def _():
            pos_copy(i + 1, sl).start()

    @pl.when(i >= 2)
    def _():
        drain(slot)

    stage[slot] = h_ref[...]

    for sl in range(2):
        @pl.when(slot == sl)
        def _():
            pos_copy(i, sl).wait()
            pos_smem = (pos_smem0, pos_smem1)[sl]

            def issue(c, carry):
                for rr in range(ISSUE_UNROLL):
                    r = c * ISSUE_UNROLL + rr
                    for k in range(TOP_K):
                        pltpu.make_async_copy(_row_tile(stage.at[sl], r), _row_tile(xs_hbm, pos_smem[TOP_K * r + k]),
                                              row_sem.at[sl]).start(priority=k % DMA_QUEUES)
                return carry
            lax.fori_loop(0, tm // ISSUE_UNROLL, issue, 0)

    @pl.when(i == n - 1)
    def _():
        drain(slot)

        @pl.when(n >= 2)
        def _():
            drain(1 - slot)


def _dispatch(pos_flat, h_tiles, xs=None, n_rows=None):
    t = h_tiles.shape[0] // SUBLANES
    tm = GATHER_TILE
    assert t % tm == 0 and tm % ISSUE_UNROLL == 0
    in_specs = [pl.BlockSpec(memory_space=pl.ANY), pl.BlockSpec((tm * SUBLANES, LANES), lambda i: (i, 0))]
    args = [pos_flat, h_tiles]
    if xs is not None:
        in_specs.append(pl.BlockSpec(memory_space=pl.ANY))
        args.append(xs)
    return pl.pallas_call(
        functools.partial(_dispatch_kernel, tm=tm),
        grid=(t // tm,),
        in_specs=in_specs,
        out_specs=pl.BlockSpec(memory_space=pl.ANY),
        out_shape=jax.ShapeDtypeStruct((n_rows * SUBLANES, LANES) if xs is None else xs.shape, F32),
        scratch_shapes=[pltpu.SMEM((tm * TOP_K,), I32), pltpu.SMEM((tm * TOP_K,), I32),
                        pltpu.VMEM((2, tm * SUBLANES, LANES), F32),
                        pltpu.SemaphoreType.DMA((2,)), pltpu.SemaphoreType.DMA((2,))],
        input_output_aliases={} if xs is None else {2: 0},
        compiler_params=_cparams(("arbitrary",)),
        name="moe_dispatch",
    )(*args)


def _pad_fill_kernel(pos_hbm, xs_in_hbm, xs_hbm, pos_smem, zero_tile, pos_sem, row_sem, *, nidx):
    del xs_in_hbm
    i = pl.program_id(0)
    cp = pltpu.make_async_copy(pos_hbm.at[pl.ds(pl.multiple_of(i * nidx, nidx), nidx)], pos_smem, pos_sem)
    cp.start()
    zero_tile[...] = jnp.zeros_like(zero_tile)
    cp.wait()

    def issue(c, carry):
        for rr in range(TOP_K * ISSUE_UNROLL):
            r = c * (TOP_K * ISSUE_UNROLL) + rr
            pltpu.make_async_copy(zero_tile, _row_tile(xs_hbm, pos_smem[r]), row_sem).start(
                priority=rr % DMA_QUEUES)
        return carry
    lax.fori_loop(0, nidx // (TOP_K * ISSUE_UNROLL), issue, 0)
    whole = xs_hbm.at[pl.ds(0, nidx * SUBLANES), :]
    pltpu.make_async_copy(whole, whole, row_sem).wait()


def _pad_fill(pad_pos, xs):
    nidx = GATHER_TILE * TOP_K
    assert pad_pos.shape[0] % nidx == 0
    return pl.pallas_call(
        functools.partial(_pad_fill_kernel, nidx=nidx),
        grid=(pad_pos.shape[0] // nidx,),
        in_specs=[pl.BlockSpec(memory_space=pl.ANY), pl.BlockSpec(memory_space=pl.ANY)],
        out_specs=pl.BlockSpec(memory_space=pl.ANY),
        out_shape=jax.ShapeDtypeStruct(xs.shape, xs.dtype),
        scratch_shapes=[pltpu.SMEM((nidx,), I32), pltpu.VMEM((SUBLANES, LANES), F32),
                        pltpu.SemaphoreType.DMA, pltpu.SemaphoreType.DMA],
        input_output_aliases={1: 0},
        compiler_params=_cparams(("arbitrary",)),
        name="moe_pad_fill",
    )(pad_pos, xs)


def _moe_kernel(be_ref, nu_ref, xs_ref, wgu_ref, bgu_ref, wd_ref, bd_ref, ys_ref, wgu_bf, wd_bf):
    i = pl.program_id(0)
    d, ff = wgu_ref.shape[1], wd_ref.shape[1]

    @pl.when(jnp.logical_and(i < nu_ref[0],
                             jnp.logical_or(i == 0, be_ref[i] != be_ref[jnp.maximum(i - 1, 0)])))
    def _():
        for r in range(0, d, LANES):
            wgu_bf[r:r + LANES, :] = wgu_ref[0, r:r + LANES, :].astype(BF16)
        for r in range(0, ff, LANES):
            wd_bf[r:r + LANES, :] = wd_ref[0, r:r + LANES, :].astype(BF16)

    @pl.when(i < nu_ref[0])
    def _():
        x = _load_row_tiles(xs_ref, MOE_BLK).astype(BF16)
        glu = _dot(x, wgu_bf[:, 0:ff]) + bgu_ref[0, :, 0:ff]
        lin = _dot(x, wgu_bf[:, ff:2 * ff]) + bgu_ref[0, :, ff:2 * ff]
        glu = jnp.minimum(glu, SWIGLU_LIMIT)
        lin = jnp.clip(lin, -SWIGLU_LIMIT, SWIGLU_LIMIT)
        act = glu * jax.nn.sigmoid(SWIGLU_ALPHA * glu) * (lin + 1.0)
        _store_row_tiles(ys_ref, _dot(act.astype(BF16), wd_bf[...]) + bd_ref[0])

    @pl.when(i >= nu_ref[0])
    def _():
        ys_ref[...] = jnp.zeros_like(ys_ref)


def _moe_mlp(block_expert, n_used, xs, w_gu, b_gu, w_down, b_down):
    ne, d, ff2 = w_gu.shape
    ff = ff2 // 2
    tb = MOE_BLK
    n_rows = block_expert.shape[0] * tb
    assert xs.shape[0] >= n_rows * SUBLANES
    blk = lambda i, be, nu: jnp.minimum(i, nu[0] - 1)
    grid_spec = pltpu.PrefetchScalarGridSpec(
        num_scalar_prefetch=2,
        grid=(n_rows // tb,),
        in_specs=[pl.BlockSpec((tb * SUBLANES, LANES), lambda i, be, nu: (blk(i, be, nu), 0)),
                  pl.BlockSpec((1, d, ff2), lambda i, be, nu: (be[blk(i, be, nu)], 0, 0)),
                  pl.BlockSpec((1, 1, ff2), lambda i, be, nu: (be[blk(i, be, nu)], 0, 0)),
                  pl.BlockSpec((1, ff, d), lambda i, be, nu: (be[blk(i, be, nu)], 0, 0)),
                  pl.BlockSpec((1, 1, d), lambda i, be, nu: (be[blk(i, be, nu)], 0, 0))],
        out_specs=pl.BlockSpec((tb * SUBLANES, LANES), lambda i, be, nu: (i, 0)),
        scratch_shapes=[pltpu.VMEM((d, ff2), BF16), pltpu.VMEM((ff, d), BF16)],
    )
    return pl.pallas_call(
        _moe_kernel,
        grid_spec=grid_spec,
        out_shape=jax.ShapeDtypeStruct((n_rows * SUBLANES, LANES), F32),
        compiler_params=_cparams(("arbitrary",)),
        name="moe_mlp",
    )(block_expert, n_used, xs, w_gu, b_gu.reshape(ne, 1, ff2), w_down, b_down.reshape(ne, 1, d))


def _combine_kernel(pos_hbm, ys_hbm, x1_ref, tg_ref, mod_ref, y_ref, pos_smem0, pos_smem1, buf, pos_sem, row_sem,
                    *, tm, nb):
    i = pl.program_id(0)
    n = pl.num_programs(0)
    slot = lax.rem(i, 2)
    nidx = tm * TOP_K

    def pos_copy(tile, sl):
        return pltpu.make_async_copy(pos_hbm.at[pl.ds(pl.multiple_of(tile * nidx, nidx), nidx)],
                                     (pos_smem0, pos_smem1)[sl], pos_sem.at[sl])

    def gather(tile, sl):
        pos_copy(tile, sl).wait()
        pos_smem = (pos_smem0, pos_smem1)[sl]

        def issue(c, carry):
            for rr in range(ISSUE_UNROLL):
                r = c * ISSUE_UNROLL + rr
                for k in range(TOP_K):
                    pltpu.make_async_copy(_row_tile(ys_hbm, pos_smem[TOP_K * r + k]),
                                          _row_tile(buf.at[sl, k], r), row_sem.at[sl]).start(
                                              priority=k % DMA_QUEUES)
            return carry
        lax.fori_loop(0, tm // ISSUE_UNROLL, issue, 0)

        @pl.when(tile + 1 < n)
        def _():
            pos_copy(tile + 1, 1 - sl).start()

    @pl.when(i == 0)
    def _():
        pos_copy(0, 0).start()
        gather(0, 0)

    for sl in range(2):
        @pl.when(jnp.logical_and(i + 1 < n, slot == 1 - sl))
        def _():
            gather(i + 1, sl)

    for k in range(TOP_K):
        pltpu.make_async_copy(buf.at[slot, k], buf.at[slot, k], row_sem.at[slot]).wait()

    tg = tg_ref[...]
    rows = tm // nb
    for c in range(SUBLANES):
        cols = slice(c * LANES, (c + 1) * LANES)
        ff = tg[:, 0:1] * buf[slot, 0, pl.ds(c, tm, stride=SUBLANES), :]
        for k in range(1, TOP_K):
            ff = ff + tg[:, k:k + 1] * buf[slot, k, pl.ds(c, tm, stride=SUBLANES), :]
        for g in range(nb):
            rs = slice(g * rows, (g + 1) * rows)
            y_ref[rs, cols] = x1_ref[rs, cols] + mod_ref[g, 5:6, cols] * ff[rs, :]


def _combine(pos_flat, ys, x1, top_gate, mod, tokens_per_batch):
    t, d = x1.shape
    tm = GATHER_TILE
    assert t % tm == 0 and tm % ISSUE_UNROLL == 0 and d == SUBLANES * LANES
    if tokens_per_batch >= tm:
        assert tokens_per_batch % tm == 0
        nb = 1
        per = tokens_per_batch // tm
        mod_spec = pl.BlockSpec((1, 6, d), lambda i: (i // per, 0, 0))
    else:
        assert tm % tokens_per_batch == 0
        nb = tm // tokens_per_batch
        mod_spec = pl.BlockSpec((nb, 6, d), lambda i: (i, 0, 0))
    return pl.pallas_call(
        functools.partial(_combine_kernel, tm=tm, nb=nb),
        grid=(t // tm,),
        in_specs=[pl.BlockSpec(memory_space=pl.ANY), pl.BlockSpec(memory_space=pl.ANY),
                  pl.BlockSpec((tm, d), lambda i: (i, 0)), pl.BlockSpec((tm, LANES), lambda i: (i, 0)), mod_spec],
        out_specs=pl.BlockSpec((tm, d), lambda i: (i, 0)),
        out_shape=jax.ShapeDtypeStruct((t, d), F32),
        scratch_shapes=[pltpu.SMEM((tm * TOP_K,), I32), pltpu.SMEM((tm * TOP_K,), I32),
                        pltpu.VMEM((2, TOP_K, tm * SUBLANES, LANES), F32),
                        pltpu.SemaphoreType.DMA((2,)), pltpu.SemaphoreType.DMA((2,))],
        compiler_params=_cparams(("arbitrary",)),
        name="moe_combine",
    )(pos_flat, ys, x1, top_gate, mod)


def _pad_lanes(a, fill=0.0):
    return jnp.pad(a, [(0, 0)] * (a.ndim - 1) + [(0, LANES - a.shape[-1])], constant_values=fill)


def _hi_lo(w):
    hi = w.astype(BF16)
    return hi, (w - hi.astype(F32)).astype(BF16)


def _mixer(x, mod, k_past, v_past, conv_past, c0, n0, m0, p):
    b, s, d = x.shape
    q_raw, k_norm, v, ml_qk, ml_v, ml_o, gates, k_tiles, v_tiles = _inproj(
        x, mod, p["g1"], p["w_main"], p["wg_hi"], p["wg_lo"], p["bg"], p["g_k"], p["widths"])
    o_sb = _sb_attention(q_raw, k_norm, v, k_past, v_past, p["g_q"])
    o_ml, c_new, n_new, m_new, conv_new = _mlstm(
        ml_qk, ml_v, ml_o, gates, conv_past, c0, n0, _pad_lanes(m0)[:, None, :],
        p["w_conv"], p["b_conv"], p["g_head"])
    x1, h2, ti, tg = _outproj(o_sb, o_ml, x, mod, p["w_out1"], p["w_out2"], p["g2"],
                              p["wr_hi"], p["wr_lo"], p["br"])
    assert v.shape[-1] == SUBLANES * SB_HEAD_DIM
    state = (k_tiles.reshape(b, s, SUBLANES, SB_HEAD_DIM), v_tiles.reshape(b, s, SUBLANES, SB_HEAD_DIM), conv_new,
             c_new, n_new, m_new[:, 0, :ML_HEADS])
    return x1.reshape(b * s, d), h2, ti.reshape(b * s, LANES), tg.reshape(b * s, LANES), state


def _layer(xp, xs_, cp, cs, kc, vc, convc, cc, nc, mc, w_ada, b_ada, g_norm1, w_in, g_q, g_k, w_conv, b_conv,
           b_gate, g_head, w_out, g_norm2, w_router, b_router, w_gu, b_gu, w_down, b_down):
    d = xp.shape[-1]
    bp, sp = xp.shape[:2]
    bs, ss = xs_.shape[:2]
    sbw = kc.shape[-2] * kc.shape[-1]
    mlw = ML_HEADS * ML_HEAD_DIM
    ngate = 2 * ML_HEADS
    gate0 = 3 * sbw + 3 * mlw
    w_main = jnp.concatenate([w_in[:, :gate0], w_in[:, gate0 + ngate:]], axis=1).astype(BF16)
    wg_hi, wg_lo = _hi_lo(_pad_lanes(w_in[:, gate0:gate0 + ngate]))
    wr_hi, wr_lo = _hi_lo(_pad_lanes(w_router))
    p = dict(
        widths=(sbw, sbw, sbw, 2 * mlw, mlw, mlw),
        g1=g_norm1[None], w_main=w_main, wg_hi=wg_hi, wg_lo=wg_lo, bg=_pad_lanes(b_gate)[None],
        g_q=jnp.tile(g_q, LANES // SB_HEAD_DIM)[None], g_k=jnp.tile(g_k, LANES // SB_HEAD_DIM)[None],
        w_conv=w_conv, b_conv=b_conv[None], g_head=g_head[:, None],
        w_out1=w_out[:sbw].astype(BF16), w_out2=w_out[sbw:].astype(BF16), g2=g_norm2[None],
        wr_hi=wr_hi, wr_lo=wr_lo, br=_pad_lanes(b_router, -1e30)[None],
    )
    mod = _ada(jnp.concatenate([cp, cs], axis=0), w_ada, b_ada)
    mod_p = mod[:bp].reshape(bp, 6, d)
    mod_s = mod[bp:].reshape(bs, 6, d)

    zeros = lambda *shp: jnp.zeros(shp, F32)
    x1p, h2p, tip, tgp, st_p = _mixer(
        xp, mod_p, None, None, zeros(bp, CONV_W - 1, 2 * mlw), zeros(bp, ML_HEADS, ML_HEAD_DIM, ML_HEAD_DIM),
        zeros(bp, ML_HEADS, ML_HEAD_DIM), zeros(bp, ML_HEADS), p)
    x1s, h2s, tis, tgs, st_s = _mixer(
        xs_, mod_s, kc.reshape(-1, SB_HEAD_DIM), vc.reshape(-1, SB_HEAD_DIM), convc, cc, nc, mc, p)

    tp, ts = bp * sp, bs * ss
    ti_all = jnp.concatenate([tip, tis], axis=0)
    rank, counts = _ranks(ti_all)
    counts = counts[0, :N_EXPERTS]
    padded = (counts + MOE_BLK - 1) // MOE_BLK * MOE_BLK
    pad_end = jnp.cumsum(padded)
    pad_start = pad_end - padded
    top_idx = ti_all[:, :TOP_K]
    pos = (pad_start[top_idx] + rank[:, :TOP_K]).astype(I32)
    n_blocks = -(-((tp + ts) * TOP_K + N_EXPERTS * (MOE_BLK - 1)) // MOE_BLK)
    n_used = (pad_end[-1] // MOE_BLK).astype(I32).reshape(1)
    block_start = jnp.arange(n_blocks, dtype=I32) * MOE_BLK
    block_expert = jnp.minimum(jnp.sum(block_start[:, None] >= pad_end[None, :], axis=1), N_EXPERTS - 1).astype(I32)
    pos_p = pos[:tp].reshape(-1)
    pos_s = pos[tp:].reshape(-1)

    n_rows = n_blocks * MOE_BLK
    slot = jnp.arange(MOE_BLK, dtype=I32)[None, :]
    spare = n_rows + jnp.arange(N_EXPERTS * MOE_BLK, dtype=I32).reshape(N_EXPERTS, MOE_BLK)
    pad_pos = jnp.where(slot < (padded - counts)[:, None], (pad_start + counts)[:, None] + slot, spare).reshape(-1)

    xs_buf = _dispatch(pos_p, h2p, n_rows=n_rows + N_EXPERTS * MOE_BLK)
    xs_buf = _dispatch(pos_s, h2s, xs_buf)
    xs_buf = _pad_fill(pad_pos.astype(I32), xs_buf)
    ys = _moe_mlp(block_expert, n_used, xs_buf, w_gu, b_gu, w_down, b_down)
    yp = _combine(pos_p, ys, x1p, tgp, mod_p, sp).reshape(bp, sp, d)
    ysm = _combine(pos_s, ys, x1s, tgs, mod_s, ss).reshape(bs, ss, d)
    return yp, ysm, st_p, st_s


def kernel(x_prompt, x_sample, c_prompt, c_sample, cache_sb_k, cache_sb_v, state_conv, state_mlstm_c, state_mlstm_n, state_mlstm_m, w_ada, b_ada, g_norm1, w_in, g_q, g_k, w_conv, b_conv, b_gate, g_head, w_out, g_norm2, w_router, b_router, w_gu, b_gu, w_down, b_down):
    assert w_ada.shape[0] == 1, "single-layer step"
    yp, ys, st_p, st_s = _layer(
        x_prompt, x_sample, c_prompt, c_sample, cache_sb_k[0], cache_sb_v[0], state_conv[0], state_mlstm_c[0],
        state_mlstm_n[0], state_mlstm_m[0], w_ada[0], b_ada[0], g_norm1[0], w_in[0], g_q[0], g_k[0], w_conv[0],
        b_conv[0], b_gate[0], g_head[0], w_out[0], g_norm2[0], w_router[0], b_router[0], w_gu[0], b_gu[0],
        w_down[0], b_down[0])
    return (yp, ys) + tuple(a[None] for a in st_p) + tuple(a[None] for a in st_s)
```

```python
import functools
import math

import jax
import jax.numpy as jnp
from jax import lax
from jax.experimental import pallas as pl
from jax.experimental.pallas import tpu as pltpu

F32 = jnp.float32
BF16 = jnp.bfloat16
I32 = jnp.int32

EPS = 1e-6
LANES = 128
SUBLANES = 8
SB_HEAD_DIM = 64
ML_HEAD_DIM = 128
ML_HEADS = 4
CONV_W = 4
N_EXPERTS = 32
TOP_K = 4
SWIGLU_LIMIT = 7.0
SWIGLU_ALPHA = 1.702
VMEM_LIMIT = 56 * 1024 * 1024

ROW_TILE = 512
MOE_BLK = 512
GATHER_TILE = 256
ISSUE_UNROLL = 8
DMA_QUEUES = 2
ML_CHUNK = 256
SB_BLK = 128
SB_SWEEP = 256
SB_UNDERFLOW = -105.0


def _cparams(sem):
    return pltpu.CompilerParams(dimension_semantics=sem, vmem_limit_bytes=VMEM_LIMIT)


def _split3(x):
    p1 = x.astype(BF16)
    r1 = x - p1.astype(F32)
    p2 = r1.astype(BF16)
    p3 = (r1 - p2.astype(F32)).astype(BF16)
    return p1, p2, p3


def _dot(a, b):
    return jnp.dot(a, b, preferred_element_type=F32)


def _dot_nt(a, b):
    return lax.dot_general(a, b, (((1,), (1,)), ((), ())), preferred_element_type=F32)


def _dot3(x, w_hi, w_lo):
    xh = x.astype(BF16)
    xl = (x - xh.astype(F32)).astype(BF16)
    return _dot(xh, w_hi) + _dot(xl, w_hi) + _dot(xh, w_lo)


def _store_row_tiles(ref, x):
    n, d = x.shape
    assert d == SUBLANES * LANES
    for c in range(SUBLANES):
        ref[pl.ds(c, n, stride=SUBLANES), :] = x[:, c * LANES:(c + 1) * LANES]


def _load_row_tiles(ref, n):
    return jnp.concatenate([ref[pl.ds(c, n, stride=SUBLANES), :] for c in range(SUBLANES)], axis=1)


def _store_head_tiles(ref, row0, head0, x):
    n = x.shape[0]
    for j in range(x.shape[1] // SB_HEAD_DIM):
        ref[pl.ds(row0 * SUBLANES + head0 + j, n, stride=SUBLANES), :] = x[:, j * SB_HEAD_DIM:(j + 1) * SB_HEAD_DIM]


def _headnorm(x, g):
    lo_half = lax.broadcasted_iota(I32, (1, LANES), 1) < SB_HEAD_DIM
    x2 = x * x
    s0 = jnp.sum(jnp.where(lo_half, x2, 0.0), axis=-1, keepdims=True)
    s1 = jnp.sum(jnp.where(lo_half, 0.0, x2), axis=-1, keepdims=True)
    r = jnp.where(lo_half, lax.rsqrt(s0 * (1.0 / SB_HEAD_DIM) + EPS), lax.rsqrt(s1 * (1.0 / SB_HEAD_DIM) + EPS))
    return x * r * g


def _log_sigmoid(x):
    return jnp.minimum(x, 0.0) - jnp.log1p(jnp.exp(-jnp.abs(x)))


def _ada_kernel(c_ref, w_ref, b_ref, o_ref):
    c = c_ref[...]
    s = c * jax.nn.sigmoid(c)
    o_ref[...] = jnp.dot(s, w_ref[...], precision=lax.Precision.HIGHEST,
                         preferred_element_type=F32) + b_ref[...]


def _ada(c, w_ada, b_ada):
    n, d = c.shape
    cols = w_ada.shape[1]
    tn = cols // 6
    return pl.pallas_call(
        _ada_kernel,
        grid=(cols // tn,),
        in_specs=[pl.BlockSpec((n, d), lambda j: (0, 0)),
                  pl.BlockSpec((d, tn), lambda j: (0, j)),
                  pl.BlockSpec((1, tn), lambda j: (0, j))],
        out_specs=pl.BlockSpec((n, tn), lambda j: (0, j)),
        out_shape=jax.ShapeDtypeStruct((n, cols), F32),
        compiler_params=_cparams(("arbitrary",)),
        name="ada_mod",
    )(c, w_ada, b_ada.reshape(1, cols))


def _inproj_kernel(x_ref, mod_ref, g1_ref, w_ref, wgh_ref, wgl_ref, bg_ref, gk_ref,
                   q_ref, k_ref, v_ref, qk_ref, mv_ref, mo_ref, gate_ref, kt_ref, vt_ref, *, widths):
    x = x_ref[0]
    mod = mod_ref[0]
    sh1 = mod[0:1]
    sc1 = mod[1:2]
    ms = jnp.mean(x * x, axis=-1, keepdims=True)
    h = x * lax.rsqrt(ms + EPS) * g1_ref[...]
    h = h * (1.0 + sc1) + sh1
    hb = h.astype(BF16)
    off = 0
    for ref, wd in zip((q_ref, k_ref, v_ref, qk_ref, mv_ref, mo_ref), widths):
        u = _dot(hb, w_ref[:, off:off + wd])
        if ref is k_ref:
            u = jnp.concatenate([_headnorm(u[:, c:c + LANES], gk_ref[...]) for c in range(0, wd, LANES)], axis=1)
            _store_head_tiles(kt_ref, 0, 0, u)
        if ref is v_ref:
            _store_head_tiles(vt_ref, 0, 0, u)
        ref[0] = u
        off += wd
    gate_ref[0] = _dot3(h, wgh_ref[...], wgl_ref[...]) + bg_ref[...]


def _inproj(x, mod, g1, w_main, wg_hi, wg_lo, bg, g_k, widths):
    b, s, d = x.shape
    tm = min(ROW_TILE, s)
    ncol = w_main.shape[1]
    tok = lambda w: pl.BlockSpec((1, tm, w), lambda bi, i: (bi, i, 0))
    const = lambda shp: pl.BlockSpec(shp, lambda bi, i: (0,) * len(shp))
    head_tiles = pl.BlockSpec((tm * SUBLANES, SB_HEAD_DIM), lambda bi, i: (bi * (s // tm) + i, 0))
    return pl.pallas_call(
        functools.partial(_inproj_kernel, widths=widths),
        grid=(b, s // tm),
        in_specs=[tok(d),
                  pl.BlockSpec((1, 6, d), lambda bi, i: (bi, 0, 0)),
                  const((1, d)), const((d, ncol)), const((d, LANES)), const((d, LANES)), const((1, LANES)),
                  const((1, LANES))],
        out_specs=[tok(w) for w in widths] + [tok(LANES), head_tiles, head_tiles],
        out_shape=[jax.ShapeDtypeStruct((b, s, w), F32) for w in widths]
        + [jax.ShapeDtypeStruct((b, s, LANES), F32)]
        + [jax.ShapeDtypeStruct((b * s * SUBLANES, SB_HEAD_DIM), F32)] * 2,
        compiler_params=_cparams(("arbitrary", "arbitrary")),
        name="inproj",
    )(x, mod, g1, w_main, wg_hi, wg_lo, bg, g_k)


def _sb_kernel(*refs, past, seq, tq, kb, npair):
    if past:
        q_ref, k_ref, v_ref, kp_ref, vp_ref, gq_ref, o_ref, ks_scr, vs_scr, run_scr, acc_scr = refs
    else:
        q_ref, k_ref, v_ref, gq_ref, o_ref, ks_scr, vs_scr, run_scr, acc_scr = refs
    i = pl.program_id(1)
    lane = lax.broadcasted_iota(I32, (1, LANES), 1)
    lo_half = lane < SB_HEAD_DIM
    pairs = [slice(hp * LANES, (hp + 1) * LANES) for hp in range(npair)]

    @pl.when(i == 0)
    def _():
        ch = min(256, seq)
        for c in range(seq // ch):
            rows = slice(c * ch, (c + 1) * ch)
            dst = slice(past + c * ch, past + (c + 1) * ch)
            ks_scr[dst, :] = k_ref[0, rows, :].astype(BF16)
            vs_scr[dst, :] = v_ref[0, rows, :].astype(BF16)
        if past:
            pc = min(256, past)
            for src, dst_scr in ((kp_ref, ks_scr), (vp_ref, vs_scr)):
                for c in range(past // pc):
                    dst_scr[c * pc:(c + 1) * pc, :] = src[0, :, c * pc:(c + 1) * pc].T.astype(BF16)

    qm = []
    for ps in pairs:
        q = _headnorm(q_ref[0, :, ps], gq_ref[...]) * (1.0 / math.sqrt(SB_HEAD_DIM))
        qm.append(jnp.concatenate([jnp.where(lo_half, q, 0.0), jnp.where(lo_half, 0.0, q)], axis=0).astype(BF16))
    nrow = 2 * npair * tq

    def neg_cum(nk):
        rj = lax.broadcasted_iota(I32, (2 * nk, LANES + nk), 0)
        cj = lax.broadcasted_iota(I32, (2 * nk, LANES + nk), 1)
        rjm = jnp.where(rj >= nk, rj - nk, rj)
        return jnp.where(cj < LANES, -1.0, jnp.where(rjm >= cj - LANES, -1.0, 0.0)).astype(BF16)

    w_diag = neg_cum(tq)
    w_sub = w_diag if tq == LANES else neg_cum(LANES)
    causal = (lax.broadcasted_iota(I32, (nrow, tq), 1)
              < jnp.bitwise_and(lax.broadcasted_iota(I32, (nrow, tq), 0), tq - 1))

    def sweep(row0, nk, diag):
        kv = [(ks_scr[pl.ds(row0, nk), ps], vs_scr[pl.ds(row0, nk), ps]) for ps in pairs]
        z = jnp.concatenate([_dot_nt(qm[hp], kv[hp][0]) for hp in range(npair)], axis=0)
        sp = jnp.maximum(z, 0.0) + jnp.log(1.0 + jnp.exp(-jnp.abs(z)))
        if diag:
            sp = jnp.where(causal, sp, 0.0)
        hi = sp.astype(BF16)
        lo = (sp - hi.astype(F32)).astype(BF16)
        sub = min(nk, LANES)
        w_neg = w_diag if sub == tq else w_sub
        run = None if diag else run_scr[...]
        args = [None] * (nk // sub)
        for j in reversed(range(nk // sub)):
            cols = slice(j * sub, (j + 1) * sub)
            t = _dot(jnp.concatenate([hi[:, cols], lo[:, cols]], axis=1), w_neg)
            args[j] = z[:, cols] + t[:, LANES:] if run is None else z[:, cols] + t[:, LANES:] + run[:, :sub]
            run = t[:, :LANES] if run is None else run + t[:, :LANES]
        run_scr[...] = run
        p = jnp.exp(args[0] if len(args) == 1 else jnp.concatenate(args, axis=1))
        if diag:
            p = jnp.where(causal, p, 0.0)
        p = p.astype(BF16)
        for hp in range(npair):
            r0 = 2 * hp * tq
            vblk = kv[hp][1]
            vm = jnp.concatenate([jnp.where(lo_half, vblk, 0.0), jnp.where(lo_half, 0.0, vblk)], axis=0)
            pv = _dot(jnp.concatenate([p[r0:r0 + tq], p[r0 + tq:r0 + 2 * tq]], axis=1), vm.astype(BF16))
            acc_scr[hp] = pv if diag else acc_scr[hp] + pv
        return jnp.max(run)

    q0 = past + i * tq
    worst0 = sweep(pl.multiple_of(q0, tq), tq, True)
    n_before = q0 // kb

    def cond(c):
        return jnp.logical_and(c[0] < n_before, c[1] > SB_UNDERFLOW)

    def body(c):
        row0 = q0 - (c[0] + 1) * kb
        return c[0] + 1, sweep(pl.multiple_of(row0, tq), kb, False)

    _, worst = lax.while_loop(cond, body, (jnp.int32(0), worst0))
    if tq < kb and seq > tq:
        @pl.when(jnp.logical_and(q0 - n_before * kb > 0, worst > SB_UNDERFLOW))
        def _():
            sweep(0, tq, False)
    for hp, ps in enumerate(pairs):
        o_ref[0, :, ps] = acc_scr[hp]


def _sb_attention(q_raw, k_norm, v, k_past, v_past, g_q):
    b, s, w = q_raw.shape
    past = 0 if k_past is None else k_past.shape[2]
    tq = min(SB_BLK, s)
    kb = SB_SWEEP
    assert s % tq == 0 and past % kb == 0 and w % LANES == 0 and kb % LANES == 0 and (kb == 2 * tq or s == tq)
    npair = w // LANES
    qblk = pl.BlockSpec((1, tq, w), lambda bi, i: (bi, i, 0))
    full = lambda n: pl.BlockSpec((1, n, w), lambda bi, i: (bi, 0, 0))
    gspec = pl.BlockSpec((1, LANES), lambda bi, i: (0, 0))
    in_specs = [qblk, full(s), full(s)]
    args = [q_raw, k_norm, v]
    if past:
        cache = pl.BlockSpec((1, w, past), lambda bi, i: (bi, 0, 0))
        in_specs += [cache, cache]
        args += [k_past, v_past]
    in_specs += [gspec]
    args += [g_q]
    return pl.pallas_call(
        functools.partial(_sb_kernel, past=past, seq=s, tq=tq, kb=kb, npair=npair),
        grid=(b, s // tq),
        in_specs=in_specs,
        out_specs=qblk,
        out_shape=jax.ShapeDtypeStruct((b, s, w), F32),
        scratch_shapes=[pltpu.VMEM((past + s, w), BF16), pltpu.VMEM((past + s, w), BF16),
                        pltpu.VMEM((2 * npair * tq, LANES), F32), pltpu.VMEM((npair, tq, LANES), F32)],
        compiler_params=_cparams(("arbitrary", "arbitrary")),
        name="sb_attention",
    )(*args)


def _mlstm_kernel(qk_ref, v_ref, og_ref, g_ref, cp_ref, c0_ref, n0_ref, m0_ref, wc_ref, bc_ref, gh_ref,
                  out_ref, cout_ref, nout_ref, mout_ref, convout_ref,
                  prev_scr, c_scr, n_scr, m_scr, *, chunk):
    ci = pl.program_id(1)
    nh, dh = ML_HEADS, ML_HEAD_DIM
    width = nh * dh
    L = chunk

    @pl.when(ci == 0)
    def _():
        prev_scr[0:8, :] = jnp.zeros((8, prev_scr.shape[1]), F32)
        prev_scr[8 - (CONV_W - 1):8, :] = cp_ref[0]
        c_scr[...] = c0_ref[0]
        n_scr[...] = n0_ref[0]
        m_scr[...] = m0_ref[0]

    prev_scr[8:8 + L, :] = qk_ref[0]
    wc = wc_ref[...]
    acc = bc_ref[...] + prev_scr[8 - (CONV_W - 1):8 - (CONV_W - 1) + L, :] * wc[0:1]
    for j in range(1, CONV_W):
        o = 8 - (CONV_W - 1) + j
        acc = acc + prev_scr[o:o + L, :] * wc[j:j + 1]
    qk = acc * jax.nn.sigmoid(acc)
    prev_scr[0:8, :] = prev_scr[L:L + 8, :]

    gt = g_ref[0]
    lf = _log_sigmoid(gt)
    gt_t = gt.T
    lf_t = lf.T
    row = lax.broadcasted_iota(I32, (L, L), 0)
    col = lax.broadcasted_iota(I32, (L, L), 1)
    tri = row >= col
    tril = jnp.where(tri, 1.0, 0.0).astype(BF16)
    triu = jnp.where(row <= col, 1.0, 0.0).astype(BF16)
    p1, p2, p3 = _split3(lf)
    b_col = _dot(tril, p1) + _dot(tril, p2) + _dot(tril, p3)
    q1, q2, q3 = _split3(lf_t[0:8, :])
    b_row = _dot(q1, triu) + _dot(q2, triu) + _dot(q3, triu)

    lane = lax.broadcasted_iota(I32, (1, LANES), 1)
    m_vec = m_scr[...]
    m_out = m_vec
    for h in range(nh):
        hs = slice(h * dh, (h + 1) * dh)
        qb = qk[:, h * dh:(h + 1) * dh].astype(BF16)
        kh = qk[:, width + h * dh:width + (h + 1) * dh] * (dh ** -0.5)
        kb = kh.astype(BF16)
        v_t = v_ref[0, :, hs].T
        b_r = b_row[nh + h:nh + h + 1, :]
        ig_r = gt_t[h:h + 1, :]
        c_col = gt[:, h:h + 1] - b_col[:, nh + h:nh + h + 1]
        m_prev = m_vec[:, h:h + 1]

        d_t = jnp.where(row <= col, b_r + c_col, -jnp.inf)
        inter = b_r + m_prev
        m_t = jnp.maximum(inter, jnp.max(d_t, axis=0, keepdims=True))
        w_inter = jnp.exp(inter - m_t)
        s_t = _dot_nt(kb, qb) * jnp.exp(d_t - m_t)
        c_h = c_scr[h]
        n_h = n_scr[h:h + 1, :]
        n_hi = n_h.astype(BF16)
        n_lo = (n_h - n_hi.astype(F32)).astype(BF16)
        num_t = _dot(v_t.astype(BF16), s_t.astype(BF16)) + w_inter * _dot_nt(c_h.astype(BF16), qb)
        den = jnp.sum(s_t, axis=0, keepdims=True) + w_inter * (_dot_nt(n_hi, qb) + _dot_nt(n_lo, qb))
        hh_t = num_t / jnp.maximum(jnp.abs(den), jnp.exp(-m_t))

        b_last = b_r[:, L - 1:L]
        g_r = (b_last - b_r) + ig_r
        m_new = jnp.maximum(b_last + m_prev, jnp.max(g_r, axis=1, keepdims=True))
        w_state = jnp.exp(g_r - m_new)
        decay = jnp.exp(b_last + m_prev - m_new)
        c_scr[h] = decay * c_h + _dot((v_t * w_state).astype(BF16), kb)
        w_hi = w_state.astype(BF16)
        w_lo = (w_state - w_hi.astype(F32)).astype(BF16)
        n_scr[h:h + 1, :] = decay * n_h + (_dot(w_hi, kb) + _dot(w_lo, kb))
        m_out = jnp.where(lane == h, m_new, m_out)

        hn_t = hh_t * lax.rsqrt(jnp.mean(hh_t * hh_t, axis=0, keepdims=True) + EPS) * gh_ref[...]
        out_ref[0, :, hs] = jax.nn.sigmoid(og_ref[0, :, hs]) * hn_t.T
    m_scr[...] = m_out

    @pl.when(ci == pl.num_programs(1) - 1)
    def _():
        cout_ref[0] = c_scr[...]
        nout_ref[0] = n_scr[...]
        mout_ref[0] = m_scr[...]
        convout_ref[0] = prev_scr[8 - (CONV_W - 1):8, :]


def _mlstm(ml_qk, ml_v, ml_o, gates, conv_past, c0, n0, m0, w_conv, b_conv, g_head):
    b, s, w2 = ml_qk.shape
    width = w2 // 2
    nh, dh = ML_HEADS, ML_HEAD_DIM
    chunk = min(ML_CHUNK, s)
    assert s % chunk == 0 and chunk >= 8
    tok = lambda w: pl.BlockSpec((1, chunk, w), lambda bi, ci: (bi, ci, 0))
    perb = lambda shp: pl.BlockSpec((1,) + shp, lambda bi, ci: (bi,) + (0,) * len(shp))
    const = lambda shp: pl.BlockSpec(shp, lambda bi, ci: (0,) * len(shp))
    return pl.pallas_call(
        functools.partial(_mlstm_kernel, chunk=chunk),
        grid=(b, s // chunk),
        in_specs=[tok(w2), tok(width), tok(width), tok(LANES),
                  perb((CONV_W - 1, w2)), perb((nh, dh, dh)), perb((nh, dh)), perb((1, LANES)),
                  const((CONV_W, w2)), const((1, w2)), const((dh, 1))],
        out_specs=[tok(width), perb((nh, dh, dh)), perb((nh, dh)), perb((1, LANES)), perb((CONV_W - 1, w2))],
        out_shape=[jax.ShapeDtypeStruct((b, s, width), F32),
                   jax.ShapeDtypeStruct((b, nh, dh, dh), F32),
                   jax.ShapeDtypeStruct((b, nh, dh), F32),
                   jax.ShapeDtypeStruct((b, 1, LANES), F32),
                   jax.ShapeDtypeStruct((b, CONV_W - 1, w2), F32)],
        scratch_shapes=[pltpu.VMEM((8 + chunk, w2), F32), pltpu.VMEM((nh, dh, dh), F32),
                        pltpu.VMEM((nh, dh), F32), pltpu.VMEM((1, LANES), F32)],
        compiler_params=_cparams(("arbitrary", "arbitrary")),
        name="mlstm",
    )(ml_qk, ml_v, ml_o, gates, conv_past, c0, n0, m0, w_conv, b_conv, g_head)


def _mod_spec(tokens_per_batch, tm, d):
    if tokens_per_batch >= tm:
        assert tokens_per_batch % tm == 0
        per = tokens_per_batch // tm
        return 1, pl.BlockSpec((1, 6, d), lambda i: (i // per, 0, 0))
    assert tm % tokens_per_batch == 0
    nb = tm // tokens_per_batch
    return nb, pl.BlockSpec((nb, 6, d), lambda i: (i, 0, 0))


def _outproj_kernel(osb_ref, oml_ref, x_ref, mod_ref, w1_ref, w2_ref, g2_ref, wrh_ref, wrl_ref, br_ref,
                    x1_ref, h2_ref, ti_ref, tg_ref, *, nb):
    tm, d = x_ref.shape
    rows = tm // nb

    def per_token(j):
        if nb == 1:
            return mod_ref[0, j:j + 1, :]
        return jnp.concatenate([jnp.broadcast_to(mod_ref[g, j:j + 1, :], (rows, d)) for g in range(nb)], axis=0)

    mix = _dot(osb_ref[...].astype(BF16), w1_ref[...]) + _dot(oml_ref[...].astype(BF16), w2_ref[...])
    x1 = x_ref[...] + per_token(2) * mix
    x1_ref[...] = x1
    ms = jnp.mean(x1 * x1, axis=-1, keepdims=True)
    h2 = x1 * lax.rsqrt(ms + EPS) * g2_ref[...]
    h2 = h2 * (1.0 + per_token(4)) + per_token(3)
    _store_row_tiles(h2_ref, h2)

    hb = h2.astype(BF16)
    hl = (h2 - hb.astype(F32)).astype(BF16)
    logits = (_dot_nt(wrh_ref[...], hb) + _dot_nt(wrh_ref[...], hl) + _dot_nt(wrl_ref[...], hb))[0:N_EXPERTS, :]
    cur = logits + br_ref[...]
    erow = lax.broadcasted_iota(I32, cur.shape, 0)
    vals, idxs = [], []
    for _ in range(TOP_K):
        m = jnp.max(cur, axis=0, keepdims=True)
        idx = jnp.min(jnp.where(cur == m, erow, N_EXPERTS), axis=0, keepdims=True)
        vals.append(m)
        idxs.append(idx)
        cur = jnp.where(erow == idx, -jnp.inf, cur)
    es = [jnp.exp(v - vals[0]) for v in vals]
    tot = es[0] + es[1] + es[2] + es[3]
    krow = lax.broadcasted_iota(I32, (SUBLANES, tm), 0)
    ti = jnp.zeros((SUBLANES, tm), I32)
    tg = jnp.zeros((SUBLANES, tm), F32)
    for k in range(TOP_K):
        ti = jnp.where(krow == k, idxs[k], ti)
        tg = jnp.where(krow == k, es[k] / tot, tg)
    ti_ref[...] = ti
    tg_ref[...] = tg


def _outproj(o_sb, o_ml, x, mod, tokens_per_batch, w1, w2, g2, wrt_hi, wrt_lo, brt):
    t, d = x.shape
    hw = o_sb.shape[1]
    tm = ROW_TILE
    assert t % tm == 0
    nb, mod_spec = _mod_spec(tokens_per_batch, tm, d)
    tok = lambda w: pl.BlockSpec((tm, w), lambda i: (i, 0))
    const = lambda shp: pl.BlockSpec(shp, lambda i: (0,) * len(shp))
    rows8 = pl.BlockSpec((SUBLANES, tm), lambda i: (0, i))
    return pl.pallas_call(
        functools.partial(_outproj_kernel, nb=nb),
        grid=(t // tm,),
        in_specs=[tok(hw), tok(hw), tok(d), mod_spec,
                  const((hw, d)), const((hw, d)), const((1, d)),
                  const((LANES, d)), const((LANES, d)), const((N_EXPERTS, 1))],
        out_specs=[tok(d), pl.BlockSpec((tm * SUBLANES, LANES), lambda i: (i, 0)), rows8, rows8],
        out_shape=[jax.ShapeDtypeStruct((t, d), F32), jax.ShapeDtypeStruct((t * SUBLANES, LANES), F32),
                   jax.ShapeDtypeStruct((SUBLANES, t), I32), jax.ShapeDtypeStruct((SUBLANES, t), F32)],
        compiler_params=_cparams(("arbitrary",)),
        name="outproj_router",
    )(o_sb, o_ml, x, mod, w1, w2, g2, wrt_hi, wrt_lo, brt)


def _rank_kernel(ti_ref, pos_ref, cnt_ref, carry_scr, start_scr):
    ph = pl.program_id(0)
    i = pl.program_id(1)
    tm = ti_ref.shape[1]
    rep = lambda a: jnp.concatenate([a] * (tm // LANES), axis=1)

    @pl.when(i == 0)
    def _():
        carry_scr[...] = jnp.zeros_like(carry_scr)

    ti = ti_ref[...]
    erow = lax.broadcasted_iota(I32, (N_EXPERTS, tm), 0)
    sel = [erow == ti[k:k + 1, :] for k in range(TOP_K)]
    hit = jnp.zeros((N_EXPERTS, tm), F32)
    for k in range(TOP_K):
        hit = hit + jnp.where(sel[k], 1.0, 0.0)
    hit = hit.astype(BF16)
    seen = carry_scr[...]

    @pl.when(ph == 1)
    def _():
        earlier = lax.broadcasted_iota(I32, (tm, tm), 0) < lax.broadcasted_iota(I32, (tm, tm), 1)
        row_of = _dot(hit, jnp.where(earlier, 1.0, 0.0).astype(BF16)) + rep(seen) + rep(start_scr[...])
        krow = lax.broadcasted_iota(I32, (SUBLANES, tm), 0)
        pos = jnp.zeros((SUBLANES, tm), F32)
        for k in range(TOP_K):
            pos = jnp.where(krow == k, jnp.sum(jnp.where(sel[k], row_of, 0.0), axis=0, keepdims=True), pos)
        pos_ref[...] = pos.astype(I32)

    carry_scr[...] = seen + _dot(hit, jnp.ones((tm, LANES), BF16))

    @pl.when(jnp.logical_and(ph == 0, i == pl.num_programs(1) - 1))
    def _():
        cnt = carry_scr[...]
        cnt_ref[...] = cnt.astype(I32)
        padded = jnp.floor((cnt + (MOE_BLK - 1)) * (1.0 / MOE_BLK)) * MOE_BLK
        lower = (lax.broadcasted_iota(I32, (N_EXPERTS, N_EXPERTS), 1)
                 < lax.broadcasted_iota(I32, (N_EXPERTS, N_EXPERTS), 0))
        lower = jnp.where(lower, 1.0, 0.0).astype(BF16)
        p1, p2, p3 = _split3(padded)
        start_scr[...] = _dot(lower, p1) + _dot(lower, p2) + _dot(lower, p3)


def _positions(top_idx_t):
    t = top_idx_t.shape[1]
    tm = ROW_TILE
    assert t % tm == 0
    return pl.pallas_call(
        _rank_kernel,
        grid=(2, t // tm),
        in_specs=[pl.BlockSpec((SUBLANES, tm), lambda ph, i: (0, i))],
        out_specs=[pl.BlockSpec((SUBLANES, tm), lambda ph, i: (0, i * ph)),
                   pl.BlockSpec((N_EXPERTS, LANES), lambda ph, i: (0, 0))],
        out_shape=[jax.ShapeDtypeStruct((SUBLANES, t), I32), jax.ShapeDtypeStruct((N_EXPERTS, LANES), I32)],
        scratch_shapes=[pltpu.VMEM((N_EXPERTS, LANES), F32), pltpu.VMEM((N_EXPERTS, LANES), F32)],
        compiler_params=_cparams(("arbitrary", "arbitrary")),
        name="expert_positions",
    )(top_idx_t)


def _row_tile(ref, r):
    return ref.at[pl.ds(pl.multiple_of(r * SUBLANES, SUBLANES), SUBLANES), :]


def _dispatch_kernel(*refs, tm):
    pos_hbm, h_ref = refs[:2]
    xs_hbm, pos_smem0, pos_smem1, stage, pos_sem, row_sem = refs[-6:]
    i = pl.program_id(0)
    n = pl.num_programs(0)
    slot = lax.rem(i, 2)
    nidx = tm * TOP_K

    def pos_copy(tile, sl):
        return pltpu.make_async_copy(pos_hbm.at[pl.ds(pl.multiple_of(tile * nidx, nidx), nidx)],
                                     (pos_smem0, pos_smem1)[sl], pos_sem.at[sl])

    def drain(sl):
        for _ in range(TOP_K):
            pltpu.make_async_copy(stage.at[sl], stage.at[sl], row_sem.at[sl]).wait()

    @pl.when(i == 0)
    def _():
        pos_copy(0, 0).start()

    for sl in range(2):
        @pl.when(jnp.logical_and(i + 1 < n, slot == 1 - sl))
        def _():
            pos_copy(i + 1, sl).start()

    @pl.when(i >= 2)
    def _():
        drain(slot)

    stage[slot] = h_ref[...]

    for sl in range(2):
        @pl.when(slot == sl)
        def _():
            pos_copy(i, sl).wait()
            pos_smem = (pos_smem0, pos_smem1)[sl]

            def issue(c, carry):
                for rr in range(ISSUE_UNROLL):
                    r = c * ISSUE_UNROLL + rr
                    for k in range(TOP_K):
                        pltpu.make_async_copy(_row_tile(stage.at[sl], r), _row_tile(xs_hbm, pos_smem[k * tm + r]),
                                              row_sem.at[sl]).start(priority=k % DMA_QUEUES)
                return carry
            lax.fori_loop(0, tm // ISSUE_UNROLL, issue, 0)

    @pl.when(i == n - 1)
    def _():
        drain(slot)

        @pl.when(n >= 2)
        def _():
            drain(1 - slot)


def _dispatch(pos_flat, h_tiles, xs=None, n_rows=None):
    t = h_tiles.shape[0] // SUBLANES
    tm = GATHER_TILE
    assert t % tm == 0 and tm % ISSUE_UNROLL == 0
    in_specs = [pl.BlockSpec(memory_space=pl.ANY), pl.BlockSpec((tm * SUBLANES, LANES), lambda i: (i, 0))]
    args = [pos_flat, h_tiles]
    if xs is not None:
        in_specs.append(pl.BlockSpec(memory_space=pl.ANY))
        args.append(xs)
    return pl.pallas_call(
        functools.partial(_dispatch_kernel, tm=tm),
        grid=(t // tm,),
        in_specs=in_specs,
        out_specs=pl.BlockSpec(memory_space=pl.ANY),
        out_shape=jax.ShapeDtypeStruct((n_rows * SUBLANES, LANES) if xs is None else xs.shape, F32),
        scratch_shapes=[pltpu.SMEM((tm * TOP_K,), I32), pltpu.SMEM((tm * TOP_K,), I32),
                        pltpu.VMEM((2, tm * SUBLANES, LANES), F32),
                        pltpu.SemaphoreType.DMA((2,)), pltpu.SemaphoreType.DMA((2,))],
        input_output_aliases={} if xs is None else {2: 0},
        compiler_params=_cparams(("arbitrary",)),
        name="moe_dispatch",
    )(*args)


def _pad_fill_kernel(pos_hbm, xs_in_hbm, xs_hbm, pos_smem, zero_tile, pos_sem, row_sem, *, nidx):
    del xs_in_hbm
    i = pl.program_id(0)
    cp = pltpu.make_async_copy(pos_hbm.at[pl.ds(pl.multiple_of(i * nidx, nidx), nidx)], pos_smem, pos_sem)
    cp.start()
    zero_tile[...] = jnp.zeros_like(zero_tile)
    cp.wait()

    def issue(c, carry):
        for rr in range(TOP_K * ISSUE_UNROLL):
            r = c * (TOP_K * ISSUE_UNROLL) + rr
            pltpu.make_async_copy(zero_tile, _row_tile(xs_hbm, pos_smem[r]), row_sem).start(
                priority=rr % DMA_QUEUES)
        return carry
    lax.fori_loop(0, nidx // (TOP_K * ISSUE_UNROLL), issue, 0)
    whole = xs_hbm.at[pl.ds(0, nidx * SUBLANES), :]
    pltpu.make_async_copy(whole, whole, row_sem).wait()


def _pad_fill(pad_pos, xs):
    nidx = GATHER_TILE * TOP_K
    assert pad_pos.shape[0] % nidx == 0
    return pl.pallas_call(
        functools.partial(_pad_fill_kernel, nidx=nidx),
        grid=(pad_pos.shape[0] // nidx,),
        in_specs=[pl.BlockSpec(memory_space=pl.ANY), pl.BlockSpec(memory_space=pl.ANY)],
        out_specs=pl.BlockSpec(memory_space=pl.ANY),
        out_shape=jax.ShapeDtypeStruct(xs.shape, xs.dtype),
        scratch_shapes=[pltpu.SMEM((nidx,), I32), pltpu.VMEM((SUBLANES, LANES), F32),
                        pltpu.SemaphoreType.DMA, pltpu.SemaphoreType.DMA],
        input_output_aliases={1: 0},
        compiler_params=_cparams(("arbitrary",)),
        name="moe_pad_fill",
    )(pad_pos, xs)


def _moe_kernel(be_ref, nu_ref, xs_ref, wgu_ref, bgu_ref, wd_ref, bd_ref, ys_ref, wgu_bf, wd_bf):
    i = pl.program_id(0)
    d, ff = wgu_ref.shape[1], wd_ref.shape[1]

    @pl.when(jnp.logical_and(i < nu_ref[0],
                             jnp.logical_or(i == 0, be_ref[i] != be_ref[jnp.maximum(i - 1, 0)])))
    def _():
        for r in range(0, d, LANES):
            wgu_bf[r:r + LANES, :] = wgu_ref[0, r:r + LANES, :].astype(BF16)
        for r in range(0, ff, LANES):
            wd_bf[r:r + LANES, :] = wd_ref[0, r:r + LANES, :].astype(BF16)

    @pl.when(i < nu_ref[0])
    def _():
        x = _load_row_tiles(xs_ref, MOE_BLK).astype(BF16)
        glu = _dot(x, wgu_bf[:, 0:ff]) + bgu_ref[0, :, 0:ff]
        lin = _dot(x, wgu_bf[:, ff:2 * ff]) + bgu_ref[0, :, ff:2 * ff]
        glu = jnp.minimum(glu, SWIGLU_LIMIT)
        lin = jnp.clip(lin, -SWIGLU_LIMIT, SWIGLU_LIMIT)
        act = glu * jax.nn.sigmoid(SWIGLU_ALPHA * glu) * (lin + 1.0)
        _store_row_tiles(ys_ref, _dot(act.astype(BF16), wd_bf[...]) + bd_ref[0])

    @pl.when(i >= nu_ref[0])
    def _():
        ys_ref[...] = jnp.zeros_like(ys_ref)


def _moe_mlp(block_expert, n_used, xs, w_gu, b_gu, w_down, b_down):
    ne, d, ff2 = w_gu.shape
    ff = ff2 // 2
    tb = MOE_BLK
    n_rows = block_expert.shape[0] * tb
    assert xs.shape[0] >= n_rows * SUBLANES
    blk = lambda i, be, nu: jnp.minimum(i, nu[0] - 1)
    grid_spec = pltpu.PrefetchScalarGridSpec(
        num_scalar_prefetch=2,
        grid=(n_rows // tb,),
        in_specs=[pl.BlockSpec((tb * SUBLANES, LANES), lambda i, be, nu: (blk(i, be, nu), 0)),
                  pl.BlockSpec((1, d, ff2), lambda i, be, nu: (be[blk(i, be, nu)], 0, 0)),
                  pl.BlockSpec((1, 1, ff2), lambda i, be, nu: (be[blk(i, be, nu)], 0, 0)),
                  pl.BlockSpec((1, ff, d), lambda i, be, nu: (be[blk(i, be, nu)], 0, 0)),
                  pl.BlockSpec((1, 1, d), lambda i, be, nu: (be[blk(i, be, nu)], 0, 0))],
        out_specs=pl.BlockSpec((tb * SUBLANES, LANES), lambda i, be, nu: (i, 0)),
        scratch_shapes=[pltpu.VMEM((d, ff2), BF16), pltpu.VMEM((ff, d), BF16)],
    )
    return pl.pallas_call(
        _moe_kernel,
        grid_spec=grid_spec,
        out_shape=jax.ShapeDtypeStruct((n_rows * SUBLANES, LANES), F32),
        compiler_params=_cparams(("arbitrary",)),
        name="moe_mlp",
    )(block_expert, n_used, xs, w_gu, b_gu.reshape(ne, 1, ff2), w_down, b_down.reshape(ne, 1, d))


def _combine_kernel(pos_hbm, ys_hbm, x1_ref, tg_ref, mod_ref, y_ref, pos_smem0, pos_smem1, buf, pos_sem, row_sem,
                    *, tm, nb):
    i = pl.program_id(0)
    n = pl.num_programs(0)
    slot = lax.rem(i, 2)
    nidx = tm * TOP_K

    def pos_copy(tile, sl):
        return pltpu.make_async_copy(pos_hbm.at[pl.ds(pl.multiple_of(tile * nidx, nidx), nidx)],
                                     (pos_smem0, pos_smem1)[sl], pos_sem.at[sl])

    def gather(tile, sl):
        pos_copy(tile, sl).wait()
        pos_smem = (pos_smem0, pos_smem1)[sl]

        def issue(c, carry):
            for rr in range(ISSUE_UNROLL):
                r = c * ISSUE_UNROLL + rr
                for k in range(TOP_K):
                    pltpu.make_async_copy(_row_tile(ys_hbm, pos_smem[k * tm + r]),
                                          _row_tile(buf.at[sl, k], r), row_sem.at[sl]).start(
                                              priority=k % DMA_QUEUES)
            return carry
        lax.fori_loop(0, tm // ISSUE_UNROLL, issue, 0)

        @pl.when(tile + 1 < n)
        def _():
            pos_copy(tile + 1, 1 - sl).start()

    @pl.when(i == 0)
    def _():
        pos_copy(0, 0).start()
        gather(0, 0)

    for sl in range(2):
        @pl.when(jnp.logical_and(i + 1 < n, slot == 1 - sl))
        def _():
            gather(i + 1, sl)

    for k in range(TOP_K):
        pltpu.make_async_copy(buf.at[slot, k], buf.at[slot, k], row_sem.at[slot]).wait()

    tg = tg_ref[...].T
    rows = tm // nb
    for c in range(SUBLANES):
        cols = slice(c * LANES, (c + 1) * LANES)
        ff = tg[:, 0:1] * buf[slot, 0, pl.ds(c, tm, stride=SUBLANES), :]
        for k in range(1, TOP_K):
            ff = ff + tg[:, k:k + 1] * buf[slot, k, pl.ds(c, tm, stride=SUBLANES), :]
        for g in range(nb):
            rs = slice(g * rows, (g + 1) * rows)
            y_ref[rs, cols] = x1_ref[rs, cols] + mod_ref[g, 5:6, cols] * ff[rs, :]


def _combine(pos_flat, ys, x1, top_gate, mod, tokens_per_batch):
    t, d = x1.shape
    tm = GATHER_TILE
    assert t % tm == 0 and tm % ISSUE_UNROLL == 0 and d == SUBLANES * LANES
    nb, mod_spec = _mod_spec(tokens_per_batch, tm, d)
    return pl.pallas_call(
        functools.partial(_combine_kernel, tm=tm, nb=nb),
        grid=(t // tm,),
        in_specs=[pl.BlockSpec(memory_space=pl.ANY), pl.BlockSpec(memory_space=pl.ANY),
                  pl.BlockSpec((tm, d), lambda i: (i, 0)), pl.BlockSpec((SUBLANES, tm), lambda i: (0, i)), mod_spec],
        out_specs=pl.BlockSpec((tm, d), lambda i: (i, 0)),
        out_shape=jax.ShapeDtypeStruct((t, d), F32),
        scratch_shapes=[pltpu.SMEM((tm * TOP_K,), I32), pltpu.SMEM((tm * TOP_K,), I32),
                        pltpu.VMEM((2, TOP_K, tm * SUBLANES, LANES), F32),
                        pltpu.SemaphoreType.DMA((2,)), pltpu.SemaphoreType.DMA((2,))],
        compiler_params=_cparams(("arbitrary",)),
        name="moe_combine",
    )(pos_flat, ys, x1, top_gate, mod)


def _pad_lanes(a, fill=0.0):
    return jnp.pad(a, [(0, 0)] * (a.ndim - 1) + [(0, LANES - a.shape[-1])], constant_values=fill)


def _hi_lo(w):
    hi = w.astype(BF16)
    return hi, (w - hi.astype(F32)).astype(BF16)


def _mixer(x, mod, k_past, v_past, conv_past, c0, n0, m0, p):
    b, s, d = x.shape
    q_raw, k_norm, v, ml_qk, ml_v, ml_o, gates, k_tiles, v_tiles = _inproj(
        x, mod, p["g1"], p["w_main"], p["wg_hi"], p["wg_lo"], p["bg"], p["g_k"], p["widths"])
    o_sb = _sb_attention(q_raw, k_norm, v, k_past, v_past, p["g_q"])
    o_ml, c_new, n_new, m_new, conv_new = _mlstm(
        ml_qk, ml_v, ml_o, gates, conv_past, c0, n0, _pad_lanes(m0)[:, None, :],
        p["w_conv"], p["b_conv"], p["g_head"])
    flat = lambda a: a.reshape(b * s, a.shape[-1])
    x1, h2, ti, tg = _outproj(flat(o_sb), flat(o_ml), flat(x), mod, s, p["w_out1"], p["w_out2"], p["g2"],
                              p["wrt_hi"], p["wrt_lo"], p["brt"])
    assert v.shape[-1] == SUBLANES * SB_HEAD_DIM
    state = (k_tiles.reshape(b, s, SUBLANES, SB_HEAD_DIM), v_tiles.reshape(b, s, SUBLANES, SB_HEAD_DIM), conv_new,
             c_new, n_new, m_new[:, 0, :ML_HEADS])
    return x1, h2, ti, tg, state


def _layer(xp, xs_, cp, cs, kc, vc, convc, cc, nc, mc, w_ada, b_ada, g_norm1, w_in, g_q, g_k, w_conv, b_conv,
           b_gate, g_head, w_out, g_norm2, w_router, b_router, w_gu, b_gu, w_down, b_down):
    d = xp.shape[-1]
    bp, sp = xp.shape[:2]
    bs, ss = xs_.shape[:2]
    sbw = kc.shape[-2] * kc.shape[-1]
    mlw = ML_HEADS * ML_HEAD_DIM
    ngate = 2 * ML_HEADS
    gate0 = 3 * sbw + 3 * mlw
    w_main = jnp.concatenate([w_in[:, :gate0], w_in[:, gate0 + ngate:]], axis=1).astype(BF16)
    wg_hi, wg_lo = _hi_lo(_pad_lanes(w_in[:, gate0:gate0 + ngate]))
    wrt_hi, wrt_lo = _hi_lo(_pad_lanes(w_router).T)
    p = dict(
        widths=(sbw, sbw, sbw, 2 * mlw, mlw, mlw),
        g1=g_norm1[None], w_main=w_main, wg_hi=wg_hi, wg_lo=wg_lo, bg=_pad_lanes(b_gate)[None],
        g_q=jnp.tile(g_q, LANES // SB_HEAD_DIM)[None], g_k=jnp.tile(g_k, LANES // SB_HEAD_DIM)[None],
        w_conv=w_conv, b_conv=b_conv[None], g_head=g_head[:, None],
        w_out1=w_out[:sbw].astype(BF16), w_out2=w_out[sbw:].astype(BF16), g2=g_norm2[None],
        wrt_hi=wrt_hi, wrt_lo=wrt_lo, brt=b_router[:, None],
    )
    mod = _ada(jnp.concatenate([cp, cs], axis=0), w_ada, b_ada)
    mod_p = mod[:bp].reshape(bp, 6, d)
    mod_s = mod[bp:].reshape(bs, 6, d)

    zeros = lambda *shp: jnp.zeros(shp, F32)
    feature_major = lambda a: jnp.transpose(a, (0, 2, 3, 1)).reshape(a.shape[0], sbw, a.shape[1])
    x1p, h2p, tip, tgp, st_p = _mixer(
        xp, mod_p, None, None, zeros(bp, CONV_W - 1, 2 * mlw), zeros(bp, ML_HEADS, ML_HEAD_DIM, ML_HEAD_DIM),
        zeros(bp, ML_HEADS, ML_HEAD_DIM), zeros(bp, ML_HEADS), p)
    x1s, h2s, tis, tgs, st_s = _mixer(
        xs_, mod_s, feature_major(kc), feature_major(vc), convc, cc, nc, mc, p)

    tp, ts = bp * sp, bs * ss
    pos, counts = _positions(jnp.concatenate([tip, tis], axis=1))
    counts = counts[:, 0]
    padded = (counts + MOE_BLK - 1) // MOE_BLK * MOE_BLK
    pad_end = jnp.cumsum(padded)
    pad_start = pad_end - padded
    n_blocks = -(-((tp + ts) * TOP_K + N_EXPERTS * (MOE_BLK - 1)) // MOE_BLK)
    n_used = (pad_end[-1] // MOE_BLK).astype(I32).reshape(1)
    block_start = jnp.arange(n_blocks, dtype=I32) * MOE_BLK
    block_expert = jnp.minimum(jnp.sum(block_start[:, None] >= pad_end[None, :], axis=1), N_EXPERTS - 1).astype(I32)

    def per_tile(rows):
        t = rows.shape[1]
        return rows.reshape(TOP_K, t // GATHER_TILE, GATHER_TILE).transpose(1, 0, 2).reshape(-1)
    pos_p = per_tile(pos[:TOP_K, :tp])
    pos_s = per_tile(pos[:TOP_K, tp:])

    n_rows = n_blocks * MOE_BLK
    slot = jnp.arange(MOE_BLK, dtype=I32)[None, :]
    spare = n_rows + jnp.arange(N_EXPERTS * MOE_BLK, dtype=I32).reshape(N_EXPERTS, MOE_BLK)
    pad_pos = jnp.where(slot < (padded - counts)[:, None], (pad_start + counts)[:, None] + slot, spare).reshape(-1)

    xs_buf = _dispatch(pos_p, h2p, n_rows=n_rows + N_EXPERTS * MOE_BLK)
    xs_buf = _dispatch(pos_s, h2s, xs_buf)
    xs_buf = _pad_fill(pad_pos.astype(I32), xs_buf)
    ys = _moe_mlp(block_expert, n_used, xs_buf, w_gu, b_gu, w_down, b_down)
    yp = _combine(pos_p, ys, x1p, tgp, mod_p, sp).reshape(bp, sp, d)
    ysm = _combine(pos_s, ys, x1s, tgs, mod_s, ss).reshape(bs, ss, d)
    return yp, ysm, st_p, st_s


def kernel(x_prompt, x_sample, c_prompt, c_sample, cache_sb_k, cache_sb_v, state_conv, state_mlstm_c, state_mlstm_n, state_mlstm_m, w_ada, b_ada, g_norm1, w_in, g_q, g_k, w_conv, b_conv, b_gate, g_head, w_out, g_norm2, w_router, b_router, w_gu, b_gu, w_down, b_down):
    assert w_ada.shape[0] == 1, "single-layer step"
    yp, ys, st_p, st_s = _layer(
        x_prompt, x_sample, c_prompt, c_sample, cache_sb_k[0], cache_sb_v[0], state_conv[0], state_mlstm_c[0],
        state_mlstm_n[0], state_mlstm_m[0], w_ada[0], b_ada[0], g_norm1[0], w_in[0], g_q[0], g_k[0], w_conv[0],
        b_conv[0], b_gate[0], g_head[0], w_out[0], g_norm2[0], w_router[0], b_router[0], w_gu[0], b_gu[0],
        w_down[0], b_down[0])
    return (yp, ys) + tuple(a[None] for a in st_p) + tuple(a[None] for a in st_s)
```

```python
import functools
import math

import jax
import jax.numpy as jnp
from jax import lax
from jax.experimental import pallas as pl
from jax.experimental.pallas import tpu as pltpu

F32 = jnp.float32
BF16 = jnp.bfloat16
I32 = jnp.int32

EPS = 1e-6
LANES = 128
SUBLANES = 8
SB_HEAD_DIM = 64
ML_HEAD_DIM = 128
ML_HEADS = 4
CONV_W = 4
N_EXPERTS = 32
TOP_K = 4
SWIGLU_LIMIT = 7.0
SWIGLU_ALPHA = 1.702
VMEM_LIMIT = 56 * 1024 * 1024

ROW_TILE = 512
MOE_BLK = 512
GATHER_TILE = 256
ISSUE_UNROLL = 8
DMA_QUEUES = 2
ML_CHUNK = 256
SB_BLK = 128
SB_SWEEP = 256
SB_UNDERFLOW = -105.0


def _cparams(sem):
    return pltpu.CompilerParams(dimension_semantics=sem, vmem_limit_bytes=VMEM_LIMIT)


def _split3(x):
    p1 = x.astype(BF16)
    r1 = x - p1.astype(F32)
    p2 = r1.astype(BF16)
    p3 = (r1 - p2.astype(F32)).astype(BF16)
    return p1, p2, p3


def _dot(a, b):
    return jnp.dot(a, b, preferred_element_type=F32)


def _dot_nt(a, b):
    return lax.dot_general(a, b, (((1,), (1,)), ((), ())), preferred_element_type=F32)


def _dot3(x, w_hi, w_lo):
    xh = x.astype(BF16)
    xl = (x - xh.astype(F32)).astype(BF16)
    return _dot(xh, w_hi) + _dot(xl, w_hi) + _dot(xh, w_lo)


def _store_row_tiles(ref, x):
    n, d = x.shape
    assert d == SUBLANES * LANES
    for c in range(SUBLANES):
        ref[pl.ds(c, n, stride=SUBLANES), :] = x[:, c * LANES:(c + 1) * LANES]


def _load_row_tiles(ref, n):
    return jnp.concatenate([ref[pl.ds(c, n, stride=SUBLANES), :] for c in range(SUBLANES)], axis=1)


def _store_head_tiles(ref, row0, head0, x):
    n = x.shape[0]
    for j in range(x.shape[1] // SB_HEAD_DIM):
        ref[pl.ds(row0 * SUBLANES + head0 + j, n, stride=SUBLANES), :] = x[:, j * SB_HEAD_DIM:(j + 1) * SB_HEAD_DIM]


def _headnorm(x, g):
    lo_half = lax.broadcasted_iota(I32, (1, LANES), 1) < SB_HEAD_DIM
    x2 = x * x
    s0 = jnp.sum(jnp.where(lo_half, x2, 0.0), axis=-1, keepdims=True)
    s1 = jnp.sum(jnp.where(lo_half, 0.0, x2), axis=-1, keepdims=True)
    r = jnp.where(lo_half, lax.rsqrt(s0 * (1.0 / SB_HEAD_DIM) + EPS), lax.rsqrt(s1 * (1.0 / SB_HEAD_DIM) + EPS))
    return x * r * g


def _mod_rows(mod_ref, j, nb, shape):
    if nb == 1:
        return mod_ref[0, j:j + 1, :]
    rows = shape[0] // nb
    return jnp.concatenate([jnp.broadcast_to(mod_ref[g, j:j + 1, :], (rows, shape[1])) for g in range(nb)], axis=0)


def _mod_spec(tokens_per_batch, tm, d):
    if tokens_per_batch >= tm:
        assert tokens_per_batch % tm == 0
        per = tokens_per_batch // tm
        return 1, pl.BlockSpec((1, 6, d), lambda i: (i // per, 0, 0))
    assert tm % tokens_per_batch == 0
    nb = tm // tokens_per_batch
    return nb, pl.BlockSpec((nb, 6, d), lambda i: (i, 0, 0))


def _log_sigmoid(x):
    return jnp.minimum(x, 0.0) - jnp.log1p(jnp.exp(-jnp.abs(x)))


def _ada_kernel(c_ref, w_ref, b_ref, o_ref):
    c = c_ref[...]
    s = c * jax.nn.sigmoid(c)
    o_ref[...] = jnp.dot(s, w_ref[...], precision=lax.Precision.HIGHEST,
                         preferred_element_type=F32) + b_ref[...]


def _ada(c, w_ada, b_ada):
    n, d = c.shape
    cols = w_ada.shape[1]
    tn = cols // 6
    return pl.pallas_call(
        _ada_kernel,
        grid=(cols // tn,),
        in_specs=[pl.BlockSpec((n, d), lambda j: (0, 0)),
                  pl.BlockSpec((d, tn), lambda j: (0, j)),
                  pl.BlockSpec((1, tn), lambda j: (0, j))],
        out_specs=pl.BlockSpec((n, tn), lambda j: (0, j)),
        out_shape=jax.ShapeDtypeStruct((n, cols), F32),
        compiler_params=_cparams(("arbitrary",)),
        name="ada_mod",
    )(c, w_ada, b_ada.reshape(1, cols))


def _inproj_kernel(x_ref, mod_ref, g1_ref, w_ref, wgh_ref, wgl_ref, bg_ref, gk_ref,
                   q_ref, k_ref, v_ref, qk_ref, mv_ref, mo_ref, gate_ref, kt_ref, vt_ref, *, widths, nb):
    x = x_ref[...]
    ms = jnp.mean(x * x, axis=-1, keepdims=True)
    h = x * lax.rsqrt(ms + EPS) * g1_ref[...]
    h = h * (1.0 + _mod_rows(mod_ref, 1, nb, x.shape)) + _mod_rows(mod_ref, 0, nb, x.shape)
    hb = h.astype(BF16)
    off = 0
    for ref, wd in zip((q_ref, k_ref, v_ref, qk_ref, mv_ref, mo_ref), widths):
        u = _dot(hb, w_ref[:, off:off + wd])
        if ref is k_ref:
            u = jnp.concatenate([_headnorm(u[:, c:c + LANES], gk_ref[...]) for c in range(0, wd, LANES)], axis=1)
            _store_head_tiles(kt_ref, 0, 0, u)
        if ref is v_ref:
            _store_head_tiles(vt_ref, 0, 0, u)
        ref[...] = u
        off += wd
    gate_ref[...] = _dot3(h, wgh_ref[...], wgl_ref[...]) + bg_ref[...]


def _inproj(x, mod, tokens_per_batch, g1, w_main, wg_hi, wg_lo, bg, g_k, widths):
    t, d = x.shape
    tm = ROW_TILE
    assert t % tm == 0
    ncol = w_main.shape[1]
    nb, mod_spec = _mod_spec(tokens_per_batch, tm, d)
    tok = lambda w: pl.BlockSpec((tm, w), lambda i: (i, 0))
    const = lambda shp: pl.BlockSpec(shp, lambda i: (0,) * len(shp))
    head_tiles = pl.BlockSpec((tm * SUBLANES, SB_HEAD_DIM), lambda i: (i, 0))
    return pl.pallas_call(
        functools.partial(_inproj_kernel, widths=widths, nb=nb),
        grid=(t // tm,),
        in_specs=[tok(d), mod_spec,
                  const((1, d)), const((d, ncol)), const((d, LANES)), const((d, LANES)), const((1, LANES)),
                  const((1, LANES))],
        out_specs=[tok(w) for w in widths] + [tok(LANES), head_tiles, head_tiles],
        out_shape=[jax.ShapeDtypeStruct((t, w), F32) for w in widths]
        + [jax.ShapeDtypeStruct((t, LANES), F32)]
        + [jax.ShapeDtypeStruct((t * SUBLANES, SB_HEAD_DIM), F32)] * 2,
        compiler_params=_cparams(("arbitrary",)),
        name="inproj",
    )(x, mod, g1, w_main, wg_hi, wg_lo, bg, g_k)


def _sb_kernel(*refs, past, seq, tq, kb, npair):
    ks_scr, vs_scr, run_scr, acc_scr, wdiag_scr, wsub_scr, causal_scr = refs[-7:]
    if past:
        q_ref, k_ref, v_ref, kp_ref, vp_ref, gq_ref, o_ref = refs[:-7]
    else:
        q_ref, k_ref, v_ref, gq_ref, o_ref = refs[:-7]
    i = pl.program_id(1)
    lane = lax.broadcasted_iota(I32, (1, LANES), 1)
    lo_half = lane < SB_HEAD_DIM
    pairs = [slice(hp * LANES, (hp + 1) * LANES) for hp in range(npair)]

    @pl.when(i == 0)
    def _():
        ch = min(256, seq)
        for c in range(seq // ch):
            rows = slice(c * ch, (c + 1) * ch)
            dst = slice(past + c * ch, past + (c + 1) * ch)
            ks_scr[dst, :] = k_ref[0, rows, :].astype(BF16)
            vs_scr[dst, :] = v_ref[0, rows, :].astype(BF16)
        if past:
            pc = min(256, past)
            for src, dst_scr in ((kp_ref, ks_scr), (vp_ref, vs_scr)):
                for c in range(past // pc):
                    dst_scr[c * pc:(c + 1) * pc, :] = src[0, :, c * pc:(c + 1) * pc].T.astype(BF16)

    qm = []
    for ps in pairs:
        q = _headnorm(q_ref[0, :, ps], gq_ref[...]) * (1.0 / math.sqrt(SB_HEAD_DIM))
        qm.append(jnp.concatenate([jnp.where(lo_half, q, 0.0), jnp.where(lo_half, 0.0, q)], axis=0).astype(BF16))
    nrow = 2 * npair * tq

    def neg_cum(nk):
        rj = lax.broadcasted_iota(I32, (2 * nk, LANES + nk), 0)
        cj = lax.broadcasted_iota(I32, (2 * nk, LANES + nk), 1)
        rjm = jnp.where(rj >= nk, rj - nk, rj)
        return jnp.where(cj < LANES, -1.0, jnp.where(rjm >= cj - LANES, -1.0, 0.0)).astype(BF16)

    @pl.when(jnp.logical_and(pl.program_id(0) == 0, i == 0))
    def _():
        wdiag_scr[...] = neg_cum(tq)
        wsub_scr[...] = neg_cum(LANES)
        key = lax.broadcasted_iota(I32, (nrow, tq), 1)
        qry = jnp.bitwise_and(lax.broadcasted_iota(I32, (nrow, tq), 0), tq - 1)
        causal_scr[...] = jnp.where(key < qry, 1.0, 0.0)

    w_diag = wdiag_scr[...]
    w_sub = wsub_scr[...]

    def sweep(row0, nk, diag):
        kv = [(ks_scr[pl.ds(row0, nk), ps], vs_scr[pl.ds(row0, nk), ps]) for ps in pairs]
        z = jnp.concatenate([_dot_nt(qm[hp], kv[hp][0]) for hp in range(npair)], axis=0)
        sp = jnp.maximum(z, 0.0) + jnp.log(1.0 + jnp.exp(-jnp.abs(z)))
        if diag:
            sp = sp * causal_scr[...]
        hi = sp.astype(BF16)
        lo = (sp - hi.astype(F32)).astype(BF16)
        sub = min(nk, LANES)
        w_neg = w_diag if sub == tq else w_sub
        run = None if diag else run_scr[...]
        args = [None] * (nk // sub)
        for j in reversed(range(nk // sub)):
            cols = slice(j * sub, (j + 1) * sub)
            t = _dot(jnp.concatenate([hi[:, cols], lo[:, cols]], axis=1), w_neg)
            args[j] = z[:, cols] + t[:, LANES:] if run is None else z[:, cols] + t[:, LANES:] + run[:, :sub]
            run = t[:, :LANES] if run is None else run + t[:, :LANES]
        run_scr[...] = run
        p = jnp.exp(args[0] if len(args) == 1 else jnp.concatenate(args, axis=1))
        if diag:
            p = jnp.where(causal_scr[...] > 0.0, p, 0.0)
        p = p.astype(BF16)
        for hp in range(npair):
            r0 = 2 * hp * tq
            vblk = kv[hp][1]
            vm = jnp.concatenate([jnp.where(lo_half, vblk, 0.0), jnp.where(lo_half, 0.0, vblk)], axis=0)
            pv = _dot(jnp.concatenate([p[r0:r0 + tq], p[r0 + tq:r0 + 2 * tq]], axis=1), vm.astype(BF16))
            acc_scr[hp] = pv if diag else acc_scr[hp] + pv
        return jnp.max(run)

    q0 = past + i * tq
    worst0 = sweep(pl.multiple_of(q0, tq), tq, True)
    n_before = q0 // kb

    def cond(c):
        return jnp.logical_and(c[0] < n_before, c[1] > SB_UNDERFLOW)

    def body(c):
        row0 = q0 - (c[0] + 1) * kb
        return c[0] + 1, sweep(pl.multiple_of(row0, tq), kb, False)

    _, worst = lax.while_loop(cond, body, (jnp.int32(0), worst0))
    if tq < kb and seq > tq:
        @pl.when(jnp.logical_and(q0 - n_before * kb > 0, worst > SB_UNDERFLOW))
        def _():
            sweep(0, tq, False)
    for hp, ps in enumerate(pairs):
        o_ref[0, :, ps] = acc_scr[hp]


def _sb_attention(q_raw, k_norm, v, k_past, v_past, g_q):
    b, s, w = q_raw.shape
    past = 0 if k_past is None else k_past.shape[2]
    tq = min(SB_BLK, s)
    kb = SB_SWEEP
    assert s % tq == 0 and past % kb == 0 and w % LANES == 0 and kb % LANES == 0 and (kb == 2 * tq or s == tq)
    npair = w // LANES
    qblk = pl.BlockSpec((1, tq, w), lambda bi, i: (bi, i, 0))
    full = lambda n: pl.BlockSpec((1, n, w), lambda bi, i: (bi, 0, 0))
    gspec = pl.BlockSpec((1, LANES), lambda bi, i: (0, 0))
    in_specs = [qblk, full(s), full(s)]
    args = [q_raw, k_norm, v]
    if past:
        cache = pl.BlockSpec((1, w, past), lambda bi, i: (bi, 0, 0))
        in_specs += [cache, cache]
        args += [k_past, v_past]
    in_specs += [gspec]
    args += [g_q]
    return pl.pallas_call(
        functools.partial(_sb_kernel, past=past, seq=s, tq=tq, kb=kb, npair=npair),
        grid=(b, s // tq),
        in_specs=in_specs,
        out_specs=qblk,
        out_shape=jax.ShapeDtypeStruct((b, s, w), F32),
        scratch_shapes=[pltpu.VMEM((past + s, w), BF16), pltpu.VMEM((past + s, w), BF16),
                        pltpu.VMEM((2 * npair * tq, LANES), F32), pltpu.VMEM((npair, tq, LANES), F32),
                        pltpu.VMEM((2 * tq, LANES + tq), BF16), pltpu.VMEM((2 * LANES, 2 * LANES), BF16),
                        pltpu.VMEM((2 * npair * tq, tq), F32)],
        compiler_params=_cparams(("arbitrary", "arbitrary")),
        name="sb_attention",
    )(*args)


def _mlstm_kernel(qk_ref, v_ref, og_ref, g_ref, cp_ref, c0_ref, n0_ref, m0_ref, wc_ref, bc_ref, gh_ref,
                  out_ref, cout_ref, nout_ref, mout_ref, convout_ref,
                  prev_scr, c_scr, n_scr, m_scr, *, chunk):
    ci = pl.program_id(1)
    nh, dh = ML_HEADS, ML_HEAD_DIM
    width = nh * dh
    L = chunk

    @pl.when(ci == 0)
    def _():
        prev_scr[0:8, :] = jnp.zeros((8, prev_scr.shape[1]), F32)
        prev_scr[8 - (CONV_W - 1):8, :] = cp_ref[0]
        c_scr[...] = c0_ref[0]
        n_scr[...] = n0_ref[0]
        m_scr[...] = m0_ref[0]

    prev_scr[8:8 + L, :] = qk_ref[0]
    wc = wc_ref[...]
    acc = bc_ref[...] + prev_scr[8 - (CONV_W - 1):8 - (CONV_W - 1) + L, :] * wc[0:1]
    for j in range(1, CONV_W):
        o = 8 - (CONV_W - 1) + j
        acc = acc + prev_scr[o:o + L, :] * wc[j:j + 1]
    qk = acc * jax.nn.sigmoid(acc)
    prev_scr[0:8, :] = prev_scr[L:L + 8, :]

    gt = g_ref[0]
    lf = _log_sigmoid(gt)
    gt_t = gt.T
    lf_t = lf.T
    row = lax.broadcasted_iota(I32, (L, L), 0)
    col = lax.broadcasted_iota(I32, (L, L), 1)
    tril = jnp.where(row >= col, 1.0, 0.0).astype(BF16)
    triu = jnp.where(row <= col, 1.0, 0.0).astype(BF16)
    p1, p2, p3 = _split3(lf)
    b_col = _dot(tril, p1) + _dot(tril, p2) + _dot(tril, p3)
    q1, q2, q3 = _split3(lf_t[0:8, :])
    b_row = _dot(q1, triu) + _dot(q2, triu) + _dot(q3, triu)

    lane = lax.broadcasted_iota(I32, (1, LANES), 1)
    m_vec = m_scr[...]
    m_out = m_vec
    for h in range(nh):
        hs = slice(h * dh, (h + 1) * dh)
        qb = qk[:, h * dh:(h + 1) * dh].astype(BF16)
        kh = qk[:, width + h * dh:width + (h + 1) * dh] * (dh ** -0.5)
        kb = kh.astype(BF16)
        v_t = v_ref[0, :, hs].T
        b_r = b_row[nh + h:nh + h + 1, :]
        ig_r = gt_t[h:h + 1, :]
        c_col = gt[:, h:h + 1] - b_col[:, nh + h:nh + h + 1]
        m_prev = m_vec[:, h:h + 1]

        d_t = jnp.where(row <= col, b_r + c_col, -jnp.inf)
        inter = b_r + m_prev
        m_t = jnp.maximum(inter, jnp.max(d_t, axis=0, keepdims=True))
        w_inter = jnp.exp(inter - m_t)
        s_t = _dot_nt(kb, qb) * jnp.exp(d_t - m_t)
        c_h = c_scr[h]
        n_h = n_scr[h:h + 1, :]
        n_hi = n_h.astype(BF16)
        n_lo = (n_h - n_hi.astype(F32)).astype(BF16)
        num_t = _dot(v_t.astype(BF16), s_t.astype(BF16)) + w_inter * _dot_nt(c_h.astype(BF16), qb)
        den = jnp.sum(s_t, axis=0, keepdims=True) + w_inter * (_dot_nt(n_hi, qb) + _dot_nt(n_lo, qb))
        hh_t = num_t / jnp.maximum(jnp.abs(den), jnp.exp(-m_t))

        b_last = b_r[:, L - 1:L]
        g_r = (b_last - b_r) + ig_r
        m_new = jnp.maximum(b_last + m_prev, jnp.max(g_r, axis=1, keepdims=True))
        w_state = jnp.exp(g_r - m_new)
        decay = jnp.exp(b_last + m_prev - m_new)
        c_scr[h] = decay * c_h + _dot((v_t * w_state).astype(BF16), kb)
        w_hi = w_state.astype(BF16)
        w_lo = (w_state - w_hi.astype(F32)).astype(BF16)
        n_scr[h:h + 1, :] = decay * n_h + (_dot(w_hi, kb) + _dot(w_lo, kb))
        m_out = jnp.where(lane == h, m_new, m_out)

        hn_t = hh_t * lax.rsqrt(jnp.mean(hh_t * hh_t, axis=0, keepdims=True) + EPS) * gh_ref[...]
        out_ref[0, :, hs] = jax.nn.sigmoid(og_ref[0, :, hs]) * hn_t.T
    m_scr[...] = m_out

    @pl.when(ci == pl.num_programs(1) - 1)
    def _():
        cout_ref[0] = c_scr[...]
        nout_ref[0] = n_scr[...]
        mout_ref[0] = m_scr[...]
        convout_ref[0] = prev_scr[8 - (CONV_W - 1):8, :]


def _mlstm(ml_qk, ml_v, ml_o, gates, conv_past, c0, n0, m0, w_conv, b_conv, g_head):
    b, s, w2 = ml_qk.shape
    width = w2 // 2
    nh, dh = ML_HEADS, ML_HEAD_DIM
    chunk = min(ML_CHUNK, s)
    assert s % chunk == 0 and chunk >= 8
    tok = lambda w: pl.BlockSpec((1, chunk, w), lambda bi, ci: (bi, ci, 0))
    perb = lambda shp: pl.BlockSpec((1,) + shp, lambda bi, ci: (bi,) + (0,) * len(shp))
    const = lambda shp: pl.BlockSpec(shp, lambda bi, ci: (0,) * len(shp))
    return pl.pallas_call(
        functools.partial(_mlstm_kernel, chunk=chunk),
        grid=(b, s // chunk),
        in_specs=[tok(w2), tok(width), tok(width), tok(LANES),
                  perb((CONV_W - 1, w2)), perb((nh, dh, dh)), perb((nh, dh)), perb((1, LANES)),
                  const((CONV_W, w2)), const((1, w2)), const((dh, 1))],
        out_specs=[tok(width), perb((nh, dh, dh)), perb((nh, dh)), perb((1, LANES)), perb((CONV_W - 1, w2))],
        out_shape=[jax.ShapeDtypeStruct((b, s, width), F32),
                   jax.ShapeDtypeStruct((b, nh, dh, dh), F32),
                   jax.ShapeDtypeStruct((b, nh, dh), F32),
                   jax.ShapeDtypeStruct((b, 1, LANES), F32),
                   jax.ShapeDtypeStruct((b, CONV_W - 1, w2), F32)],
        scratch_shapes=[pltpu.VMEM((8 + chunk, w2), F32), pltpu.VMEM((nh, dh, dh), F32),
                        pltpu.VMEM((nh, dh), F32), pltpu.VMEM((1, LANES), F32)],
        compiler_params=_cparams(("arbitrary", "arbitrary")),
        name="mlstm",
    )(ml_qk, ml_v, ml_o, gates, conv_past, c0, n0, m0, w_conv, b_conv, g_head)


def _outproj_kernel(osb_ref, oml_ref, x_ref, mod_ref, w1_ref, w2_ref, g2_ref, wrh_ref, wrl_ref, br_ref,
                    x1_ref, h2_ref, ti_ref, tg_ref, *, nb):
    tm = x_ref.shape[0]
    per_token = lambda j: _mod_rows(mod_ref, j, nb, x_ref.shape)

    mix = _dot(osb_ref[...].astype(BF16), w1_ref[...]) + _dot(oml_ref[...].astype(BF16), w2_ref[...])
    x1 = x_ref[...] + per_token(2) * mix
    x1_ref[...] = x1
    ms = jnp.mean(x1 * x1, axis=-1, keepdims=True)
    h2 = x1 * lax.rsqrt(ms + EPS) * g2_ref[...]
    h2 = h2 * (1.0 + per_token(4)) + per_token(3)
    _store_row_tiles(h2_ref, h2)

    hb = h2.astype(BF16)
    hl = (h2 - hb.astype(F32)).astype(BF16)
    logits = (_dot_nt(wrh_ref[...], hb) + _dot_nt(wrh_ref[...], hl) + _dot_nt(wrl_ref[...], hb))[0:N_EXPERTS, :]
    cur = logits + br_ref[...]
    erow = lax.broadcasted_iota(I32, cur.shape, 0)
    vals, idxs = [], []
    for _ in range(TOP_K):
        m = jnp.max(cur, axis=0, keepdims=True)
        idx = jnp.min(jnp.where(cur == m, erow, N_EXPERTS), axis=0, keepdims=True)
        vals.append(m)
        idxs.append(idx)
        cur = jnp.where(erow == idx, -jnp.inf, cur)
    es = [jnp.exp(v - vals[0]) for v in vals]
    tot = es[0] + es[1] + es[2] + es[3]
    krow = lax.broadcasted_iota(I32, (SUBLANES, tm), 0)
    ti = jnp.zeros((SUBLANES, tm), I32)
    tg = jnp.zeros((SUBLANES, tm), F32)
    for k in range(TOP_K):
        ti = jnp.where(krow == k, idxs[k], ti)
        tg = jnp.where(krow == k, es[k] / tot, tg)
    ti_ref[...] = ti
    tg_ref[...] = tg


def _outproj(o_sb, o_ml, x, mod, tokens_per_batch, w1, w2, g2, wrt_hi, wrt_lo, brt):
    t, d = x.shape
    hw = o_sb.shape[1]
    tm = ROW_TILE
    assert t % tm == 0
    nb, mod_spec = _mod_spec(tokens_per_batch, tm, d)
    tok = lambda w: pl.BlockSpec((tm, w), lambda i: (i, 0))
    const = lambda shp: pl.BlockSpec(shp, lambda i: (0,) * len(shp))
    rows8 = pl.BlockSpec((SUBLANES, tm), lambda i: (0, i))
    return pl.pallas_call(
        functools.partial(_outproj_kernel, nb=nb),
        grid=(t // tm,),
        in_specs=[tok(hw), tok(hw), tok(d), mod_spec,
                  const((hw, d)), const((hw, d)), const((1, d)),
                  const((LANES, d)), const((LANES, d)), const((N_EXPERTS, 1))],
        out_specs=[tok(d), pl.BlockSpec((tm * SUBLANES, LANES), lambda i: (i, 0)), rows8, rows8],
        out_shape=[jax.ShapeDtypeStruct((t, d), F32), jax.ShapeDtypeStruct((t * SUBLANES, LANES), F32),
                   jax.ShapeDtypeStruct((SUBLANES, t), I32), jax.ShapeDtypeStruct((SUBLANES, t), F32)],
        compiler_params=_cparams(("arbitrary",)),
        name="outproj_router",
    )(o_sb, o_ml, x, mod, w1, w2, g2, wrt_hi, wrt_lo, brt)


def _rank_kernel(ti_ref, pos_ref, cnt_ref, carry_scr, start_scr, earlier_scr):
    ph = pl.program_id(0)
    i = pl.program_id(1)
    tm = ti_ref.shape[1]
    rep = lambda a: jnp.concatenate([a] * (tm // LANES), axis=1)

    @pl.when(i == 0)
    def _():
        carry_scr[...] = jnp.zeros_like(carry_scr)

    @pl.when(jnp.logical_and(ph == 1, i == 0))
    def _():
        earlier = lax.broadcasted_iota(I32, (tm, tm), 0) < lax.broadcasted_iota(I32, (tm, tm), 1)
        earlier_scr[...] = jnp.where(earlier, 1.0, 0.0).astype(BF16)

    ti = ti_ref[...]
    erow = lax.broadcasted_iota(I32, (N_EXPERTS, tm), 0)
    sel = [erow == ti[k:k + 1, :] for k in range(TOP_K)]
    hit = jnp.zeros((N_EXPERTS, tm), F32)
    for k in range(TOP_K):
        hit = hit + jnp.where(sel[k], 1.0, 0.0)
    hit = hit.astype(BF16)
    seen = carry_scr[...]

    @pl.when(ph == 1)
    def _():
        row_of = _dot(hit, earlier_scr[...]) + rep(seen) + rep(start_scr[...])
        krow = lax.broadcasted_iota(I32, (SUBLANES, tm), 0)
        pos = jnp.zeros((SUBLANES, tm), F32)
        for k in range(TOP_K):
            pos = jnp.where(krow == k, jnp.sum(jnp.where(sel[k], row_of, 0.0), axis=0, keepdims=True), pos)
        pos_ref[...] = pos.astype(I32)

    carry_scr[...] = seen + _dot(hit, jnp.ones((tm, LANES), BF16))

    @pl.when(jnp.logical_and(ph == 0, i == pl.num_programs(1) - 1))
    def _():
        cnt = carry_scr[...]
        cnt_ref[...] = cnt.astype(I32)
        padded = jnp.floor((cnt + (MOE_BLK - 1)) * (1.0 / MOE_BLK)) * MOE_BLK
        lower = (lax.broadcasted_iota(I32, (N_EXPERTS, N_EXPERTS), 1)
                 < lax.broadcasted_iota(I32, (N_EXPERTS, N_EXPERTS), 0))
        lower = jnp.where(lower, 1.0, 0.0).astype(BF16)
        p1, p2, p3 = _split3(padded)
        start_scr[...] = _dot(lower, p1) + _dot(lower, p2) + _dot(lower, p3)


def _positions(top_idx_t):
    t = top_idx_t.shape[1]
    tm = ROW_TILE
    assert t % tm == 0
    return pl.pallas_call(
        _rank_kernel,
        grid=(2, t // tm),
        in_specs=[pl.BlockSpec((SUBLANES, tm), lambda ph, i: (0, i))],
        out_specs=[pl.BlockSpec((SUBLANES, tm), lambda ph, i: (0, i * ph)),
                   pl.BlockSpec((N_EXPERTS, LANES), lambda ph, i: (0, 0))],
        out_shape=[jax.ShapeDtypeStruct((SUBLANES, t), I32), jax.ShapeDtypeStruct((N_EXPERTS, LANES), I32)],
        scratch_shapes=[pltpu.VMEM((N_EXPERTS, LANES), F32), pltpu.VMEM((N_EXPERTS, LANES), F32),
                        pltpu.VMEM((tm, tm), BF16)],
        compiler_params=_cparams(("arbitrary", "arbitrary")),
        name="expert_positions",
    )(top_idx_t)


def _row_tile(ref, r):
    return ref.at[pl.ds(pl.multiple_of(r * SUBLANES, SUBLANES), SUBLANES), :]


def _dispatch_kernel(*refs, tm):
    pos_hbm, h_ref = refs[:2]
    xs_hbm, pos_smem0, pos_smem1, stage, pos_sem, row_sem = refs[-6:]
    i = pl.program_id(0)
    n = pl.num_programs(0)
    slot = lax.rem(i, 2)
    nidx = tm * TOP_K

    def pos_copy(tile, sl):
        return pltpu.make_async_copy(pos_hbm.at[pl.ds(pl.multiple_of(tile * nidx, nidx), nidx)],
                                     (pos_smem0, pos_smem1)[sl], pos_sem.at[sl])

    def drain(sl):
        for _ in range(TOP_K):
            pltpu.make_async_copy(stage.at[sl], stage.at[sl], row_sem.at[sl]).wait()

    @pl.when(i == 0)
    def _():
        pos_copy(0, 0).start()

    for sl in range(2):
        @pl.when(jnp.logical_and(i + 1 < n, slot == 1 - sl))
        def _():
            pos_copy(i + 1, sl).start()

    @pl.when(i >= 2)
    def _():
        drain(slot)

    stage[slot] = h_ref[...]

    for sl in range(2):
        @pl.when(slot == sl)
        def _():
            pos_copy(i, sl).wait()
            pos_smem = (pos_smem0, pos_smem1)[sl]

            def issue(c, carry):
                for rr in range(ISSUE_UNROLL):
                    r = c * ISSUE_UNROLL + rr
                    for k in range(TOP_K):
                        pltpu.make_async_copy(_row_tile(stage.at[sl], r), _row_tile(xs_hbm, pos_smem[k * tm + r]),
                                              row_sem.at[sl]).start(priority=k % DMA_QUEUES)
                return carry
            lax.fori_loop(0, tm // ISSUE_UNROLL, issue, 0)

    @pl.when(i == n - 1)
    def _():
        drain(slot)

        @pl.when(n >= 2)
        def _():
            drain(1 - slot)


def _dispatch(pos_flat, h_tiles, xs=None, n_rows=None):
    t = h_tiles.shape[0] // SUBLANES
    tm = GATHER_TILE
    assert t % tm == 0 and tm % ISSUE_UNROLL == 0
    in_specs = [pl.BlockSpec(memory_space=pl.ANY), pl.BlockSpec((tm * SUBLANES, LANES), lambda i: (i, 0))]
    args = [pos_flat, h_tiles]
    if xs is not None:
        in_specs.append(pl.BlockSpec(memory_space=pl.ANY))
        args.append(xs)
    return pl.pallas_call(
        functools.partial(_dispatch_kernel, tm=tm),
        grid=(t // tm,),
        in_specs=in_specs,
        out_specs=pl.BlockSpec(memory_space=pl.ANY),
        out_shape=jax.ShapeDtypeStruct((n_rows * SUBLANES, LANES) if xs is None else xs.shape, F32),
        scratch_shapes=[pltpu.SMEM((tm * TOP_K,), I32), pltpu.SMEM((tm * TOP_K,), I32),
                        pltpu.VMEM((2, tm * SUBLANES, LANES), F32),
                        pltpu.SemaphoreType.DMA((2,)), pltpu.SemaphoreType.DMA((2,))],
        input_output_aliases={} if xs is None else {2: 0},
        compiler_params=_cparams(("arbitrary",)),
        name="moe_dispatch",
    )(*args)


def _pad_fill_kernel(pos_hbm, xs_in_hbm, xs_hbm, pos_smem, zero_tile, pos_sem, row_sem, *, nidx):
    del xs_in_hbm
    i = pl.program_id(0)
    cp = pltpu.make_async_copy(pos_hbm.at[pl.ds(pl.multiple_of(i * nidx, nidx), nidx)], pos_smem, pos_sem)
    cp.start()
    zero_tile[...] = jnp.zeros_like(zero_tile)
    cp.wait()

    def issue(c, carry):
        for rr in range(TOP_K * ISSUE_UNROLL):
            r = c * (TOP_K * ISSUE_UNROLL) + rr
            pltpu.make_async_copy(zero_tile, _row_tile(xs_hbm, pos_smem[r]), row_sem).start(
                priority=rr % DMA_QUEUES)
        return carry
    lax.fori_loop(0, nidx // (TOP_K * ISSUE_UNROLL), issue, 0)
    whole = xs_hbm.at[pl.ds(0, nidx * SUBLANES), :]
    pltpu.make_async_copy(whole, whole, row_sem).wait()


def _pad_fill(pad_pos, xs):
    nidx = GATHER_TILE * TOP_K
    assert pad_pos.shape[0] % nidx == 0
    return pl.pallas_call(
        functools.partial(_pad_fill_kernel, nidx=nidx),
        grid=(pad_pos.shape[0] // nidx,),
        in_specs=[pl.BlockSpec(memory_space=pl.ANY), pl.BlockSpec(memory_space=pl.ANY)],
        out_specs=pl.BlockSpec(memory_space=pl.ANY),
        out_shape=jax.ShapeDtypeStruct(xs.shape, xs.dtype),
        scratch_shapes=[pltpu.SMEM((nidx,), I32), pltpu.VMEM((SUBLANES, LANES), F32),
                        pltpu.SemaphoreType.DMA, pltpu.SemaphoreType.DMA],
        input_output_aliases={1: 0},
        compiler_params=_cparams(("arbitrary",)),
        name="moe_pad_fill",
    )(pad_pos, xs)


def _moe_kernel(be_ref, nu_ref, xs_ref, wgu_ref, bgu_ref, wd_ref, bd_ref, ys_ref, wgu_bf, wd_bf):
    i = pl.program_id(0)
    d, ff = wgu_ref.shape[1], wd_ref.shape[1]

    @pl.when(jnp.logical_and(i < nu_ref[0],
                             jnp.logical_or(i == 0, be_ref[i] != be_ref[jnp.maximum(i - 1, 0)])))
    def _():
        for r in range(0, d, LANES):
            wgu_bf[r:r + LANES, :] = wgu_ref[0, r:r + LANES, :].astype(BF16)
        for r in range(0, ff, LANES):
            wd_bf[r:r + LANES, :] = wd_ref[0, r:r + LANES, :].astype(BF16)

    @pl.when(i < nu_ref[0])
    def _():
        x = _load_row_tiles(xs_ref, MOE_BLK).astype(BF16)
        glu = _dot(x, wgu_bf[:, 0:ff]) + bgu_ref[0, :, 0:ff]
        lin = _dot(x, wgu_bf[:, ff:2 * ff]) + bgu_ref[0, :, ff:2 * ff]
        glu = jnp.minimum(glu, SWIGLU_LIMIT)
        lin = jnp.clip(lin, -SWIGLU_LIMIT, SWIGLU_LIMIT)
        act = glu * jax.nn.sigmoid(SWIGLU_ALPHA * glu) * (lin + 1.0)
        _store_row_tiles(ys_ref, _dot(act.astype(BF16), wd_bf[...]) + bd_ref[0])


def _moe_mlp(block_expert, n_used, xs, w_gu, b_gu, w_down, b_down):
    ne, d, ff2 = w_gu.shape
    ff = ff2 // 2
    tb = MOE_BLK
    n_rows = block_expert.shape[0] * tb
    assert xs.shape[0] >= n_rows * SUBLANES
    blk = lambda i, be, nu: jnp.minimum(i, nu[0] - 1)
    grid_spec = pltpu.PrefetchScalarGridSpec(
        num_scalar_prefetch=2,
        grid=(n_rows // tb,),
        in_specs=[pl.BlockSpec((tb * SUBLANES, LANES), lambda i, be, nu: (blk(i, be, nu), 0)),
                  pl.BlockSpec((1, d, ff2), lambda i, be, nu: (be[blk(i, be, nu)], 0, 0)),
                  pl.BlockSpec((1, 1, ff2), lambda i, be, nu: (be[blk(i, be, nu)], 0, 0)),
                  pl.BlockSpec((1, ff, d), lambda i, be, nu: (be[blk(i, be, nu)], 0, 0)),
                  pl.BlockSpec((1, 1, d), lambda i, be, nu: (be[blk(i, be, nu)], 0, 0))],
        out_specs=pl.BlockSpec((tb * SUBLANES, LANES), lambda i, be, nu: (blk(i, be, nu), 0)),
        scratch_shapes=[pltpu.VMEM((d, ff2), BF16), pltpu.VMEM((ff, d), BF16)],
    )
    return pl.pallas_call(
        _moe_kernel,
        grid_spec=grid_spec,
        out_shape=jax.ShapeDtypeStruct((n_rows * SUBLANES, LANES), F32),
        compiler_params=_cparams(("arbitrary",)),
        name="moe_mlp",
    )(block_expert, n_used, xs, w_gu, b_gu.reshape(ne, 1, ff2), w_down, b_down.reshape(ne, 1, d))


def _combine_kernel(pos_hbm, ys_hbm, x1_ref, tg_ref, mod_ref, y_ref, pos_smem0, pos_smem1, buf, pos_sem, row_sem,
                    *, tm, nb):
    i = pl.program_id(0)
    n = pl.num_programs(0)
    slot = lax.rem(i, 2)
    nidx = tm * TOP_K

    def pos_copy(tile, sl):
        return pltpu.make_async_copy(pos_hbm.at[pl.ds(pl.multiple_of(tile * nidx, nidx), nidx)],
                                     (pos_smem0, pos_smem1)[sl], pos_sem.at[sl])

    def gather(tile, sl):
        pos_copy(tile, sl).wait()
        pos_smem = (pos_smem0, pos_smem1)[sl]

        def issue(c, carry):
            for rr in range(ISSUE_UNROLL):
                r = c * ISSUE_UNROLL + rr
                for k in range(TOP_K):
                    pltpu.make_async_copy(_row_tile(ys_hbm, pos_smem[k * tm + r]),
                                          _row_tile(buf.at[sl, k], r), row_sem.at[sl]).start(
                                              priority=k % DMA_QUEUES)
            return carry
        lax.fori_loop(0, tm // ISSUE_UNROLL, issue, 0)

        @pl.when(tile + 1 < n)
        def _():
            pos_copy(tile + 1, 1 - sl).start()

    @pl.when(i == 0)
    def _():
        pos_copy(0, 0).start()
        gather(0, 0)

    for sl in range(2):
        @pl.when(jnp.logical_and(i + 1 < n, slot == 1 - sl))
        def _():
            gather(i + 1, sl)

    for k in range(TOP_K):
        pltpu.make_async_copy(buf.at[slot, k], buf.at[slot, k], row_sem.at[slot]).wait()

    tg = tg_ref[...].T
    rows = tm // nb
    for c in range(SUBLANES):
        cols = slice(c * LANES, (c + 1) * LANES)
        ff = tg[:, 0:1] * buf[slot, 0, pl.ds(c, tm, stride=SUBLANES), :]
        for k in range(1, TOP_K):
            ff = ff + tg[:, k:k + 1] * buf[slot, k, pl.ds(c, tm, stride=SUBLANES), :]
        for g in range(nb):
            rs = slice(g * rows, (g + 1) * rows)
            y_ref[rs, cols] = x1_ref[rs, cols] + mod_ref[g, 5:6, cols] * ff[rs, :]


def _combine(pos_flat, ys, x1, top_gate, mod, tokens_per_batch):
    t, d = x1.shape
    tm = GATHER_TILE
    assert t % tm == 0 and tm % ISSUE_UNROLL == 0 and d == SUBLANES * LANES
    nb, mod_spec = _mod_spec(tokens_per_batch, tm, d)
    return pl.pallas_call(
        functools.partial(_combine_kernel, tm=tm, nb=nb),
        grid=(t // tm,),
        in_specs=[pl.BlockSpec(memory_space=pl.ANY), pl.BlockSpec(memory_space=pl.ANY),
                  pl.BlockSpec((tm, d), lambda i: (i, 0)), pl.BlockSpec((SUBLANES, tm), lambda i: (0, i)), mod_spec],
        out_specs=pl.BlockSpec((tm, d), lambda i: (i, 0)),
        out_shape=jax.ShapeDtypeStruct((t, d), F32),
        scratch_shapes=[pltpu.SMEM((tm * TOP_K,), I32), pltpu.SMEM((tm * TOP_K,), I32),
                        pltpu.VMEM((2, TOP_K, tm * SUBLANES, LANES), F32),
                        pltpu.SemaphoreType.DMA((2,)), pltpu.SemaphoreType.DMA((2,))],
        compiler_params=_cparams(("arbitrary",)),
        name="moe_combine",
    )(pos_flat, ys, x1, top_gate, mod)


def _pad_lanes(a, fill=0.0):
    return jnp.pad(a, [(0, 0)] * (a.ndim - 1) + [(0, LANES - a.shape[-1])], constant_values=fill)


def _hi_lo(w):
    hi = w.astype(BF16)
    return hi, (w - hi.astype(F32)).astype(BF16)


def _mixer(x, mod, k_past, v_past, conv_past, c0, n0, m0, p):
    b, s, d = x.shape
    outs = _inproj(x.reshape(b * s, d), mod, s, p["g1"], p["w_main"], p["wg_hi"], p["wg_lo"], p["bg"], p["g_k"],
                   p["widths"])
    q_raw, k_norm, v, ml_qk, ml_v, ml_o, gates = [a.reshape(b, s, a.shape[-1]) for a in outs[:7]]
    k_tiles, v_tiles = outs[7:]
    o_sb = _sb_attention(q_raw, k_norm, v, k_past, v_past, p["g_q"])
    o_ml, c_new, n_new, m_new, conv_new = _mlstm(
        ml_qk, ml_v, ml_o, gates, conv_past, c0, n0, _pad_lanes(m0)[:, None, :],
        p["w_conv"], p["b_conv"], p["g_head"])
    flat = lambda a: a.reshape(b * s, a.shape[-1])
    x1, h2, ti, tg = _outproj(flat(o_sb), flat(o_ml), flat(x), mod, s, p["w_out1"], p["w_out2"], p["g2"],
                              p["wrt_hi"], p["wrt_lo"], p["brt"])
    assert v.shape[-1] == SUBLANES * SB_HEAD_DIM
    state = (k_tiles.reshape(b, s, SUBLANES, SB_HEAD_DIM), v_tiles.reshape(b, s, SUBLANES, SB_HEAD_DIM), conv_new,
             c_new, n_new, m_new[:, 0, :ML_HEADS])
    return x1, h2, ti, tg, state


def _layer(xp, xs_, cp, cs, kc, vc, convc, cc, nc, mc, w_ada, b_ada, g_norm1, w_in, g_q, g_k, w_conv, b_conv,
           b_gate, g_head, w_out, g_norm2, w_router, b_router, w_gu, b_gu, w_down, b_down):
    d = xp.shape[-1]
    bp, sp = xp.shape[:2]
    bs, ss = xs_.shape[:2]
    sbw = kc.shape[-2] * kc.shape[-1]
    mlw = ML_HEADS * ML_HEAD_DIM
    ngate = 2 * ML_HEADS
    gate0 = 3 * sbw + 3 * mlw
    w_main = jnp.concatenate([w_in[:, :gate0], w_in[:, gate0 + ngate:]], axis=1).astype(BF16)
    wg_hi, wg_lo = _hi_lo(_pad_lanes(w_in[:, gate0:gate0 + ngate]))
    wrt_hi, wrt_lo = _hi_lo(_pad_lanes(w_router).T)
    p = dict(
        widths=(sbw, sbw, sbw, 2 * mlw, mlw, mlw),
        g1=g_norm1[None], w_main=w_main, wg_hi=wg_hi, wg_lo=wg_lo, bg=_pad_lanes(b_gate)[None],
        g_q=jnp.tile(g_q, LANES // SB_HEAD_DIM)[None], g_k=jnp.tile(g_k, LANES // SB_HEAD_DIM)[None],
        w_conv=w_conv, b_conv=b_conv[None], g_head=g_head[:, None],
        w_out1=w_out[:sbw].astype(BF16), w_out2=w_out[sbw:].astype(BF16), g2=g_norm2[None],
        wrt_hi=wrt_hi, wrt_lo=wrt_lo, brt=b_router[:, None],
    )
    mod = _ada(jnp.concatenate([cp, cs], axis=0), w_ada, b_ada)
    mod_p = mod[:bp].reshape(bp, 6, d)
    mod_s = mod[bp:].reshape(bs, 6, d)

    zeros = lambda *shp: jnp.zeros(shp, F32)
    feature_major = lambda a: jnp.transpose(a, (0, 2, 3, 1)).reshape(a.shape[0], sbw, a.shape[1])
    x1p, h2p, tip, tgp, st_p = _mixer(
        xp, mod_p, None, None, zeros(bp, CONV_W - 1, 2 * mlw), zeros(bp, ML_HEADS, ML_HEAD_DIM, ML_HEAD_DIM),
        zeros(bp, ML_HEADS, ML_HEAD_DIM), zeros(bp, ML_HEADS), p)
    x1s, h2s, tis, tgs, st_s = _mixer(
        xs_, mod_s, feature_major(kc), feature_major(vc), convc, cc, nc, mc, p)

    tp, ts = bp * sp, bs * ss
    pos, counts = _positions(jnp.concatenate([tip, tis], axis=1))
    counts = counts[:, 0]
    padded = (counts + MOE_BLK - 1) // MOE_BLK * MOE_BLK
    pad_end = jnp.cumsum(padded)
    pad_start = pad_end - padded
    n_blocks = -(-((tp + ts) * TOP_K + N_EXPERTS * (MOE_BLK - 1)) // MOE_BLK)
    n_used = (pad_end[-1] // MOE_BLK).astype(I32).reshape(1)
    block_start = jnp.arange(n_blocks, dtype=I32) * MOE_BLK
    block_expert = jnp.minimum(jnp.sum(block_start[:, None] >= pad_end[None, :], axis=1), N_EXPERTS - 1).astype(I32)

    def per_tile(rows):
        t = rows.shape[1]
        return rows.reshape(TOP_K, t // GATHER_TILE, GATHER_TILE).transpose(1, 0, 2).reshape(-1)
    pos_p = per_tile(pos[:TOP_K, :tp])
    pos_s = per_tile(pos[:TOP_K, tp:])

    n_rows = n_blocks * MOE_BLK
    slot = jnp.arange(MOE_BLK, dtype=I32)[None, :]
    spare = n_rows + jnp.arange(N_EXPERTS * MOE_BLK, dtype=I32).reshape(N_EXPERTS, MOE_BLK)
    pad_pos = jnp.where(slot < (padded - counts)[:, None], (pad_start + counts)[:, None] + slot, spare).reshape(-1)

    xs_buf = _dispatch(pos_p, h2p, n_rows=n_rows + N_EXPERTS * MOE_BLK)
    xs_buf = _dispatch(pos_s, h2s, xs_buf)
    xs_buf = _pad_fill(pad_pos.astype(I32), xs_buf)
    ys = _moe_mlp(block_expert, n_used, xs_buf, w_gu, b_gu, w_down, b_down)
    yp = _combine(pos_p, ys, x1p, tgp, mod_p, sp).reshape(bp, sp, d)
    ysm = _combine(pos_s, ys, x1s, tgs, mod_s, ss).reshape(bs, ss, d)
    return yp, ysm, st_p, st_s


def kernel(x_prompt, x_sample, c_prompt, c_sample, cache_sb_k, cache_sb_v, state_conv, state_mlstm_c, state_mlstm_n, state_mlstm_m, w_ada, b_ada, g_norm1, w_in, g_q, g_k, w_conv, b_conv, b_gate, g_head, w_out, g_norm2, w_router, b_router, w_gu, b_gu, w_down, b_down):
    assert w_ada.shape[0] == 1, "single-layer step"
    yp, ys, st_p, st_s = _layer(
        x_prompt, x_sample, c_prompt, c_sample, cache_sb_k[0], cache_sb_v[0], state_conv[0], state_mlstm_c[0],
        state_mlstm_n[0], state_mlstm_m[0], w_ada[0], b_ada[0], g_norm1[0], w_in[0], g_q[0], g_k[0], w_conv[0],
        b_conv[0], b_gate[0], g_head[0], w_out[0], g_norm2[0], w_router[0], b_router[0], w_gu[0], b_gu[0],
        w_down[0], b_down[0])
    return (yp, ys) + tuple(a[None] for a in st_p) + tuple(a[None] for a in st_s)
```

```python
import functools
import math

import jax
import jax.numpy as jnp
from jax import lax
from jax.experimental import pallas as pl
from jax.experimental.pallas import tpu as pltpu

F32 = jnp.float32
BF16 = jnp.bfloat16
I32 = jnp.int32

EPS = 1e-6
LANES = 128
SUBLANES = 8
SB_HEAD_DIM = 64
ML_HEAD_DIM = 128
ML_HEADS = 4
CONV_W = 4
N_EXPERTS = 32
TOP_K = 4
SWIGLU_LIMIT = 7.0
SWIGLU_ALPHA = 1.702
VMEM_LIMIT = 56 * 1024 * 1024

ROW_TILE = 512
MOE_BLK = 512
GATHER_TILE = 256
ISSUE_UNROLL = 8
DMA_QUEUES = 2
ML_CHUNK = 256
SB_BLK = 128
SB_SWEEP = 256
SB_UNDERFLOW = -105.0


def _cparams(sem):
    return pltpu.CompilerParams(dimension_semantics=sem, vmem_limit_bytes=VMEM_LIMIT)


def _split3(x):
    p1 = x.astype(BF16)
    r1 = x - p1.astype(F32)
    p2 = r1.astype(BF16)
    p3 = (r1 - p2.astype(F32)).astype(BF16)
    return p1, p2, p3


def _dot(a, b):
    return jnp.dot(a, b, preferred_element_type=F32)


def _dot_nt(a, b):
    return lax.dot_general(a, b, (((1,), (1,)), ((), ())), preferred_element_type=F32)


def _dot3(x, w_hi, w_lo):
    xh = x.astype(BF16)
    xl = (x - xh.astype(F32)).astype(BF16)
    return _dot(xh, w_hi) + _dot(xl, w_hi) + _dot(xh, w_lo)


def _store_row_tiles(ref, x):
    n, d = x.shape
    assert d == SUBLANES * LANES
    for c in range(SUBLANES):
        ref[pl.ds(c, n, stride=SUBLANES), :] = x[:, c * LANES:(c + 1) * LANES]


def _load_row_tiles(ref, n):
    return jnp.concatenate([ref[pl.ds(c, n, stride=SUBLANES), :] for c in range(SUBLANES)], axis=1)


def _store_head_tiles(ref, row0, head0, x):
    n = x.shape[0]
    for j in range(x.shape[1] // SB_HEAD_DIM):
        ref[pl.ds(row0 * SUBLANES + head0 + j, n, stride=SUBLANES), :] = x[:, j * SB_HEAD_DIM:(j + 1) * SB_HEAD_DIM]


def _headnorm(x, g):
    lo_half = lax.broadcasted_iota(I32, (1, LANES), 1) < SB_HEAD_DIM
    x2 = x * x
    s0 = jnp.sum(jnp.where(lo_half, x2, 0.0), axis=-1, keepdims=True)
    s1 = jnp.sum(jnp.where(lo_half, 0.0, x2), axis=-1, keepdims=True)
    r = jnp.where(lo_half, lax.rsqrt(s0 * (1.0 / SB_HEAD_DIM) + EPS), lax.rsqrt(s1 * (1.0 / SB_HEAD_DIM) + EPS))
    return x * r * g


def _mod_rows(mod_ref, j, nb, shape):
    if nb == 1:
        return mod_ref[0, j:j + 1, :]
    rows = shape[0] // nb
    return jnp.concatenate([jnp.broadcast_to(mod_ref[g, j:j + 1, :], (rows, shape[1])) for g in range(nb)], axis=0)


def _mod_spec(tokens_per_batch, tm, d):
    if tokens_per_batch >= tm:
        assert tokens_per_batch % tm == 0
        per = tokens_per_batch // tm
        return 1, pl.BlockSpec((1, 6, d), lambda i: (i // per, 0, 0))
    assert tm % tokens_per_batch == 0
    nb = tm // tokens_per_batch
    return nb, pl.BlockSpec((nb, 6, d), lambda i: (i, 0, 0))


def _log_sigmoid(x):
    return jnp.minimum(x, 0.0) - jnp.log1p(jnp.exp(-jnp.abs(x)))


def _ada_kernel(c_ref, w_ref, b_ref, o_ref):
    c = c_ref[...]
    s = c * jax.nn.sigmoid(c)
    o_ref[...] = jnp.dot(s, w_ref[...], precision=lax.Precision.HIGHEST,
                         preferred_element_type=F32) + b_ref[...]


def _ada(c, w_ada, b_ada):
    n, d = c.shape
    cols = w_ada.shape[1]
    tn = cols // 6
    return pl.pallas_call(
        _ada_kernel,
        grid=(cols // tn,),
        in_specs=[pl.BlockSpec((n, d), lambda j: (0, 0)),
                  pl.BlockSpec((d, tn), lambda j: (0, j)),
                  pl.BlockSpec((1, tn), lambda j: (0, j))],
        out_specs=pl.BlockSpec((n, tn), lambda j: (0, j)),
        out_shape=jax.ShapeDtypeStruct((n, cols), F32),
        compiler_params=_cparams(("arbitrary",)),
        name="ada_mod",
    )(c, w_ada, b_ada.reshape(1, cols))


def _inproj_kernel(x_ref, mod_ref, g1_ref, w_ref, wgh_ref, wgl_ref, bg_ref, gk_ref,
                   q_ref, k_ref, v_ref, qk_ref, mv_ref, mo_ref, gate_ref, kt_ref, vt_ref, *, widths, nb):
    x = x_ref[...]
    ms = jnp.mean(x * x, axis=-1, keepdims=True)
    h = x * lax.rsqrt(ms + EPS) * g1_ref[...]
    h = h * (1.0 + _mod_rows(mod_ref, 1, nb, x.shape)) + _mod_rows(mod_ref, 0, nb, x.shape)
    hb = h.astype(BF16)
    off = 0
    for ref, wd in zip((q_ref, k_ref, v_ref, qk_ref, mv_ref, mo_ref), widths):
        u = _dot(hb, w_ref[:, off:off + wd])
        if ref is k_ref:
            u = jnp.concatenate([_headnorm(u[:, c:c + LANES], gk_ref[...]) for c in range(0, wd, LANES)], axis=1)
            _store_head_tiles(kt_ref, 0, 0, u)
        if ref is v_ref:
            _store_head_tiles(vt_ref, 0, 0, u)
        ref[...] = u
        off += wd
    gate_ref[...] = _dot3(h, wgh_ref[...], wgl_ref[...]) + bg_ref[...]


def _inproj(x, mod, tokens_per_batch, g1, w_main, wg_hi, wg_lo, bg, g_k, widths):
    t, d = x.shape
    tm = ROW_TILE
    assert t % tm == 0
    ncol = w_main.shape[1]
    nb, mod_spec = _mod_spec(tokens_per_batch, tm, d)
    tok = lambda w: pl.BlockSpec((tm, w), lambda i: (i, 0))
    const = lambda shp: pl.BlockSpec(shp, lambda i: (0,) * len(shp))
    head_tiles = pl.BlockSpec((tm * SUBLANES, SB_HEAD_DIM), lambda i: (i, 0))
    return pl.pallas_call(
        functools.partial(_inproj_kernel, widths=widths, nb=nb),
        grid=(t // tm,),
        in_specs=[tok(d), mod_spec,
                  const((1, d)), const((d, ncol)), const((d, LANES)), const((d, LANES)), const((1, LANES)),
                  const((1, LANES))],
        out_specs=[tok(w) for w in widths] + [tok(LANES), head_tiles, head_tiles],
        out_shape=[jax.ShapeDtypeStruct((t, w), F32) for w in widths]
        + [jax.ShapeDtypeStruct((t, LANES), F32)]
        + [jax.ShapeDtypeStruct((t * SUBLANES, SB_HEAD_DIM), F32)] * 2,
        compiler_params=_cparams(("arbitrary",)),
        name="inproj",
    )(x, mod, g1, w_main, wg_hi, wg_lo, bg, g_k)


def _sb_kernel(*refs, past, seq, tq, kb, npair):
    ks_scr, vs_scr, run_scr, acc_scr = refs[-4:]
    if past:
        q_ref, k_ref, v_ref, kp_ref, vp_ref, gq_ref, o_ref = refs[:-4]
    else:
        q_ref, k_ref, v_ref, gq_ref, o_ref = refs[:-4]
    i = pl.program_id(1)
    lane = lax.broadcasted_iota(I32, (1, LANES), 1)
    lo_half = lane < SB_HEAD_DIM
    pairs = [slice(hp * LANES, (hp + 1) * LANES) for hp in range(npair)]

    @pl.when(i == 0)
    def _():
        ch = min(256, seq)
        for c in range(seq // ch):
            rows = slice(c * ch, (c + 1) * ch)
            dst = slice(past + c * ch, past + (c + 1) * ch)
            ks_scr[dst, :] = k_ref[0, rows, :].astype(BF16)
            vs_scr[dst, :] = v_ref[0, rows, :].astype(BF16)
        if past:
            pc = min(256, past)
            for src, dst_scr in ((kp_ref, ks_scr), (vp_ref, vs_scr)):
                for c in range(past // pc):
                    dst_scr[c * pc:(c + 1) * pc, :] = src[0, :, c * pc:(c + 1) * pc].T.astype(BF16)

    qm = []
    for ps in pairs:
        q = _headnorm(q_ref[0, :, ps], gq_ref[...]) * (1.0 / math.sqrt(SB_HEAD_DIM))
        qm.append(jnp.concatenate([jnp.where(lo_half, q, 0.0), jnp.where(lo_half, 0.0, q)], axis=0).astype(BF16))
    nrow = 2 * npair * tq

    def neg_cum(nk):
        rj = lax.broadcasted_iota(I32, (2 * nk, LANES + nk), 0)
        cj = lax.broadcasted_iota(I32, (2 * nk, LANES + nk), 1)
        rjm = jnp.where(rj >= nk, rj - nk, rj)
        return jnp.where(cj < LANES, -1.0, jnp.where(rjm >= cj - LANES, -1.0, 0.0)).astype(BF16)

    w_diag = neg_cum(tq)
    w_sub = w_diag if tq == LANES else neg_cum(LANES)
    causal = (lax.broadcasted_iota(I32, (nrow, tq), 1)
              < jnp.bitwise_and(lax.broadcasted_iota(I32, (nrow, tq), 0), tq - 1))

    def mask_diag(a):
        tail = jnp.where(causal, a[:, a.shape[1] - tq:], 0.0)
        return tail if a.shape[1] == tq else jnp.concatenate([a[:, :a.shape[1] - tq], tail], axis=1)

    def sweep(row0, nk, diag):
        kv = [(ks_scr[pl.ds(row0, nk), ps], vs_scr[pl.ds(row0, nk), ps]) for ps in pairs]
        z = jnp.concatenate([_dot_nt(qm[hp], kv[hp][0]) for hp in range(npair)], axis=0)
        sp = jnp.maximum(z, 0.0) + jnp.log(1.0 + jnp.exp(-jnp.abs(z)))
        if diag:
            sp = mask_diag(sp)
        hi = sp.astype(BF16)
        lo = (sp - hi.astype(F32)).astype(BF16)
        sub = min(nk, LANES)
        w_neg = w_diag if sub == tq else w_sub
        run = None if diag else run_scr[...]
        args = [None] * (nk // sub)
        for j in reversed(range(nk // sub)):
            cols = slice(j * sub, (j + 1) * sub)
            t = _dot(jnp.concatenate([hi[:, cols], lo[:, cols]], axis=1), w_neg)
            args[j] = z[:, cols] + t[:, LANES:] if run is None else z[:, cols] + t[:, LANES:] + run[:, :sub]
            run = t[:, :LANES] if run is None else run + t[:, :LANES]
        run_scr[...] = run
        p = jnp.exp(args[0] if len(args) == 1 else jnp.concatenate(args, axis=1))
        if diag:
            p = mask_diag(p)
        p = p.astype(BF16)
        for hp in range(npair):
            r0 = 2 * hp * tq
            vblk = kv[hp][1]
            vm = jnp.concatenate([jnp.where(lo_half, vblk, 0.0), jnp.where(lo_half, 0.0, vblk)], axis=0)
            pv = _dot(jnp.concatenate([p[r0:r0 + tq], p[r0 + tq:r0 + 2 * tq]], axis=1), vm.astype(BF16))
            acc_scr[hp] = pv if diag else acc_scr[hp] + pv
        return jnp.max(run)

    q0 = past + i * tq
    n_before = q0 // kb
    if tq == LANES:
        merged = q0 >= kb
        worst0 = lax.cond(merged,
                          lambda: sweep(pl.multiple_of(q0 - kb, tq), kb + tq, True),
                          lambda: sweep(pl.multiple_of(q0, tq), tq, True))
        done0 = merged.astype(I32)
    else:
        worst0 = sweep(pl.multiple_of(q0, tq), tq, True)
        done0 = jnp.int32(0)

    def cond(c):
        return jnp.logical_and(c[0] < n_before, c[1] > SB_UNDERFLOW)

    def body(c):
        row0 = q0 - (c[0] + 1) * kb
        return c[0] + 1, sweep(pl.multiple_of(row0, tq), kb, False)

    _, worst = lax.while_loop(cond, body, (done0, worst0))
    if tq < kb and seq > tq:
        @pl.when(jnp.logical_and(q0 - n_before * kb > 0, worst > SB_UNDERFLOW))
        def _():
            sweep(0, tq, False)
    for hp, ps in enumerate(pairs):
        o_ref[0, :, ps] = acc_scr[hp]


def _sb_attention(q_raw, k_norm, v, k_past, v_past, g_q):
    b, s, w = q_raw.shape
    past = 0 if k_past is None else k_past.shape[2]
    tq = min(SB_BLK, s)
    kb = SB_SWEEP
    assert s % tq == 0 and past % kb == 0 and w % LANES == 0 and kb % LANES == 0 and (kb == 2 * tq or s == tq)
    npair = w // LANES
    qblk = pl.BlockSpec((1, tq, w), lambda bi, i: (bi, i, 0))
    full = lambda n: pl.BlockSpec((1, n, w), lambda bi, i: (bi, 0, 0))
    gspec = pl.BlockSpec((1, LANES), lambda bi, i: (0, 0))
    in_specs = [qblk, full(s), full(s)]
    args = [q_raw, k_norm, v]
    if past:
        cache = pl.BlockSpec((1, w, past), lambda bi, i: (bi, 0, 0))
        in_specs += [cache, cache]
        args += [k_past, v_past]
    in_specs += [gspec]
    args += [g_q]
    return pl.pallas_call(
        functools.partial(_sb_kernel, past=past, seq=s, tq=tq, kb=kb, npair=npair),
        grid=(b, s // tq),
        in_specs=in_specs,
        out_specs=qblk,
        out_shape=jax.ShapeDtypeStruct((b, s, w), F32),
        scratch_shapes=[pltpu.VMEM((past + s, w), BF16), pltpu.VMEM((past + s, w), BF16),
                        pltpu.VMEM((2 * npair * tq, LANES), F32), pltpu.VMEM((npair, tq, LANES), F32)],
        compiler_params=_cparams(("arbitrary", "arbitrary")),
        name="sb_attention",
    )(*args)


def _mlstm_kernel(qk_ref, v_ref, og_ref, g_ref, cp_ref, c0_ref, n0_ref, m0_ref, wc_ref, bc_ref, gh_ref,
                  out_ref, cout_ref, nout_ref, mout_ref, convout_ref,
                  prev_scr, c_scr, n_scr, m_scr, *, chunk):
    ci = pl.program_id(1)
    nh, dh = ML_HEADS, ML_HEAD_DIM
    width = nh * dh
    L = chunk

    @pl.when(ci == 0)
    def _():
        prev_scr[0:8, :] = jnp.zeros((8, prev_scr.shape[1]), F32)
        prev_scr[8 - (CONV_W - 1):8, :] = cp_ref[0]
        c_scr[...] = c0_ref[0]
        n_scr[...] = n0_ref[0]
        m_scr[...] = m0_ref[0]

    prev_scr[8:8 + L, :] = qk_ref[0]
    wc = wc_ref[...]
    acc = bc_ref[...] + prev_scr[8 - (CONV_W - 1):8 - (CONV_W - 1) + L, :] * wc[0:1]
    for j in range(1, CONV_W):
        o = 8 - (CONV_W - 1) + j
        acc = acc + prev_scr[o:o + L, :] * wc[j:j + 1]
    qk = acc * jax.nn.sigmoid(acc)
    prev_scr[0:8, :] = prev_scr[L:L + 8, :]

    gt = g_ref[0]
    lf = _log_sigmoid(gt)
    gt_t = gt.T
    lf_t = lf.T
    row = lax.broadcasted_iota(I32, (L, L), 0)
    col = lax.broadcasted_iota(I32, (L, L), 1)
    tril = jnp.where(row >= col, 1.0, 0.0).astype(BF16)
    triu = jnp.where(row <= col, 1.0, 0.0).astype(BF16)
    p1, p2, p3 = _split3(lf)
    b_col = _dot(tril, p1) + _dot(tril, p2) + _dot(tril, p3)
    q1, q2, q3 = _split3(lf_t[0:8, :])
    b_row = _dot(q1, triu) + _dot(q2, triu) + _dot(q3, triu)

    lane = lax.broadcasted_iota(I32, (1, LANES), 1)
    m_vec = m_scr[...]
    m_out = m_vec
    for h in range(nh):
        hs = slice(h * dh, (h + 1) * dh)
        qb = qk[:, h * dh:(h + 1) * dh].astype(BF16)
        kh = qk[:, width + h * dh:width + (h + 1) * dh] * (dh ** -0.5)
        kb = kh.astype(BF16)
        v_t = v_ref[0, :, hs].T
        b_r = b_row[nh + h:nh + h + 1, :]
        ig_r = gt_t[h:h + 1, :]
        c_col = gt[:, h:h + 1] - b_col[:, nh + h:nh + h + 1]
        m_prev = m_vec[:, h:h + 1]

        d_t = jnp.where(row <= col, b_r + c_col, -jnp.inf)
        inter = b_r + m_prev
        m_t = jnp.maximum(inter, jnp.max(d_t, axis=0, keepdims=True))
        w_inter = jnp.exp(inter - m_t)
        s_t = _dot_nt(kb, qb) * jnp.exp(d_t - m_t)
        c_h = c_scr[h]
        n_h = n_scr[h:h + 1, :]
        n_hi = n_h.astype(BF16)
        n_lo = (n_h - n_hi.astype(F32)).astype(BF16)
        num_t = _dot(v_t.astype(BF16), s_t.astype(BF16)) + w_inter * _dot_nt(c_h.astype(BF16), qb)
        den = jnp.sum(s_t, axis=0, keepdims=True) + w_inter * (_dot_nt(n_hi, qb) + _dot_nt(n_lo, qb))
        hh_t = num_t / jnp.maximum(jnp.abs(den), jnp.exp(-m_t))

        b_last = b_r[:, L - 1:L]
        g_r = (b_last - b_r) + ig_r
        m_new = jnp.maximum(b_last + m_prev, jnp.max(g_r, axis=1, keepdims=True))
        w_state = jnp.exp(g_r - m_new)
        decay = jnp.exp(b_last + m_prev - m_new)
        c_scr[h] = decay * c_h + _dot((v_t * w_state).astype(BF16), kb)
        w_hi = w_state.astype(BF16)
        w_lo = (w_state - w_hi.astype(F32)).astype(BF16)
        n_scr[h:h + 1, :] = decay * n_h + (_dot(w_hi, kb) + _dot(w_lo, kb))
        m_out = jnp.where(lane == h, m_new, m_out)

        hn_t = hh_t * lax.rsqrt(jnp.mean(hh_t * hh_t, axis=0, keepdims=True) + EPS) * gh_ref[...]
        out_ref[0, :, hs] = jax.nn.sigmoid(og_ref[0, :, hs]) * hn_t.T
    m_scr[...] = m_out

    @pl.when(ci == pl.num_programs(1) - 1)
    def _():
        cout_ref[0] = c_scr[...]
        nout_ref[0] = n_scr[...]
        mout_ref[0] = m_scr[...]
        convout_ref[0] = prev_scr[8 - (CONV_W - 1):8, :]


def _mlstm(ml_qk, ml_v, ml_o, gates, conv_past, c0, n0, m0, w_conv, b_conv, g_head):
    b, s, w2 = ml_qk.shape
    width = w2 // 2
    nh, dh = ML_HEADS, ML_HEAD_DIM
    chunk = min(ML_CHUNK, s)
    assert s % chunk == 0 and chunk >= 8
    tok = lambda w: pl.BlockSpec((1, chunk, w), lambda bi, ci: (bi, ci, 0))
    perb = lambda shp: pl.BlockSpec((1,) + shp, lambda bi, ci: (bi,) + (0,) * len(shp))
    const = lambda shp: pl.BlockSpec(shp, lambda bi, ci: (0,) * len(shp))
    return pl.pallas_call(
        functools.partial(_mlstm_kernel, chunk=chunk),
        grid=(b, s // chunk),
        in_specs=[tok(w2), tok(width), tok(width), tok(LANES),
                  perb((CONV_W - 1, w2)), perb((nh, dh, dh)), perb((nh, dh)), perb((1, LANES)),
                  const((CONV_W, w2)), const((1, w2)), const((dh, 1))],
        out_specs=[tok(width), perb((nh, dh, dh)), perb((nh, dh)), perb((1, LANES)), perb((CONV_W - 1, w2))],
        out_shape=[jax.ShapeDtypeStruct((b, s, width), F32),
                   jax.ShapeDtypeStruct((b, nh, dh, dh), F32),
                   jax.ShapeDtypeStruct((b, nh, dh), F32),
                   jax.ShapeDtypeStruct((b, 1, LANES), F32),
                   jax.ShapeDtypeStruct((b, CONV_W - 1, w2), F32)],
        scratch_shapes=[pltpu.VMEM((8 + chunk, w2), F32), pltpu.VMEM((nh, dh, dh), F32),
                        pltpu.VMEM((nh, dh), F32), pltpu.VMEM((1, LANES), F32)],
        compiler_params=_cparams(("arbitrary", "arbitrary")),
        name="mlstm",
    )(ml_qk, ml_v, ml_o, gates, conv_past, c0, n0, m0, w_conv, b_conv, g_head)


def _outproj_kernel(osb_ref, oml_ref, x_ref, mod_ref, w1_ref, w2_ref, g2_ref, wrh_ref, wrl_ref, br_ref,
                    x1_ref, h2_ref, ti_ref, tg_ref, *, nb):
    tm = x_ref.shape[0]
    per_token = lambda j: _mod_rows(mod_ref, j, nb, x_ref.shape)

    mix = _dot(osb_ref[...].astype(BF16), w1_ref[...]) + _dot(oml_ref[...].astype(BF16), w2_ref[...])
    x1 = x_ref[...] + per_token(2) * mix
    x1_ref[...] = x1
    ms = jnp.mean(x1 * x1, axis=-1, keepdims=True)
    h2 = x1 * lax.rsqrt(ms + EPS) * g2_ref[...]
    h2 = h2 * (1.0 + per_token(4)) + per_token(3)
    _store_row_tiles(h2_ref, h2)

    hb = h2.astype(BF16)
    hl = (h2 - hb.astype(F32)).astype(BF16)
    logits = (_dot_nt(wrh_ref[...], hb) + _dot_nt(wrh_ref[...], hl) + _dot_nt(wrl_ref[...], hb))[0:N_EXPERTS, :]
    cur = logits + br_ref[...]
    erow = lax.broadcasted_iota(I32, cur.shape, 0)
    vals, idxs = [], []
    for _ in range(TOP_K):
        m = jnp.max(cur, axis=0, keepdims=True)
        idx = jnp.min(jnp.where(cur == m, erow, N_EXPERTS), axis=0, keepdims=True)
        vals.append(m)
        idxs.append(idx)
        cur = jnp.where(erow == idx, -jnp.inf, cur)
    es = [jnp.exp(v - vals[0]) for v in vals]
    tot = es[0] + es[1] + es[2] + es[3]
    krow = lax.broadcasted_iota(I32, (SUBLANES, tm), 0)
    ti = jnp.zeros((SUBLANES, tm), I32)
    tg = jnp.zeros((SUBLANES, tm), F32)
    for k in range(TOP_K):
        ti = jnp.where(krow == k, idxs[k], ti)
        tg = jnp.where(krow == k, es[k] / tot, tg)
    ti_ref[...] = ti
    tg_ref[...] = tg


def _outproj(o_sb, o_ml, x, mod, tokens_per_batch, w1, w2, g2, wrt_hi, wrt_lo, brt):
    t, d = x.shape
    hw = o_sb.shape[1]
    tm = ROW_TILE
    assert t % tm == 0
    nb, mod_spec = _mod_spec(tokens_per_batch, tm, d)
    tok = lambda w: pl.BlockSpec((tm, w), lambda i: (i, 0))
    const = lambda shp: pl.BlockSpec(shp, lambda i: (0,) * len(shp))
    rows8 = pl.BlockSpec((SUBLANES, tm), lambda i: (0, i))
    return pl.pallas_call(
        functools.partial(_outproj_kernel, nb=nb),
        grid=(t // tm,),
        in_specs=[tok(hw), tok(hw), tok(d), mod_spec,
                  const((hw, d)), const((hw, d)), const((1, d)),
                  const((LANES, d)), const((LANES, d)), const((N_EXPERTS, 1))],
        out_specs=[tok(d), pl.BlockSpec((tm * SUBLANES, LANES), lambda i: (i, 0)), rows8, rows8],
        out_shape=[jax.ShapeDtypeStruct((t, d), F32), jax.ShapeDtypeStruct((t * SUBLANES, LANES), F32),
                   jax.ShapeDtypeStruct((SUBLANES, t), I32), jax.ShapeDtypeStruct((SUBLANES, t), F32)],
        compiler_params=_cparams(("arbitrary",)),
        name="outproj_router",
    )(o_sb, o_ml, x, mod, w1, w2, g2, wrt_hi, wrt_lo, brt)


def _rank_kernel(ti_ref, pos_ref, cnt_ref, carry_scr, start_scr):
    ph = pl.program_id(0)
    i = pl.program_id(1)
    tm = ti_ref.shape[1]
    rep = lambda a: jnp.concatenate([a] * (tm // LANES), axis=1)

    @pl.when(i == 0)
    def _():
        carry_scr[...] = jnp.zeros_like(carry_scr)

    ti = ti_ref[...]
    erow = lax.broadcasted_iota(I32, (N_EXPERTS, tm), 0)
    sel = [erow == ti[k:k + 1, :] for k in range(TOP_K)]
    hit = jnp.zeros((N_EXPERTS, tm), F32)
    for k in range(TOP_K):
        hit = hit + jnp.where(sel[k], 1.0, 0.0)
    hit = hit.astype(BF16)
    seen = carry_scr[...]

    @pl.when(ph == 1)
    def _():
        earlier = lax.broadcasted_iota(I32, (tm, tm), 0) < lax.broadcasted_iota(I32, (tm, tm), 1)
        row_of = _dot(hit, jnp.where(earlier, 1.0, 0.0).astype(BF16)) + rep(seen) + rep(start_scr[...])
        krow = lax.broadcasted_iota(I32, (SUBLANES, tm), 0)
        pos = jnp.zeros((SUBLANES, tm), F32)
        for k in range(TOP_K):
            pos = jnp.where(krow == k, jnp.sum(jnp.where(sel[k], row_of, 0.0), axis=0, keepdims=True), pos)
        pos_ref[...] = pos.astype(I32)

    carry_scr[...] = seen + _dot(hit, jnp.ones((tm, LANES), BF16))

    @pl.when(jnp.logical_and(ph == 0, i == pl.num_programs(1) - 1))
    def _():
        cnt = carry_scr[...]
        cnt_ref[...] = cnt.astype(I32)
        padded = jnp.floor((cnt + (MOE_BLK - 1)) * (1.0 / MOE_BLK)) * MOE_BLK
        lower = (lax.broadcasted_iota(I32, (N_EXPERTS, N_EXPERTS), 1)
                 < lax.broadcasted_iota(I32, (N_EXPERTS, N_EXPERTS), 0))
        lower = jnp.where(lower, 1.0, 0.0).astype(BF16)
        p1, p2, p3 = _split3(padded)
        start_scr[...] = _dot(lower, p1) + _dot(lower, p2) + _dot(lower, p3)


def _positions(top_idx_t):
    t = top_idx_t.shape[1]
    tm = ROW_TILE
    assert t % tm == 0
    return pl.pallas_call(
        _rank_kernel,
        grid=(2, t // tm),
        in_specs=[pl.BlockSpec((SUBLANES, tm), lambda ph, i: (0, i))],
        out_specs=[pl.BlockSpec((SUBLANES, tm), lambda ph, i: (0, i * ph)),
                   pl.BlockSpec((N_EXPERTS, LANES), lambda ph, i: (0, 0))],
        out_shape=[jax.ShapeDtypeStruct((SUBLANES, t), I32), jax.ShapeDtypeStruct((N_EXPERTS, LANES), I32)],
        scratch_shapes=[pltpu.VMEM((N_EXPERTS, LANES), F32), pltpu.VMEM((N_EXPERTS, LANES), F32)],
        compiler_params=_cparams(("arbitrary", "arbitrary")),
        name="expert_positions",
    )(top_idx_t)


def _row_tile(ref, r):
    return ref.at[pl.ds(pl.multiple_of(r * SUBLANES, SUBLANES), SUBLANES), :]


def _dispatch_kernel(*refs, tm):
    pos_hbm, h_ref = refs[:2]
    xs_hbm, pos_smem0, pos_smem1, stage, pos_sem, row_sem = refs[-6:]
    i = pl.program_id(0)
    n = pl.num_programs(0)
    slot = lax.rem(i, 2)
    nidx = tm * TOP_K

    def pos_copy(tile, sl):
        return pltpu.make_async_copy(pos_hbm.at[pl.ds(pl.multiple_of(tile * nidx, nidx), nidx)],
                                     (pos_smem0, pos_smem1)[sl], pos_sem.at[sl])

    def drain(sl):
        for _ in range(TOP_K):
            pltpu.make_async_copy(stage.at[sl], stage.at[sl], row_sem.at[sl]).wait()

    @pl.when(i == 0)
    def _():
        pos_copy(0, 0).start()

    for sl in range(2):
        @pl.when(jnp.logical_and(i + 1 < n, slot == 1 - sl))
        def _():
            pos_copy(i + 1, sl).start()

    @pl.when(i >= 2)
    def _():
        drain(slot)

    stage[slot] = h_ref[...]

    for sl in range(2):
        @pl.when(slot == sl)
        def _():
            pos_copy(i, sl).wait()
            pos_smem = (pos_smem0, pos_smem1)[sl]

            def issue(c, carry):
                for rr in range(ISSUE_UNROLL):
                    r = c * ISSUE_UNROLL + rr
                    for k in range(TOP_K):
                        pltpu.make_async_copy(_row_tile(stage.at[sl], r), _row_tile(xs_hbm, pos_smem[k * tm + r]),
                                              row_sem.at[sl]).start(priority=k % DMA_QUEUES)
                return carry
            lax.fori_loop(0, tm // ISSUE_UNROLL, issue, 0)

    @pl.when(i == n - 1)
    def _():
        drain(slot)

        @pl.when(n >= 2)
        def _():
            drain(1 - slot)


def _dispatch(pos_flat, h_tiles, xs=None, n_rows=None):
    t = h_tiles.shape[0] // SUBLANES
    tm = GATHER_TILE
    assert t % tm == 0 and tm % ISSUE_UNROLL == 0
    in_specs = [pl.BlockSpec(memory_space=pl.ANY), pl.BlockSpec((tm * SUBLANES, LANES), lambda i: (i, 0))]
    args = [pos_flat, h_tiles]
    if xs is not None:
        in_specs.append(pl.BlockSpec(memory_space=pl.ANY))
        args.append(xs)
    return pl.pallas_call(
        functools.partial(_dispatch_kernel, tm=tm),
        grid=(t // tm,),
        in_specs=in_specs,
        out_specs=pl.BlockSpec(memory_space=pl.ANY),
        out_shape=jax.ShapeDtypeStruct((n_rows * SUBLANES, LANES) if xs is None else xs.shape, F32),
        scratch_shapes=[pltpu.SMEM((tm * TOP_K,), I32), pltpu.SMEM((tm * TOP_K,), I32),
                        pltpu.VMEM((2, tm * SUBLANES, LANES), F32),
                        pltpu.SemaphoreType.DMA((2,)), pltpu.SemaphoreType.DMA((2,))],
        input_output_aliases={} if xs is None else {2: 0},
        compiler_params=_cparams(("arbitrary",)),
        name="moe_dispatch",
    )(*args)


def _pad_fill_kernel(pos_hbm, xs_in_hbm, xs_hbm, pos_smem, zero_tile, pos_sem, row_sem, *, nidx):
    del xs_in_hbm
    i = pl.program_id(0)
    cp = pltpu.make_async_copy(pos_hbm.at[pl.ds(pl.multiple_of(i * nidx, nidx), nidx)], pos_smem, pos_sem)
    cp.start()
    zero_tile[...] = jnp.zeros_like(zero_tile)
    cp.wait()

    def issue(c, carry):
        for rr in range(TOP_K * ISSUE_UNROLL):
            r = c * (TOP_K * ISSUE_UNROLL) + rr
            pltpu.make_async_copy(zero_tile, _row_tile(xs_hbm, pos_smem[r]), row_sem).start(
                priority=rr % DMA_QUEUES)
        return carry
    lax.fori_loop(0, nidx // (TOP_K * ISSUE_UNROLL), issue, 0)
    whole = xs_hbm.at[pl.ds(0, nidx * SUBLANES), :]
    pltpu.make_async_copy(whole, whole, row_sem).wait()


def _pad_fill(pad_pos, xs):
    nidx = GATHER_TILE * TOP_K
    assert pad_pos.shape[0] % nidx == 0
    return pl.pallas_call(
        functools.partial(_pad_fill_kernel, nidx=nidx),
        grid=(pad_pos.shape[0] // nidx,),
        in_specs=[pl.BlockSpec(memory_space=pl.ANY), pl.BlockSpec(memory_space=pl.ANY)],
        out_specs=pl.BlockSpec(memory_space=pl.ANY),
        out_shape=jax.ShapeDtypeStruct(xs.shape, xs.dtype),
        scratch_shapes=[pltpu.SMEM((nidx,), I32), pltpu.VMEM((SUBLANES, LANES), F32),
                        pltpu.SemaphoreType.DMA, pltpu.SemaphoreType.DMA],
        input_output_aliases={1: 0},
        compiler_params=_cparams(("arbitrary",)),
        name="moe_pad_fill",
    )(pad_pos, xs)


def _moe_kernel(be_ref, nu_ref, xs_ref, wgu_ref, bgu_ref, wd_ref, bd_ref, ys_ref, wgu_bf, wd_bf):
    i = pl.program_id(0)
    d, ff = wgu_ref.shape[1], wd_ref.shape[1]

    @pl.when(jnp.logical_and(i < nu_ref[0],
                             jnp.logical_or(i == 0, be_ref[i] != be_ref[jnp.maximum(i - 1, 0)])))
    def _():
        for r in range(0, d, LANES):
            wgu_bf[r:r + LANES, :] = wgu_ref[0, r:r + LANES, :].astype(BF16)
        for r in range(0, ff, LANES):
            wd_bf[r:r + LANES, :] = wd_ref[0, r:r + LANES, :].astype(BF16)

    @pl.when(i < nu_ref[0])
    def _():
        x = _load_row_tiles(xs_ref, MOE_BLK).astype(BF16)
        glu = _dot(x, wgu_bf[:, 0:ff]) + bgu_ref[0, :, 0:ff]
        lin = _dot(x, wgu_bf[:, ff:2 * ff]) + bgu_ref[0, :, ff:2 * ff]
        glu = jnp.minimum(glu, SWIGLU_LIMIT)
        lin = jnp.clip(lin, -SWIGLU_LIMIT, SWIGLU_LIMIT)
        act = glu * jax.nn.sigmoid(SWIGLU_ALPHA * glu) * (lin + 1.0)
        _store_row_tiles(ys_ref, _dot(act.astype(BF16), wd_bf[...]) + bd_ref[0])


def _moe_mlp(block_expert, n_used, xs, w_gu, b_gu, w_down, b_down):
    ne, d, ff2 = w_gu.shape
    ff = ff2 // 2
    tb = MOE_BLK
    n_rows = block_expert.shape[0] * tb
    assert xs.shape[0] >= n_rows * SUBLANES
    blk = lambda i, be, nu: jnp.minimum(i, nu[0] - 1)
    grid_spec = pltpu.PrefetchScalarGridSpec(
        num_scalar_prefetch=2,
        grid=(n_rows // tb,),
        in_specs=[pl.BlockSpec((tb * SUBLANES, LANES), lambda i, be, nu: (blk(i, be, nu), 0)),
                  pl.BlockSpec((1, d, ff2), lambda i, be, nu: (be[blk(i, be, nu)], 0, 0)),
                  pl.BlockSpec((1, 1, ff2), lambda i, be, nu: (be[blk(i, be, nu)], 0, 0)),
                  pl.BlockSpec((1, ff, d), lambda i, be, nu: (be[blk(i, be, nu)], 0, 0)),
                  pl.BlockSpec((1, 1, d), lambda i, be, nu: (be[blk(i, be, nu)], 0, 0))],
        out_specs=pl.BlockSpec((tb * SUBLANES, LANES), lambda i, be, nu: (blk(i, be, nu), 0)),
        scratch_shapes=[pltpu.VMEM((d, ff2), BF16), pltpu.VMEM((ff, d), BF16)],
    )
    return pl.pallas_call(
        _moe_kernel,
        grid_spec=grid_spec,
        out_shape=jax.ShapeDtypeStruct((n_rows * SUBLANES, LANES), F32),
        compiler_params=_cparams(("arbitrary",)),
        name="moe_mlp",
    )(block_expert, n_used, xs, w_gu, b_gu.reshape(ne, 1, ff2), w_down, b_down.reshape(ne, 1, d))


def _combine_kernel(pos_hbm, ys_hbm, x1_ref, tg_ref, mod_ref, y_ref, pos_smem0, pos_smem1, buf, pos_sem, row_sem,
                    *, tm, nb):
    i = pl.program_id(0)
    n = pl.num_programs(0)
    slot = lax.rem(i, 2)
    nidx = tm * TOP_K

    def pos_copy(tile, sl):
        return pltpu.make_async_copy(pos_hbm.at[pl.ds(pl.multiple_of(tile * nidx, nidx), nidx)],
                                     (pos_smem0, pos_smem1)[sl], pos_sem.at[sl])

    def gather(tile, sl):
        pos_copy(tile, sl).wait()
        pos_smem = (pos_smem0, pos_smem1)[sl]

        def issue(c, carry):
            for rr in range(ISSUE_UNROLL):
                r = c * ISSUE_UNROLL + rr
                for k in range(TOP_K):
                    pltpu.make_async_copy(_row_tile(ys_hbm, pos_smem[k * tm + r]),
                                          _row_tile(buf.at[sl, k], r), row_sem.at[sl]).start(
                                              priority=k % DMA_QUEUES)
            return carry
        lax.fori_loop(0, tm // ISSUE_UNROLL, issue, 0)

        @pl.when(tile + 1 < n)
        def _():
            pos_copy(tile + 1, 1 - sl).start()

    @pl.when(i == 0)
    def _():
        pos_copy(0, 0).start()
        gather(0, 0)

    for sl in range(2):
        @pl.when(jnp.logical_and(i + 1 < n, slot == 1 - sl))
        def _():
            gather(i + 1, sl)

    for k in range(TOP_K):
        pltpu.make_async_copy(buf.at[slot, k], buf.at[slot, k], row_sem.at[slot]).wait()

    tg = tg_ref[...].T
    rows = tm // nb
    for c in range(SUBLANES):
        cols = slice(c * LANES, (c + 1) * LANES)
        ff = tg[:, 0:1] * buf[slot, 0, pl.ds(c, tm, stride=SUBLANES), :]
        for k in range(1, TOP_K):
            ff = ff + tg[:, k:k + 1] * buf[slot, k, pl.ds(c, tm, stride=SUBLANES), :]
        for g in range(nb):
            rs = slice(g * rows, (g + 1) * rows)
            y_ref[rs, cols] = x1_ref[rs, cols] + mod_ref[g, 5:6, cols] * ff[rs, :]


def _combine(pos_flat, ys, x1, top_gate, mod, tokens_per_batch):
    t, d = x1.shape
    tm = GATHER_TILE
    assert t % tm == 0 and tm % ISSUE_UNROLL == 0 and d == SUBLANES * LANES
    nb, mod_spec = _mod_spec(tokens_per_batch, tm, d)
    return pl.pallas_call(
        functools.partial(_combine_kernel, tm=tm, nb=nb),
        grid=(t // tm,),
        in_specs=[pl.BlockSpec(memory_space=pl.ANY), pl.BlockSpec(memory_space=pl.ANY),
                  pl.BlockSpec((tm, d), lambda i: (i, 0)), pl.BlockSpec((SUBLANES, tm), lambda i: (0, i)), mod_spec],
        out_specs=pl.BlockSpec((tm, d), lambda i: (i, 0)),
        out_shape=jax.ShapeDtypeStruct((t, d), F32),
        scratch_shapes=[pltpu.SMEM((tm * TOP_K,), I32), pltpu.SMEM((tm * TOP_K,), I32),
                        pltpu.VMEM((2, TOP_K, tm * SUBLANES, LANES), F32),
                        pltpu.SemaphoreType.DMA((2,)), pltpu.SemaphoreType.DMA((2,))],
        compiler_params=_cparams(("arbitrary",)),
        name="moe_combine",
    )(pos_flat, ys, x1, top_gate, mod)


def _pad_lanes(a, fill=0.0):
    return jnp.pad(a, [(0, 0)] * (a.ndim - 1) + [(0, LANES - a.shape[-1])], constant_values=fill)


def _hi_lo(w):
    hi = w.astype(BF16)
    return hi, (w - hi.astype(F32)).astype(BF16)


def _mixer(x, mod, k_past, v_past, conv_past, c0, n0, m0, p):
    b, s, d = x.shape
    outs = _inproj(x.reshape(b * s, d), mod, s, p["g1"], p["w_main"], p["wg_hi"], p["wg_lo"], p["bg"], p["g_k"],
                   p["widths"])
    q_raw, k_norm, v, ml_qk, ml_v, ml_o, gates = [a.reshape(b, s, a.shape[-1]) for a in outs[:7]]
    k_tiles, v_tiles = outs[7:]
    o_sb = _sb_attention(q_raw, k_norm, v, k_past, v_past, p["g_q"])
    o_ml, c_new, n_new, m_new, conv_new = _mlstm(
        ml_qk, ml_v, ml_o, gates, conv_past, c0, n0, _pad_lanes(m0)[:, None, :],
        p["w_conv"], p["b_conv"], p["g_head"])
    flat = lambda a: a.reshape(b * s, a.shape[-1])
    x1, h2, ti, tg = _outproj(flat(o_sb), flat(o_ml), flat(x), mod, s, p["w_out1"], p["w_out2"], p["g2"],
                              p["wrt_hi"], p["wrt_lo"], p["brt"])
    assert v.shape[-1] == SUBLANES * SB_HEAD_DIM
    state = (k_tiles.reshape(b, s, SUBLANES, SB_HEAD_DIM), v_tiles.reshape(b, s, SUBLANES, SB_HEAD_DIM), conv_new,
             c_new, n_new, m_new[:, 0, :ML_HEADS])
    return x1, h2, ti, tg, state


def _layer(xp, xs_, cp, cs, kc, vc, convc, cc, nc, mc, w_ada, b_ada, g_norm1, w_in, g_q, g_k, w_conv, b_conv,
           b_gate, g_head, w_out, g_norm2, w_router, b_router, w_gu, b_gu, w_down, b_down):
    d = xp.shape[-1]
    bp, sp = xp.shape[:2]
    bs, ss = xs_.shape[:2]
    sbw = kc.shape[-2] * kc.shape[-1]
    mlw = ML_HEADS * ML_HEAD_DIM
    ngate = 2 * ML_HEADS
    gate0 = 3 * sbw + 3 * mlw
    w_main = jnp.concatenate([w_in[:, :gate0], w_in[:, gate0 + ngate:]], axis=1).astype(BF16)
    wg_hi, wg_lo = _hi_lo(_pad_lanes(w_in[:, gate0:gate0 + ngate]))
    wrt_hi, wrt_lo = _hi_lo(_pad_lanes(w_router).T)
    p = dict(
        widths=(sbw, sbw, sbw, 2 * mlw, mlw, mlw),
        g1=g_norm1[None], w_main=w_main, wg_hi=wg_hi, wg_lo=wg_lo, bg=_pad_lanes(b_gate)[None],
        g_q=jnp.tile(g_q, LANES // SB_HEAD_DIM)[None], g_k=jnp.tile(g_k, LANES // SB_HEAD_DIM)[None],
        w_conv=w_conv, b_conv=b_conv[None], g_head=g_head[:, None],
        w_out1=w_out[:sbw].astype(BF16), w_out2=w_out[sbw:].astype(BF16), g2=g_norm2[None],
        wrt_hi=wrt_hi, wrt_lo=wrt_lo, brt=b_router[:, None],
    )
    mod = _ada(jnp.concatenate([cp, cs], axis=0), w_ada, b_ada)
    mod_p = mod[:bp].reshape(bp, 6, d)
    mod_s = mod[bp:].reshape(bs, 6, d)

    zeros = lambda *shp: jnp.zeros(shp, F32)
    feature_major = lambda a: jnp.transpose(a, (0, 2, 3, 1)).reshape(a.shape[0], sbw, a.shape[1])
    x1p, h2p, tip, tgp, st_p = _mixer(
        xp, mod_p, None, None, zeros(bp, CONV_W - 1, 2 * mlw), zeros(bp, ML_HEADS, ML_HEAD_DIM, ML_HEAD_DIM),
        zeros(bp, ML_HEADS, ML_HEAD_DIM), zeros(bp, ML_HEADS), p)
    x1s, h2s, tis, tgs, st_s = _mixer(
        xs_, mod_s, feature_major(kc), feature_major(vc), convc, cc, nc, mc, p)

    tp, ts = bp * sp, bs * ss
    pos, counts = _positions(jnp.concatenate([tip, tis], axis=1))
    counts = counts[:, 0]
    padded = (counts + MOE_BLK - 1) // MOE_BLK * MOE_BLK
    pad_end = jnp.cumsum(padded)
    pad_start = pad_end - padded
    n_blocks = -(-((tp + ts) * TOP_K + N_EXPERTS * (MOE_BLK - 1)) // MOE_BLK)
    n_used = (pad_end[-1] // MOE_BLK).astype(I32).reshape(1)
    block_start = jnp.arange(n_blocks, dtype=I32) * MOE_BLK
    block_expert = jnp.minimum(jnp.sum(block_start[:, None] >= pad_end[None, :], axis=1), N_EXPERTS - 1).astype(I32)

    def per_tile(rows):
        t = rows.shape[1]
        return rows.reshape(TOP_K, t // GATHER_TILE, GATHER_TILE).transpose(1, 0, 2).reshape(-1)
    pos_p = per_tile(pos[:TOP_K, :tp])
    pos_s = per_tile(pos[:TOP_K, tp:])

    n_rows = n_blocks * MOE_BLK
    slot = jnp.arange(MOE_BLK, dtype=I32)[None, :]
    spare = n_rows + jnp.arange(N_EXPERTS * MOE_BLK, dtype=I32).reshape(N_EXPERTS, MOE_BLK)
    pad_pos = jnp.where(slot < (padded - counts)[:, None], (pad_start + counts)[:, None] + slot, spare).reshape(-1)

    xs_buf = _dispatch(pos_p, h2p, n_rows=n_rows + N_EXPERTS * MOE_BLK)
    xs_buf = _dispatch(pos_s, h2s, xs_buf)
    xs_buf = _pad_fill(pad_pos.astype(I32), xs_buf)
    ys = _moe_mlp(block_expert, n_used, xs_buf, w_gu, b_gu, w_down, b_down)
    yp = _combine(pos_p, ys, x1p, tgp, mod_p, sp).reshape(bp, sp, d)
    ysm = _combine(pos_s, ys, x1s, tgs, mod_s, ss).reshape(bs, ss, d)
    return yp, ysm, st_p, st_s


def kernel(x_prompt, x_sample, c_prompt, c_sample, cache_sb_k, cache_sb_v, state_conv, state_mlstm_c, state_mlstm_n, state_mlstm_m, w_ada, b_ada, g_norm1, w_in, g_q, g_k, w_conv, b_conv, b_gate, g_head, w_out, g_norm2, w_router, b_router, w_gu, b_gu, w_down, b_down):
    assert w_ada.shape[0] == 1, "single-layer step"
    yp, ys, st_p, st_s = _layer(
        x_prompt, x_sample, c_prompt, c_sample, cache_sb_k[0], cache_sb_v[0], state_conv[0], state_mlstm_c[0],
        state_mlstm_n[0], state_mlstm_m[0], w_ada[0], b_ada[0], g_norm1[0], w_in[0], g_q[0], g_k[0], w_conv[0],
        b_conv[0], b_gate[0], g_head[0], w_out[0], g_norm2[0], w_router[0], b_router[0], w_gu[0], b_gu[0],
        w_down[0], b_down[0])
    return (yp, ys) + tuple(a[None] for a in st_p) + tuple(a[None] for a in st_s)
```

```python
import functools
import math

import jax
import jax.numpy as jnp
from jax import lax
from jax.experimental import pallas as pl
from jax.experimental.pallas import tpu as pltpu

F32 = jnp.float32
BF16 = jnp.bfloat16
I32 = jnp.int32

EPS = 1e-6
LANES = 128
SUBLANES = 8
SB_HEAD_DIM = 64
ML_HEAD_DIM = 128
ML_HEADS = 4
CONV_W = 4
N_EXPERTS = 32
TOP_K = 4
SWIGLU_LIMIT = 7.0
SWIGLU_ALPHA = 1.702
VMEM_LIMIT = 56 * 1024 * 1024

ROW_TILE = 512
MOE_BLK = 512
GATHER_TILE = 256
ISSUE_UNROLL = 8
DMA_QUEUES = 2
ML_CHUNK = 256
SB_BLK = 128
SB_SWEEP = 256
SB_UNDERFLOW = -105.0


def _cparams(sem):
    return pltpu.CompilerParams(dimension_semantics=sem, vmem_limit_bytes=VMEM_LIMIT)


def _split3(x):
    p1 = x.astype(BF16)
    r1 = x - p1.astype(F32)
    p2 = r1.astype(BF16)
    p3 = (r1 - p2.astype(F32)).astype(BF16)
    return p1, p2, p3


def _dot(a, b):
    return jnp.dot(a, b, preferred_element_type=F32)


def _dot_nt(a, b):
    return lax.dot_general(a, b, (((1,), (1,)), ((), ())), preferred_element_type=F32)


def _dot3(x, w_hi, w_lo):
    xh = x.astype(BF16)
    xl = (x - xh.astype(F32)).astype(BF16)
    return _dot(xh, w_hi) + _dot(xl, w_hi) + _dot(xh, w_lo)


def _store_row_tiles(ref, x):
    n, d = x.shape
    assert d == SUBLANES * LANES
    for c in range(SUBLANES):
        ref[pl.ds(c, n, stride=SUBLANES), :] = x[:, c * LANES:(c + 1) * LANES]


def _load_row_tiles(ref, n):
    return jnp.concatenate([ref[pl.ds(c, n, stride=SUBLANES), :] for c in range(SUBLANES)], axis=1)


def _store_head_tiles(ref, row0, head0, x):
    n = x.shape[0]
    for j in range(x.shape[1] // SB_HEAD_DIM):
        ref[pl.ds(row0 * SUBLANES + head0 + j, n, stride=SUBLANES), :] = x[:, j * SB_HEAD_DIM:(j + 1) * SB_HEAD_DIM]


def _headnorm(x, g):
    lo_half = lax.broadcasted_iota(I32, (1, LANES), 1) < SB_HEAD_DIM
    x2 = x * x
    s0 = jnp.sum(jnp.where(lo_half, x2, 0.0), axis=-1, keepdims=True)
    s1 = jnp.sum(jnp.where(lo_half, 0.0, x2), axis=-1, keepdims=True)
    r = jnp.where(lo_half, lax.rsqrt(s0 * (1.0 / SB_HEAD_DIM) + EPS), lax.rsqrt(s1 * (1.0 / SB_HEAD_DIM) + EPS))
    return x * r * g


def _mod_rows(mod_ref, j, nb, shape):
    if nb == 1:
        return mod_ref[0, j:j + 1, :]
    rows = shape[0] // nb
    return jnp.concatenate([jnp.broadcast_to(mod_ref[g, j:j + 1, :], (rows, shape[1])) for g in range(nb)], axis=0)


def _mod_spec(tokens_per_batch, tm, d):
    if tokens_per_batch >= tm:
        assert tokens_per_batch % tm == 0
        per = tokens_per_batch // tm
        return 1, pl.BlockSpec((1, 6, d), lambda i: (i // per, 0, 0))
    assert tm % tokens_per_batch == 0
    nb = tm // tokens_per_batch
    return nb, pl.BlockSpec((nb, 6, d), lambda i: (i, 0, 0))


def _log_sigmoid(x):
    return jnp.minimum(x, 0.0) - jnp.log1p(jnp.exp(-jnp.abs(x)))


def _ada_kernel(c_ref, w_ref, b_ref, o_ref):
    c = c_ref[...]
    s = c * jax.nn.sigmoid(c)
    o_ref[...] = jnp.dot(s, w_ref[...], precision=lax.Precision.HIGHEST,
                         preferred_element_type=F32) + b_ref[...]


def _ada(c, w_ada, b_ada):
    n, d = c.shape
    cols = w_ada.shape[1]
    tn = cols // 6
    return pl.pallas_call(
        _ada_kernel,
        grid=(cols // tn,),
        in_specs=[pl.BlockSpec((n, d), lambda j: (0, 0)),
                  pl.BlockSpec((d, tn), lambda j: (0, j)),
                  pl.BlockSpec((1, tn), lambda j: (0, j))],
        out_specs=pl.BlockSpec((n, tn), lambda j: (0, j)),
        out_shape=jax.ShapeDtypeStruct((n, cols), F32),
        compiler_params=_cparams(("arbitrary",)),
        name="ada_mod",
    )(c, w_ada, b_ada.reshape(1, cols))


def _inproj_kernel(x_ref, mod_ref, g1_ref, w_ref, wgh_ref, wgl_ref, bg_ref, gk_ref,
                   q_ref, k_ref, v_ref, qk_ref, mv_ref, mo_ref, gate_ref, kt_ref, vt_ref, *, widths, nb):
    x = x_ref[...]
    ms = jnp.mean(x * x, axis=-1, keepdims=True)
    h = x * lax.rsqrt(ms + EPS) * g1_ref[...]
    h = h * (1.0 + _mod_rows(mod_ref, 1, nb, x.shape)) + _mod_rows(mod_ref, 0, nb, x.shape)
    hb = h.astype(BF16)
    off = 0
    for ref, wd in zip((q_ref, k_ref, v_ref, qk_ref, mv_ref, mo_ref), widths):
        u = _dot(hb, w_ref[:, off:off + wd])
        if ref is k_ref:
            u = jnp.concatenate([_headnorm(u[:, c:c + LANES], gk_ref[...]) for c in range(0, wd, LANES)], axis=1)
            _store_head_tiles(kt_ref, 0, 0, u)
        if ref is v_ref:
            _store_head_tiles(vt_ref, 0, 0, u)
        ref[...] = u
        off += wd
    gate_ref[...] = _dot3(h, wgh_ref[...], wgl_ref[...]) + bg_ref[...]


def _inproj(x, mod, tokens_per_batch, g1, w_main, wg_hi, wg_lo, bg, g_k, widths):
    t, d = x.shape
    tm = ROW_TILE
    assert t % tm == 0
    ncol = w_main.shape[1]
    nb, mod_spec = _mod_spec(tokens_per_batch, tm, d)
    tok = lambda w: pl.BlockSpec((tm, w), lambda i: (i, 0))
    const = lambda shp: pl.BlockSpec(shp, lambda i: (0,) * len(shp))
    head_tiles = pl.BlockSpec((tm * SUBLANES, SB_HEAD_DIM), lambda i: (i, 0))
    return pl.pallas_call(
        functools.partial(_inproj_kernel, widths=widths, nb=nb),
        grid=(t // tm,),
        in_specs=[tok(d), mod_spec,
                  const((1, d)), const((d, ncol)), const((d, LANES)), const((d, LANES)), const((1, LANES)),
                  const((1, LANES))],
        out_specs=[tok(w) for w in widths] + [tok(LANES), head_tiles, head_tiles],
        out_shape=[jax.ShapeDtypeStruct((t, w), F32) for w in widths]
        + [jax.ShapeDtypeStruct((t, LANES), F32)]
        + [jax.ShapeDtypeStruct((t * SUBLANES, SB_HEAD_DIM), F32)] * 2,
        compiler_params=_cparams(("arbitrary",)),
        name="inproj",
    )(x, mod, g1, w_main, wg_hi, wg_lo, bg, g_k)


def _sb_kernel(*refs, past, seq, tq, kb, npair):
    ks_scr, vs_scr, run_scr, acc_scr = refs[-4:]
    if past:
        q_ref, k_ref, v_ref, kp_ref, vp_ref, gq_ref, o_ref = refs[:-4]
    else:
        q_ref, k_ref, v_ref, gq_ref, o_ref = refs[:-4]
    i = pl.program_id(1)
    lane = lax.broadcasted_iota(I32, (1, LANES), 1)
    lo_half = lane < SB_HEAD_DIM
    pairs = [slice(hp * LANES, (hp + 1) * LANES) for hp in range(npair)]

    @pl.when(i == 0)
    def _():
        ch = min(256, seq)
        for c in range(seq // ch):
            rows = slice(c * ch, (c + 1) * ch)
            dst = slice(past + c * ch, past + (c + 1) * ch)
            ks_scr[dst, :] = k_ref[0, rows, :].astype(BF16)
            vs_scr[dst, :] = v_ref[0, rows, :].astype(BF16)
        if past:
            pc = min(256, past)
            for src, dst_scr in ((kp_ref, ks_scr), (vp_ref, vs_scr)):
                for c in range(past // pc):
                    dst_scr[c * pc:(c + 1) * pc, :] = src[0, :, c * pc:(c + 1) * pc].T.astype(BF16)

    qm = []
    for ps in pairs:
        q = _headnorm(q_ref[0, :, ps], gq_ref[...]) * (1.0 / math.sqrt(SB_HEAD_DIM))
        qm.append(jnp.concatenate([jnp.where(lo_half, q, 0.0), jnp.where(lo_half, 0.0, q)], axis=0).astype(BF16))
    nrow = 2 * npair * tq

    def neg_cum(nk):
        rj = lax.broadcasted_iota(I32, (2 * nk, LANES + nk), 0)
        cj = lax.broadcasted_iota(I32, (2 * nk, LANES + nk), 1)
        rjm = jnp.where(rj >= nk, rj - nk, rj)
        return jnp.where(cj < LANES, -1.0, jnp.where(rjm >= cj - LANES, -1.0, 0.0)).astype(BF16)

    w_diag = neg_cum(tq)
    w_sub = w_diag if tq == LANES else neg_cum(LANES)
    causal = (lax.broadcasted_iota(I32, (nrow, tq), 1)
              < jnp.bitwise_and(lax.broadcasted_iota(I32, (nrow, tq), 0), tq - 1))

    def mask_diag(a):
        tail = jnp.where(causal, a[:, a.shape[1] - tq:], 0.0)
        return tail if a.shape[1] == tq else jnp.concatenate([a[:, :a.shape[1] - tq], tail], axis=1)

    def sweep(row0, nk, diag):
        kv = [(ks_scr[pl.ds(row0, nk), ps], vs_scr[pl.ds(row0, nk), ps]) for ps in pairs]
        z = jnp.concatenate([_dot_nt(qm[hp], kv[hp][0]) for hp in range(npair)], axis=0)
        sp = jnp.maximum(z, 0.0) + jnp.log(1.0 + jnp.exp(-jnp.abs(z)))
        if diag:
            sp = mask_diag(sp)
        hi = sp.astype(BF16)
        lo = (sp - hi.astype(F32)).astype(BF16)
        sub = min(nk, LANES)
        w_neg = w_diag if sub == tq else w_sub
        run = None if diag else run_scr[...]
        args = [None] * (nk // sub)
        for j in reversed(range(nk // sub)):
            cols = slice(j * sub, (j + 1) * sub)
            t = _dot(jnp.concatenate([hi[:, cols], lo[:, cols]], axis=1), w_neg)
            args[j] = z[:, cols] + t[:, LANES:] if run is None else z[:, cols] + t[:, LANES:] + run[:, :sub]
            run = t[:, :LANES] if run is None else run + t[:, :LANES]
        run_scr[...] = run
        p = jnp.exp(args[0] if len(args) == 1 else jnp.concatenate(args, axis=1))
        if diag:
            p = mask_diag(p)
        p = p.astype(BF16)
        for hp in range(npair):
            r0 = 2 * hp * tq
            vblk = kv[hp][1]
            vm = jnp.concatenate([jnp.where(lo_half, vblk, 0.0), jnp.where(lo_half, 0.0, vblk)], axis=0)
            pv = _dot(jnp.concatenate([p[r0:r0 + tq], p[r0 + tq:r0 + 2 * tq]], axis=1), vm.astype(BF16))
            acc_scr[hp] = pv if diag else acc_scr[hp] + pv
        return jnp.max(run)

    q0 = past + i * tq
    n_before = q0 // kb
    if tq == LANES:
        merged = q0 >= kb
        worst0 = lax.cond(merged,
                          lambda: sweep(pl.multiple_of(q0 - kb, tq), kb + tq, True),
                          lambda: sweep(pl.multiple_of(q0, tq), tq, True))
        done0 = merged.astype(I32)
    else:
        worst0 = sweep(pl.multiple_of(q0, tq), tq, True)
        done0 = jnp.int32(0)

    def cond(c):
        return jnp.logical_and(c[0] < n_before, c[1] > SB_UNDERFLOW)

    def body(c):
        row0 = q0 - (c[0] + 1) * kb
        return c[0] + 1, sweep(pl.multiple_of(row0, tq), kb, False)

    _, worst = lax.while_loop(cond, body, (done0, worst0))
    if tq < kb and seq > tq:
        @pl.when(jnp.logical_and(q0 - n_before * kb > 0, worst > SB_UNDERFLOW))
        def _():
            sweep(0, tq, False)
    for hp, ps in enumerate(pairs):
        o_ref[0, :, ps] = acc_scr[hp]


def _sb_attention(q_raw, k_norm, v, k_past, v_past, g_q):
    b, s, w = q_raw.shape
    past = 0 if k_past is None else k_past.shape[2]
    tq = min(SB_BLK, s)
    kb = SB_SWEEP
    assert s % tq == 0 and past % kb == 0 and w % LANES == 0 and kb % LANES == 0 and (kb == 2 * tq or s == tq)
    npair = w // LANES
    qblk = pl.BlockSpec((1, tq, w), lambda bi, i: (bi, i, 0))
    full = lambda n: pl.BlockSpec((1, n, w), lambda bi, i: (bi, 0, 0))
    gspec = pl.BlockSpec((1, LANES), lambda bi, i: (0, 0))
    in_specs = [qblk, full(s), full(s)]
    args = [q_raw, k_norm, v]
    if past:
        cache = pl.BlockSpec((1, w, past), lambda bi, i: (bi, 0, 0))
        in_specs += [cache, cache]
        args += [k_past, v_past]
    in_specs += [gspec]
    args += [g_q]
    return pl.pallas_call(
        functools.partial(_sb_kernel, past=past, seq=s, tq=tq, kb=kb, npair=npair),
        grid=(b, s // tq),
        in_specs=in_specs,
        out_specs=qblk,
        out_shape=jax.ShapeDtypeStruct((b, s, w), F32),
        scratch_shapes=[pltpu.VMEM((past + s, w), BF16), pltpu.VMEM((past + s, w), BF16),
                        pltpu.VMEM((2 * npair * tq, LANES), F32), pltpu.VMEM((npair, tq, LANES), F32)],
        compiler_params=_cparams(("arbitrary", "arbitrary")),
        name="sb_attention",
    )(*args)


def _mlstm_kernel(qk_ref, v_ref, og_ref, g_ref, cp_ref, c0_ref, n0_ref, m0_ref, wc_ref, bc_ref, gh_ref,
                  out_ref, cout_ref, nout_ref, mout_ref, convout_ref,
                  prev_scr, c_scr, n_scr, m_scr, *, chunk):
    ci = pl.program_id(1)
    nh, dh = ML_HEADS, ML_HEAD_DIM
    width = nh * dh
    L = chunk

    @pl.when(ci == 0)
    def _():
        prev_scr[0:8, :] = jnp.zeros((8, prev_scr.shape[1]), F32)
        prev_scr[8 - (CONV_W - 1):8, :] = cp_ref[0]
        c_scr[...] = c0_ref[0]
        n_scr[...] = n0_ref[0]
        m_scr[...] = m0_ref[0]

    prev_scr[8:8 + L, :] = qk_ref[0]
    wc = wc_ref[...]
    acc = bc_ref[...] + prev_scr[8 - (CONV_W - 1):8 - (CONV_W - 1) + L, :] * wc[0:1]
    for j in range(1, CONV_W):
        o = 8 - (CONV_W - 1) + j
        acc = acc + prev_scr[o:o + L, :] * wc[j:j + 1]
    qk = acc * jax.nn.sigmoid(acc)
    prev_scr[0:8, :] = prev_scr[L:L + 8, :]

    gt = g_ref[0]
    lf = _log_sigmoid(gt)
    gt_t = gt.T
    lf_t = lf.T
    row = lax.broadcasted_iota(I32, (L, L), 0)
    col = lax.broadcasted_iota(I32, (L, L), 1)
    tril = jnp.where(row >= col, 1.0, 0.0).astype(BF16)
    triu = jnp.where(row <= col, 1.0, 0.0).astype(BF16)
    p1, p2, p3 = _split3(lf)
    b_col = _dot(tril, p1) + _dot(tril, p2) + _dot(tril, p3)
    q1, q2, q3 = _split3(lf_t[0:8, :])
    b_row = _dot(q1, triu) + _dot(q2, triu) + _dot(q3, triu)

    lane = lax.broadcasted_iota(I32, (1, LANES), 1)
    m_vec = m_scr[...]
    m_out = m_vec
    for h in range(nh):
        hs = slice(h * dh, (h + 1) * dh)
        qb = qk[:, h * dh:(h + 1) * dh].astype(BF16)
        kh = qk[:, width + h * dh:width + (h + 1) * dh] * (dh ** -0.5)
        kb = kh.astype(BF16)
        v_t = v_ref[0, :, hs].T
        b_r = b_row[nh + h:nh + h + 1, :]
        ig_r = gt_t[h:h + 1, :]
        c_col = gt[:, h:h + 1] - b_col[:, nh + h:nh + h + 1]
        m_prev = m_vec[:, h:h + 1]

        d_t = jnp.where(row <= col, b_r + c_col, -jnp.inf)
        inter = b_r + m_prev
        m_t = jnp.maximum(inter, jnp.max(d_t, axis=0, keepdims=True))
        w_inter = jnp.exp(inter - m_t)
        s_t = _dot_nt(kb, qb) * jnp.exp(d_t - m_t)
        c_h = c_scr[h]
        n_h = n_scr[h:h + 1, :]
        n_hi = n_h.astype(BF16)
        n_lo = (n_h - n_hi.astype(F32)).astype(BF16)
        num_t = _dot(v_t.astype(BF16), s_t.astype(BF16)) + w_inter * _dot_nt(c_h.astype(BF16), qb)
        den = jnp.sum(s_t, axis=0, keepdims=True) + w_inter * (_dot_nt(n_hi, qb) + _dot_nt(n_lo, qb))
        hh_t = num_t / jnp.maximum(jnp.abs(den), jnp.exp(-m_t))

        b_last = b_r[:, L - 1:L]
        g_r = (b_last - b_r) + ig_r
        m_new = jnp.maximum(b_last + m_prev, jnp.max(g_r, axis=1, keepdims=True))
        w_state = jnp.exp(g_r - m_new)
        decay = jnp.exp(b_last + m_prev - m_new)
        c_scr[h] = decay * c_h + _dot((v_t * w_state).astype(BF16), kb)
        w_hi = w_state.astype(BF16)
        w_lo = (w_state - w_hi.astype(F32)).astype(BF16)
        n_scr[h:h + 1, :] = decay * n_h + (_dot(w_hi, kb) + _dot(w_lo, kb))
        m_out = jnp.where(lane == h, m_new, m_out)

        hn_t = hh_t * lax.rsqrt(jnp.mean(hh_t * hh_t, axis=0, keepdims=True) + EPS) * gh_ref[...]
        out_ref[0, :, hs] = jax.nn.sigmoid(og_ref[0, :, hs]) * hn_t.T
    m_scr[...] = m_out

    @pl.when(ci == pl.num_programs(1) - 1)
    def _():
        cout_ref[0] = c_scr[...]
        nout_ref[0] = n_scr[...]
        mout_ref[0] = m_scr[...]
        convout_ref[0] = prev_scr[8 - (CONV_W - 1):8, :]


def _mlstm(ml_qk, ml_v, ml_o, gates, conv_past, c0, n0, m0, w_conv, b_conv, g_head):
    b, s, w2 = ml_qk.shape
    width = w2 // 2
    nh, dh = ML_HEADS, ML_HEAD_DIM
    chunk = min(ML_CHUNK, s)
    assert s % chunk == 0 and chunk >= 8
    tok = lambda w: pl.BlockSpec((1, chunk, w), lambda bi, ci: (bi, ci, 0))
    perb = lambda shp: pl.BlockSpec((1,) + shp, lambda bi, ci: (bi,) + (0,) * len(shp))
    const = lambda shp: pl.BlockSpec(shp, lambda bi, ci: (0,) * len(shp))
    return pl.pallas_call(
        functools.partial(_mlstm_kernel, chunk=chunk),
        grid=(b, s // chunk),
        in_specs=[tok(w2), tok(width), tok(width), tok(LANES),
                  perb((CONV_W - 1, w2)), perb((nh, dh, dh)), perb((nh, dh)), perb((1, LANES)),
                  const((CONV_W, w2)), const((1, w2)), const((dh, 1))],
        out_specs=[tok(width), perb((nh, dh, dh)), perb((nh, dh)), perb((1, LANES)), perb((CONV_W - 1, w2))],
        out_shape=[jax.ShapeDtypeStruct((b, s, width), F32),
                   jax.ShapeDtypeStruct((b, nh, dh, dh), F32),
                   jax.ShapeDtypeStruct((b, nh, dh), F32),
                   jax.ShapeDtypeStruct((b, 1, LANES), F32),
                   jax.ShapeDtypeStruct((b, CONV_W - 1, w2), F32)],
        scratch_shapes=[pltpu.VMEM((8 + chunk, w2), F32), pltpu.VMEM((nh, dh, dh), F32),
                        pltpu.VMEM((nh, dh), F32), pltpu.VMEM((1, LANES), F32)],
        compiler_params=_cparams(("arbitrary", "arbitrary")),
        name="mlstm",
    )(ml_qk, ml_v, ml_o, gates, conv_past, c0, n0, m0, w_conv, b_conv, g_head)


def _outproj_kernel(osb_ref, oml_ref, x_ref, mod_ref, w1_ref, w2_ref, g2_ref, wrh_ref, wrl_ref, br_ref,
                    x1_ref, h2_ref, ti_ref, tg_ref, *, nb):
    tm = x_ref.shape[0]
    per_token = lambda j: _mod_rows(mod_ref, j, nb, x_ref.shape)

    mix = _dot(osb_ref[...].astype(BF16), w1_ref[...]) + _dot(oml_ref[...].astype(BF16), w2_ref[...])
    x1 = x_ref[...] + per_token(2) * mix
    x1_ref[...] = x1
    ms = jnp.mean(x1 * x1, axis=-1, keepdims=True)
    h2 = x1 * lax.rsqrt(ms + EPS) * g2_ref[...]
    h2 = h2 * (1.0 + per_token(4)) + per_token(3)
    _store_row_tiles(h2_ref, h2)

    hb = h2.astype(BF16)
    hl = (h2 - hb.astype(F32)).astype(BF16)
    logits = (_dot_nt(wrh_ref[...], hb) + _dot_nt(wrh_ref[...], hl) + _dot_nt(wrl_ref[...], hb))[0:N_EXPERTS, :]
    cur = logits + br_ref[...]
    erow = lax.broadcasted_iota(I32, cur.shape, 0)
    vals, idxs = [], []
    for _ in range(TOP_K):
        m = jnp.max(cur, axis=0, keepdims=True)
        idx = jnp.min(jnp.where(cur == m, erow, N_EXPERTS), axis=0, keepdims=True)
        vals.append(m)
        idxs.append(idx)
        cur = jnp.where(erow == idx, -jnp.inf, cur)
    es = [jnp.exp(v - vals[0]) for v in vals]
    tot = es[0] + es[1] + es[2] + es[3]
    krow = lax.broadcasted_iota(I32, (SUBLANES, tm), 0)
    ti = jnp.zeros((SUBLANES, tm), I32)
    tg = jnp.zeros((SUBLANES, tm), F32)
    for k in range(TOP_K):
        ti = jnp.where(krow == k, idxs[k], ti)
        tg = jnp.where(krow == k, es[k] / tot, tg)
    ti_ref[...] = ti
    tg_ref[...] = tg


def _outproj(o_sb, o_ml, x, mod, tokens_per_batch, w1, w2, g2, wrt_hi, wrt_lo, brt):
    t, d = x.shape
    hw = o_sb.shape[1]
    tm = ROW_TILE
    assert t % tm == 0
    nb, mod_spec = _mod_spec(tokens_per_batch, tm, d)
    tok = lambda w: pl.BlockSpec((tm, w), lambda i: (i, 0))
    const = lambda shp: pl.BlockSpec(shp, lambda i: (0,) * len(shp))
    rows8 = pl.BlockSpec((SUBLANES, tm), lambda i: (0, i))
    return pl.pallas_call(
        functools.partial(_outproj_kernel, nb=nb),
        grid=(t // tm,),
        in_specs=[tok(hw), tok(hw), tok(d), mod_spec,
                  const((hw, d)), const((hw, d)), const((1, d)),
                  const((LANES, d)), const((LANES, d)), const((N_EXPERTS, 1))],
        out_specs=[tok(d), pl.BlockSpec((tm * SUBLANES, LANES), lambda i: (i, 0)), rows8, rows8],
        out_shape=[jax.ShapeDtypeStruct((t, d), F32), jax.ShapeDtypeStruct((t * SUBLANES, LANES), F32),
                   jax.ShapeDtypeStruct((SUBLANES, t), I32), jax.ShapeDtypeStruct((SUBLANES, t), F32)],
        compiler_params=_cparams(("arbitrary",)),
        name="outproj_router",
    )(o_sb, o_ml, x, mod, w1, w2, g2, wrt_hi, wrt_lo, brt)


def _rank_kernel(ti_ref, pos_ref, cnt_ref, carry_scr, start_scr):
    ph = pl.program_id(0)
    i = pl.program_id(1)
    tm = ti_ref.shape[1]
    rep = lambda a: jnp.concatenate([a] * (tm // LANES), axis=1)

    @pl.when(i == 0)
    def _():
        carry_scr[...] = jnp.zeros_like(carry_scr)

    ti = ti_ref[...]
    erow = lax.broadcasted_iota(I32, (N_EXPERTS, tm), 0)
    sel = [erow == ti[k:k + 1, :] for k in range(TOP_K)]
    hit = jnp.zeros((N_EXPERTS, tm), F32)
    for k in range(TOP_K):
        hit = hit + jnp.where(sel[k], 1.0, 0.0)
    hit = hit.astype(BF16)
    seen = carry_scr[...]

    @pl.when(ph == 1)
    def _():
        earlier = lax.broadcasted_iota(I32, (tm, tm), 0) < lax.broadcasted_iota(I32, (tm, tm), 1)
        row_of = _dot(hit, jnp.where(earlier, 1.0, 0.0).astype(BF16)) + rep(seen) + rep(start_scr[...])
        krow = lax.broadcasted_iota(I32, (SUBLANES, tm), 0)
        pos = jnp.zeros((SUBLANES, tm), F32)
        for k in range(TOP_K):
            pos = jnp.where(krow == k, jnp.sum(jnp.where(sel[k], row_of, 0.0), axis=0, keepdims=True), pos)
        pos_ref[...] = pos.astype(I32)

    carry_scr[...] = seen + _dot(hit, jnp.ones((tm, LANES), BF16))

    @pl.when(jnp.logical_and(ph == 0, i == pl.num_programs(1) - 1))
    def _():
        cnt = carry_scr[...]
        cnt_ref[...] = cnt.astype(I32)
        padded = jnp.floor((cnt + (MOE_BLK - 1)) * (1.0 / MOE_BLK)) * MOE_BLK
        lower = (lax.broadcasted_iota(I32, (N_EXPERTS, N_EXPERTS), 1)
                 < lax.broadcasted_iota(I32, (N_EXPERTS, N_EXPERTS), 0))
        lower = jnp.where(lower, 1.0, 0.0).astype(BF16)
        p1, p2, p3 = _split3(padded)
        start_scr[...] = _dot(lower, p1) + _dot(lower, p2) + _dot(lower, p3)


def _positions(top_idx_t):
    t = top_idx_t.shape[1]
    tm = ROW_TILE
    assert t % tm == 0
    return pl.pallas_call(
        _rank_kernel,
        grid=(2, t // tm),
        in_specs=[pl.BlockSpec((SUBLANES, tm), lambda ph, i: (0, i))],
        out_specs=[pl.BlockSpec((SUBLANES, tm), lambda ph, i: (0, i * ph)),
                   pl.BlockSpec((N_EXPERTS, LANES), lambda ph, i: (0, 0))],
        out_shape=[jax.ShapeDtypeStruct((SUBLANES, t), I32), jax.ShapeDtypeStruct((N_EXPERTS, LANES), I32)],
        scratch_shapes=[pltpu.VMEM((N_EXPERTS, LANES), F32), pltpu.VMEM((N_EXPERTS, LANES), F32)],
        compiler_params=_cparams(("arbitrary", "arbitrary")),
        name="expert_positions",
    )(top_idx_t)


def _row_tile(ref, r):
    return ref.at[pl.ds(pl.multiple_of(r * SUBLANES, SUBLANES), SUBLANES), :]


def _dispatch_kernel(*refs, tm):
    pos_hbm, h_ref = refs[:2]
    xs_hbm, pos_smem0, pos_smem1, stage, pos_sem, row_sem = refs[-6:]
    i = pl.program_id(0)
    n = pl.num_programs(0)
    slot = lax.rem(i, 2)
    nidx = tm * TOP_K

    def pos_copy(tile, sl):
        return pltpu.make_async_copy(pos_hbm.at[pl.ds(pl.multiple_of(tile * nidx, nidx), nidx)],
                                     (pos_smem0, pos_smem1)[sl], pos_sem.at[sl])

    def drain(sl):
        for _ in range(TOP_K):
            pltpu.make_async_copy(stage.at[sl], stage.at[sl], row_sem.at[sl]).wait()

    @pl.when(i == 0)
    def _():
        pos_copy(0, 0).start()

    for sl in range(2):
        @pl.when(jnp.logical_and(i + 1 < n, slot == 1 - sl))
        def _():
            pos_copy(i + 1, sl).start()

    @pl.when(i >= 2)
    def _():
        drain(slot)

    stage[slot] = h_ref[...]

    for sl in range(2):
        @pl.when(slot == sl)
        def _():
            pos_copy(i, sl).wait()
            pos_smem = (pos_smem0, pos_smem1)[sl]

            def issue(c, carry):
                for rr in range(ISSUE_UNROLL):
                    r = c * ISSUE_UNROLL + rr
                    for k in range(TOP_K):
                        pltpu.make_async_copy(_row_tile(stage.at[sl], r), _row_tile(xs_hbm, pos_smem[k * tm + r]),
                                              row_sem.at[sl]).start(priority=k % DMA_QUEUES)
                return carry
            lax.fori_loop(0, tm // ISSUE_UNROLL, issue, 0)

    @pl.when(i == n - 1)
    def _():
        drain(slot)

        @pl.when(n >= 2)
        def _():
            drain(1 - slot)


def _dispatch(pos_flat, h_tiles, xs=None, n_rows=None):
    t = h_tiles.shape[0] // SUBLANES
    tm = GATHER_TILE
    assert t % tm == 0 and tm % ISSUE_UNROLL == 0
    in_specs = [pl.BlockSpec(memory_space=pl.ANY), pl.BlockSpec((tm * SUBLANES, LANES), lambda i: (i, 0))]
    args = [pos_flat, h_tiles]
    if xs is not None:
        in_specs.append(pl.BlockSpec(memory_space=pl.ANY))
        args.append(xs)
    return pl.pallas_call(
        functools.partial(_dispatch_kernel, tm=tm),
        grid=(t // tm,),
        in_specs=in_specs,
        out_specs=pl.BlockSpec(memory_space=pl.ANY),
        out_shape=jax.ShapeDtypeStruct((n_rows * SUBLANES, LANES) if xs is None else xs.shape, F32),
        scratch_shapes=[pltpu.SMEM((tm * TOP_K,), I32), pltpu.SMEM((tm * TOP_K,), I32),
                        pltpu.VMEM((2, tm * SUBLANES, LANES), F32),
                        pltpu.SemaphoreType.DMA((2,)), pltpu.SemaphoreType.DMA((2,))],
        input_output_aliases={} if xs is None else {2: 0},
        compiler_params=_cparams(("arbitrary",)),
        name="moe_dispatch",
    )(*args)


def _pad_fill_kernel(pos_hbm, xs_in_hbm, xs_hbm, pos_smem, zero_tile, pos_sem, row_sem, *, nidx):
    del xs_in_hbm
    i = pl.program_id(0)
    cp = pltpu.make_async_copy(pos_hbm.at[pl.ds(pl.multiple_of(i * nidx, nidx), nidx)], pos_smem, pos_sem)
    cp.start()
    zero_tile[...] = jnp.zeros_like(zero_tile)
    cp.wait()

    def issue(c, carry):
        for rr in range(TOP_K * ISSUE_UNROLL):
            r = c * (TOP_K * ISSUE_UNROLL) + rr
            pltpu.make_async_copy(zero_tile, _row_tile(xs_hbm, pos_smem[r]), row_sem).start(
                priority=rr % DMA_QUEUES)
        return carry
    lax.fori_loop(0, nidx // (TOP_K * ISSUE_UNROLL), issue, 0)
    whole = xs_hbm.at[pl.ds(0, nidx * SUBLANES), :]
    pltpu.make_async_copy(whole, whole, row_sem).wait()


def _pad_fill(pad_pos, xs):
    nidx = GATHER_TILE * TOP_K
    assert pad_pos.shape[0] % nidx == 0
    return pl.pallas_call(
        functools.partial(_pad_fill_kernel, nidx=nidx),
        grid=(pad_pos.shape[0] // nidx,),
        in_specs=[pl.BlockSpec(memory_space=pl.ANY), pl.BlockSpec(memory_space=pl.ANY)],
        out_specs=pl.BlockSpec(memory_space=pl.ANY),
        out_shape=jax.ShapeDtypeStruct(xs.shape, xs.dtype),
        scratch_shapes=[pltpu.SMEM((nidx,), I32), pltpu.VMEM((SUBLANES, LANES), F32),
                        pltpu.SemaphoreType.DMA, pltpu.SemaphoreType.DMA],
        input_output_aliases={1: 0},
        compiler_params=_cparams(("arbitrary",)),
        name="moe_pad_fill",
    )(pad_pos, xs)


def _moe_kernel(be_ref, nu_ref, xs_ref, wgu_ref, bgu_ref, wd_ref, bd_ref, ys_ref, wgu_bf, wd_bf):
    i = pl.program_id(0)
    d, ff = wgu_ref.shape[1], wd_ref.shape[1]

    @pl.when(jnp.logical_and(i < nu_ref[0],
                             jnp.logical_or(i == 0, be_ref[i] != be_ref[jnp.maximum(i - 1, 0)])))
    def _():
        for r in range(0, d, LANES):
            wgu_bf[r:r + LANES, :] = wgu_ref[0, r:r + LANES, :].astype(BF16)
        for r in range(0, ff, LANES):
            wd_bf[r:r + LANES, :] = wd_ref[0, r:r + LANES, :].astype(BF16)

    @pl.when(i < nu_ref[0])
    def _():
        x = _load_row_tiles(xs_ref, MOE_BLK).astype(BF16)
        glu = _dot(x, wgu_bf[:, 0:ff]) + bgu_ref[0, :, 0:ff]
        lin = _dot(x, wgu_bf[:, ff:2 * ff]) + bgu_ref[0, :, ff:2 * ff]
        glu = jnp.minimum(glu, SWIGLU_LIMIT)
        lin = jnp.clip(lin, -SWIGLU_LIMIT, SWIGLU_LIMIT)
        act = glu * jax.nn.sigmoid(SWIGLU_ALPHA * glu) * (lin + 1.0)
        _store_row_tiles(ys_ref, _dot(act.astype(BF16), wd_bf[...]) + bd_ref[0])


def _moe_mlp(block_expert, n_used, xs, w_gu, b_gu, w_down, b_down):
    ne, d, ff2 = w_gu.shape
    ff = ff2 // 2
    tb = MOE_BLK
    n_rows = block_expert.shape[0] * tb
    assert xs.shape[0] >= n_rows * SUBLANES
    blk = lambda i, be, nu: jnp.minimum(i, nu[0] - 1)
    grid_spec = pltpu.PrefetchScalarGridSpec(
        num_scalar_prefetch=2,
        grid=(n_rows // tb,),
        in_specs=[pl.BlockSpec((tb * SUBLANES, LANES), lambda i, be, nu: (blk(i, be, nu), 0)),
                  pl.BlockSpec((1, d, ff2), lambda i, be, nu: (be[blk(i, be, nu)], 0, 0)),
                  pl.BlockSpec((1, 1, ff2), lambda i, be, nu: (be[blk(i, be, nu)], 0, 0)),
                  pl.BlockSpec((1, ff, d), lambda i, be, nu: (be[blk(i, be, nu)], 0, 0)),
                  pl.BlockSpec((1, 1, d), lambda i, be, nu: (be[blk(i, be, nu)], 0, 0))],
        out_specs=pl.BlockSpec((tb * SUBLANES, LANES), lambda i, be, nu: (blk(i, be, nu), 0)),
        scratch_shapes=[pltpu.VMEM((d, ff2), BF16), pltpu.VMEM((ff, d), BF16)],
    )
    return pl.pallas_call(
        _moe_kernel,
        grid_spec=grid_spec,
        out_shape=jax.ShapeDtypeStruct((n_rows * SUBLANES, LANES), F32),
        compiler_params=_cparams(("arbitrary",)),
        name="moe_mlp",
    )(block_expert, n_used, xs, w_gu, b_gu.reshape(ne, 1, ff2), w_down, b_down.reshape(ne, 1, d))


def _combine_kernel(pos_hbm, ys_hbm, x1_ref, tg_ref, mod_ref, y_ref, pos_smem0, pos_smem1, buf, pos_sem, row_sem,
                    *, tm, nb):
    i = pl.program_id(0)
    n = pl.num_programs(0)
    slot = lax.rem(i, 2)
    nidx = tm * TOP_K
    rows = tm // nb
    chunk_rows = tm // SUBLANES

    def pos_copy(tile, sl):
        return pltpu.make_async_copy(pos_hbm.at[pl.ds(pl.multiple_of(tile * nidx, nidx), nidx)],
                                     (pos_smem0, pos_smem1)[sl], pos_sem.at[sl])

    def issue_row(pos_smem, sl, r):
        for k in range(TOP_K):
            pltpu.make_async_copy(_row_tile(ys_hbm, pos_smem[k * tm + r]), _row_tile(buf.at[sl, k], r),
                                  row_sem.at[sl]).start(priority=k % DMA_QUEUES)

    def wait_rows(sl):
        for k in range(TOP_K):
            pltpu.make_async_copy(buf.at[sl, k], buf.at[sl, k], row_sem.at[sl]).wait()

    @pl.when(i == 0)
    def _():
        pos_copy(0, 0).start()
        pos_copy(0, 0).wait()

        def issue(c, carry):
            for rr in range(ISSUE_UNROLL):
                issue_row(pos_smem0, 0, c * ISSUE_UNROLL + rr)
            return carry
        lax.fori_loop(0, tm // ISSUE_UNROLL, issue, 0)
        pos_copy(jnp.minimum(1, n - 1), 1).start()

    for sl in range(2):
        @pl.when(slot == sl)
        def _():
            nxt = 1 - sl
            pos_next = (pos_smem0, pos_smem1)[nxt]
            pos_copy(jnp.minimum(i + 1, n - 1), nxt).wait()
            wait_rows(sl)
            tg = tg_ref[...].T
            for c in range(SUBLANES):
                cols = slice(c * LANES, (c + 1) * LANES)
                ff = tg[:, 0:1] * buf[sl, 0, pl.ds(c, tm, stride=SUBLANES), :]
                for k in range(1, TOP_K):
                    ff = ff + tg[:, k:k + 1] * buf[sl, k, pl.ds(c, tm, stride=SUBLANES), :]
                for g in range(nb):
                    rs = slice(g * rows, (g + 1) * rows)
                    y_ref[rs, cols] = x1_ref[rs, cols] + mod_ref[g, 5:6, cols] * ff[rs, :]
                for r in range(c * chunk_rows, (c + 1) * chunk_rows):
                    issue_row(pos_next, nxt, r)
            pos_copy(jnp.minimum(i + 2, n - 1), sl).start()

            @pl.when(i == n - 1)
            def _():
                wait_rows(nxt)
                pos_copy(n - 1, sl).wait()


def _combine(pos_flat, ys, x1, top_gate, mod, tokens_per_batch):
    t, d = x1.shape
    tm = GATHER_TILE
    assert t % tm == 0 and tm % ISSUE_UNROLL == 0 and d == SUBLANES * LANES
    nb, mod_spec = _mod_spec(tokens_per_batch, tm, d)
    return pl.pallas_call(
        functools.partial(_combine_kernel, tm=tm, nb=nb),
        grid=(t // tm,),
        in_specs=[pl.BlockSpec(memory_space=pl.ANY), pl.BlockSpec(memory_space=pl.ANY),
                  pl.BlockSpec((tm, d), lambda i: (i, 0)), pl.BlockSpec((SUBLANES, tm), lambda i: (0, i)), mod_spec],
        out_specs=pl.BlockSpec((tm, d), lambda i: (i, 0)),
        out_shape=jax.ShapeDtypeStruct((t, d), F32),
        scratch_shapes=[pltpu.SMEM((tm * TOP_K,), I32), pltpu.SMEM((tm * TOP_K,), I32),
                        pltpu.VMEM((2, TOP_K, tm * SUBLANES, LANES), F32),
                        pltpu.SemaphoreType.DMA((2,)), pltpu.SemaphoreType.DMA((2,))],
        compiler_params=_cparams(("arbitrary",)),
        name="moe_combine",
    )(pos_flat, ys, x1, top_gate, mod)


def _pad_lanes(a, fill=0.0):
    return jnp.pad(a, [(0, 0)] * (a.ndim - 1) + [(0, LANES - a.shape[-1])], constant_values=fill)


def _hi_lo(w):
    hi = w.astype(BF16)
    return hi, (w - hi.astype(F32)).astype(BF16)


def _mixer(x, mod, k_past, v_past, conv_past, c0, n0, m0, p):
    b, s, d = x.shape
    outs = _inproj(x.reshape(b * s, d), mod, s, p["g1"], p["w_main"], p["wg_hi"], p["wg_lo"], p["bg"], p["g_k"],
                   p["widths"])
    q_raw, k_norm, v, ml_qk, ml_v, ml_o, gates = [a.reshape(b, s, a.shape[-1]) for a in outs[:7]]
    k_tiles, v_tiles = outs[7:]
    o_sb = _sb_attention(q_raw, k_norm, v, k_past, v_past, p["g_q"])
    o_ml, c_new, n_new, m_new, conv_new = _mlstm(
        ml_qk, ml_v, ml_o, gates, conv_past, c0, n0, _pad_lanes(m0)[:, None, :],
        p["w_conv"], p["b_conv"], p["g_head"])
    flat = lambda a: a.reshape(b * s, a.shape[-1])
    x1, h2, ti, tg = _outproj(flat(o_sb), flat(o_ml), flat(x), mod, s, p["w_out1"], p["w_out2"], p["g2"],
                              p["wrt_hi"], p["wrt_lo"], p["brt"])
    assert v.shape[-1] == SUBLANES * SB_HEAD_DIM
    state = (k_tiles.reshape(b, s, SUBLANES, SB_HEAD_DIM), v_tiles.reshape(b, s, SUBLANES, SB_HEAD_DIM), conv_new,
             c_new, n_new, m_new[:, 0, :ML_HEADS])
    return x1, h2, ti, tg, state


def _layer(xp, xs_, cp, cs, kc, vc, convc, cc, nc, mc, w_ada, b_ada, g_norm1, w_in, g_q, g_k, w_conv, b_conv,
           b_gate, g_head, w_out, g_norm2, w_router, b_router, w_gu, b_gu, w_down, b_down):
    d = xp.shape[-1]
    bp, sp = xp.shape[:2]
    bs, ss = xs_.shape[:2]
    sbw = kc.shape[-2] * kc.shape[-1]
    mlw = ML_HEADS * ML_HEAD_DIM
    ngate = 2 * ML_HEADS
    gate0 = 3 * sbw + 3 * mlw
    w_main = jnp.concatenate([w_in[:, :gate0], w_in[:, gate0 + ngate:]], axis=1).astype(BF16)
    wg_hi, wg_lo = _hi_lo(_pad_lanes(w_in[:, gate0:gate0 + ngate]))
    wrt_hi, wrt_lo = _hi_lo(_pad_lanes(w_router).T)
    p = dict(
        widths=(sbw, sbw, sbw, 2 * mlw, mlw, mlw),
        g1=g_norm1[None], w_main=w_main, wg_hi=wg_hi, wg_lo=wg_lo, bg=_pad_lanes(b_gate)[None],
        g_q=jnp.tile(g_q, LANES // SB_HEAD_DIM)[None], g_k=jnp.tile(g_k, LANES // SB_HEAD_DIM)[None],
        w_conv=w_conv, b_conv=b_conv[None], g_head=g_head[:, None],
        w_out1=w_out[:sbw].astype(BF16), w_out2=w_out[sbw:].astype(BF16), g2=g_norm2[None],
        wrt_hi=wrt_hi, wrt_lo=wrt_lo, brt=b_router[:, None],
    )
    mod = _ada(jnp.concatenate([cp, cs], axis=0), w_ada, b_ada)
    mod_p = mod[:bp].reshape(bp, 6, d)
    mod_s = mod[bp:].reshape(bs, 6, d)

    zeros = lambda *shp: jnp.zeros(shp, F32)
    feature_major = lambda a: jnp.transpose(a, (0, 2, 3, 1)).reshape(a.shape[0], sbw, a.shape[1])
    x1p, h2p, tip, tgp, st_p = _mixer(
        xp, mod_p, None, None, zeros(bp, CONV_W - 1, 2 * mlw), zeros(bp, ML_HEADS, ML_HEAD_DIM, ML_HEAD_DIM),
        zeros(bp, ML_HEADS, ML_HEAD_DIM), zeros(bp, ML_HEADS), p)
    x1s, h2s, tis, tgs, st_s = _mixer(
        xs_, mod_s, feature_major(kc), feature_major(vc), convc, cc, nc, mc, p)

    tp, ts = bp * sp, bs * ss
    pos, counts = _positions(jnp.concatenate([tip, tis], axis=1))
    counts = counts[:, 0]
    padded = (counts + MOE_BLK - 1) // MOE_BLK * MOE_BLK
    pad_end = jnp.cumsum(padded)
    pad_start = pad_end - padded
    n_blocks = -(-((tp + ts) * TOP_K + N_EXPERTS * (MOE_BLK - 1)) // MOE_BLK)
    n_used = (pad_end[-1] // MOE_BLK).astype(I32).reshape(1)
    block_start = jnp.arange(n_blocks, dtype=I32) * MOE_BLK
    block_expert = jnp.minimum(jnp.sum(block_start[:, None] >= pad_end[None, :], axis=1), N_EXPERTS - 1).astype(I32)

    def per_tile(rows):
        t = rows.shape[1]
        return rows.reshape(TOP_K, t // GATHER_TILE, GATHER_TILE).transpose(1, 0, 2).reshape(-1)
    pos_p = per_tile(pos[:TOP_K, :tp])
    pos_s = per_tile(pos[:TOP_K, tp:])

    n_rows = n_blocks * MOE_BLK
    slot = jnp.arange(MOE_BLK, dtype=I32)[None, :]
    spare = n_rows + jnp.arange(N_EXPERTS * MOE_BLK, dtype=I32).reshape(N_EXPERTS, MOE_BLK)
    pad_pos = jnp.where(slot < (padded - counts)[:, None], (pad_start + counts)[:, None] + slot, spare).reshape(-1)

    xs_buf = _dispatch(pos_p, h2p, n_rows=n_rows + N_EXPERTS * MOE_BLK)
    xs_buf = _dispatch(pos_s, h2s, xs_buf)
    xs_buf = _pad_fill(pad_pos.astype(I32), xs_buf)
    ys = _moe_mlp(block_expert, n_used, xs_buf, w_gu, b_gu, w_down, b_down)
    yp = _combine(pos_p, ys, x1p, tgp, mod_p, sp).reshape(bp, sp, d)
    ysm = _combine(pos_s, ys, x1s, tgs, mod_s, ss).reshape(bs, ss, d)
    return yp, ysm, st_p, st_s


def kernel(x_prompt, x_sample, c_prompt, c_sample, cache_sb_k, cache_sb_v, state_conv, state_mlstm_c, state_mlstm_n, state_mlstm_m, w_ada, b_ada, g_norm1, w_in, g_q, g_k, w_conv, b_conv, b_gate, g_head, w_out, g_norm2, w_router, b_router, w_gu, b_gu, w_down, b_down):
    assert w_ada.shape[0] == 1, "single-layer step"
    yp, ys, st_p, st_s = _layer(
        x_prompt, x_sample, c_prompt, c_sample, cache_sb_k[0], cache_sb_v[0], state_conv[0], state_mlstm_c[0],
        state_mlstm_n[0], state_mlstm_m[0], w_ada[0], b_ada[0], g_norm1[0], w_in[0], g_q[0], g_k[0], w_conv[0],
        b_conv[0], b_gate[0], g_head[0], w_out[0], g_norm2[0], w_router[0], b_router[0], w_gu[0], b_gu[0],
        w_down[0], b_down[0])
    return (yp, ys) + tuple(a[None] for a in st_p) + tuple(a[None] for a in st_s)
```

```python
import functools
import math

import jax
import jax.numpy as jnp
from jax import lax
from jax.experimental import pallas as pl
from jax.experimental.pallas import tpu as pltpu

F32 = jnp.float32
BF16 = jnp.bfloat16
I32 = jnp.int32

EPS = 1e-6
LANES = 128
SUBLANES = 8
SB_HEAD_DIM = 64
ML_HEAD_DIM = 128
ML_HEADS = 4
CONV_W = 4
N_EXPERTS = 32
TOP_K = 4
SWIGLU_LIMIT = 7.0
SWIGLU_ALPHA = 1.702
VMEM_LIMIT = 56 * 1024 * 1024

ROW_TILE = 512
MOE_BLK = 512
GATHER_TILE = 512
ISSUE_UNROLL = 8
DMA_QUEUES = 2
ML_CHUNK = 256
SB_BLK = 128
SB_SWEEP = 256
SB_UNDERFLOW = -105.0


def _cparams(sem):
    return pltpu.CompilerParams(dimension_semantics=sem, vmem_limit_bytes=VMEM_LIMIT)


def _split3(x):
    p1 = x.astype(BF16)
    r1 = x - p1.astype(F32)
    p2 = r1.astype(BF16)
    p3 = (r1 - p2.astype(F32)).astype(BF16)
    return p1, p2, p3


def _dot(a, b):
    return jnp.dot(a, b, preferred_element_type=F32)


def _dot_nt(a, b):
    return lax.dot_general(a, b, (((1,), (1,)), ((), ())), preferred_element_type=F32)


def _dot3(x, w_hi, w_lo):
    xh = x.astype(BF16)
    xl = (x - xh.astype(F32)).astype(BF16)
    return _dot(xh, w_hi) + _dot(xl, w_hi) + _dot(xh, w_lo)


def _store_row_tiles(ref, x):
    n, d = x.shape
    assert d == SUBLANES * LANES
    for c in range(SUBLANES):
        ref[pl.ds(c, n, stride=SUBLANES), :] = x[:, c * LANES:(c + 1) * LANES]


def _load_row_tiles(ref, n):
    return jnp.concatenate([ref[pl.ds(c, n, stride=SUBLANES), :] for c in range(SUBLANES)], axis=1)


def _store_head_tiles(ref, row0, head0, x):
    n = x.shape[0]
    for j in range(x.shape[1] // SB_HEAD_DIM):
        ref[pl.ds(row0 * SUBLANES + head0 + j, n, stride=SUBLANES), :] = x[:, j * SB_HEAD_DIM:(j + 1) * SB_HEAD_DIM]


def _headnorm(x, g):
    lo_half = lax.broadcasted_iota(I32, (1, LANES), 1) < SB_HEAD_DIM
    x2 = x * x
    s0 = jnp.sum(jnp.where(lo_half, x2, 0.0), axis=-1, keepdims=True)
    s1 = jnp.sum(jnp.where(lo_half, 0.0, x2), axis=-1, keepdims=True)
    r = jnp.where(lo_half, lax.rsqrt(s0 * (1.0 / SB_HEAD_DIM) + EPS), lax.rsqrt(s1 * (1.0 / SB_HEAD_DIM) + EPS))
    return x * r * g


def _mod_rows(mod_ref, j, nb, shape):
    if nb == 1:
        return mod_ref[0, j:j + 1, :]
    rows = shape[0] // nb
    return jnp.concatenate([jnp.broadcast_to(mod_ref[g, j:j + 1, :], (rows, shape[1])) for g in range(nb)], axis=0)


def _mod_spec(tokens_per_batch, tm, d):
    if tokens_per_batch >= tm:
        assert tokens_per_batch % tm == 0
        per = tokens_per_batch // tm
        return 1, pl.BlockSpec((1, 6, d), lambda i: (i // per, 0, 0))
    assert tm % tokens_per_batch == 0
    nb = tm // tokens_per_batch
    return nb, pl.BlockSpec((nb, 6, d), lambda i: (i, 0, 0))


def _log_sigmoid(x):
    return jnp.minimum(x, 0.0) - jnp.log1p(jnp.exp(-jnp.abs(x)))


def _ada_kernel(c_ref, w_ref, b_ref, o_ref):
    c = c_ref[...]
    s = c * jax.nn.sigmoid(c)
    o_ref[...] = jnp.dot(s, w_ref[...], precision=lax.Precision.HIGHEST,
                         preferred_element_type=F32) + b_ref[...]


def _ada(c, w_ada, b_ada):
    n, d = c.shape
    cols = w_ada.shape[1]
    tn = cols // 6
    return pl.pallas_call(
        _ada_kernel,
        grid=(cols // tn,),
        in_specs=[pl.BlockSpec((n, d), lambda j: (0, 0)),
                  pl.BlockSpec((d, tn), lambda j: (0, j)),
                  pl.BlockSpec((1, tn), lambda j: (0, j))],
        out_specs=pl.BlockSpec((n, tn), lambda j: (0, j)),
        out_shape=jax.ShapeDtypeStruct((n, cols), F32),
        compiler_params=_cparams(("arbitrary",)),
        name="ada_mod",
    )(c, w_ada, b_ada.reshape(1, cols))


def _inproj_kernel(x_ref, mod_ref, g1_ref, w_ref, wgh_ref, wgl_ref, bg_ref, gk_ref,
                   q_ref, k_ref, v_ref, qk_ref, mv_ref, mo_ref, gate_ref, kt_ref, vt_ref, *, widths, nb):
    x = x_ref[...]
    ms = jnp.mean(x * x, axis=-1, keepdims=True)
    h = x * lax.rsqrt(ms + EPS) * g1_ref[...]
    h = h * (1.0 + _mod_rows(mod_ref, 1, nb, x.shape)) + _mod_rows(mod_ref, 0, nb, x.shape)
    hb = h.astype(BF16)
    off = 0
    for ref, wd in zip((q_ref, k_ref, v_ref, qk_ref, mv_ref, mo_ref), widths):
        u = _dot(hb, w_ref[:, off:off + wd])
        if ref is k_ref:
            u = jnp.concatenate([_headnorm(u[:, c:c + LANES], gk_ref[...]) for c in range(0, wd, LANES)], axis=1)
            _store_head_tiles(kt_ref, 0, 0, u)
        if ref is v_ref:
            _store_head_tiles(vt_ref, 0, 0, u)
        ref[...] = u
        off += wd
    gate_ref[...] = _dot3(h, wgh_ref[...], wgl_ref[...]) + bg_ref[...]


def _inproj(x, mod, tokens_per_batch, g1, w_main, wg_hi, wg_lo, bg, g_k, widths):
    t, d = x.shape
    tm = ROW_TILE
    assert t % tm == 0
    ncol = w_main.shape[1]
    nb, mod_spec = _mod_spec(tokens_per_batch, tm, d)
    tok = lambda w: pl.BlockSpec((tm, w), lambda i: (i, 0))
    const = lambda shp: pl.BlockSpec(shp, lambda i: (0,) * len(shp))
    head_tiles = pl.BlockSpec((tm * SUBLANES, SB_HEAD_DIM), lambda i: (i, 0))
    return pl.pallas_call(
        functools.partial(_inproj_kernel, widths=widths, nb=nb),
        grid=(t // tm,),
        in_specs=[tok(d), mod_spec,
                  const((1, d)), const((d, ncol)), const((d, LANES)), const((d, LANES)), const((1, LANES)),
                  const((1, LANES))],
        out_specs=[tok(w) for w in widths] + [tok(LANES), head_tiles, head_tiles],
        out_shape=[jax.ShapeDtypeStruct((t, w), F32) for w in widths]
        + [jax.ShapeDtypeStruct((t, LANES), F32)]
        + [jax.ShapeDtypeStruct((t * SUBLANES, SB_HEAD_DIM), F32)] * 2,
        compiler_params=_cparams(("arbitrary",)),
        name="inproj",
    )(x, mod, g1, w_main, wg_hi, wg_lo, bg, g_k)


def _sb_kernel(*refs, past, seq, tq, kb, npair):
    ks_scr, vs_scr, run_scr, acc_scr = refs[-4:]
    if past:
        q_ref, k_ref, v_ref, kp_ref, vp_ref, gq_ref, o_ref = refs[:-4]
    else:
        q_ref, k_ref, v_ref, gq_ref, o_ref = refs[:-4]
    i = pl.program_id(1)
    lane = lax.broadcasted_iota(I32, (1, LANES), 1)
    lo_half = lane < SB_HEAD_DIM
    pairs = [slice(hp * LANES, (hp + 1) * LANES) for hp in range(npair)]

    @pl.when(i == 0)
    def _():
        ch = min(256, seq)
        for c in range(seq // ch):
            rows = slice(c * ch, (c + 1) * ch)
            dst = slice(past + c * ch, past + (c + 1) * ch)
            ks_scr[dst, :] = k_ref[0, rows, :].astype(BF16)
            vs_scr[dst, :] = v_ref[0, rows, :].astype(BF16)
        if past:
            pc = min(256, past)
            for src, dst_scr in ((kp_ref, ks_scr), (vp_ref, vs_scr)):
                for c in range(past // pc):
                    dst_scr[c * pc:(c + 1) * pc, :] = src[0, :, c * pc:(c + 1) * pc].T.astype(BF16)

    qm = []
    for ps in pairs:
        q = _headnorm(q_ref[0, :, ps], gq_ref[...]) * (1.0 / math.sqrt(SB_HEAD_DIM))
        qm.append(jnp.concatenate([jnp.where(lo_half, q, 0.0), jnp.where(lo_half, 0.0, q)], axis=0).astype(BF16))
    nrow = 2 * npair * tq

    def neg_cum(nk):
        rj = lax.broadcasted_iota(I32, (2 * nk, LANES + nk), 0)
        cj = lax.broadcasted_iota(I32, (2 * nk, LANES + nk), 1)
        rjm = jnp.where(rj >= nk, rj - nk, rj)
        return jnp.where(cj < LANES, -1.0, jnp.where(rjm >= cj - LANES, -1.0, 0.0)).astype(BF16)

    w_diag = neg_cum(tq)
    w_sub = w_diag if tq == LANES else neg_cum(LANES)
    causal = (lax.broadcasted_iota(I32, (nrow, tq), 1)
              < jnp.bitwise_and(lax.broadcasted_iota(I32, (nrow, tq), 0), tq - 1))

    def mask_diag(a):
        tail = jnp.where(causal, a[:, a.shape[1] - tq:], 0.0)
        return tail if a.shape[1] == tq else jnp.concatenate([a[:, :a.shape[1] - tq], tail], axis=1)

    def sweep(row0, nk, diag):
        kv = [(ks_scr[pl.ds(row0, nk), ps], vs_scr[pl.ds(row0, nk), ps]) for ps in pairs]
        z = jnp.concatenate([_dot_nt(qm[hp], kv[hp][0]) for hp in range(npair)], axis=0)
        sp = jnp.maximum(z, 0.0) + jnp.log(1.0 + jnp.exp(-jnp.abs(z)))
        if diag:
            sp = mask_diag(sp)
        hi = sp.astype(BF16)
        lo = (sp - hi.astype(F32)).astype(BF16)
        sub = min(nk, LANES)
        w_neg = w_diag if sub == tq else w_sub
        run = None if diag else run_scr[...]
        args = [None] * (nk // sub)
        for j in reversed(range(nk // sub)):
            cols = slice(j * sub, (j + 1) * sub)
            t = _dot(jnp.concatenate([hi[:, cols], lo[:, cols]], axis=1), w_neg)
            args[j] = z[:, cols] + t[:, LANES:] if run is None else z[:, cols] + t[:, LANES:] + run[:, :sub]
            run = t[:, :LANES] if run is None else run + t[:, :LANES]
        run_scr[...] = run
        p = jnp.exp(args[0] if len(args) == 1 else jnp.concatenate(args, axis=1))
        if diag:
            p = mask_diag(p)
        p = p.astype(BF16)
        for hp in range(npair):
            r0 = 2 * hp * tq
            vblk = kv[hp][1]
            vm = jnp.concatenate([jnp.where(lo_half, vblk, 0.0), jnp.where(lo_half, 0.0, vblk)], axis=0)
            pv = _dot(jnp.concatenate([p[r0:r0 + tq], p[r0 + tq:r0 + 2 * tq]], axis=1), vm.astype(BF16))
            acc_scr[hp] = pv if diag else acc_scr[hp] + pv
        return jnp.max(run)

    q0 = past + i * tq
    n_before = q0 // kb
    if tq == LANES:
        merged = q0 >= kb
        worst0 = lax.cond(merged,
                          lambda: sweep(pl.multiple_of(q0 - kb, tq), kb + tq, True),
                          lambda: sweep(pl.multiple_of(q0, tq), tq, True))
        done0 = merged.astype(I32)
    else:
        worst0 = sweep(pl.multiple_of(q0, tq), tq, True)
        done0 = jnp.int32(0)

    def cond(c):
        return jnp.logical_and(c[0] < n_before, c[1] > SB_UNDERFLOW)

    def body(c):
        row0 = q0 - (c[0] + 1) * kb
        return c[0] + 1, sweep(pl.multiple_of(row0, tq), kb, False)

    _, worst = lax.while_loop(cond, body, (done0, worst0))
    if tq < kb and seq > tq:
        @pl.when(jnp.logical_and(q0 - n_before * kb > 0, worst > SB_UNDERFLOW))
        def _():
            sweep(0, tq, False)
    for hp, ps in enumerate(pairs):
        o_ref[0, :, ps] = acc_scr[hp]


def _sb_attention(q_raw, k_norm, v, k_past, v_past, g_q):
    b, s, w = q_raw.shape
    past = 0 if k_past is None else k_past.shape[2]
    tq = min(SB_BLK, s)
    kb = SB_SWEEP
    assert s % tq == 0 and past % kb == 0 and w % LANES == 0 and kb % LANES == 0 and (kb == 2 * tq or s == tq)
    npair = w // LANES
    qblk = pl.BlockSpec((1, tq, w), lambda bi, i: (bi, i, 0))
    full = lambda n: pl.BlockSpec((1, n, w), lambda bi, i: (bi, 0, 0))
    gspec = pl.BlockSpec((1, LANES), lambda bi, i: (0, 0))
    in_specs = [qblk, full(s), full(s)]
    args = [q_raw, k_norm, v]
    if past:
        cache = pl.BlockSpec((1, w, past), lambda bi, i: (bi, 0, 0))
        in_specs += [cache, cache]
        args += [k_past, v_past]
    in_specs += [gspec]
    args += [g_q]
    return pl.pallas_call(
        functools.partial(_sb_kernel, past=past, seq=s, tq=tq, kb=kb, npair=npair),
        grid=(b, s // tq),
        in_specs=in_specs,
        out_specs=qblk,
        out_shape=jax.ShapeDtypeStruct((b, s, w), F32),
        scratch_shapes=[pltpu.VMEM((past + s, w), BF16), pltpu.VMEM((past + s, w), BF16),
                        pltpu.VMEM((2 * npair * tq, LANES), F32), pltpu.VMEM((npair, tq, LANES), F32)],
        compiler_params=_cparams(("arbitrary", "arbitrary")),
        name="sb_attention",
    )(*args)


def _mlstm_kernel(qk_ref, v_ref, og_ref, g_ref, cp_ref, c0_ref, n0_ref, m0_ref, wc_ref, bc_ref, gh_ref,
                  out_ref, cout_ref, nout_ref, mout_ref, convout_ref,
                  prev_scr, c_scr, n_scr, m_scr, *, chunk):
    ci = pl.program_id(1)
    nh, dh = ML_HEADS, ML_HEAD_DIM
    width = nh * dh
    L = chunk

    @pl.when(ci == 0)
    def _():
        prev_scr[0:8, :] = jnp.zeros((8, prev_scr.shape[1]), F32)
        prev_scr[8 - (CONV_W - 1):8, :] = cp_ref[0]
        c_scr[...] = c0_ref[0]
        n_scr[...] = n0_ref[0]
        m_scr[...] = m0_ref[0]

    prev_scr[8:8 + L, :] = qk_ref[0]
    wc = wc_ref[...]
    acc = bc_ref[...] + prev_scr[8 - (CONV_W - 1):8 - (CONV_W - 1) + L, :] * wc[0:1]
    for j in range(1, CONV_W):
        o = 8 - (CONV_W - 1) + j
        acc = acc + prev_scr[o:o + L, :] * wc[j:j + 1]
    qk = acc * jax.nn.sigmoid(acc)
    prev_scr[0:8, :] = prev_scr[L:L + 8, :]

    gt = g_ref[0]
    lf = _log_sigmoid(gt)
    gt_t = gt.T
    lf_t = lf.T
    row = lax.broadcasted_iota(I32, (L, L), 0)
    col = lax.broadcasted_iota(I32, (L, L), 1)
    tril = jnp.where(row >= col, 1.0, 0.0).astype(BF16)
    triu = jnp.where(row <= col, 1.0, 0.0).astype(BF16)
    p1, p2, p3 = _split3(lf)
    b_col = _dot(tril, p1) + _dot(tril, p2) + _dot(tril, p3)
    q1, q2, q3 = _split3(lf_t[0:8, :])
    b_row = _dot(q1, triu) + _dot(q2, triu) + _dot(q3, triu)

    lane = lax.broadcasted_iota(I32, (1, LANES), 1)
    m_vec = m_scr[...]
    m_out = m_vec
    for h in range(nh):
        hs = slice(h * dh, (h + 1) * dh)
        qb = qk[:, h * dh:(h + 1) * dh].astype(BF16)
        kh = qk[:, width + h * dh:width + (h + 1) * dh] * (dh ** -0.5)
        kb = kh.astype(BF16)
        v_t = v_ref[0, :, hs].T
        b_r = b_row[nh + h:nh + h + 1, :]
        ig_r = gt_t[h:h + 1, :]
        c_col = gt[:, h:h + 1] - b_col[:, nh + h:nh + h + 1]
        m_prev = m_vec[:, h:h + 1]

        d_t = jnp.where(row <= col, b_r + c_col, -jnp.inf)
        inter = b_r + m_prev
        m_t = jnp.maximum(inter, jnp.max(d_t, axis=0, keepdims=True))
        w_inter = jnp.exp(inter - m_t)
        s_t = _dot_nt(kb, qb) * jnp.exp(d_t - m_t)
        c_h = c_scr[h]
        n_h = n_scr[h:h + 1, :]
        n_hi = n_h.astype(BF16)
        n_lo = (n_h - n_hi.astype(F32)).astype(BF16)
        num_t = _dot(v_t.astype(BF16), s_t.astype(BF16)) + w_inter * _dot_nt(c_h.astype(BF16), qb)
        den = jnp.sum(s_t, axis=0, keepdims=True) + w_inter * (_dot_nt(n_hi, qb) + _dot_nt(n_lo, qb))
        hh_t = num_t / jnp.maximum(jnp.abs(den), jnp.exp(-m_t))

        b_last = b_r[:, L - 1:L]
        g_r = (b_last - b_r) + ig_r
        m_new = jnp.maximum(b_last + m_prev, jnp.max(g_r, axis=1, keepdims=True))
        w_state = jnp.exp(g_r - m_new)
        decay = jnp.exp(b_last + m_prev - m_new)
        c_scr[h] = decay * c_h + _dot((v_t * w_state).astype(BF16), kb)
        w_hi = w_state.astype(BF16)
        w_lo = (w_state - w_hi.astype(F32)).astype(BF16)
        n_scr[h:h + 1, :] = decay * n_h + (_dot(w_hi, kb) + _dot(w_lo, kb))
        m_out = jnp.where(lane == h, m_new, m_out)

        hn_t = hh_t * lax.rsqrt(jnp.mean(hh_t * hh_t, axis=0, keepdims=True) + EPS) * gh_ref[...]
        out_ref[0, :, hs] = jax.nn.sigmoid(og_ref[0, :, hs]) * hn_t.T
    m_scr[...] = m_out

    @pl.when(ci == pl.num_programs(1) - 1)
    def _():
        cout_ref[0] = c_scr[...]
        nout_ref[0] = n_scr[...]
        mout_ref[0] = m_scr[...]
        convout_ref[0] = prev_scr[8 - (CONV_W - 1):8, :]


def _mlstm(ml_qk, ml_v, ml_o, gates, conv_past, c0, n0, m0, w_conv, b_conv, g_head):
    b, s, w2 = ml_qk.shape
    width = w2 // 2
    nh, dh = ML_HEADS, ML_HEAD_DIM
    chunk = min(ML_CHUNK, s)
    assert s % chunk == 0 and chunk >= 8
    tok = lambda w: pl.BlockSpec((1, chunk, w), lambda bi, ci: (bi, ci, 0))
    perb = lambda shp: pl.BlockSpec((1,) + shp, lambda bi, ci: (bi,) + (0,) * len(shp))
    const = lambda shp: pl.BlockSpec(shp, lambda bi, ci: (0,) * len(shp))
    return pl.pallas_call(
        functools.partial(_mlstm_kernel, chunk=chunk),
        grid=(b, s // chunk),
        in_specs=[tok(w2), tok(width), tok(width), tok(LANES),
                  perb((CONV_W - 1, w2)), perb((nh, dh, dh)), perb((nh, dh)), perb((1, LANES)),
                  const((CONV_W, w2)), const((1, w2)), const((dh, 1))],
        out_specs=[tok(width), perb((nh, dh, dh)), perb((nh, dh)), perb((1, LANES)), perb((CONV_W - 1, w2))],
        out_shape=[jax.ShapeDtypeStruct((b, s, width), F32),
                   jax.ShapeDtypeStruct((b, nh, dh, dh), F32),
                   jax.ShapeDtypeStruct((b, nh, dh), F32),
                   jax.ShapeDtypeStruct((b, 1, LANES), F32),
                   jax.ShapeDtypeStruct((b, CONV_W - 1, w2), F32)],
        scratch_shapes=[pltpu.VMEM((8 + chunk, w2), F32), pltpu.VMEM((nh, dh, dh), F32),
                        pltpu.VMEM((nh, dh), F32), pltpu.VMEM((1, LANES), F32)],
        compiler_params=_cparams(("arbitrary", "arbitrary")),
        name="mlstm",
    )(ml_qk, ml_v, ml_o, gates, conv_past, c0, n0, m0, w_conv, b_conv, g_head)


def _outproj_kernel(osb_ref, oml_ref, x_ref, mod_ref, w1_ref, w2_ref, g2_ref, wrh_ref, wrl_ref, br_ref,
                    x1_ref, h2_ref, ti_ref, tg_ref, *, nb):
    tm = x_ref.shape[0]
    per_token = lambda j: _mod_rows(mod_ref, j, nb, x_ref.shape)

    mix = _dot(osb_ref[...].astype(BF16), w1_ref[...]) + _dot(oml_ref[...].astype(BF16), w2_ref[...])
    x1 = x_ref[...] + per_token(2) * mix
    x1_ref[...] = x1
    ms = jnp.mean(x1 * x1, axis=-1, keepdims=True)
    h2 = x1 * lax.rsqrt(ms + EPS) * g2_ref[...]
    h2 = h2 * (1.0 + per_token(4)) + per_token(3)
    _store_row_tiles(h2_ref, h2)

    hb = h2.astype(BF16)
    hl = (h2 - hb.astype(F32)).astype(BF16)
    logits = (_dot_nt(wrh_ref[...], hb) + _dot_nt(wrh_ref[...], hl) + _dot_nt(wrl_ref[...], hb))[0:N_EXPERTS, :]
    cur = logits + br_ref[...]
    erow = lax.broadcasted_iota(I32, cur.shape, 0)
    vals, idxs = [], []
    for _ in range(TOP_K):
        m = jnp.max(cur, axis=0, keepdims=True)
        idx = jnp.min(jnp.where(cur == m, erow, N_EXPERTS), axis=0, keepdims=True)
        vals.append(m)
        idxs.append(idx)
        cur = jnp.where(erow == idx, -jnp.inf, cur)
    es = [jnp.exp(v - vals[0]) for v in vals]
    tot = es[0] + es[1] + es[2] + es[3]
    krow = lax.broadcasted_iota(I32, (SUBLANES, tm), 0)
    ti = jnp.zeros((SUBLANES, tm), I32)
    tg = jnp.zeros((SUBLANES, tm), F32)
    for k in range(TOP_K):
        ti = jnp.where(krow == k, idxs[k], ti)
        tg = jnp.where(krow == k, es[k] / tot, tg)
    ti_ref[...] = ti
    tg_ref[...] = tg


def _outproj(o_sb, o_ml, x, mod, tokens_per_batch, w1, w2, g2, wrt_hi, wrt_lo, brt):
    t, d = x.shape
    hw = o_sb.shape[1]
    tm = ROW_TILE
    assert t % tm == 0
    nb, mod_spec = _mod_spec(tokens_per_batch, tm, d)
    tok = lambda w: pl.BlockSpec((tm, w), lambda i: (i, 0))
    const = lambda shp: pl.BlockSpec(shp, lambda i: (0,) * len(shp))
    rows8 = pl.BlockSpec((SUBLANES, tm), lambda i: (0, i))
    return pl.pallas_call(
        functools.partial(_outproj_kernel, nb=nb),
        grid=(t // tm,),
        in_specs=[tok(hw), tok(hw), tok(d), mod_spec,
                  const((hw, d)), const((hw, d)), const((1, d)),
                  const((LANES, d)), const((LANES, d)), const((N_EXPERTS, 1))],
        out_specs=[tok(d), pl.BlockSpec((tm * SUBLANES, LANES), lambda i: (i, 0)), rows8, rows8],
        out_shape=[jax.ShapeDtypeStruct((t, d), F32), jax.ShapeDtypeStruct((t * SUBLANES, LANES), F32),
                   jax.ShapeDtypeStruct((SUBLANES, t), I32), jax.ShapeDtypeStruct((SUBLANES, t), F32)],
        compiler_params=_cparams(("arbitrary",)),
        name="outproj_router",
    )(o_sb, o_ml, x, mod, w1, w2, g2, wrt_hi, wrt_lo, brt)


def _rank_kernel(ti_ref, pos_ref, cnt_ref, carry_scr, start_scr):
    ph = pl.program_id(0)
    i = pl.program_id(1)
    tm = ti_ref.shape[1]
    rep = lambda a: jnp.concatenate([a] * (tm // LANES), axis=1)

    @pl.when(i == 0)
    def _():
        carry_scr[...] = jnp.zeros_like(carry_scr)

    ti = ti_ref[...]
    erow = lax.broadcasted_iota(I32, (N_EXPERTS, tm), 0)
    sel = [erow == ti[k:k + 1, :] for k in range(TOP_K)]
    hit = jnp.zeros((N_EXPERTS, tm), F32)
    for k in range(TOP_K):
        hit = hit + jnp.where(sel[k], 1.0, 0.0)
    hit = hit.astype(BF16)
    seen = carry_scr[...]

    @pl.when(ph == 1)
    def _():
        earlier = lax.broadcasted_iota(I32, (tm, tm), 0) < lax.broadcasted_iota(I32, (tm, tm), 1)
        row_of = _dot(hit, jnp.where(earlier, 1.0, 0.0).astype(BF16)) + rep(seen) + rep(start_scr[...])
        krow = lax.broadcasted_iota(I32, (SUBLANES, tm), 0)
        pos = jnp.zeros((SUBLANES, tm), F32)
        for k in range(TOP_K):
            pos = jnp.where(krow == k, jnp.sum(jnp.where(sel[k], row_of, 0.0), axis=0, keepdims=True), pos)
        pos_ref[...] = pos.astype(I32)

    carry_scr[...] = seen + _dot(hit, jnp.ones((tm, LANES), BF16))

    @pl.when(jnp.logical_and(ph == 0, i == pl.num_programs(1) - 1))
    def _():
        cnt = carry_scr[...]
        cnt_ref[...] = cnt.astype(I32)
        padded = jnp.floor((cnt + (MOE_BLK - 1)) * (1.0 / MOE_BLK)) * MOE_BLK
        lower = (lax.broadcasted_iota(I32, (N_EXPERTS, N_EXPERTS), 1)
                 < lax.broadcasted_iota(I32, (N_EXPERTS, N_EXPERTS), 0))
        lower = jnp.where(lower, 1.0, 0.0).astype(BF16)
        p1, p2, p3 = _split3(padded)
        start_scr[...] = _dot(lower, p1) + _dot(lower, p2) + _dot(lower, p3)


def _positions(top_idx_t):
    t = top_idx_t.shape[1]
    tm = ROW_TILE
    assert t % tm == 0
    return pl.pallas_call(
        _rank_kernel,
        grid=(2, t // tm),
        in_specs=[pl.BlockSpec((SUBLANES, tm), lambda ph, i: (0, i))],
        out_specs=[pl.BlockSpec((SUBLANES, tm), lambda ph, i: (0, i * ph)),
                   pl.BlockSpec((N_EXPERTS, LANES), lambda ph, i: (0, 0))],
        out_shape=[jax.ShapeDtypeStruct((SUBLANES, t), I32), jax.ShapeDtypeStruct((N_EXPERTS, LANES), I32)],
        scratch_shapes=[pltpu.VMEM((N_EXPERTS, LANES), F32), pltpu.VMEM((N_EXPERTS, LANES), F32)],
        compiler_params=_cparams(("arbitrary", "arbitrary")),
        name="expert_positions",
    )(top_idx_t)


def _row_tile(ref, r):
    return ref.at[pl.ds(pl.multiple_of(r * SUBLANES, SUBLANES), SUBLANES), :]


def _dispatch_kernel(*refs, tm):
    pos_hbm, h_ref = refs[:2]
    xs_hbm, pos_smem0, pos_smem1, stage, pos_sem, row_sem = refs[-6:]
    i = pl.program_id(0)
    n = pl.num_programs(0)
    slot = lax.rem(i, 2)
    nidx = tm * TOP_K

    def pos_copy(tile, sl):
        return pltpu.make_async_copy(pos_hbm.at[pl.ds(pl.multiple_of(tile * nidx, nidx), nidx)],
                                     (pos_smem0, pos_smem1)[sl], pos_sem.at[sl])

    def drain(sl):
        for _ in range(TOP_K):
            pltpu.make_async_copy(stage.at[sl], stage.at[sl], row_sem.at[sl]).wait()

    @pl.when(i == 0)
    def _():
        pos_copy(0, 0).start()

    for sl in range(2):
        @pl.when(jnp.logical_and(i + 1 < n, slot == 1 - sl))
        def _():
            pos_copy(i + 1, sl).start()

    @pl.when(i >= 2)
    def _():
        drain(slot)

    stage[slot] = h_ref[...]

    for sl in range(2):
        @pl.when(slot == sl)
        def _():
            pos_copy(i, sl).wait()
            pos_smem = (pos_smem0, pos_smem1)[sl]

            def issue(c, carry):
                for rr in range(ISSUE_UNROLL):
                    r = c * ISSUE_UNROLL + rr
                    for k in range(TOP_K):
                        pltpu.make_async_copy(_row_tile(stage.at[sl], r), _row_tile(xs_hbm, pos_smem[k * tm + r]),
                                              row_sem.at[sl]).start(priority=k % DMA_QUEUES)
                return carry
            lax.fori_loop(0, tm // ISSUE_UNROLL, issue, 0)

    @pl.when(i == n - 1)
    def _():
        drain(slot)

        @pl.when(n >= 2)
        def _():
            drain(1 - slot)


def _dispatch(pos_flat, h_tiles, xs=None, n_rows=None):
    t = h_tiles.shape[0] // SUBLANES
    tm = GATHER_TILE
    assert t % tm == 0 and tm % ISSUE_UNROLL == 0
    in_specs = [pl.BlockSpec(memory_space=pl.ANY), pl.BlockSpec((tm * SUBLANES, LANES), lambda i: (i, 0))]
    args = [pos_flat, h_tiles]
    if xs is not None:
        in_specs.append(pl.BlockSpec(memory_space=pl.ANY))
        args.append(xs)
    return pl.pallas_call(
        functools.partial(_dispatch_kernel, tm=tm),
        grid=(t // tm,),
        in_specs=in_specs,
        out_specs=pl.BlockSpec(memory_space=pl.ANY),
        out_shape=jax.ShapeDtypeStruct((n_rows * SUBLANES, LANES) if xs is None else xs.shape, F32),
        scratch_shapes=[pltpu.SMEM((tm * TOP_K,), I32), pltpu.SMEM((tm * TOP_K,), I32),
                        pltpu.VMEM((2, tm * SUBLANES, LANES), F32),
                        pltpu.SemaphoreType.DMA((2,)), pltpu.SemaphoreType.DMA((2,))],
        input_output_aliases={} if xs is None else {2: 0},
        compiler_params=_cparams(("arbitrary",)),
        name="moe_dispatch",
    )(*args)


def _pad_fill_kernel(pos_hbm, xs_in_hbm, xs_hbm, pos_smem, zero_tile, pos_sem, row_sem, *, nidx):
    del xs_in_hbm
    i = pl.program_id(0)
    cp = pltpu.make_async_copy(pos_hbm.at[pl.ds(pl.multiple_of(i * nidx, nidx), nidx)], pos_smem, pos_sem)
    cp.start()
    zero_tile[...] = jnp.zeros_like(zero_tile)
    cp.wait()

    def issue(c, carry):
        for rr in range(TOP_K * ISSUE_UNROLL):
            r = c * (TOP_K * ISSUE_UNROLL) + rr
            pltpu.make_async_copy(zero_tile, _row_tile(xs_hbm, pos_smem[r]), row_sem).start(
                priority=rr % DMA_QUEUES)
        return carry
    lax.fori_loop(0, nidx // (TOP_K * ISSUE_UNROLL), issue, 0)
    whole = xs_hbm.at[pl.ds(0, nidx * SUBLANES), :]
    pltpu.make_async_copy(whole, whole, row_sem).wait()


def _pad_fill(pad_pos, xs):
    nidx = GATHER_TILE * TOP_K
    assert pad_pos.shape[0] % nidx == 0
    return pl.pallas_call(
        functools.partial(_pad_fill_kernel, nidx=nidx),
        grid=(pad_pos.shape[0] // nidx,),
        in_specs=[pl.BlockSpec(memory_space=pl.ANY), pl.BlockSpec(memory_space=pl.ANY)],
        out_specs=pl.BlockSpec(memory_space=pl.ANY),
        out_shape=jax.ShapeDtypeStruct(xs.shape, xs.dtype),
        scratch_shapes=[pltpu.SMEM((nidx,), I32), pltpu.VMEM((SUBLANES, LANES), F32),
                        pltpu.SemaphoreType.DMA, pltpu.SemaphoreType.DMA],
        input_output_aliases={1: 0},
        compiler_params=_cparams(("arbitrary",)),
        name="moe_pad_fill",
    )(pad_pos, xs)


def _moe_kernel(be_ref, nu_ref, xs_ref, wgu_ref, bgu_ref, wd_ref, bd_ref, ys_ref, wgu_bf, wd_bf):
    i = pl.program_id(0)
    d, ff = wgu_ref.shape[1], wd_ref.shape[1]

    @pl.when(jnp.logical_and(i < nu_ref[0],
                             jnp.logical_or(i == 0, be_ref[i] != be_ref[jnp.maximum(i - 1, 0)])))
    def _():
        for r in range(0, d, LANES):
            wgu_bf[r:r + LANES, :] = wgu_ref[0, r:r + LANES, :].astype(BF16)
        for r in range(0, ff, LANES):
            wd_bf[r:r + LANES, :] = wd_ref[0, r:r + LANES, :].astype(BF16)

    @pl.when(i < nu_ref[0])
    def _():
        x = _load_row_tiles(xs_ref, MOE_BLK).astype(BF16)
        glu = _dot(x, wgu_bf[:, 0:ff]) + bgu_ref[0, :, 0:ff]
        lin = _dot(x, wgu_bf[:, ff:2 * ff]) + bgu_ref[0, :, ff:2 * ff]
        glu = jnp.minimum(glu, SWIGLU_LIMIT)
        lin = jnp.clip(lin, -SWIGLU_LIMIT, SWIGLU_LIMIT)
        act = glu * jax.nn.sigmoid(SWIGLU_ALPHA * glu) * (lin + 1.0)
        _store_row_tiles(ys_ref, _dot(act.astype(BF16), wd_bf[...]) + bd_ref[0])


def _moe_mlp(block_expert, n_used, xs, w_gu, b_gu, w_down, b_down):
    ne, d, ff2 = w_gu.shape
    ff = ff2 // 2
    tb = MOE_BLK
    n_rows = block_expert.shape[0] * tb
    assert xs.shape[0] >= n_rows * SUBLANES
    blk = lambda i, be, nu: jnp.minimum(i, nu[0] - 1)
    grid_spec = pltpu.PrefetchScalarGridSpec(
        num_scalar_prefetch=2,
        grid=(n_rows // tb,),
        in_specs=[pl.BlockSpec((tb * SUBLANES, LANES), lambda i, be, nu: (blk(i, be, nu), 0)),
                  pl.BlockSpec((1, d, ff2), lambda i, be, nu: (be[blk(i, be, nu)], 0, 0)),
                  pl.BlockSpec((1, 1, ff2), lambda i, be, nu: (be[blk(i, be, nu)], 0, 0)),
                  pl.BlockSpec((1, ff, d), lambda i, be, nu: (be[blk(i, be, nu)], 0, 0)),
                  pl.BlockSpec((1, 1, d), lambda i, be, nu: (be[blk(i, be, nu)], 0, 0))],
        out_specs=pl.BlockSpec((tb * SUBLANES, LANES), lambda i, be, nu: (blk(i, be, nu), 0)),
        scratch_shapes=[pltpu.VMEM((d, ff2), BF16), pltpu.VMEM((ff, d), BF16)],
    )
    return pl.pallas_call(
        _moe_kernel,
        grid_spec=grid_spec,
        out_shape=jax.ShapeDtypeStruct((n_rows * SUBLANES, LANES), F32),
        compiler_params=_cparams(("arbitrary",)),
        name="moe_mlp",
    )(block_expert, n_used, xs, w_gu, b_gu.reshape(ne, 1, ff2), w_down, b_down.reshape(ne, 1, d))


def _combine_kernel(pos_hbm, ys_hbm, x1_ref, tg_ref, mod_ref, y_ref, pos_smem0, pos_smem1, buf, pos_sem, row_sem,
                    *, tm, nb):
    i = pl.program_id(0)
    n = pl.num_programs(0)
    slot = lax.rem(i, 2)
    nidx = tm * TOP_K

    def pos_copy(tile, sl):
        return pltpu.make_async_copy(pos_hbm.at[pl.ds(pl.multiple_of(tile * nidx, nidx), nidx)],
                                     (pos_smem0, pos_smem1)[sl], pos_sem.at[sl])

    def gather(tile, sl):
        pos_copy(tile, sl).wait()
        pos_smem = (pos_smem0, pos_smem1)[sl]

        def issue(c, carry):
            for rr in range(ISSUE_UNROLL):
                r = c * ISSUE_UNROLL + rr
                for k in range(TOP_K):
                    pltpu.make_async_copy(_row_tile(ys_hbm, pos_smem[k * tm + r]),
                                          _row_tile(buf.at[sl, k], r), row_sem.at[sl]).start(
                                              priority=k % DMA_QUEUES)
            return carry
        lax.fori_loop(0, tm // ISSUE_UNROLL, issue, 0)

        @pl.when(tile + 1 < n)
        def _():
            pos_copy(tile + 1, 1 - sl).start()

    @pl.when(i == 0)
    def _():
        pos_copy(0, 0).start()
        gather(0, 0)

    for sl in range(2):
        @pl.when(jnp.logical_and(i + 1 < n, slot == 1 - sl))
        def _():
            gather(i + 1, sl)

    for k in range(TOP_K):
        pltpu.make_async_copy(buf.at[slot, k], buf.at[slot, k], row_sem.at[slot]).wait()

    tg = tg_ref[...].T
    rows = tm // nb
    for c in range(SUBLANES):
        cols = slice(c * LANES, (c + 1) * LANES)
        ff = tg[:, 0:1] * buf[slot, 0, pl.ds(c, tm, stride=SUBLANES), :]
        for k in range(1, TOP_K):
            ff = ff + tg[:, k:k + 1] * buf[slot, k, pl.ds(c, tm, stride=SUBLANES), :]
        for g in range(nb):
            rs = slice(g * rows, (g + 1) * rows)
            y_ref[rs, cols] = x1_ref[rs, cols] + mod_ref[g, 5:6, cols] * ff[rs, :]


def _combine(pos_flat, ys, x1, top_gate, mod, tokens_per_batch):
    t, d = x1.shape
    tm = GATHER_TILE
    assert t % tm == 0 and tm % ISSUE_UNROLL == 0 and d == SUBLANES * LANES
    nb, mod_spec = _mod_spec(tokens_per_batch, tm, d)
    return pl.pallas_call(
        functools.partial(_combine_kernel, tm=tm, nb=nb),
        grid=(t // tm,),
        in_specs=[pl.BlockSpec(memory_space=pl.ANY), pl.BlockSpec(memory_space=pl.ANY),
                  pl.BlockSpec((tm, d), lambda i: (i, 0)), pl.BlockSpec((SUBLANES, tm), lambda i: (0, i)), mod_spec],
        out_specs=pl.BlockSpec((tm, d), lambda i: (i, 0)),
        out_shape=jax.ShapeDtypeStruct((t, d), F32),
        scratch_shapes=[pltpu.SMEM((tm * TOP_K,), I32), pltpu.SMEM((tm * TOP_K,), I32),
                        pltpu.VMEM((2, TOP_K, tm * SUBLANES, LANES), F32),
                        pltpu.SemaphoreType.DMA((2,)), pltpu.SemaphoreType.DMA((2,))],
        compiler_params=_cparams(("arbitrary",)),
        name="moe_combine",
    )(pos_flat, ys, x1, top_gate, mod)


def _pad_lanes(a, fill=0.0):
    return jnp.pad(a, [(0, 0)] * (a.ndim - 1) + [(0, LANES - a.shape[-1])], constant_values=fill)


def _hi_lo(w):
    hi = w.astype(BF16)
    return hi, (w - hi.astype(F32)).astype(BF16)


def _mixer(x, mod, k_past, v_past, conv_past, c0, n0, m0, p):
    b, s, d = x.shape
    outs = _inproj(x.reshape(b * s, d), mod, s, p["g1"], p["w_main"], p["wg_hi"], p["wg_lo"], p["bg"], p["g_k"],
                   p["widths"])
    q_raw, k_norm, v, ml_qk, ml_v, ml_o, gates = [a.reshape(b, s, a.shape[-1]) for a in outs[:7]]
    k_tiles, v_tiles = outs[7:]
    o_sb = _sb_attention(q_raw, k_norm, v, k_past, v_past, p["g_q"])
    o_ml, c_new, n_new, m_new, conv_new = _mlstm(
        ml_qk, ml_v, ml_o, gates, conv_past, c0, n0, _pad_lanes(m0)[:, None, :],
        p["w_conv"], p["b_conv"], p["g_head"])
    flat = lambda a: a.reshape(b * s, a.shape[-1])
    x1, h2, ti, tg = _outproj(flat(o_sb), flat(o_ml), flat(x), mod, s, p["w_out1"], p["w_out2"], p["g2"],
                              p["wrt_hi"], p["wrt_lo"], p["brt"])
    assert v.shape[-1] == SUBLANES * SB_HEAD_DIM
    state = (k_tiles.reshape(b, s, SUBLANES, SB_HEAD_DIM), v_tiles.reshape(b, s, SUBLANES, SB_HEAD_DIM), conv_new,
             c_new, n_new, m_new[:, 0, :ML_HEADS])
    return x1, h2, ti, tg, state


def _layer(xp, xs_, cp, cs, kc, vc, convc, cc, nc, mc, w_ada, b_ada, g_norm1, w_in, g_q, g_k, w_conv, b_conv,
           b_gate, g_head, w_out, g_norm2, w_router, b_router, w_gu, b_gu, w_down, b_down):
    d = xp.shape[-1]
    bp, sp = xp.shape[:2]
    bs, ss = xs_.shape[:2]
    sbw = kc.shape[-2] * kc.shape[-1]
    mlw = ML_HEADS * ML_HEAD_DIM
    ngate = 2 * ML_HEADS
    gate0 = 3 * sbw + 3 * mlw
    w_main = jnp.concatenate([w_in[:, :gate0], w_in[:, gate0 + ngate:]], axis=1).astype(BF16)
    wg_hi, wg_lo = _hi_lo(_pad_lanes(w_in[:, gate0:gate0 + ngate]))
    wrt_hi, wrt_lo = _hi_lo(_pad_lanes(w_router).T)
    p = dict(
        widths=(sbw, sbw, sbw, 2 * mlw, mlw, mlw),
        g1=g_norm1[None], w_main=w_main, wg_hi=wg_hi, wg_lo=wg_lo, bg=_pad_lanes(b_gate)[None],
        g_q=jnp.tile(g_q, LANES // SB_HEAD_DIM)[None], g_k=jnp.tile(g_k, LANES // SB_HEAD_DIM)[None],
        w_conv=w_conv, b_conv=b_conv[None], g_head=g_head[:, None],
        w_out1=w_out[:sbw].astype(BF16), w_out2=w_out[sbw:].astype(BF16), g2=g_norm2[None],
        wrt_hi=wrt_hi, wrt_lo=wrt_lo, brt=b_router[:, None],
    )
    mod = _ada(jnp.concatenate([cp, cs], axis=0), w_ada, b_ada)
    mod_p = mod[:bp].reshape(bp, 6, d)
    mod_s = mod[bp:].reshape(bs, 6, d)

    zeros = lambda *shp: jnp.zeros(shp, F32)
    feature_major = lambda a: jnp.transpose(a, (0, 2, 3, 1)).reshape(a.shape[0], sbw, a.shape[1])
    x1p, h2p, tip, tgp, st_p = _mixer(
        xp, mod_p, None, None, zeros(bp, CONV_W - 1, 2 * mlw), zeros(bp, ML_HEADS, ML_HEAD_DIM, ML_HEAD_DIM),
        zeros(bp, ML_HEADS, ML_HEAD_DIM), zeros(bp, ML_HEADS), p)
    x1s, h2s, tis, tgs, st_s = _mixer(
        xs_, mod_s, feature_major(kc), feature_major(vc), convc, cc, nc, mc, p)

    tp, ts = bp * sp, bs * ss
    pos, counts = _positions(jnp.concatenate([tip, tis], axis=1))
    counts = counts[:, 0]
    padded = (counts + MOE_BLK - 1) // MOE_BLK * MOE_BLK
    pad_end = jnp.cumsum(padded)
    pad_start = pad_end - padded
    n_blocks = -(-((tp + ts) * TOP_K + N_EXPERTS * (MOE_BLK - 1)) // MOE_BLK)
    n_used = (pad_end[-1] // MOE_BLK).astype(I32).reshape(1)
    block_start = jnp.arange(n_blocks, dtype=I32) * MOE_BLK
    block_expert = jnp.minimum(jnp.sum(block_start[:, None] >= pad_end[None, :], axis=1), N_EXPERTS - 1).astype(I32)

    def per_tile(rows):
        t = rows.shape[1]
        return rows.reshape(TOP_K, t // GATHER_TILE, GATHER_TILE).transpose(1, 0, 2).reshape(-1)
    pos_p = per_tile(pos[:TOP_K, :tp])
    pos_s = per_tile(pos[:TOP_K, tp:])

    n_rows = n_blocks * MOE_BLK
    slot = jnp.arange(MOE_BLK, dtype=I32)[None, :]
    spare = n_rows + jnp.arange(N_EXPERTS * MOE_BLK, dtype=I32).reshape(N_EXPERTS, MOE_BLK)
    pad_pos = jnp.where(slot < (padded - counts)[:, None], (pad_start + counts)[:, None] + slot, spare).reshape(-1)

    xs_buf = _dispatch(pos_p, h2p, n_rows=n_rows + N_EXPERTS * MOE_BLK)
    xs_buf = _dispatch(pos_s, h2s, xs_buf)
    xs_buf = _pad_fill(pad_pos.astype(I32), xs_buf)
    ys = _moe_mlp(block_expert, n_used, xs_buf, w_gu, b_gu, w_down, b_down)
    yp = _combine(pos_p, ys, x1p, tgp, mod_p, sp).reshape(bp, sp, d)
    ysm = _combine(pos_s, ys, x1s, tgs, mod_s, ss).reshape(bs, ss, d)
    return yp, ysm, st_p, st_s


def kernel(x_prompt, x_sample, c_prompt, c_sample, cache_sb_k, cache_sb_v, state_conv, state_mlstm_c, state_mlstm_n, state_mlstm_m, w_ada, b_ada, g_norm1, w_in, g_q, g_k, w_conv, b_conv, b_gate, g_head, w_out, g_norm2, w_router, b_router, w_gu, b_gu, w_down, b_down):
    assert w_ada.shape[0] == 1, "single-layer step"
    yp, ys, st_p, st_s = _layer(
        x_prompt, x_sample, c_prompt, c_sample, cache_sb_k[0], cache_sb_v[0], state_conv[0], state_mlstm_c[0],
        state_mlstm_n[0], state_mlstm_m[0], w_ada[0], b_ada[0], g_norm1[0], w_in[0], g_q[0], g_k[0], w_conv[0],
        b_conv[0], b_gate[0], g_head[0], w_out[0], g_norm2[0], w_router[0], b_router[0], w_gu[0], b_gu[0],
        w_down[0], b_down[0])
    return (yp, ys) + tuple(a[None] for a in st_p) + tuple(a[None] for a in st_s)
```

```python
import functools
import math

import jax
import jax.numpy as jnp
from jax import lax
from jax.experimental import pallas as pl
from jax.experimental.pallas import tpu as pltpu

F32 = jnp.float32
BF16 = jnp.bfloat16
I32 = jnp.int32

EPS = 1e-6
LANES = 128
SUBLANES = 8
SB_HEAD_DIM = 64
ML_HEAD_DIM = 128
ML_HEADS = 4
CONV_W = 4
N_EXPERTS = 32
TOP_K = 4
SWIGLU_LIMIT = 7.0
SWIGLU_ALPHA = 1.702
VMEM_LIMIT = 56 * 1024 * 1024

ROW_TILE = 512
POS_TILE = 1024
MOE_BLK = 512
GATHER_TILE = 256
ISSUE_UNROLL = 8
DMA_QUEUES = 2
ML_CHUNK = 256
SB_BLK = 128
SB_SWEEP = 256
SB_UNDERFLOW = -105.0


def _cparams(sem):
    return pltpu.CompilerParams(dimension_semantics=sem, vmem_limit_bytes=VMEM_LIMIT)


def _split3(x):
    p1 = x.astype(BF16)
    r1 = x - p1.astype(F32)
    p2 = r1.astype(BF16)
    p3 = (r1 - p2.astype(F32)).astype(BF16)
    return p1, p2, p3


def _dot(a, b):
    return jnp.dot(a, b, preferred_element_type=F32)


def _dot_nt(a, b):
    return lax.dot_general(a, b, (((1,), (1,)), ((), ())), preferred_element_type=F32)


def _dot3(x, w_hi, w_lo):
    xh = x.astype(BF16)
    xl = (x - xh.astype(F32)).astype(BF16)
    return _dot(xh, w_hi) + _dot(xl, w_hi) + _dot(xh, w_lo)


def _store_row_tiles(ref, x):
    n, d = x.shape
    assert d == SUBLANES * LANES
    for c in range(SUBLANES):
        ref[pl.ds(c, n, stride=SUBLANES), :] = x[:, c * LANES:(c + 1) * LANES]


def _load_row_tiles(ref, n):
    return jnp.concatenate([ref[pl.ds(c, n, stride=SUBLANES), :] for c in range(SUBLANES)], axis=1)


def _store_head_tiles(ref, row0, head0, x):
    n = x.shape[0]
    for j in range(x.shape[1] // SB_HEAD_DIM):
        ref[pl.ds(row0 * SUBLANES + head0 + j, n, stride=SUBLANES), :] = x[:, j * SB_HEAD_DIM:(j + 1) * SB_HEAD_DIM]


def _headnorm(x, g):
    lo_half = lax.broadcasted_iota(I32, (1, LANES), 1) < SB_HEAD_DIM
    x2 = x * x
    s0 = jnp.sum(jnp.where(lo_half, x2, 0.0), axis=-1, keepdims=True)
    s1 = jnp.sum(jnp.where(lo_half, 0.0, x2), axis=-1, keepdims=True)
    r = jnp.where(lo_half, lax.rsqrt(s0 * (1.0 / SB_HEAD_DIM) + EPS), lax.rsqrt(s1 * (1.0 / SB_HEAD_DIM) + EPS))
    return x * r * g


def _mod_rows(mod_ref, j, nb, shape):
    if nb == 1:
        return mod_ref[0, j:j + 1, :]
    rows = shape[0] // nb
    return jnp.concatenate([jnp.broadcast_to(mod_ref[g, j:j + 1, :], (rows, shape[1])) for g in range(nb)], axis=0)


def _mod_spec(tokens_per_batch, tm, d):
    if tokens_per_batch >= tm:
        assert tokens_per_batch % tm == 0
        per = tokens_per_batch // tm
        return 1, pl.BlockSpec((1, 6, d), lambda i: (i // per, 0, 0))
    assert tm % tokens_per_batch == 0
    nb = tm // tokens_per_batch
    return nb, pl.BlockSpec((nb, 6, d), lambda i: (i, 0, 0))


def _log_sigmoid(x):
    return jnp.minimum(x, 0.0) - jnp.log1p(jnp.exp(-jnp.abs(x)))


def _ada_kernel(c_ref, w_ref, b_ref, o_ref):
    c = c_ref[...]
    s = c * jax.nn.sigmoid(c)
    o_ref[...] = jnp.dot(s, w_ref[...], precision=lax.Precision.HIGHEST,
                         preferred_element_type=F32) + b_ref[...]


def _ada(c, w_ada, b_ada):
    n, d = c.shape
    cols = w_ada.shape[1]
    tn = cols // 6
    return pl.pallas_call(
        _ada_kernel,
        grid=(cols // tn,),
        in_specs=[pl.BlockSpec((n, d), lambda j: (0, 0)),
                  pl.BlockSpec((d, tn), lambda j: (0, j)),
                  pl.BlockSpec((1, tn), lambda j: (0, j))],
        out_specs=pl.BlockSpec((n, tn), lambda j: (0, j)),
        out_shape=jax.ShapeDtypeStruct((n, cols), F32),
        compiler_params=_cparams(("arbitrary",)),
        name="ada_mod",
    )(c, w_ada, b_ada.reshape(1, cols))


def _inproj_kernel(x_ref, mod_ref, g1_ref, w_ref, wgh_ref, wgl_ref, bg_ref, gk_ref,
                   q_ref, k_ref, v_ref, qk_ref, mv_ref, mo_ref, gate_ref, kt_ref, vt_ref, *, widths, nb):
    x = x_ref[...]
    ms = jnp.mean(x * x, axis=-1, keepdims=True)
    h = x * lax.rsqrt(ms + EPS) * g1_ref[...]
    h = h * (1.0 + _mod_rows(mod_ref, 1, nb, x.shape)) + _mod_rows(mod_ref, 0, nb, x.shape)
    hb = h.astype(BF16)
    off = 0
    for ref, wd in zip((q_ref, k_ref, v_ref, qk_ref, mv_ref, mo_ref), widths):
        u = _dot(hb, w_ref[:, off:off + wd])
        if ref is k_ref:
            u = jnp.concatenate([_headnorm(u[:, c:c + LANES], gk_ref[...]) for c in range(0, wd, LANES)], axis=1)
            _store_head_tiles(kt_ref, 0, 0, u)
        if ref is v_ref:
            _store_head_tiles(vt_ref, 0, 0, u)
        ref[...] = u
        off += wd
    gate_ref[...] = _dot3(h, wgh_ref[...], wgl_ref[...]) + bg_ref[...]


def _inproj(x, mod, tokens_per_batch, g1, w_main, wg_hi, wg_lo, bg, g_k, widths):
    t, d = x.shape
    tm = ROW_TILE
    assert t % tm == 0
    ncol = w_main.shape[1]
    nb, mod_spec = _mod_spec(tokens_per_batch, tm, d)
    tok = lambda w: pl.BlockSpec((tm, w), lambda i: (i, 0))
    const = lambda shp: pl.BlockSpec(shp, lambda i: (0,) * len(shp))
    head_tiles = pl.BlockSpec((tm * SUBLANES, SB_HEAD_DIM), lambda i: (i, 0))
    return pl.pallas_call(
        functools.partial(_inproj_kernel, widths=widths, nb=nb),
        grid=(t // tm,),
        in_specs=[tok(d), mod_spec,
                  const((1, d)), const((d, ncol)), const((d, LANES)), const((d, LANES)), const((1, LANES)),
                  const((1, LANES))],
        out_specs=[tok(w) for w in widths] + [tok(LANES), head_tiles, head_tiles],
        out_shape=[jax.ShapeDtypeStruct((t, w), F32) for w in widths]
        + [jax.ShapeDtypeStruct((t, LANES), F32)]
        + [jax.ShapeDtypeStruct((t * SUBLANES, SB_HEAD_DIM), F32)] * 2,
        compiler_params=_cparams(("arbitrary",)),
        name="inproj",
    )(x, mod, g1, w_main, wg_hi, wg_lo, bg, g_k)


def _sb_kernel(*refs, past, seq, tq, kb, npair):
    ks_scr, vs_scr, run_scr, acc_scr = refs[-4:]
    if past:
        q_ref, k_ref, v_ref, kp_ref, vp_ref, gq_ref, o_ref = refs[:-4]
    else:
        q_ref, k_ref, v_ref, gq_ref, o_ref = refs[:-4]
    i = pl.program_id(1)
    lane = lax.broadcasted_iota(I32, (1, LANES), 1)
    lo_half = lane < SB_HEAD_DIM
    pairs = [slice(hp * LANES, (hp + 1) * LANES) for hp in range(npair)]

    @pl.when(i == 0)
    def _():
        ch = min(256, seq)
        for c in range(seq // ch):
            rows = slice(c * ch, (c + 1) * ch)
            dst = slice(past + c * ch, past + (c + 1) * ch)
            ks_scr[dst, :] = k_ref[0, rows, :].astype(BF16)
            vs_scr[dst, :] = v_ref[0, rows, :].astype(BF16)
        if past:
            pc = min(256, past)
            for src, dst_scr in ((kp_ref, ks_scr), (vp_ref, vs_scr)):
                for c in range(past // pc):
                    dst_scr[c * pc:(c + 1) * pc, :] = src[0, :, c * pc:(c + 1) * pc].T.astype(BF16)

    qm = []
    for ps in pairs:
        q = _headnorm(q_ref[0, :, ps], gq_ref[...]) * (1.0 / math.sqrt(SB_HEAD_DIM))
        qm.append(jnp.concatenate([jnp.where(lo_half, q, 0.0), jnp.where(lo_half, 0.0, q)], axis=0).astype(BF16))
    nrow = 2 * npair * tq

    def neg_cum(nk):
        rj = lax.broadcasted_iota(I32, (2 * nk, LANES + nk), 0)
        cj = lax.broadcasted_iota(I32, (2 * nk, LANES + nk), 1)
        rjm = jnp.where(rj >= nk, rj - nk, rj)
        return jnp.where(cj < LANES, -1.0, jnp.where(rjm >= cj - LANES, -1.0, 0.0)).astype(BF16)

    w_diag = neg_cum(tq)
    w_sub = w_diag if tq == LANES else neg_cum(LANES)
    causal = (lax.broadcasted_iota(I32, (nrow, tq), 1)
              < jnp.bitwise_and(lax.broadcasted_iota(I32, (nrow, tq), 0), tq - 1))

    def mask_diag(a):
        tail = jnp.where(causal, a[:, a.shape[1] - tq:], 0.0)
        return tail if a.shape[1] == tq else jnp.concatenate([a[:, :a.shape[1] - tq], tail], axis=1)

    def sweep(row0, nk, diag):
        kv = [(ks_scr[pl.ds(row0, nk), ps], vs_scr[pl.ds(row0, nk), ps]) for ps in pairs]
        z = jnp.concatenate([_dot_nt(qm[hp], kv[hp][0]) for hp in range(npair)], axis=0)
        sp = jnp.maximum(z, 0.0) + jnp.log(1.0 + jnp.exp(-jnp.abs(z)))
        if diag:
            sp = mask_diag(sp)
        hi = sp.astype(BF16)
        lo = (sp - hi.astype(F32)).astype(BF16)
        sub = min(nk, LANES)
        w_neg = w_diag if sub == tq else w_sub
        run = None if diag else run_scr[...]
        args = [None] * (nk // sub)
        for j in reversed(range(nk // sub)):
            cols = slice(j * sub, (j + 1) * sub)
            t = _dot(jnp.concatenate([hi[:, cols], lo[:, cols]], axis=1), w_neg)
            args[j] = z[:, cols] + t[:, LANES:] if run is None else z[:, cols] + t[:, LANES:] + run[:, :sub]
            run = t[:, :LANES] if run is None else run + t[:, :LANES]
        run_scr[...] = run
        p = jnp.exp(args[0] if len(args) == 1 else jnp.concatenate(args, axis=1))
        if diag:
            p = mask_diag(p)
        p = p.astype(BF16)
        for hp in range(npair):
            r0 = 2 * hp * tq
            vblk = kv[hp][1]
            vm = jnp.concatenate([jnp.where(lo_half, vblk, 0.0), jnp.where(lo_half, 0.0, vblk)], axis=0)
            pv = _dot(jnp.concatenate([p[r0:r0 + tq], p[r0 + tq:r0 + 2 * tq]], axis=1), vm.astype(BF16))
            acc_scr[hp] = pv if diag else acc_scr[hp] + pv
        return jnp.max(run)

    q0 = past + i * tq
    n_before = q0 // kb
    if tq == LANES:
        merged = q0 >= kb
        worst0 = lax.cond(merged,
                          lambda: sweep(pl.multiple_of(q0 - kb, tq), kb + tq, True),
                          lambda: sweep(pl.multiple_of(q0, tq), tq, True))
        done0 = merged.astype(I32)
    else:
        worst0 = sweep(pl.multiple_of(q0, tq), tq, True)
        done0 = jnp.int32(0)

    def cond(c):
        return jnp.logical_and(c[0] < n_before, c[1] > SB_UNDERFLOW)

    def body(c):
        row0 = q0 - (c[0] + 1) * kb
        return c[0] + 1, sweep(pl.multiple_of(row0, tq), kb, False)

    _, worst = lax.while_loop(cond, body, (done0, worst0))
    if tq < kb and seq > tq:
        @pl.when(jnp.logical_and(q0 - n_before * kb > 0, worst > SB_UNDERFLOW))
        def _():
            sweep(0, tq, False)
    for hp, ps in enumerate(pairs):
        o_ref[0, :, ps] = acc_scr[hp]


def _sb_attention(q_raw, k_norm, v, k_past, v_past, g_q):
    b, s, w = q_raw.shape
    past = 0 if k_past is None else k_past.shape[2]
    tq = min(SB_BLK, s)
    kb = SB_SWEEP
    assert s % tq == 0 and past % kb == 0 and w % LANES == 0 and kb % LANES == 0 and (kb == 2 * tq or s == tq)
    npair = w // LANES
    qblk = pl.BlockSpec((1, tq, w), lambda bi, i: (bi, i, 0))
    full = lambda n: pl.BlockSpec((1, n, w), lambda bi, i: (bi, 0, 0))
    gspec = pl.BlockSpec((1, LANES), lambda bi, i: (0, 0))
    in_specs = [qblk, full(s), full(s)]
    args = [q_raw, k_norm, v]
    if past:
        cache = pl.BlockSpec((1, w, past), lambda bi, i: (bi, 0, 0))
        in_specs += [cache, cache]
        args += [k_past, v_past]
    in_specs += [gspec]
    args += [g_q]
    return pl.pallas_call(
        functools.partial(_sb_kernel, past=past, seq=s, tq=tq, kb=kb, npair=npair),
        grid=(b, s // tq),
        in_specs=in_specs,
        out_specs=qblk,
        out_shape=jax.ShapeDtypeStruct((b, s, w), F32),
        scratch_shapes=[pltpu.VMEM((past + s, w), BF16), pltpu.VMEM((past + s, w), BF16),
                        pltpu.VMEM((2 * npair * tq, LANES), F32), pltpu.VMEM((npair, tq, LANES), F32)],
        compiler_params=_cparams(("arbitrary", "arbitrary")),
        name="sb_attention",
    )(*args)


def _mlstm_kernel(qk_ref, v_ref, og_ref, g_ref, cp_ref, c0_ref, n0_ref, m0_ref, wc_ref, bc_ref, gh_ref,
                  out_ref, cout_ref, nout_ref, mout_ref, convout_ref,
                  prev_scr, c_scr, n_scr, m_scr, *, chunk):
    ci = pl.program_id(1)
    nh, dh = ML_HEADS, ML_HEAD_DIM
    width = nh * dh
    L = chunk

    @pl.when(ci == 0)
    def _():
        prev_scr[0:8, :] = jnp.zeros((8, prev_scr.shape[1]), F32)
        prev_scr[8 - (CONV_W - 1):8, :] = cp_ref[0]
        c_scr[...] = c0_ref[0]
        n_scr[...] = n0_ref[0]
        m_scr[...] = m0_ref[0]

    prev_scr[8:8 + L, :] = qk_ref[0]
    wc = wc_ref[...]
    acc = bc_ref[...] + prev_scr[8 - (CONV_W - 1):8 - (CONV_W - 1) + L, :] * wc[0:1]
    for j in range(1, CONV_W):
        o = 8 - (CONV_W - 1) + j
        acc = acc + prev_scr[o:o + L, :] * wc[j:j + 1]
    qk = acc * jax.nn.sigmoid(acc)
    prev_scr[0:8, :] = prev_scr[L:L + 8, :]

    gt = g_ref[0]
    lf = _log_sigmoid(gt)
    gt_t = gt.T
    lf_t = lf.T
    row = lax.broadcasted_iota(I32, (L, L), 0)
    col = lax.broadcasted_iota(I32, (L, L), 1)
    tril = jnp.where(row >= col, 1.0, 0.0).astype(BF16)
    triu = jnp.where(row <= col, 1.0, 0.0).astype(BF16)
    p1, p2, p3 = _split3(lf)
    b_col = _dot(tril, p1) + _dot(tril, p2) + _dot(tril, p3)
    q1, q2, q3 = _split3(lf_t[0:8, :])
    b_row = _dot(q1, triu) + _dot(q2, triu) + _dot(q3, triu)

    lane = lax.broadcasted_iota(I32, (1, LANES), 1)
    m_vec = m_scr[...]
    m_out = m_vec
    for h in range(nh):
        hs = slice(h * dh, (h + 1) * dh)
        qb = qk[:, h * dh:(h + 1) * dh].astype(BF16)
        kh = qk[:, width + h * dh:width + (h + 1) * dh] * (dh ** -0.5)
        kb = kh.astype(BF16)
        v_t = v_ref[0, :, hs].T
        b_r = b_row[nh + h:nh + h + 1, :]
        ig_r = gt_t[h:h + 1, :]
        c_col = gt[:, h:h + 1] - b_col[:, nh + h:nh + h + 1]
        m_prev = m_vec[:, h:h + 1]

        d_t = jnp.where(row <= col, b_r + c_col, -jnp.inf)
        inter = b_r + m_prev
        m_t = jnp.maximum(inter, jnp.max(d_t, axis=0, keepdims=True))
        w_inter = jnp.exp(inter - m_t)
        s_t = _dot_nt(kb, qb) * jnp.exp(d_t - m_t)
        c_h = c_scr[h]
        n_h = n_scr[h:h + 1, :]
        n_hi = n_h.astype(BF16)
        n_lo = (n_h - n_hi.astype(F32)).astype(BF16)
        num_t = _dot(v_t.astype(BF16), s_t.astype(BF16)) + w_inter * _dot_nt(c_h.astype(BF16), qb)
        den = jnp.sum(s_t, axis=0, keepdims=True) + w_inter * (_dot_nt(n_hi, qb) + _dot_nt(n_lo, qb))
        hh_t = num_t / jnp.maximum(jnp.abs(den), jnp.exp(-m_t))

        b_last = b_r[:, L - 1:L]
        g_r = (b_last - b_r) + ig_r
        m_new = jnp.maximum(b_last + m_prev, jnp.max(g_r, axis=1, keepdims=True))
        w_state = jnp.exp(g_r - m_new)
        decay = jnp.exp(b_last + m_prev - m_new)
        c_scr[h] = decay * c_h + _dot((v_t * w_state).astype(BF16), kb)
        w_hi = w_state.astype(BF16)
        w_lo = (w_state - w_hi.astype(F32)).astype(BF16)
        n_scr[h:h + 1, :] = decay * n_h + (_dot(w_hi, kb) + _dot(w_lo, kb))
        m_out = jnp.where(lane == h, m_new, m_out)

        hn_t = hh_t * lax.rsqrt(jnp.mean(hh_t * hh_t, axis=0, keepdims=True) + EPS) * gh_ref[...]
        out_ref[0, :, hs] = jax.nn.sigmoid(og_ref[0, :, hs]) * hn_t.T
    m_scr[...] = m_out

    @pl.when(ci == pl.num_programs(1) - 1)
    def _():
        cout_ref[0] = c_scr[...]
        nout_ref[0] = n_scr[...]
        mout_ref[0] = m_scr[...]
        convout_ref[0] = prev_scr[8 - (CONV_W - 1):8, :]


def _mlstm(ml_qk, ml_v, ml_o, gates, conv_past, c0, n0, m0, w_conv, b_conv, g_head):
    b, s, w2 = ml_qk.shape
    width = w2 // 2
    nh, dh = ML_HEADS, ML_HEAD_DIM
    chunk = min(ML_CHUNK, s)
    assert s % chunk == 0 and chunk >= 8
    tok = lambda w: pl.BlockSpec((1, chunk, w), lambda bi, ci: (bi, ci, 0))
    perb = lambda shp: pl.BlockSpec((1,) + shp, lambda bi, ci: (bi,) + (0,) * len(shp))
    const = lambda shp: pl.BlockSpec(shp, lambda bi, ci: (0,) * len(shp))
    return pl.pallas_call(
        functools.partial(_mlstm_kernel, chunk=chunk),
        grid=(b, s // chunk),
        in_specs=[tok(w2), tok(width), tok(width), tok(LANES),
                  perb((CONV_W - 1, w2)), perb((nh, dh, dh)), perb((nh, dh)), perb((1, LANES)),
                  const((CONV_W, w2)), const((1, w2)), const((dh, 1))],
        out_specs=[tok(width), perb((nh, dh, dh)), perb((nh, dh)), perb((1, LANES)), perb((CONV_W - 1, w2))],
        out_shape=[jax.ShapeDtypeStruct((b, s, width), F32),
                   jax.ShapeDtypeStruct((b, nh, dh, dh), F32),
                   jax.ShapeDtypeStruct((b, nh, dh), F32),
                   jax.ShapeDtypeStruct((b, 1, LANES), F32),
                   jax.ShapeDtypeStruct((b, CONV_W - 1, w2), F32)],
        scratch_shapes=[pltpu.VMEM((8 + chunk, w2), F32), pltpu.VMEM((nh, dh, dh), F32),
                        pltpu.VMEM((nh, dh), F32), pltpu.VMEM((1, LANES), F32)],
        compiler_params=_cparams(("arbitrary", "arbitrary")),
        name="mlstm",
    )(ml_qk, ml_v, ml_o, gates, conv_past, c0, n0, m0, w_conv, b_conv, g_head)


def _outproj_kernel(osb_ref, oml_ref, x_ref, mod_ref, w1_ref, w2_ref, g2_ref, wrh_ref, wrl_ref, br_ref,
                    x1_ref, h2_ref, ti_ref, tg_ref, *, nb):
    tm = x_ref.shape[0]
    per_token = lambda j: _mod_rows(mod_ref, j, nb, x_ref.shape)

    mix = _dot(osb_ref[...].astype(BF16), w1_ref[...]) + _dot(oml_ref[...].astype(BF16), w2_ref[...])
    x1 = x_ref[...] + per_token(2) * mix
    x1_ref[...] = x1
    ms = jnp.mean(x1 * x1, axis=-1, keepdims=True)
    h2 = x1 * lax.rsqrt(ms + EPS) * g2_ref[...]
    h2 = h2 * (1.0 + per_token(4)) + per_token(3)
    _store_row_tiles(h2_ref, h2)

    hb = h2.astype(BF16)
    hl = (h2 - hb.astype(F32)).astype(BF16)
    logits = (_dot_nt(wrh_ref[...], hb) + _dot_nt(wrh_ref[...], hl) + _dot_nt(wrl_ref[...], hb))[0:N_EXPERTS, :]
    cur = logits + br_ref[...]
    erow = lax.broadcasted_iota(I32, cur.shape, 0)
    vals, idxs = [], []
    for _ in range(TOP_K):
        m = jnp.max(cur, axis=0, keepdims=True)
        idx = jnp.min(jnp.where(cur == m, erow, N_EXPERTS), axis=0, keepdims=True)
        vals.append(m)
        idxs.append(idx)
        cur = jnp.where(erow == idx, -jnp.inf, cur)
    es = [jnp.exp(v - vals[0]) for v in vals]
    tot = es[0] + es[1] + es[2] + es[3]
    krow = lax.broadcasted_iota(I32, (SUBLANES, tm), 0)
    ti = jnp.zeros((SUBLANES, tm), I32)
    tg = jnp.zeros((SUBLANES, tm), F32)
    for k in range(TOP_K):
        ti = jnp.where(krow == k, idxs[k], ti)
        tg = jnp.where(krow == k, es[k] / tot, tg)
    ti_ref[...] = ti
    tg_ref[...] = tg


def _outproj(o_sb, o_ml, x, mod, tokens_per_batch, w1, w2, g2, wrt_hi, wrt_lo, brt):
    t, d = x.shape
    hw = o_sb.shape[1]
    tm = ROW_TILE
    assert t % tm == 0
    nb, mod_spec = _mod_spec(tokens_per_batch, tm, d)
    tok = lambda w: pl.BlockSpec((tm, w), lambda i: (i, 0))
    const = lambda shp: pl.BlockSpec(shp, lambda i: (0,) * len(shp))
    rows8 = pl.BlockSpec((SUBLANES, tm), lambda i: (0, i))
    return pl.pallas_call(
        functools.partial(_outproj_kernel, nb=nb),
        grid=(t // tm,),
        in_specs=[tok(hw), tok(hw), tok(d), mod_spec,
                  const((hw, d)), const((hw, d)), const((1, d)),
                  const((LANES, d)), const((LANES, d)), const((N_EXPERTS, 1))],
        out_specs=[tok(d), pl.BlockSpec((tm * SUBLANES, LANES), lambda i: (i, 0)), rows8, rows8],
        out_shape=[jax.ShapeDtypeStruct((t, d), F32), jax.ShapeDtypeStruct((t * SUBLANES, LANES), F32),
                   jax.ShapeDtypeStruct((SUBLANES, t), I32), jax.ShapeDtypeStruct((SUBLANES, t), F32)],
        compiler_params=_cparams(("arbitrary",)),
        name="outproj_router",
    )(o_sb, o_ml, x, mod, w1, w2, g2, wrt_hi, wrt_lo, brt)


def _rank_kernel(ti_ref, pos_ref, cnt_ref, carry_scr, start_scr):
    ph = pl.program_id(0)
    i = pl.program_id(1)
    tm = ti_ref.shape[1]
    rep = lambda a: jnp.concatenate([a] * (tm // LANES), axis=1)

    @pl.when(i == 0)
    def _():
        carry_scr[...] = jnp.zeros_like(carry_scr)

    ti = ti_ref[...]
    erow = lax.broadcasted_iota(I32, (N_EXPERTS, tm), 0)
    sel = [erow == ti[k:k + 1, :] for k in range(TOP_K)]
    hit = jnp.zeros((N_EXPERTS, tm), F32)
    for k in range(TOP_K):
        hit = hit + jnp.where(sel[k], 1.0, 0.0)
    hit = hit.astype(BF16)
    seen = carry_scr[...]

    @pl.when(ph == 1)
    def _():
        earlier = lax.broadcasted_iota(I32, (tm, tm), 0) < lax.broadcasted_iota(I32, (tm, tm), 1)
        row_of = _dot(hit, jnp.where(earlier, 1.0, 0.0).astype(BF16)) + rep(seen) + rep(start_scr[...])
        krow = lax.broadcasted_iota(I32, (SUBLANES, tm), 0)
        pos = jnp.zeros((SUBLANES, tm), F32)
        for k in range(TOP_K):
            pos = jnp.where(krow == k, jnp.sum(jnp.where(sel[k], row_of, 0.0), axis=0, keepdims=True), pos)
        pos_ref[...] = pos.astype(I32)

    carry_scr[...] = seen + _dot(hit, jnp.ones((tm, LANES), BF16))

    @pl.when(jnp.logical_and(ph == 0, i == pl.num_programs(1) - 1))
    def _():
        cnt = carry_scr[...]
        cnt_ref[...] = cnt.astype(I32)
        padded = jnp.floor((cnt + (MOE_BLK - 1)) * (1.0 / MOE_BLK)) * MOE_BLK
        lower = (lax.broadcasted_iota(I32, (N_EXPERTS, N_EXPERTS), 1)
                 < lax.broadcasted_iota(I32, (N_EXPERTS, N_EXPERTS), 0))
        lower = jnp.where(lower, 1.0, 0.0).astype(BF16)
        p1, p2, p3 = _split3(padded)
        start_scr[...] = _dot(lower, p1) + _dot(lower, p2) + _dot(lower, p3)


def _positions(top_idx_t):
    t = top_idx_t.shape[1]
    tm = POS_TILE
    assert t % tm == 0
    return pl.pallas_call(
        _rank_kernel,
        grid=(2, t // tm),
        in_specs=[pl.BlockSpec((SUBLANES, tm), lambda ph, i: (0, i))],
        out_specs=[pl.BlockSpec((SUBLANES, tm), lambda ph, i: (0, i * ph)),
                   pl.BlockSpec((N_EXPERTS, LANES), lambda ph, i: (0, 0))],
        out_shape=[jax.ShapeDtypeStruct((SUBLANES, t), I32), jax.ShapeDtypeStruct((N_EXPERTS, LANES), I32)],
        scratch_shapes=[pltpu.VMEM((N_EXPERTS, LANES), F32), pltpu.VMEM((N_EXPERTS, LANES), F32)],
        compiler_params=_cparams(("arbitrary", "arbitrary")),
        name="expert_positions",
    )(top_idx_t)


def _row_tile(ref, r):
    return ref.at[pl.ds(pl.multiple_of(r * SUBLANES, SUBLANES), SUBLANES), :]


def _dispatch_kernel(*refs, tm):
    pos_hbm, h_ref = refs[:2]
    xs_hbm, pos_smem0, pos_smem1, stage, pos_sem, row_sem = refs[-6:]
    i = pl.program_id(0)
    n = pl.num_programs(0)
    slot = lax.rem(i, 2)
    nidx = tm * TOP_K

    def pos_copy(tile, sl):
        return pltpu.make_async_copy(pos_hbm.at[pl.ds(pl.multiple_of(tile * nidx, nidx), nidx)],
                                     (pos_smem0, pos_smem1)[sl], pos_sem.at[sl])

    def drain(sl):
        for _ in range(TOP_K):
            pltpu.make_async_copy(stage.at[sl], stage.at[sl], row_sem.at[sl]).wait()

    @pl.when(i == 0)
    def _():
        pos_copy(0, 0).start()

    for sl in range(2):
        @pl.when(jnp.logical_and(i + 1 < n, slot == 1 - sl))
        def _():
            pos_copy(i + 1, sl).start()

    @pl.when(i >= 2)
    def _():
        drain(slot)

    stage[slot] = h_ref[...]

    for sl in range(2):
        @pl.when(slot == sl)
        def _():
            pos_copy(i, sl).wait()
            pos_smem = (pos_smem0, pos_smem1)[sl]

            def issue(c, carry):
                for rr in range(ISSUE_UNROLL):
                    r = c * ISSUE_UNROLL + rr
                    for k in range(TOP_K):
                        pltpu.make_async_copy(_row_tile(stage.at[sl], r), _row_tile(xs_hbm, pos_smem[k * tm + r]),
                                              row_sem.at[sl]).start(priority=k % DMA_QUEUES)
                return carry
            lax.fori_loop(0, tm // ISSUE_UNROLL, issue, 0)

    @pl.when(i == n - 1)
    def _():
        drain(slot)

        @pl.when(n >= 2)
        def _():
            drain(1 - slot)


def _dispatch(pos_flat, h_tiles, xs=None, n_rows=None):
    t = h_tiles.shape[0] // SUBLANES
    tm = GATHER_TILE
    assert t % tm == 0 and tm % ISSUE_UNROLL == 0
    in_specs = [pl.BlockSpec(memory_space=pl.ANY), pl.BlockSpec((tm * SUBLANES, LANES), lambda i: (i, 0))]
    args = [pos_flat, h_tiles]
    if xs is not None:
        in_specs.append(pl.BlockSpec(memory_space=pl.ANY))
        args.append(xs)
    return pl.pallas_call(
        functools.partial(_dispatch_kernel, tm=tm),
        grid=(t // tm,),
        in_specs=in_specs,
        out_specs=pl.BlockSpec(memory_space=pl.ANY),
        out_shape=jax.ShapeDtypeStruct((n_rows * SUBLANES, LANES) if xs is None else xs.shape, F32),
        scratch_shapes=[pltpu.SMEM((tm * TOP_K,), I32), pltpu.SMEM((tm * TOP_K,), I32),
                        pltpu.VMEM((2, tm * SUBLANES, LANES), F32),
                        pltpu.SemaphoreType.DMA((2,)), pltpu.SemaphoreType.DMA((2,))],
        input_output_aliases={} if xs is None else {2: 0},
        compiler_params=_cparams(("arbitrary",)),
        name="moe_dispatch",
    )(*args)


def _pad_fill_kernel(pos_hbm, xs_in_hbm, xs_hbm, pos_smem, zero_tile, pos_sem, row_sem, *, nidx):
    del xs_in_hbm
    i = pl.program_id(0)
    cp = pltpu.make_async_copy(pos_hbm.at[pl.ds(pl.multiple_of(i * nidx, nidx), nidx)], pos_smem, pos_sem)
    cp.start()
    zero_tile[...] = jnp.zeros_like(zero_tile)
    cp.wait()

    def issue(c, carry):
        for rr in range(TOP_K * ISSUE_UNROLL):
            r = c * (TOP_K * ISSUE_UNROLL) + rr
            pltpu.make_async_copy(zero_tile, _row_tile(xs_hbm, pos_smem[r]), row_sem).start(
                priority=rr % DMA_QUEUES)
        return carry
    lax.fori_loop(0, nidx // (TOP_K * ISSUE_UNROLL), issue, 0)
    whole = xs_hbm.at[pl.ds(0, nidx * SUBLANES), :]
    pltpu.make_async_copy(whole, whole, row_sem).wait()


def _pad_fill(pad_pos, xs):
    nidx = GATHER_TILE * TOP_K
    assert pad_pos.shape[0] % nidx == 0
    return pl.pallas_call(
        functools.partial(_pad_fill_kernel, nidx=nidx),
        grid=(pad_pos.shape[0] // nidx,),
        in_specs=[pl.BlockSpec(memory_space=pl.ANY), pl.BlockSpec(memory_space=pl.ANY)],
        out_specs=pl.BlockSpec(memory_space=pl.ANY),
        out_shape=jax.ShapeDtypeStruct(xs.shape, xs.dtype),
        scratch_shapes=[pltpu.SMEM((nidx,), I32), pltpu.VMEM((SUBLANES, LANES), F32),
                        pltpu.SemaphoreType.DMA, pltpu.SemaphoreType.DMA],
        input_output_aliases={1: 0},
        compiler_params=_cparams(("arbitrary",)),
        name="moe_pad_fill",
    )(pad_pos, xs)


def _moe_kernel(be_ref, nu_ref, xs_ref, wgu_ref, bgu_ref, wd_ref, bd_ref, ys_ref, wgu_bf, wd_bf):
    i = pl.program_id(0)
    d, ff = wgu_ref.shape[1], wd_ref.shape[1]

    @pl.when(jnp.logical_and(i < nu_ref[0],
                             jnp.logical_or(i == 0, be_ref[i] != be_ref[jnp.maximum(i - 1, 0)])))
    def _():
        for r in range(0, d, LANES):
            wgu_bf[r:r + LANES, :] = wgu_ref[0, r:r + LANES, :].astype(BF16)
        for r in range(0, ff, LANES):
            wd_bf[r:r + LANES, :] = wd_ref[0, r:r + LANES, :].astype(BF16)

    @pl.when(i < nu_ref[0])
    def _():
        x = _load_row_tiles(xs_ref, MOE_BLK).astype(BF16)
        glu = _dot(x, wgu_bf[:, 0:ff]) + bgu_ref[0, :, 0:ff]
        lin = _dot(x, wgu_bf[:, ff:2 * ff]) + bgu_ref[0, :, ff:2 * ff]
        glu = jnp.minimum(glu, SWIGLU_LIMIT)
        lin = jnp.clip(lin, -SWIGLU_LIMIT, SWIGLU_LIMIT)
        act = glu * jax.nn.sigmoid(SWIGLU_ALPHA * glu) * (lin + 1.0)
        _store_row_tiles(ys_ref, _dot(act.astype(BF16), wd_bf[...]) + bd_ref[0])


def _moe_mlp(block_expert, n_used, xs, w_gu, b_gu, w_down, b_down):
    ne, d, ff2 = w_gu.shape
    ff = ff2 // 2
    tb = MOE_BLK
    n_rows = block_expert.shape[0] * tb
    assert xs.shape[0] >= n_rows * SUBLANES
    blk = lambda i, be, nu: jnp.minimum(i, nu[0] - 1)
    grid_spec = pltpu.PrefetchScalarGridSpec(
        num_scalar_prefetch=2,
        grid=(n_rows // tb,),
        in_specs=[pl.BlockSpec((tb * SUBLANES, LANES), lambda i, be, nu: (blk(i, be, nu), 0)),
                  pl.BlockSpec((1, d, ff2), lambda i, be, nu: (be[blk(i, be, nu)], 0, 0)),
                  pl.BlockSpec((1, 1, ff2), lambda i, be, nu: (be[blk(i, be, nu)], 0, 0)),
                  pl.BlockSpec((1, ff, d), lambda i, be, nu: (be[blk(i, be, nu)], 0, 0)),
                  pl.BlockSpec((1, 1, d), lambda i, be, nu: (be[blk(i, be, nu)], 0, 0))],
        out_specs=pl.BlockSpec((tb * SUBLANES, LANES), lambda i, be, nu: (blk(i, be, nu), 0)),
        scratch_shapes=[pltpu.VMEM((d, ff2), BF16), pltpu.VMEM((ff, d), BF16)],
    )
    return pl.pallas_call(
        _moe_kernel,
        grid_spec=grid_spec,
        out_shape=jax.ShapeDtypeStruct((n_rows * SUBLANES, LANES), F32),
        compiler_params=_cparams(("arbitrary",)),
        name="moe_mlp",
    )(block_expert, n_used, xs, w_gu, b_gu.reshape(ne, 1, ff2), w_down, b_down.reshape(ne, 1, d))


def _combine_kernel(pos_hbm, ys_hbm, x1_ref, tg_ref, mod_ref, y_ref, pos_smem0, pos_smem1, buf, pos_sem, row_sem,
                    *, tm, nb):
    i = pl.program_id(0)
    n = pl.num_programs(0)
    slot = lax.rem(i, 2)
    nidx = tm * TOP_K

    def pos_copy(tile, sl):
        return pltpu.make_async_copy(pos_hbm.at[pl.ds(pl.multiple_of(tile * nidx, nidx), nidx)],
                                     (pos_smem0, pos_smem1)[sl], pos_sem.at[sl])

    def gather(tile, sl):
        pos_copy(tile, sl).wait()
        pos_smem = (pos_smem0, pos_smem1)[sl]

        def issue(c, carry):
            for rr in range(ISSUE_UNROLL):
                r = c * ISSUE_UNROLL + rr
                for k in range(TOP_K):
                    pltpu.make_async_copy(_row_tile(ys_hbm, pos_smem[k * tm + r]),
                                          _row_tile(buf.at[sl, k], r), row_sem.at[sl]).start(
                                              priority=k % DMA_QUEUES)
            return carry
        lax.fori_loop(0, tm // ISSUE_UNROLL, issue, 0)

        @pl.when(tile + 1 < n)
        def _():
            pos_copy(tile + 1, 1 - sl).start()

    @pl.when(i == 0)
    def _():
        pos_copy(0, 0).start()
        gather(0, 0)

    for sl in range(2):
        @pl.when(jnp.logical_and(i + 1 < n, slot == 1 - sl))
        def _():
            gather(i + 1, sl)

    for k in range(TOP_K):
        pltpu.make_async_copy(buf.at[slot, k], buf.at[slot, k], row_sem.at[slot]).wait()

    tg = tg_ref[...].T
    rows = tm // nb
    for c in range(SUBLANES):
        cols = slice(c * LANES, (c + 1) * LANES)
        ff = tg[:, 0:1] * buf[slot, 0, pl.ds(c, tm, stride=SUBLANES), :]
        for k in range(1, TOP_K):
            ff = ff + tg[:, k:k + 1] * buf[slot, k, pl.ds(c, tm, stride=SUBLANES), :]
        for g in range(nb):
            rs = slice(g * rows, (g + 1) * rows)
            y_ref[rs, cols] = x1_ref[rs, cols] + mod_ref[g, 5:6, cols] * ff[rs, :]


def _combine(pos_flat, ys, x1, top_gate, mod, tokens_per_batch):
    t, d = x1.shape
    tm = GATHER_TILE
    assert t % tm == 0 and tm % ISSUE_UNROLL == 0 and d == SUBLANES * LANES
    nb, mod_spec = _mod_spec(tokens_per_batch, tm, d)
    return pl.pallas_call(
        functools.partial(_combine_kernel, tm=tm, nb=nb),
        grid=(t // tm,),
        in_specs=[pl.BlockSpec(memory_space=pl.ANY), pl.BlockSpec(memory_space=pl.ANY),
                  pl.BlockSpec((tm, d), lambda i: (i, 0)), pl.BlockSpec((SUBLANES, tm), lambda i: (0, i)), mod_spec],
        out_specs=pl.BlockSpec((tm, d), lambda i: (i, 0)),
        out_shape=jax.ShapeDtypeStruct((t, d), F32),
        scratch_shapes=[pltpu.SMEM((tm * TOP_K,), I32), pltpu.SMEM((tm * TOP_K,), I32),
                        pltpu.VMEM((2, TOP_K, tm * SUBLANES, LANES), F32),
                        pltpu.SemaphoreType.DMA((2,)), pltpu.SemaphoreType.DMA((2,))],
        compiler_params=_cparams(("arbitrary",)),
        name="moe_combine",
    )(pos_flat, ys, x1, top_gate, mod)


def _pad_lanes(a, fill=0.0):
    return jnp.pad(a, [(0, 0)] * (a.ndim - 1) + [(0, LANES - a.shape[-1])], constant_values=fill)


def _hi_lo(w):
    hi = w.astype(BF16)
    return hi, (w - hi.astype(F32)).astype(BF16)


def _mixer(x, mod, k_past, v_past, conv_past, c0, n0, m0, p):
    b, s, d = x.shape
    outs = _inproj(x.reshape(b * s, d), mod, s, p["g1"], p["w_main"], p["wg_hi"], p["wg_lo"], p["bg"], p["g_k"],
                   p["widths"])
    q_raw, k_norm, v, ml_qk, ml_v, ml_o, gates = [a.reshape(b, s, a.shape[-1]) for a in outs[:7]]
    k_tiles, v_tiles = outs[7:]
    o_sb = _sb_attention(q_raw, k_norm, v, k_past, v_past, p["g_q"])
    o_ml, c_new, n_new, m_new, conv_new = _mlstm(
        ml_qk, ml_v, ml_o, gates, conv_past, c0, n0, _pad_lanes(m0)[:, None, :],
        p["w_conv"], p["b_conv"], p["g_head"])
    flat = lambda a: a.reshape(b * s, a.shape[-1])
    x1, h2, ti, tg = _outproj(flat(o_sb), flat(o_ml), flat(x), mod, s, p["w_out1"], p["w_out2"], p["g2"],
                              p["wrt_hi"], p["wrt_lo"], p["brt"])
    assert v.shape[-1] == SUBLANES * SB_HEAD_DIM
    state = (k_tiles.reshape(b, s, SUBLANES, SB_HEAD_DIM), v_tiles.reshape(b, s, SUBLANES, SB_HEAD_DIM), conv_new,
             c_new, n_new, m_new[:, 0, :ML_HEADS])
    return x1, h2, ti, tg, state


def _layer(xp, xs_, cp, cs, kc, vc, convc, cc, nc, mc, w_ada, b_ada, g_norm1, w_in, g_q, g_k, w_conv, b_conv,
           b_gate, g_head, w_out, g_norm2, w_router, b_router, w_gu, b_gu, w_down, b_down):
    d = xp.shape[-1]
    bp, sp = xp.shape[:2]
    bs, ss = xs_.shape[:2]
    sbw = kc.shape[-2] * kc.shape[-1]
    mlw = ML_HEADS * ML_HEAD_DIM
    ngate = 2 * ML_HEADS
    gate0 = 3 * sbw + 3 * mlw
    w_main = jnp.concatenate([w_in[:, :gate0], w_in[:, gate0 + ngate:]], axis=1).astype(BF16)
    wg_hi, wg_lo = _hi_lo(_pad_lanes(w_in[:, gate0:gate0 + ngate]))
    wrt_hi, wrt_lo = _hi_lo(_pad_lanes(w_router).T)
    p = dict(
        widths=(sbw, sbw, sbw, 2 * mlw, mlw, mlw),
        g1=g_norm1[None], w_main=w_main, wg_hi=wg_hi, wg_lo=wg_lo, bg=_pad_lanes(b_gate)[None],
        g_q=jnp.tile(g_q, LANES // SB_HEAD_DIM)[None], g_k=jnp.tile(g_k, LANES // SB_HEAD_DIM)[None],
        w_conv=w_conv, b_conv=b_conv[None], g_head=g_head[:, None],
        w_out1=w_out[:sbw].astype(BF16), w_out2=w_out[sbw:].astype(BF16), g2=g_norm2[None],
        wrt_hi=wrt_hi, wrt_lo=wrt_lo, brt=b_router[:, None],
    )
    mod = _ada(jnp.concatenate([cp, cs], axis=0), w_ada, b_ada)
    mod_p = mod[:bp].reshape(bp, 6, d)
    mod_s = mod[bp:].reshape(bs, 6, d)

    zeros = lambda *shp: jnp.zeros(shp, F32)
    feature_major = lambda a: jnp.transpose(a, (0, 2, 3, 1)).reshape(a.shape[0], sbw, a.shape[1])
    x1p, h2p, tip, tgp, st_p = _mixer(
        xp, mod_p, None, None, zeros(bp, CONV_W - 1, 2 * mlw), zeros(bp, ML_HEADS, ML_HEAD_DIM, ML_HEAD_DIM),
        zeros(bp, ML_HEADS, ML_HEAD_DIM), zeros(bp, ML_HEADS), p)
    x1s, h2s, tis, tgs, st_s = _mixer(
        xs_, mod_s, feature_major(kc), feature_major(vc), convc, cc, nc, mc, p)

    tp, ts = bp * sp, bs * ss
    pos, counts = _positions(jnp.concatenate([tip, tis], axis=1))
    counts = counts[:, 0]
    padded = (counts + MOE_BLK - 1) // MOE_BLK * MOE_BLK
    pad_end = jnp.cumsum(padded)
    pad_start = pad_end - padded
    n_blocks = -(-((tp + ts) * TOP_K + N_EXPERTS * (MOE_BLK - 1)) // MOE_BLK)
    n_used = (pad_end[-1] // MOE_BLK).astype(I32).reshape(1)
    block_start = jnp.arange(n_blocks, dtype=I32) * MOE_BLK
    block_expert = jnp.minimum(jnp.sum(block_start[:, None] >= pad_end[None, :], axis=1), N_EXPERTS - 1).astype(I32)

    def per_tile(rows):
        t = rows.shape[1]
        return rows.reshape(TOP_K, t // GATHER_TILE, GATHER_TILE).transpose(1, 0, 2).reshape(-1)
    pos_p = per_tile(pos[:TOP_K, :tp])
    pos_s = per_tile(pos[:TOP_K, tp:])

    n_rows = n_blocks * MOE_BLK
    slot = jnp.arange(MOE_BLK, dtype=I32)[None, :]
    spare = n_rows + jnp.arange(N_EXPERTS * MOE_BLK, dtype=I32).reshape(N_EXPERTS, MOE_BLK)
    pad_pos = jnp.where(slot < (padded - counts)[:, None], (pad_start + counts)[:, None] + slot, spare).reshape(-1)

    xs_buf = _dispatch(pos_p, h2p, n_rows=n_rows + N_EXPERTS * MOE_BLK)
    xs_buf = _dispatch(pos_s, h2s, xs_buf)
    xs_buf = _pad_fill(pad_pos.astype(I32), xs_buf)
    ys = _moe_mlp(block_expert, n_used, xs_buf, w_gu, b_gu, w_down, b_down)
    yp = _combine(pos_p, ys, x1p, tgp, mod_p, sp).reshape(bp, sp, d)
    ysm = _combine(pos_s, ys, x1s, tgs, mod_s, ss).reshape(bs, ss, d)
    return yp, ysm, st_p, st_s


def kernel(x_prompt, x_sample, c_prompt, c_sample, cache_sb_k, cache_sb_v, state_conv, state_mlstm_c, state_mlstm_n, state_mlstm_m, w_ada, b_ada, g_norm1, w_in, g_q, g_k, w_conv, b_conv, b_gate, g_head, w_out, g_norm2, w_router, b_router, w_gu, b_gu, w_down, b_down):
    assert w_ada.shape[0] == 1, "single-layer step"
    yp, ys, st_p, st_s = _layer(
        x_prompt, x_sample, c_prompt, c_sample, cache_sb_k[0], cache_sb_v[0], state_conv[0], state_mlstm_c[0],
        state_mlstm_n[0], state_mlstm_m[0], w_ada[0], b_ada[0], g_norm1[0], w_in[0], g_q[0], g_k[0], w_conv[0],
        b_conv[0], b_gate[0], g_head[0], w_out[0], g_norm2[0], w_router[0], b_router[0], w_gu[0], b_gu[0],
        w_down[0], b_down[0])
    return (yp, ys) + tuple(a[None] for a in st_p) + tuple(a[None] for a in st_s)
```

```python
import functools
import math

import jax
import jax.numpy as jnp
from jax import lax
from jax.experimental import pallas as pl
from jax.experimental.pallas import tpu as pltpu

F32 = jnp.float32
BF16 = jnp.bfloat16
I32 = jnp.int32

EPS = 1e-6
LANES = 128
SUBLANES = 8
SB_HEAD_DIM = 64
ML_HEAD_DIM = 128
ML_HEADS = 4
CONV_W = 4
N_EXPERTS = 32
TOP_K = 4
SWIGLU_LIMIT = 7.0
SWIGLU_ALPHA = 1.702
VMEM_LIMIT = 56 * 1024 * 1024

ROW_TILE = 512
POS_TILE = 1024
MOE_BLK = 512
GATHER_TILE = 256
ISSUE_UNROLL = 8
DMA_QUEUES = 2
ML_CHUNK = 256
SB_BLK = 128
SB_SWEEP = 256
SB_UNDERFLOW = -105.0


def _cparams(sem):
    return pltpu.CompilerParams(dimension_semantics=sem, vmem_limit_bytes=VMEM_LIMIT)


def _split3(x):
    p1 = x.astype(BF16)
    r1 = x - p1.astype(F32)
    p2 = r1.astype(BF16)
    p3 = (r1 - p2.astype(F32)).astype(BF16)
    return p1, p2, p3


def _dot(a, b):
    return jnp.dot(a, b, preferred_element_type=F32)


def _dot_nt(a, b):
    return lax.dot_general(a, b, (((1,), (1,)), ((), ())), preferred_element_type=F32)


def _dot3(x, w_hi, w_lo):
    xh = x.astype(BF16)
    xl = (x - xh.astype(F32)).astype(BF16)
    return _dot(xh, w_hi) + _dot(xl, w_hi) + _dot(xh, w_lo)


def _store_row_tiles(ref, x):
    n, d = x.shape
    assert d == SUBLANES * LANES
    for c in range(SUBLANES):
        ref[pl.ds(c, n, stride=SUBLANES), :] = x[:, c * LANES:(c + 1) * LANES]


def _load_row_tiles(ref, n):
    return jnp.concatenate([ref[pl.ds(c, n, stride=SUBLANES), :] for c in range(SUBLANES)], axis=1)


def _store_head_tiles(ref, row0, head0, x):
    n = x.shape[0]
    for j in range(x.shape[1] // SB_HEAD_DIM):
        ref[pl.ds(row0 * SUBLANES + head0 + j, n, stride=SUBLANES), :] = x[:, j * SB_HEAD_DIM:(j + 1) * SB_HEAD_DIM]


def _headnorm(x, g):
    lo_half = lax.broadcasted_iota(I32, (1, LANES), 1) < SB_HEAD_DIM
    x2 = x * x
    s0 = jnp.sum(jnp.where(lo_half, x2, 0.0), axis=-1, keepdims=True)
    s1 = jnp.sum(jnp.where(lo_half, 0.0, x2), axis=-1, keepdims=True)
    r = jnp.where(lo_half, lax.rsqrt(s0 * (1.0 / SB_HEAD_DIM) + EPS), lax.rsqrt(s1 * (1.0 / SB_HEAD_DIM) + EPS))
    return x * r * g


def _mod_rows(mod_ref, j, nb, shape):
    if nb == 1:
        return mod_ref[0, j:j + 1, :]
    rows = shape[0] // nb
    return jnp.concatenate([jnp.broadcast_to(mod_ref[g, j:j + 1, :], (rows, shape[1])) for g in range(nb)], axis=0)


def _mod_spec(tokens_per_batch, tm, d):
    if tokens_per_batch >= tm:
        assert tokens_per_batch % tm == 0
        per = tokens_per_batch // tm
        return 1, pl.BlockSpec((1, 6, d), lambda i: (i // per, 0, 0))
    assert tm % tokens_per_batch == 0
    nb = tm // tokens_per_batch
    return nb, pl.BlockSpec((nb, 6, d), lambda i: (i, 0, 0))


def _log_sigmoid(x):
    return jnp.minimum(x, 0.0) - jnp.log1p(jnp.exp(-jnp.abs(x)))


def _ada_kernel(c_ref, w_ref, b_ref, o_ref):
    c = c_ref[...]
    s = c * jax.nn.sigmoid(c)
    o_ref[...] = jnp.dot(s, w_ref[...], precision=lax.Precision.HIGHEST,
                         preferred_element_type=F32) + b_ref[...]


def _ada(c, w_ada, b_ada):
    n, d = c.shape
    cols = w_ada.shape[1]
    tn = cols // 6
    return pl.pallas_call(
        _ada_kernel,
        grid=(cols // tn,),
        in_specs=[pl.BlockSpec((n, d), lambda j: (0, 0)),
                  pl.BlockSpec((d, tn), lambda j: (0, j)),
                  pl.BlockSpec((1, tn), lambda j: (0, j))],
        out_specs=pl.BlockSpec((n, tn), lambda j: (0, j)),
        out_shape=jax.ShapeDtypeStruct((n, cols), F32),
        compiler_params=_cparams(("arbitrary",)),
        name="ada_mod",
    )(c, w_ada, b_ada.reshape(1, cols))


def _inproj_kernel(x_ref, mod_ref, g1_ref, w_ref, wgh_ref, wgl_ref, bg_ref, gk_ref,
                   q_ref, k_ref, v_ref, qk_ref, mv_ref, mo_ref, gate_ref, kt_ref, vt_ref, *, widths, nb):
    x = x_ref[...]
    ms = jnp.mean(x * x, axis=-1, keepdims=True)
    h = x * lax.rsqrt(ms + EPS) * g1_ref[...]
    h = h * (1.0 + _mod_rows(mod_ref, 1, nb, x.shape)) + _mod_rows(mod_ref, 0, nb, x.shape)
    hb = h.astype(BF16)
    off = 0
    for ref, wd in zip((q_ref, k_ref, v_ref, qk_ref, mv_ref, mo_ref), widths):
        u = _dot(hb, w_ref[:, off:off + wd])
        if ref is k_ref:
            u = jnp.concatenate([_headnorm(u[:, c:c + LANES], gk_ref[...]) for c in range(0, wd, LANES)], axis=1)
            _store_head_tiles(kt_ref, 0, 0, u)
        if ref is v_ref:
            _store_head_tiles(vt_ref, 0, 0, u)
        ref[...] = u.astype(ref.dtype)
        off += wd
    gate_ref[...] = _dot3(h, wgh_ref[...], wgl_ref[...]) + bg_ref[...]


def _inproj(x, mod, tokens_per_batch, g1, w_main, wg_hi, wg_lo, bg, g_k, widths):
    t, d = x.shape
    tm = ROW_TILE
    assert t % tm == 0
    ncol = w_main.shape[1]
    nb, mod_spec = _mod_spec(tokens_per_batch, tm, d)
    tok = lambda w: pl.BlockSpec((tm, w), lambda i: (i, 0))
    const = lambda shp: pl.BlockSpec(shp, lambda i: (0,) * len(shp))
    head_tiles = pl.BlockSpec((tm * SUBLANES, SB_HEAD_DIM), lambda i: (i, 0))
    return pl.pallas_call(
        functools.partial(_inproj_kernel, widths=widths, nb=nb),
        grid=(t // tm,),
        in_specs=[tok(d), mod_spec,
                  const((1, d)), const((d, ncol)), const((d, LANES)), const((d, LANES)), const((1, LANES)),
                  const((1, LANES))],
        out_specs=[tok(w) for w in widths] + [tok(LANES), head_tiles, head_tiles],
        out_shape=[jax.ShapeDtypeStruct((t, w), BF16 if j in (1, 2) else F32) for j, w in enumerate(widths)]
        + [jax.ShapeDtypeStruct((t, LANES), F32)]
        + [jax.ShapeDtypeStruct((t * SUBLANES, SB_HEAD_DIM), F32)] * 2,
        compiler_params=_cparams(("arbitrary",)),
        name="inproj",
    )(x, mod, g1, w_main, wg_hi, wg_lo, bg, g_k)


def _sb_kernel(*refs, past, seq, tq, kb, npair):
    ks_scr, vs_scr, run_scr, acc_scr = refs[-4:]
    if past:
        q_ref, k_ref, v_ref, kp_ref, vp_ref, gq_ref, o_ref = refs[:-4]
    else:
        q_ref, k_ref, v_ref, gq_ref, o_ref = refs[:-4]
    i = pl.program_id(1)
    lane = lax.broadcasted_iota(I32, (1, LANES), 1)
    lo_half = lane < SB_HEAD_DIM
    pairs = [slice(hp * LANES, (hp + 1) * LANES) for hp in range(npair)]

    @pl.when(i == 0)
    def _():
        ch = min(256, seq)
        for c in range(seq // ch):
            rows = slice(c * ch, (c + 1) * ch)
            dst = slice(past + c * ch, past + (c + 1) * ch)
            ks_scr[dst, :] = k_ref[0, rows, :].astype(BF16)
            vs_scr[dst, :] = v_ref[0, rows, :].astype(BF16)
        if past:
            pc = min(256, past)
            for src, dst_scr in ((kp_ref, ks_scr), (vp_ref, vs_scr)):
                for c in range(past // pc):
                    dst_scr[c * pc:(c + 1) * pc, :] = src[0, :, c * pc:(c + 1) * pc].T.astype(BF16)

    qm = []
    for ps in pairs:
        q = _headnorm(q_ref[0, :, ps], gq_ref[...]) * (1.0 / math.sqrt(SB_HEAD_DIM))
        qm.append(jnp.concatenate([jnp.where(lo_half, q, 0.0), jnp.where(lo_half, 0.0, q)], axis=0).astype(BF16))
    nrow = 2 * npair * tq

    def neg_cum(nk):
        rj = lax.broadcasted_iota(I32, (2 * nk, LANES + nk), 0)
        cj = lax.broadcasted_iota(I32, (2 * nk, LANES + nk), 1)
        rjm = jnp.where(rj >= nk, rj - nk, rj)
        return jnp.where(cj < LANES, -1.0, jnp.where(rjm >= cj - LANES, -1.0, 0.0)).astype(BF16)

    w_diag = neg_cum(tq)
    w_sub = w_diag if tq == LANES else neg_cum(LANES)
    causal = (lax.broadcasted_iota(I32, (nrow, tq), 1)
              < jnp.bitwise_and(lax.broadcasted_iota(I32, (nrow, tq), 0), tq - 1))

    def mask_diag(a):
        tail = jnp.where(causal, a[:, a.shape[1] - tq:], 0.0)
        return tail if a.shape[1] == tq else jnp.concatenate([a[:, :a.shape[1] - tq], tail], axis=1)

    def sweep(row0, nk, diag):
        kv = [(ks_scr[pl.ds(row0, nk), ps], vs_scr[pl.ds(row0, nk), ps]) for ps in pairs]
        z = jnp.concatenate([_dot_nt(qm[hp], kv[hp][0]) for hp in range(npair)], axis=0)
        sp = jnp.maximum(z, 0.0) + jnp.log(1.0 + jnp.exp(-jnp.abs(z)))
        if diag:
            sp = mask_diag(sp)
        hi = sp.astype(BF16)
        lo = (sp - hi.astype(F32)).astype(BF16)
        sub = min(nk, LANES)
        w_neg = w_diag if sub == tq else w_sub
        run = None if diag else run_scr[...]
        args = [None] * (nk // sub)
        for j in reversed(range(nk // sub)):
            cols = slice(j * sub, (j + 1) * sub)
            t = _dot(jnp.concatenate([hi[:, cols], lo[:, cols]], axis=1), w_neg)
            args[j] = z[:, cols] + t[:, LANES:] if run is None else z[:, cols] + t[:, LANES:] + run[:, :sub]
            run = t[:, :LANES] if run is None else run + t[:, :LANES]
        run_scr[...] = run
        p = jnp.exp(args[0] if len(args) == 1 else jnp.concatenate(args, axis=1))
        if diag:
            p = mask_diag(p)
        p = p.astype(BF16)
        for hp in range(npair):
            r0 = 2 * hp * tq
            vblk = kv[hp][1]
            vm = jnp.concatenate([jnp.where(lo_half, vblk, 0.0), jnp.where(lo_half, 0.0, vblk)], axis=0)
            pv = _dot(jnp.concatenate([p[r0:r0 + tq], p[r0 + tq:r0 + 2 * tq]], axis=1), vm.astype(BF16))
            acc_scr[hp] = pv if diag else acc_scr[hp] + pv
        return jnp.max(run)

    q0 = past + i * tq
    n_before = q0 // kb
    if tq == LANES:
        merged = q0 >= kb
        worst0 = lax.cond(merged,
                          lambda: sweep(pl.multiple_of(q0 - kb, tq), kb + tq, True),
                          lambda: sweep(pl.multiple_of(q0, tq), tq, True))
        done0 = merged.astype(I32)
    else:
        worst0 = sweep(pl.multiple_of(q0, tq), tq, True)
        done0 = jnp.int32(0)

    def cond(c):
        return jnp.logical_and(c[0] < n_before, c[1] > SB_UNDERFLOW)

    def body(c):
        row0 = q0 - (c[0] + 1) * kb
        return c[0] + 1, sweep(pl.multiple_of(row0, tq), kb, False)

    _, worst = lax.while_loop(cond, body, (done0, worst0))
    if tq < kb and seq > tq:
        @pl.when(jnp.logical_and(q0 - n_before * kb > 0, worst > SB_UNDERFLOW))
        def _():
            sweep(0, tq, False)
    for hp, ps in enumerate(pairs):
        o_ref[0, :, ps] = acc_scr[hp]


def _sb_attention(q_raw, k_norm, v, k_past, v_past, g_q):
    b, s, w = q_raw.shape
    past = 0 if k_past is None else k_past.shape[2]
    tq = min(SB_BLK, s)
    kb = SB_SWEEP
    assert s % tq == 0 and past % kb == 0 and w % LANES == 0 and kb % LANES == 0 and (kb == 2 * tq or s == tq)
    npair = w // LANES
    qblk = pl.BlockSpec((1, tq, w), lambda bi, i: (bi, i, 0))
    full = lambda n: pl.BlockSpec((1, n, w), lambda bi, i: (bi, 0, 0))
    gspec = pl.BlockSpec((1, LANES), lambda bi, i: (0, 0))
    in_specs = [qblk, full(s), full(s)]
    args = [q_raw, k_norm, v]
    if past:
        cache = pl.BlockSpec((1, w, past), lambda bi, i: (bi, 0, 0))
        in_specs += [cache, cache]
        args += [k_past, v_past]
    in_specs += [gspec]
    args += [g_q]
    return pl.pallas_call(
        functools.partial(_sb_kernel, past=past, seq=s, tq=tq, kb=kb, npair=npair),
        grid=(b, s // tq),
        in_specs=in_specs,
        out_specs=qblk,
        out_shape=jax.ShapeDtypeStruct((b, s, w), F32),
        scratch_shapes=[pltpu.VMEM((past + s, w), BF16), pltpu.VMEM((past + s, w), BF16),
                        pltpu.VMEM((2 * npair * tq, LANES), F32), pltpu.VMEM((npair, tq, LANES), F32)],
        compiler_params=_cparams(("arbitrary", "arbitrary")),
        name="sb_attention",
    )(*args)


def _mlstm_kernel(qk_ref, v_ref, og_ref, g_ref, cp_ref, c0_ref, n0_ref, m0_ref, wc_ref, bc_ref, gh_ref,
                  out_ref, cout_ref, nout_ref, mout_ref, convout_ref,
                  prev_scr, c_scr, n_scr, m_scr, *, chunk):
    ci = pl.program_id(1)
    nh, dh = ML_HEADS, ML_HEAD_DIM
    width = nh * dh
    L = chunk

    @pl.when(ci == 0)
    def _():
        prev_scr[0:8, :] = jnp.zeros((8, prev_scr.shape[1]), F32)
        prev_scr[8 - (CONV_W - 1):8, :] = cp_ref[0]
        c_scr[...] = c0_ref[0]
        n_scr[...] = n0_ref[0]
        m_scr[...] = m0_ref[0]

    prev_scr[8:8 + L, :] = qk_ref[0]
    wc = wc_ref[...]
    acc = bc_ref[...] + prev_scr[8 - (CONV_W - 1):8 - (CONV_W - 1) + L, :] * wc[0:1]
    for j in range(1, CONV_W):
        o = 8 - (CONV_W - 1) + j
        acc = acc + prev_scr[o:o + L, :] * wc[j:j + 1]
    qk = acc * jax.nn.sigmoid(acc)
    prev_scr[0:8, :] = prev_scr[L:L + 8, :]

    gt = g_ref[0]
    lf = _log_sigmoid(gt)
    gt_t = gt.T
    lf_t = lf.T
    row = lax.broadcasted_iota(I32, (L, L), 0)
    col = lax.broadcasted_iota(I32, (L, L), 1)
    tril = jnp.where(row >= col, 1.0, 0.0).astype(BF16)
    triu = jnp.where(row <= col, 1.0, 0.0).astype(BF16)
    p1, p2, p3 = _split3(lf)
    b_col = _dot(tril, p1) + _dot(tril, p2) + _dot(tril, p3)
    q1, q2, q3 = _split3(lf_t[0:8, :])
    b_row = _dot(q1, triu) + _dot(q2, triu) + _dot(q3, triu)

    lane = lax.broadcasted_iota(I32, (1, LANES), 1)
    m_vec = m_scr[...]
    m_out = m_vec
    for h in range(nh):
        hs = slice(h * dh, (h + 1) * dh)
        qb = qk[:, h * dh:(h + 1) * dh].astype(BF16)
        kh = qk[:, width + h * dh:width + (h + 1) * dh] * (dh ** -0.5)
        kb = kh.astype(BF16)
        v_t = v_ref[0, :, hs].T
        b_r = b_row[nh + h:nh + h + 1, :]
        ig_r = gt_t[h:h + 1, :]
        c_col = gt[:, h:h + 1] - b_col[:, nh + h:nh + h + 1]
        m_prev = m_vec[:, h:h + 1]

        d_t = jnp.where(row <= col, b_r + c_col, -jnp.inf)
        inter = b_r + m_prev
        m_t = jnp.maximum(inter, jnp.max(d_t, axis=0, keepdims=True))
        w_inter = jnp.exp(inter - m_t)
        s_t = _dot_nt(kb, qb) * jnp.exp(d_t - m_t)
        c_h = c_scr[h]
        n_h = n_scr[h:h + 1, :]
        n_hi = n_h.astype(BF16)
        n_lo = (n_h - n_hi.astype(F32)).astype(BF16)
        num_t = _dot(v_t.astype(BF16), s_t.astype(BF16)) + w_inter * _dot_nt(c_h.astype(BF16), qb)
        den = jnp.sum(s_t, axis=0, keepdims=True) + w_inter * (_dot_nt(n_hi, qb) + _dot_nt(n_lo, qb))
        hh_t = num_t / jnp.maximum(jnp.abs(den), jnp.exp(-m_t))

        b_last = b_r[:, L - 1:L]
        g_r = (b_last - b_r) + ig_r
        m_new = jnp.maximum(b_last + m_prev, jnp.max(g_r, axis=1, keepdims=True))
        w_state = jnp.exp(g_r - m_new)
        decay = jnp.exp(b_last + m_prev - m_new)
        c_scr[h] = decay * c_h + _dot((v_t * w_state).astype(BF16), kb)
        w_hi = w_state.astype(BF16)
        w_lo = (w_state - w_hi.astype(F32)).astype(BF16)
        n_scr[h:h + 1, :] = decay * n_h + (_dot(w_hi, kb) + _dot(w_lo, kb))
        m_out = jnp.where(lane == h, m_new, m_out)

        hn_t = hh_t * lax.rsqrt(jnp.mean(hh_t * hh_t, axis=0, keepdims=True) + EPS) * gh_ref[...]
        out_ref[0, :, hs] = jax.nn.sigmoid(og_ref[0, :, hs]) * hn_t.T
    m_scr[...] = m_out

    @pl.when(ci == pl.num_programs(1) - 1)
    def _():
        cout_ref[0] = c_scr[...]
        nout_ref[0] = n_scr[...]
        mout_ref[0] = m_scr[...]
        convout_ref[0] = prev_scr[8 - (CONV_W - 1):8, :]


def _mlstm(ml_qk, ml_v, ml_o, gates, conv_past, c0, n0, m0, w_conv, b_conv, g_head):
    b, s, w2 = ml_qk.shape
    width = w2 // 2
    nh, dh = ML_HEADS, ML_HEAD_DIM
    chunk = min(ML_CHUNK, s)
    assert s % chunk == 0 and chunk >= 8
    tok = lambda w: pl.BlockSpec((1, chunk, w), lambda bi, ci: (bi, ci, 0))
    perb = lambda shp: pl.BlockSpec((1,) + shp, lambda bi, ci: (bi,) + (0,) * len(shp))
    const = lambda shp: pl.BlockSpec(shp, lambda bi, ci: (0,) * len(shp))
    return pl.pallas_call(
        functools.partial(_mlstm_kernel, chunk=chunk),
        grid=(b, s // chunk),
        in_specs=[tok(w2), tok(width), tok(width), tok(LANES),
                  perb((CONV_W - 1, w2)), perb((nh, dh, dh)), perb((nh, dh)), perb((1, LANES)),
                  const((CONV_W, w2)), const((1, w2)), const((dh, 1))],
        out_specs=[tok(width), perb((nh, dh, dh)), perb((nh, dh)), perb((1, LANES)), perb((CONV_W - 1, w2))],
        out_shape=[jax.ShapeDtypeStruct((b, s, width), F32),
                   jax.ShapeDtypeStruct((b, nh, dh, dh), F32),
                   jax.ShapeDtypeStruct((b, nh, dh), F32),
                   jax.ShapeDtypeStruct((b, 1, LANES), F32),
                   jax.ShapeDtypeStruct((b, CONV_W - 1, w2), F32)],
        scratch_shapes=[pltpu.VMEM((8 + chunk, w2), F32), pltpu.VMEM((nh, dh, dh), F32),
                        pltpu.VMEM((nh, dh), F32), pltpu.VMEM((1, LANES), F32)],
        compiler_params=_cparams(("arbitrary", "arbitrary")),
        name="mlstm",
    )(ml_qk, ml_v, ml_o, gates, conv_past, c0, n0, m0, w_conv, b_conv, g_head)


def _outproj_kernel(osb_ref, oml_ref, x_ref, mod_ref, w1_ref, w2_ref, g2_ref, wrh_ref, wrl_ref, br_ref,
                    x1_ref, h2_ref, ti_ref, tg_ref, *, nb):
    tm = x_ref.shape[0]
    per_token = lambda j: _mod_rows(mod_ref, j, nb, x_ref.shape)

    mix = _dot(osb_ref[...].astype(BF16), w1_ref[...]) + _dot(oml_ref[...].astype(BF16), w2_ref[...])
    x1 = x_ref[...] + per_token(2) * mix
    x1_ref[...] = x1
    ms = jnp.mean(x1 * x1, axis=-1, keepdims=True)
    h2 = x1 * lax.rsqrt(ms + EPS) * g2_ref[...]
    h2 = h2 * (1.0 + per_token(4)) + per_token(3)
    _store_row_tiles(h2_ref, h2)

    hb = h2.astype(BF16)
    hl = (h2 - hb.astype(F32)).astype(BF16)
    logits = (_dot_nt(wrh_ref[...], hb) + _dot_nt(wrh_ref[...], hl) + _dot_nt(wrl_ref[...], hb))[0:N_EXPERTS, :]
    cur = logits + br_ref[...]
    erow = lax.broadcasted_iota(I32, cur.shape, 0)
    vals, idxs = [], []
    for _ in range(TOP_K):
        m = jnp.max(cur, axis=0, keepdims=True)
        idx = jnp.min(jnp.where(cur == m, erow, N_EXPERTS), axis=0, keepdims=True)
        vals.append(m)
        idxs.append(idx)
        cur = jnp.where(erow == idx, -jnp.inf, cur)
    es = [jnp.exp(v - vals[0]) for v in vals]
    tot = es[0] + es[1] + es[2] + es[3]
    krow = lax.broadcasted_iota(I32, (SUBLANES, tm), 0)
    ti = jnp.zeros((SUBLANES, tm), I32)
    tg = jnp.zeros((SUBLANES, tm), F32)
    for k in range(TOP_K):
        ti = jnp.where(krow == k, idxs[k], ti)
        tg = jnp.where(krow == k, es[k] / tot, tg)
    ti_ref[...] = ti
    tg_ref[...] = tg


def _outproj(o_sb, o_ml, x, mod, tokens_per_batch, w1, w2, g2, wrt_hi, wrt_lo, brt):
    t, d = x.shape
    hw = o_sb.shape[1]
    tm = ROW_TILE
    assert t % tm == 0
    nb, mod_spec = _mod_spec(tokens_per_batch, tm, d)
    tok = lambda w: pl.BlockSpec((tm, w), lambda i: (i, 0))
    const = lambda shp: pl.BlockSpec(shp, lambda i: (0,) * len(shp))
    rows8 = pl.BlockSpec((SUBLANES, tm), lambda i: (0, i))
    return pl.pallas_call(
        functools.partial(_outproj_kernel, nb=nb),
        grid=(t // tm,),
        in_specs=[tok(hw), tok(hw), tok(d), mod_spec,
                  const((hw, d)), const((hw, d)), const((1, d)),
                  const((LANES, d)), const((LANES, d)), const((N_EXPERTS, 1))],
        out_specs=[tok(d), pl.BlockSpec((tm * SUBLANES, LANES), lambda i: (i, 0)), rows8, rows8],
        out_shape=[jax.ShapeDtypeStruct((t, d), F32), jax.ShapeDtypeStruct((t * SUBLANES, LANES), F32),
                   jax.ShapeDtypeStruct((SUBLANES, t), I32), jax.ShapeDtypeStruct((SUBLANES, t), F32)],
        compiler_params=_cparams(("arbitrary",)),
        name="outproj_router",
    )(o_sb, o_ml, x, mod, w1, w2, g2, wrt_hi, wrt_lo, brt)


def _rank_kernel(ti_ref, pos_ref, cnt_ref, carry_scr, start_scr):
    ph = pl.program_id(0)
    i = pl.program_id(1)
    tm = ti_ref.shape[1]
    rep = lambda a: jnp.concatenate([a] * (tm // LANES), axis=1)

    @pl.when(i == 0)
    def _():
        carry_scr[...] = jnp.zeros_like(carry_scr)

    ti = ti_ref[...]
    erow = lax.broadcasted_iota(I32, (N_EXPERTS, tm), 0)
    sel = [erow == ti[k:k + 1, :] for k in range(TOP_K)]
    hit = jnp.zeros((N_EXPERTS, tm), F32)
    for k in range(TOP_K):
        hit = hit + jnp.where(sel[k], 1.0, 0.0)
    hit = hit.astype(BF16)
    seen = carry_scr[...]

    @pl.when(ph == 1)
    def _():
        earlier = lax.broadcasted_iota(I32, (tm, tm), 0) < lax.broadcasted_iota(I32, (tm, tm), 1)
        row_of = _dot(hit, jnp.where(earlier, 1.0, 0.0).astype(BF16)) + rep(seen) + rep(start_scr[...])
        krow = lax.broadcasted_iota(I32, (SUBLANES, tm), 0)
        pos = jnp.zeros((SUBLANES, tm), F32)
        for k in range(TOP_K):
            pos = jnp.where(krow == k, jnp.sum(jnp.where(sel[k], row_of, 0.0), axis=0, keepdims=True), pos)
        pos_ref[...] = pos.astype(I32)

    carry_scr[...] = seen + _dot(hit, jnp.ones((tm, LANES), BF16))

    @pl.when(jnp.logical_and(ph == 0, i == pl.num_programs(1) - 1))
    def _():
        cnt = carry_scr[...]
        cnt_ref[...] = cnt.astype(I32)
        padded = jnp.floor((cnt + (MOE_BLK - 1)) * (1.0 / MOE_BLK)) * MOE_BLK
        lower = (lax.broadcasted_iota(I32, (N_EXPERTS, N_EXPERTS), 1)
                 < lax.broadcasted_iota(I32, (N_EXPERTS, N_EXPERTS), 0))
        lower = jnp.where(lower, 1.0, 0.0).astype(BF16)
        p1, p2, p3 = _split3(padded)
        start_scr[...] = _dot(lower, p1) + _dot(lower, p2) + _dot(lower, p3)


def _positions(top_idx_t):
    t = top_idx_t.shape[1]
    tm = POS_TILE
    assert t % tm == 0
    return pl.pallas_call(
        _rank_kernel,
        grid=(2, t // tm),
        in_specs=[pl.BlockSpec((SUBLANES, tm), lambda ph, i: (0, i))],
        out_specs=[pl.BlockSpec((SUBLANES, tm), lambda ph, i: (0, i * ph)),
                   pl.BlockSpec((N_EXPERTS, LANES), lambda ph, i: (0, 0))],
        out_shape=[jax.ShapeDtypeStruct((SUBLANES, t), I32), jax.ShapeDtypeStruct((N_EXPERTS, LANES), I32)],
        scratch_shapes=[pltpu.VMEM((N_EXPERTS, LANES), F32), pltpu.VMEM((N_EXPERTS, LANES), F32)],
        compiler_params=_cparams(("arbitrary", "arbitrary")),
        name="expert_positions",
    )(top_idx_t)


def _row_tile(ref, r):
    return ref.at[pl.ds(pl.multiple_of(r * SUBLANES, SUBLANES), SUBLANES), :]


def _dispatch_kernel(*refs, tm):
    pos_hbm, h_ref = refs[:2]
    xs_hbm, pos_smem0, pos_smem1, stage, pos_sem, row_sem = refs[-6:]
    i = pl.program_id(0)
    n = pl.num_programs(0)
    slot = lax.rem(i, 2)
    nidx = tm * TOP_K

    def pos_copy(tile, sl):
        return pltpu.make_async_copy(pos_hbm.at[pl.ds(pl.multiple_of(tile * nidx, nidx), nidx)],
                                     (pos_smem0, pos_smem1)[sl], pos_sem.at[sl])

    def drain(sl):
        for _ in range(TOP_K):
            pltpu.make_async_copy(stage.at[sl], stage.at[sl], row_sem.at[sl]).wait()

    @pl.when(i == 0)
    def _():
        pos_copy(0, 0).start()

    for sl in range(2):
        @pl.when(jnp.logical_and(i + 1 < n, slot == 1 - sl))
        def _():
            pos_copy(i + 1, sl).start()

    @pl.when(i >= 2)
    def _():
        drain(slot)

    stage[slot] = h_ref[...]

    for sl in range(2):
        @pl.when(slot == sl)
        def _():
            pos_copy(i, sl).wait()
            pos_smem = (pos_smem0, pos_smem1)[sl]

            def issue(c, carry):
                for rr in range(ISSUE_UNROLL):
                    r = c * ISSUE_UNROLL + rr
                    for k in range(TOP_K):
                        pltpu.make_async_copy(_row_tile(stage.at[sl], r), _row_tile(xs_hbm, pos_smem[k * tm + r]),
                                              row_sem.at[sl]).start(priority=k % DMA_QUEUES)
                return carry
            lax.fori_loop(0, tm // ISSUE_UNROLL, issue, 0)

    @pl.when(i == n - 1)
    def _():
        drain(slot)

        @pl.when(n >= 2)
        def _():
            drain(1 - slot)


def _dispatch(pos_flat, h_tiles, xs=None, n_rows=None):
    t = h_tiles.shape[0] // SUBLANES
    tm = GATHER_TILE
    assert t % tm == 0 and tm % ISSUE_UNROLL == 0
    in_specs = [pl.BlockSpec(memory_space=pl.ANY), pl.BlockSpec((tm * SUBLANES, LANES), lambda i: (i, 0))]
    args = [pos_flat, h_tiles]
    if xs is not None:
        in_specs.append(pl.BlockSpec(memory_space=pl.ANY))
        args.append(xs)
    return pl.pallas_call(
        functools.partial(_dispatch_kernel, tm=tm),
        grid=(t // tm,),
        in_specs=in_specs,
        out_specs=pl.BlockSpec(memory_space=pl.ANY),
        out_shape=jax.ShapeDtypeStruct((n_rows * SUBLANES, LANES) if xs is None else xs.shape, F32),
        scratch_shapes=[pltpu.SMEM((tm * TOP_K,), I32), pltpu.SMEM((tm * TOP_K,), I32),
                        pltpu.VMEM((2, tm * SUBLANES, LANES), F32),
                        pltpu.SemaphoreType.DMA((2,)), pltpu.SemaphoreType.DMA((2,))],
        input_output_aliases={} if xs is None else {2: 0},
        compiler_params=_cparams(("arbitrary",)),
        name="moe_dispatch",
    )(*args)


def _pad_fill_kernel(pos_hbm, xs_in_hbm, xs_hbm, pos_smem, zero_tile, pos_sem, row_sem, *, nidx):
    del xs_in_hbm
    i = pl.program_id(0)
    cp = pltpu.make_async_copy(pos_hbm.at[pl.ds(pl.multiple_of(i * nidx, nidx), nidx)], pos_smem, pos_sem)
    cp.start()
    zero_tile[...] = jnp.zeros_like(zero_tile)
    cp.wait()

    def issue(c, carry):
        for rr in range(TOP_K * ISSUE_UNROLL):
            r = c * (TOP_K * ISSUE_UNROLL) + rr
            pltpu.make_async_copy(zero_tile, _row_tile(xs_hbm, pos_smem[r]), row_sem).start(
                priority=rr % DMA_QUEUES)
        return carry
    lax.fori_loop(0, nidx // (TOP_K * ISSUE_UNROLL), issue, 0)
    whole = xs_hbm.at[pl.ds(0, nidx * SUBLANES), :]
    pltpu.make_async_copy(whole, whole, row_sem).wait()


def _pad_fill(pad_pos, xs):
    nidx = GATHER_TILE * TOP_K
    assert pad_pos.shape[0] % nidx == 0
    return pl.pallas_call(
        functools.partial(_pad_fill_kernel, nidx=nidx),
        grid=(pad_pos.shape[0] // nidx,),
        in_specs=[pl.BlockSpec(memory_space=pl.ANY), pl.BlockSpec(memory_space=pl.ANY)],
        out_specs=pl.BlockSpec(memory_space=pl.ANY),
        out_shape=jax.ShapeDtypeStruct(xs.shape, xs.dtype),
        scratch_shapes=[pltpu.SMEM((nidx,), I32), pltpu.VMEM((SUBLANES, LANES), F32),
                        pltpu.SemaphoreType.DMA, pltpu.SemaphoreType.DMA],
        input_output_aliases={1: 0},
        compiler_params=_cparams(("arbitrary",)),
        name="moe_pad_fill",
    )(pad_pos, xs)


def _moe_kernel(be_ref, nu_ref, xs_ref, wgu_ref, bgu_ref, wd_ref, bd_ref, ys_ref, wgu_bf, wd_bf):
    i = pl.program_id(0)
    d, ff = wgu_ref.shape[1], wd_ref.shape[1]

    @pl.when(jnp.logical_and(i < nu_ref[0],
                             jnp.logical_or(i == 0, be_ref[i] != be_ref[jnp.maximum(i - 1, 0)])))
    def _():
        for r in range(0, d, LANES):
            wgu_bf[r:r + LANES, :] = wgu_ref[0, r:r + LANES, :].astype(BF16)
        for r in range(0, ff, LANES):
            wd_bf[r:r + LANES, :] = wd_ref[0, r:r + LANES, :].astype(BF16)

    @pl.when(i < nu_ref[0])
    def _():
        x = _load_row_tiles(xs_ref, MOE_BLK).astype(BF16)
        glu = _dot(x, wgu_bf[:, 0:ff]) + bgu_ref[0, :, 0:ff]
        lin = _dot(x, wgu_bf[:, ff:2 * ff]) + bgu_ref[0, :, ff:2 * ff]
        glu = jnp.minimum(glu, SWIGLU_LIMIT)
        lin = jnp.clip(lin, -SWIGLU_LIMIT, SWIGLU_LIMIT)
        act = glu * jax.nn.sigmoid(SWIGLU_ALPHA * glu) * (lin + 1.0)
        _store_row_tiles(ys_ref, _dot(act.astype(BF16), wd_bf[...]) + bd_ref[0])


def _moe_mlp(block_expert, n_used, xs, w_gu, b_gu, w_down, b_down):
    ne, d, ff2 = w_gu.shape
    ff = ff2 // 2
    tb = MOE_BLK
    n_rows = block_expert.shape[0] * tb
    assert xs.shape[0] >= n_rows * SUBLANES
    blk = lambda i, be, nu: jnp.minimum(i, nu[0] - 1)
    grid_spec = pltpu.PrefetchScalarGridSpec(
        num_scalar_prefetch=2,
        grid=(n_rows // tb,),
        in_specs=[pl.BlockSpec((tb * SUBLANES, LANES), lambda i, be, nu: (blk(i, be, nu), 0)),
                  pl.BlockSpec((1, d, ff2), lambda i, be, nu: (be[blk(i, be, nu)], 0, 0)),
                  pl.BlockSpec((1, 1, ff2), lambda i, be, nu: (be[blk(i, be, nu)], 0, 0)),
                  pl.BlockSpec((1, ff, d), lambda i, be, nu: (be[blk(i, be, nu)], 0, 0)),
                  pl.BlockSpec((1, 1, d), lambda i, be, nu: (be[blk(i, be, nu)], 0, 0))],
        out_specs=pl.BlockSpec((tb * SUBLANES, LANES), lambda i, be, nu: (blk(i, be, nu), 0)),
        scratch_shapes=[pltpu.VMEM((d, ff2), BF16), pltpu.VMEM((ff, d), BF16)],
    )
    return pl.pallas_call(
        _moe_kernel,
        grid_spec=grid_spec,
        out_shape=jax.ShapeDtypeStruct((n_rows * SUBLANES, LANES), F32),
        compiler_params=_cparams(("arbitrary",)),
        name="moe_mlp",
    )(block_expert, n_used, xs, w_gu, b_gu.reshape(ne, 1, ff2), w_down, b_down.reshape(ne, 1, d))


def _combine_kernel(pos_hbm, ys_hbm, x1_ref, tg_ref, mod_ref, y_ref, pos_smem0, pos_smem1, buf, pos_sem, row_sem,
                    *, tm, nb):
    i = pl.program_id(0)
    n = pl.num_programs(0)
    slot = lax.rem(i, 2)
    nidx = tm * TOP_K

    def pos_copy(tile, sl):
        return pltpu.make_async_copy(pos_hbm.at[pl.ds(pl.multiple_of(tile * nidx, nidx), nidx)],
                                     (pos_smem0, pos_smem1)[sl], pos_sem.at[sl])

    def gather(tile, sl):
        pos_copy(tile, sl).wait()
        pos_smem = (pos_smem0, pos_smem1)[sl]

        def issue(c, carry):
            for rr in range(ISSUE_UNROLL):
                r = c * ISSUE_UNROLL + rr
                for k in range(TOP_K):
                    pltpu.make_async_copy(_row_tile(ys_hbm, pos_smem[k * tm + r]),
                                          _row_tile(buf.at[sl, k], r), row_sem.at[sl]).start(
                                              priority=k % DMA_QUEUES)
            return carry
        lax.fori_loop(0, tm // ISSUE_UNROLL, issue, 0)

        @pl.when(tile + 1 < n)
        def _():
            pos_copy(tile + 1, 1 - sl).start()

    @pl.when(i == 0)
    def _():
        pos_copy(0, 0).start()
        gather(0, 0)

    for sl in range(2):
        @pl.when(jnp.logical_and(i + 1 < n, slot == 1 - sl))
        def _():
            gather(i + 1, sl)

    for k in range(TOP_K):
        pltpu.make_async_copy(buf.at[slot, k], buf.at[slot, k], row_sem.at[slot]).wait()

    tg = tg_ref[...].T
    rows = tm // nb
    for c in range(SUBLANES):
        cols = slice(c * LANES, (c + 1) * LANES)
        ff = tg[:, 0:1] * buf[slot, 0, pl.ds(c, tm, stride=SUBLANES), :]
        for k in range(1, TOP_K):
            ff = ff + tg[:, k:k + 1] * buf[slot, k, pl.ds(c, tm, stride=SUBLANES), :]
        for g in range(nb):
            rs = slice(g * rows, (g + 1) * rows)
            y_ref[rs, cols] = x1_ref[rs, cols] + mod_ref[g, 5:6, cols] * ff[rs, :]


def _combine(pos_flat, ys, x1, top_gate, mod, tokens_per_batch):
    t, d = x1.shape
    tm = GATHER_TILE
    assert t % tm == 0 and tm % ISSUE_UNROLL == 0 and d == SUBLANES * LANES
    nb, mod_spec = _mod_spec(tokens_per_batch, tm, d)
    return pl.pallas_call(
        functools.partial(_combine_kernel, tm=tm, nb=nb),
        grid=(t // tm,),
        in_specs=[pl.BlockSpec(memory_space=pl.ANY), pl.BlockSpec(memory_space=pl.ANY),
                  pl.BlockSpec((tm, d), lambda i: (i, 0)), pl.BlockSpec((SUBLANES, tm), lambda i: (0, i)), mod_spec],
        out_specs=pl.BlockSpec((tm, d), lambda i: (i, 0)),
        out_shape=jax.ShapeDtypeStruct((t, d), F32),
        scratch_shapes=[pltpu.SMEM((tm * TOP_K,), I32), pltpu.SMEM((tm * TOP_K,), I32),
                        pltpu.VMEM((2, TOP_K, tm * SUBLANES, LANES), F32),
                        pltpu.SemaphoreType.DMA((2,)), pltpu.SemaphoreType.DMA((2,))],
        compiler_params=_cparams(("arbitrary",)),
        name="moe_combine",
    )(pos_flat, ys, x1, top_gate, mod)


def _pad_lanes(a, fill=0.0):
    return jnp.pad(a, [(0, 0)] * (a.ndim - 1) + [(0, LANES - a.shape[-1])], constant_values=fill)


def _hi_lo(w):
    hi = w.astype(BF16)
    return hi, (w - hi.astype(F32)).astype(BF16)


def _mixer(x, mod, k_past, v_past, conv_past, c0, n0, m0, p):
    b, s, d = x.shape
    outs = _inproj(x.reshape(b * s, d), mod, s, p["g1"], p["w_main"], p["wg_hi"], p["wg_lo"], p["bg"], p["g_k"],
                   p["widths"])
    q_raw, k_norm, v, ml_qk, ml_v, ml_o, gates = [a.reshape(b, s, a.shape[-1]) for a in outs[:7]]
    k_tiles, v_tiles = outs[7:]
    o_sb = _sb_attention(q_raw, k_norm, v, k_past, v_past, p["g_q"])
    o_ml, c_new, n_new, m_new, conv_new = _mlstm(
        ml_qk, ml_v, ml_o, gates, conv_past, c0, n0, _pad_lanes(m0)[:, None, :],
        p["w_conv"], p["b_conv"], p["g_head"])
    flat = lambda a: a.reshape(b * s, a.shape[-1])
    x1, h2, ti, tg = _outproj(flat(o_sb), flat(o_ml), flat(x), mod, s, p["w_out1"], p["w_out2"], p["g2"],
                              p["wrt_hi"], p["wrt_lo"], p["brt"])
    assert v.shape[-1] == SUBLANES * SB_HEAD_DIM
    state = (k_tiles.reshape(b, s, SUBLANES, SB_HEAD_DIM), v_tiles.reshape(b, s, SUBLANES, SB_HEAD_DIM), conv_new,
             c_new, n_new, m_new[:, 0, :ML_HEADS])
    return x1, h2, ti, tg, state


def _layer(xp, xs_, cp, cs, kc, vc, convc, cc, nc, mc, w_ada, b_ada, g_norm1, w_in, g_q, g_k, w_conv, b_conv,
           b_gate, g_head, w_out, g_norm2, w_router, b_router, w_gu, b_gu, w_down, b_down):
    d = xp.shape[-1]
    bp, sp = xp.shape[:2]
    bs, ss = xs_.shape[:2]
    sbw = kc.shape[-2] * kc.shape[-1]
    mlw = ML_HEADS * ML_HEAD_DIM
    ngate = 2 * ML_HEADS
    gate0 = 3 * sbw + 3 * mlw
    w_main = jnp.concatenate([w_in[:, :gate0], w_in[:, gate0 + ngate:]], axis=1).astype(BF16)
    wg_hi, wg_lo = _hi_lo(_pad_lanes(w_in[:, gate0:gate0 + ngate]))
    wrt_hi, wrt_lo = _hi_lo(_pad_lanes(w_router).T)
    p = dict(
        widths=(sbw, sbw, sbw, 2 * mlw, mlw, mlw),
        g1=g_norm1[None], w_main=w_main, wg_hi=wg_hi, wg_lo=wg_lo, bg=_pad_lanes(b_gate)[None],
        g_q=jnp.tile(g_q, LANES // SB_HEAD_DIM)[None], g_k=jnp.tile(g_k, LANES // SB_HEAD_DIM)[None],
        w_conv=w_conv, b_conv=b_conv[None], g_head=g_head[:, None],
        w_out1=w_out[:sbw].astype(BF16), w_out2=w_out[sbw:].astype(BF16), g2=g_norm2[None],
        wrt_hi=wrt_hi, wrt_lo=wrt_lo, brt=b_router[:, None],
    )
    mod = _ada(jnp.concatenate([cp, cs], axis=0), w_ada, b_ada)
    mod_p = mod[:bp].reshape(bp, 6, d)
    mod_s = mod[bp:].reshape(bs, 6, d)

    zeros = lambda *shp: jnp.zeros(shp, F32)
    feature_major = lambda a: jnp.transpose(a, (0, 2, 3, 1)).reshape(a.shape[0], sbw, a.shape[1])
    x1p, h2p, tip, tgp, st_p = _mixer(
        xp, mod_p, None, None, zeros(bp, CONV_W - 1, 2 * mlw), zeros(bp, ML_HEADS, ML_HEAD_DIM, ML_HEAD_DIM),
        zeros(bp, ML_HEADS, ML_HEAD_DIM), zeros(bp, ML_HEADS), p)
    x1s, h2s, tis, tgs, st_s = _mixer(
        xs_, mod_s, feature_major(kc), feature_major(vc), convc, cc, nc, mc, p)

    tp, ts = bp * sp, bs * ss
    pos, counts = _positions(jnp.concatenate([tip, tis], axis=1))
    counts = counts[:, 0]
    padded = (counts + MOE_BLK - 1) // MOE_BLK * MOE_BLK
    pad_end = jnp.cumsum(padded)
    pad_start = pad_end - padded
    n_blocks = -(-((tp + ts) * TOP_K + N_EXPERTS * (MOE_BLK - 1)) // MOE_BLK)
    n_used = (pad_end[-1] // MOE_BLK).astype(I32).reshape(1)
    block_start = jnp.arange(n_blocks, dtype=I32) * MOE_BLK
    block_expert = jnp.minimum(jnp.sum(block_start[:, None] >= pad_end[None, :], axis=1), N_EXPERTS - 1).astype(I32)

    def per_tile(rows):
        t = rows.shape[1]
        return rows.reshape(TOP_K, t // GATHER_TILE, GATHER_TILE).transpose(1, 0, 2).reshape(-1)
    pos_p = per_tile(pos[:TOP_K, :tp])
    pos_s = per_tile(pos[:TOP_K, tp:])

    n_rows = n_blocks * MOE_BLK
    slot = jnp.arange(MOE_BLK, dtype=I32)[None, :]
    spare = n_rows + jnp.arange(N_EXPERTS * MOE_BLK, dtype=I32).reshape(N_EXPERTS, MOE_BLK)
    pad_pos = jnp.where(slot < (padded - counts)[:, None], (pad_start + counts)[:, None] + slot, spare).reshape(-1)

    xs_buf = _dispatch(pos_p, h2p, n_rows=n_rows + N_EXPERTS * MOE_BLK)
    xs_buf = _dispatch(pos_s, h2s, xs_buf)
    xs_buf = _pad_fill(pad_pos.astype(I32), xs_buf)
    ys = _moe_mlp(block_expert, n_used, xs_buf, w_gu, b_gu, w_down, b_down)
    yp = _combine(pos_p, ys, x1p, tgp, mod_p, sp).reshape(bp, sp, d)
    ysm = _combine(pos_s, ys, x1s, tgs, mod_s, ss).reshape(bs, ss, d)
    return yp, ysm, st_p, st_s


def kernel(x_prompt, x_sample, c_prompt, c_sample, cache_sb_k, cache_sb_v, state_conv, state_mlstm_c, state_mlstm_n, state_mlstm_m, w_ada, b_ada, g_norm1, w_in, g_q, g_k, w_conv, b_conv, b_gate, g_head, w_out, g_norm2, w_router, b_router, w_gu, b_gu, w_down, b_down):
    assert w_ada.shape[0] == 1, "single-layer step"
    yp, ys, st_p, st_s = _layer(
        x_prompt, x_sample, c_prompt, c_sample, cache_sb_k[0], cache_sb_v[0], state_conv[0], state_mlstm_c[0],
        state_mlstm_n[0], state_mlstm_m[0], w_ada[0], b_ada[0], g_norm1[0], w_in[0], g_q[0], g_k[0], w_conv[0],
        b_conv[0], b_gate[0], g_head[0], w_out[0], g_norm2[0], w_router[0], b_router[0], w_gu[0], b_gu[0],
        w_down[0], b_down[0])
    return (yp, ys) + tuple(a[None] for a in st_p) + tuple(a[None] for a in st_s)
```
